```python
import math
import jax, jax.numpy as jnp
from jax import lax
import numpy as np

D_MODEL = 1024
BATCH = 8
SEQ = 8192
DEPTH = 2

D_MIX = D_MODEL
HEAD_DIM = 64
D_SC = D_MIX // 4
D_ATT = D_MIX // 2
D_CC = D_MIX // 4
N_Q_HEADS = D_ATT // HEAD_DIM
N_KV_HEADS = 2
GQA_GROUP = N_Q_HEADS // N_KV_HEADS
SC_WIDTH = 3
CC_WIDTH = 31
WINDOW = 128
BLOCK = 128
ROPE_THETA = 10000.0
D_FF = 2816
LN_EPS = 1e-5
ALPHA = (2.0 * DEPTH) ** 0.25
BETA = (8.0 * DEPTH) ** -0.25
IN_SIZES = (D_SC, D_SC, D_SC, N_Q_HEADS * HEAD_DIM, N_KV_HEADS * HEAD_DIM, N_KV_HEADS * HEAD_DIM, D_CC, D_CC)
D_IN = sum(IN_SIZES)
IN_OFFSETS = tuple(int(v) for v in np.cumsum(IN_SIZES)[:-1])

kernel_name = "hybrid_parallel_conv_swa_conformer_encoder"


def layer_norm(x, g, b):
    xf = x.astype(jnp.float32)
    mu = jnp.mean(xf, axis=-1, keepdims=True)
    var = jnp.mean(jnp.square(xf - mu), axis=-1, keepdims=True)
    y = (xf - mu) * lax.rsqrt(var + LN_EPS)
    return (y * g.astype(jnp.float32) + b.astype(jnp.float32)).astype(x.dtype)


def swiglu(x, w_gu, w_down):
    g, u = jnp.split(x @ w_gu, 2, axis=-1)
    return (jax.nn.silu(g) * u) @ w_down


def depthwise_conv(x, w):
    k = w.shape[0]
    pad = (k - 1) // 2
    return lax.conv_general_dilated(
        x, w[:, None, :].astype(x.dtype), window_strides=(1,), padding=[(pad, pad)],
        dimension_numbers=("NWC", "WIO", "NWC"), feature_group_count=x.shape[-1])


def rope(x, positions):
    half = HEAD_DIM // 2
    inv_freq = ROPE_THETA ** (-jnp.arange(half, dtype=jnp.float32) / half)
    ang = positions.astype(jnp.float32)[:, None] * inv_freq[None, :]
    cos = jnp.cos(ang)[None, :, None, :]
    sin = jnp.sin(ang)[None, :, None, :]
    xf = x.astype(jnp.float32)
    x1, x2 = xf[..., :half], xf[..., half:]
    return jnp.concatenate([x1 * cos - x2 * sin, x2 * cos + x1 * sin], axis=-1).astype(x.dtype)


def banded_window_attention(q, k, v, sink):
    b, s = q.shape[0], q.shape[1]
    nb = s // BLOCK
    qb = q.reshape(b, nb, BLOCK, N_KV_HEADS, GQA_GROUP, HEAD_DIM)

    def band(t):
        tp = jnp.pad(t, ((0, 0), (BLOCK, BLOCK), (0, 0), (0, 0)))
        parts = [tp[:, o * BLOCK:o * BLOCK + s].reshape(b, nb, BLOCK, N_KV_HEADS, HEAD_DIM) for o in range(3)]
        return jnp.concatenate(parts, axis=2)

    kb, vb = band(k), band(v)
    scores = jnp.einsum("bnqkgd,bnskd->bnkgqs", qb, kb).astype(jnp.float32) * (HEAD_DIM ** -0.5)
    qpos = jnp.arange(s).reshape(nb, BLOCK)
    kpos = jnp.arange(nb)[:, None] * BLOCK - BLOCK + jnp.arange(3 * BLOCK)[None, :]
    valid = (jnp.abs(qpos[:, :, None] - kpos[:, None, :]) <= WINDOW) \
        & (kpos >= 0)[:, None, :] & (kpos < s)[:, None, :]
    scores = jnp.where(valid[None, :, None, None], scores, -1e30)
    sink_f = sink.astype(jnp.float32).reshape(N_KV_HEADS, GQA_GROUP)[None, None, :, :, None]
    m = jnp.maximum(jnp.max(scores, axis=-1), sink_f)
    p = jnp.exp(scores - m[..., None])
    denom = jnp.sum(p, axis=-1) + jnp.exp(sink_f - m)
    o = jnp.einsum("bnkgqs,bnskd->bnqkgd", p.astype(v.dtype), vb).astype(jnp.float32)
    o = o / jnp.transpose(denom, (0, 1, 4, 2, 3))[..., None]
    return o.reshape(b, s, D_ATT).astype(q.dtype)


def hybrid_mixer(x, w_in, sc_conv_w, attn_sink, cc_conv_w, cc_conv_b, cc_ln_g, cc_ln_b, w_out):
    b, s, _ = x.shape
    positions = jnp.arange(s)
    z = x @ w_in
    sc_b, sc_c, sc_h, q, k, v, cc_a, cc_gate = jnp.split(z, IN_OFFSETS, axis=-1)
    y_sc = sc_b * depthwise_conv(sc_c * sc_h, sc_conv_w)
    q = rope(q.reshape(b, s, N_Q_HEADS, HEAD_DIM), positions)
    k = rope(k.reshape(b, s, N_KV_HEADS, HEAD_DIM), positions)
    v = v.reshape(b, s, N_KV_HEADS, HEAD_DIM)
    y_att = banded_window_attention(q, k, v, attn_sink)
    u = cc_a * jax.nn.sigmoid(cc_gate)
    u = depthwise_conv(u, cc_conv_w) + cc_conv_b
    y_cc = jax.nn.silu(layer_norm(u, cc_ln_g, cc_ln_b))
    return jnp.concatenate([y_sc, y_att, y_cc], axis=-1) @ w_out


def _fwd_setup_inputs(seed: int = 0) -> dict:
    key = jax.random.key(seed)
    ks = jax.random.split(key, 24)
    f32 = jnp.float32

    def nrm(k, shape, scale):
        return jax.random.normal(k, shape, f32) * scale

    def gain(k, n):
        return 1.0 + 0.02 * jax.random.normal(k, (DEPTH, n), f32)

    return {
        "x": jax.random.normal(ks[0], (BATCH, SEQ, D_MODEL), f32),
        "ffn1_w_gu": nrm(ks[1], (DEPTH, D_MODEL, 2 * D_FF), D_MODEL ** -0.5),
        "ffn1_w_down": nrm(ks[2], (DEPTH, D_FF, D_MODEL), BETA * D_FF ** -0.5),
        "ln1_g": gain(ks[3], D_MODEL),
        "ln1_b": nrm(ks[4], (DEPTH, D_MODEL), 0.02),
        "w_in": nrm(ks[5], (DEPTH, D_MODEL, D_IN), D_MODEL ** -0.5),
        "sc_conv_w": nrm(ks[6], (DEPTH, SC_WIDTH, D_SC), SC_WIDTH ** -0.5),
        "attn_sink": nrm(ks[7], (DEPTH, N_Q_HEADS), 0.5),
        "cc_conv_w": nrm(ks[8], (DEPTH, CC_WIDTH, D_CC), CC_WIDTH ** -0.5),
        "cc_conv_b": nrm(ks[9], (DEPTH, D_CC), 0.02),
        "cc_ln_g": gain(ks[10], D_CC),
        "cc_ln_b": nrm(ks[11], (DEPTH, D_CC), 0.02),
        "w_out": nrm(ks[12], (DEPTH, D_MIX, D_MODEL), BETA * D_MIX ** -0.5),
        "ln2_g": gain(ks[13], D_MODEL),
        "ln2_b": nrm(ks[14], (DEPTH, D_MODEL), 0.02),
        "ffn2_w_gu": nrm(ks[15], (DEPTH, D_MODEL, 2 * D_FF), D_MODEL ** -0.5),
        "ffn2_w_down": nrm(ks[16], (DEPTH, D_FF, D_MODEL), BETA * D_FF ** -0.5),
        "ln3_g": gain(ks[17], D_MODEL),
        "ln3_b": nrm(ks[18], (DEPTH, D_MODEL), 0.02),
    }


def _fwd_reference(x, ffn1_w_gu, ffn1_w_down, ln1_g, ln1_b, w_in, sc_conv_w, attn_sink, cc_conv_w,
              cc_conv_b, cc_ln_g, cc_ln_b, w_out, ln2_g, ln2_b, ffn2_w_gu, ffn2_w_down, ln3_g, ln3_b):
    for l in range(DEPTH):
        x = layer_norm(ALPHA * x + 0.5 * swiglu(x, ffn1_w_gu[l], ffn1_w_down[l]), ln1_g[l], ln1_b[l])
        x = layer_norm(ALPHA * x + hybrid_mixer(x, w_in[l], sc_conv_w[l], attn_sink[l], cc_conv_w[l],
                                                 cc_conv_b[l], cc_ln_g[l], cc_ln_b[l], w_out[l]),
                       ln2_g[l], ln2_b[l])
        x = layer_norm(ALPHA * x + 0.5 * swiglu(x, ffn2_w_gu[l], ffn2_w_down[l]), ln3_g[l], ln3_b[l])
    return x


import jax as _jax
import jax.numpy as _jnp

TWIN_FORMAT = 'train_step'
FWD_PARAMS = ['x', 'ffn1_w_gu', 'ffn1_w_down', 'ln1_g', 'ln1_b', 'w_in', 'sc_conv_w', 'attn_sink', 'cc_conv_w', 'cc_conv_b', 'cc_ln_g', 'cc_ln_b', 'w_out', 'ln2_g', 'ln2_b', 'ffn2_w_gu', 'ffn2_w_down', 'ln3_g', 'ln3_b']
TWIN_WEIGHTS = ['ffn1_w_gu', 'ffn1_w_down', 'ln1_g', 'ln1_b', 'w_in', 'sc_conv_w', 'attn_sink', 'cc_conv_w', 'cc_conv_b', 'cc_ln_g', 'cc_ln_b', 'w_out', 'ln2_g', 'ln2_b', 'ffn2_w_gu', 'ffn2_w_down', 'ln3_g', 'ln3_b']
TWIN_DIFF_INPUT = 'x'
TWIN_INPUTS = ['x', 'ffn1_w_gu', 'ffn1_w_down', 'ln1_g', 'ln1_b', 'w_in', 'sc_conv_w', 'attn_sink', 'cc_conv_w', 'cc_conv_b', 'cc_ln_g', 'cc_ln_b', 'w_out', 'ln2_g', 'ln2_b', 'ffn2_w_gu', 'ffn2_w_down', 'ln3_g', 'ln3_b', 'loss_target', 'm_ffn1_w_gu', 'm_ffn1_w_down', 'm_ln1_g', 'm_ln1_b', 'm_w_in', 'm_sc_conv_w', 'm_attn_sink', 'm_cc_conv_w', 'm_cc_conv_b', 'm_cc_ln_g', 'm_cc_ln_b', 'm_w_out', 'm_ln2_g', 'm_ln2_b', 'm_ffn2_w_gu', 'm_ffn2_w_down', 'm_ln3_g', 'm_ln3_b', 'v_ffn1_w_gu', 'v_ffn1_w_down', 'v_ln1_g', 'v_ln1_b', 'v_w_in', 'v_sc_conv_w', 'v_attn_sink', 'v_cc_conv_w', 'v_cc_conv_b', 'v_cc_ln_g', 'v_cc_ln_b', 'v_w_out', 'v_ln2_g', 'v_ln2_b', 'v_ffn2_w_gu', 'v_ffn2_w_down', 'v_ln3_g', 'v_ln3_b']
TWIN_OUTPUTS = ['loss', 'grad_x', 'grad_ffn1_w_gu', 'grad_ffn1_w_down', 'grad_ln1_g', 'grad_ln1_b', 'grad_w_in', 'grad_sc_conv_w', 'grad_attn_sink', 'grad_cc_conv_w', 'grad_cc_conv_b', 'grad_cc_ln_g', 'grad_cc_ln_b', 'grad_w_out', 'grad_ln2_g', 'grad_ln2_b', 'grad_ffn2_w_gu', 'grad_ffn2_w_down', 'grad_ln3_g', 'grad_ln3_b', 'delta_ffn1_w_gu', 'delta_ffn1_w_down', 'delta_ln1_g', 'delta_ln1_b', 'delta_w_in', 'delta_sc_conv_w', 'delta_attn_sink', 'delta_cc_conv_w', 'delta_cc_conv_b', 'delta_cc_ln_g', 'delta_cc_ln_b', 'delta_w_out', 'delta_ln2_g', 'delta_ln2_b', 'delta_ffn2_w_gu', 'delta_ffn2_w_down', 'delta_ln3_g', 'delta_ln3_b', 'new_m_ffn1_w_gu', 'new_m_ffn1_w_down', 'new_m_ln1_g', 'new_m_ln1_b', 'new_m_w_in', 'new_m_sc_conv_w', 'new_m_attn_sink', 'new_m_cc_conv_w', 'new_m_cc_conv_b', 'new_m_cc_ln_g', 'new_m_cc_ln_b', 'new_m_w_out', 'new_m_ln2_g', 'new_m_ln2_b', 'new_m_ffn2_w_gu', 'new_m_ffn2_w_down', 'new_m_ln3_g', 'new_m_ln3_b', 'new_v_ffn1_w_gu', 'new_v_ffn1_w_down', 'new_v_ln1_g', 'new_v_ln1_b', 'new_v_w_in', 'new_v_sc_conv_w', 'new_v_attn_sink', 'new_v_cc_conv_w', 'new_v_cc_conv_b', 'new_v_cc_ln_g', 'new_v_cc_ln_b', 'new_v_w_out', 'new_v_ln2_g', 'new_v_ln2_b', 'new_v_ffn2_w_gu', 'new_v_ffn2_w_down', 'new_v_ln3_g', 'new_v_ln3_b']
TWIN_LEAF_KINDS = {'loss': 'loss', 'grad_x': 'grad_x', 'grad_ffn1_w_gu': 'grad_w', 'grad_ffn1_w_down': 'grad_w', 'grad_ln1_g': 'grad_w', 'grad_ln1_b': 'grad_w', 'grad_w_in': 'grad_w', 'grad_sc_conv_w': 'grad_w', 'grad_attn_sink': 'grad_w', 'grad_cc_conv_w': 'grad_w', 'grad_cc_conv_b': 'grad_w', 'grad_cc_ln_g': 'grad_w', 'grad_cc_ln_b': 'grad_w', 'grad_w_out': 'grad_w', 'grad_ln2_g': 'grad_w', 'grad_ln2_b': 'grad_w', 'grad_ffn2_w_gu': 'grad_w', 'grad_ffn2_w_down': 'grad_w', 'grad_ln3_g': 'grad_w', 'grad_ln3_b': 'grad_w', 'delta_ffn1_w_gu': 'delta_w', 'delta_ffn1_w_down': 'delta_w', 'delta_ln1_g': 'delta_w', 'delta_ln1_b': 'delta_w', 'delta_w_in': 'delta_w', 'delta_sc_conv_w': 'delta_w', 'delta_attn_sink': 'delta_w', 'delta_cc_conv_w': 'delta_w', 'delta_cc_conv_b': 'delta_w', 'delta_cc_ln_g': 'delta_w', 'delta_cc_ln_b': 'delta_w', 'delta_w_out': 'delta_w', 'delta_ln2_g': 'delta_w', 'delta_ln2_b': 'delta_w', 'delta_ffn2_w_gu': 'delta_w', 'delta_ffn2_w_down': 'delta_w', 'delta_ln3_g': 'delta_w', 'delta_ln3_b': 'delta_w', 'new_m_ffn1_w_gu': 'new_m', 'new_m_ffn1_w_down': 'new_m', 'new_m_ln1_g': 'new_m', 'new_m_ln1_b': 'new_m', 'new_m_w_in': 'new_m', 'new_m_sc_conv_w': 'new_m', 'new_m_attn_sink': 'new_m', 'new_m_cc_conv_w': 'new_m', 'new_m_cc_conv_b': 'new_m', 'new_m_cc_ln_g': 'new_m', 'new_m_cc_ln_b': 'new_m', 'new_m_w_out': 'new_m', 'new_m_ln2_g': 'new_m', 'new_m_ln2_b': 'new_m', 'new_m_ffn2_w_gu': 'new_m', 'new_m_ffn2_w_down': 'new_m', 'new_m_ln3_g': 'new_m', 'new_m_ln3_b': 'new_m', 'new_v_ffn1_w_gu': 'new_v', 'new_v_ffn1_w_down': 'new_v', 'new_v_ln1_g': 'new_v', 'new_v_ln1_b': 'new_v', 'new_v_w_in': 'new_v', 'new_v_sc_conv_w': 'new_v', 'new_v_attn_sink': 'new_v', 'new_v_cc_conv_w': 'new_v', 'new_v_cc_conv_b': 'new_v', 'new_v_cc_ln_g': 'new_v', 'new_v_cc_ln_b': 'new_v', 'new_v_w_out': 'new_v', 'new_v_ln2_g': 'new_v', 'new_v_ln2_b': 'new_v', 'new_v_ffn2_w_gu': 'new_v', 'new_v_ffn2_w_down': 'new_v', 'new_v_ln3_g': 'new_v', 'new_v_ln3_b': 'new_v'}


def _forward(args):
    return _fwd_reference(*[args[k] for k in FWD_PARAMS])


def _output_shape():
    def fwd():
        inp = _fwd_setup_inputs(0)
        return _fwd_reference(*[inp[k] for k in FWD_PARAMS])
    out = _jax.eval_shape(fwd)
    return out.shape, out.dtype

N_MICROBATCH = 1
ADAM_LR = 0.001
ADAM_B1 = 0.9
ADAM_B2 = 0.999
ADAM_EPS = 1e-08
ADAM_WD = 0.01
ADAM_STEP = 10
PER_EXAMPLE_BATCH_AXIS = {'x': 0, 'loss_target': 0}
SHARED_INPUTS = []
_WEIGHT_DTYPES = {'ffn1_w_gu': _jnp.float32, 'ffn1_w_down': _jnp.float32, 'ln1_g': _jnp.float32, 'ln1_b': _jnp.float32, 'w_in': _jnp.float32, 'sc_conv_w': _jnp.float32, 'attn_sink': _jnp.float32, 'cc_conv_w': _jnp.float32, 'cc_conv_b': _jnp.float32, 'cc_ln_g': _jnp.float32, 'cc_ln_b': _jnp.float32, 'w_out': _jnp.float32, 'ln2_g': _jnp.float32, 'ln2_b': _jnp.float32, 'ffn2_w_gu': _jnp.float32, 'ffn2_w_down': _jnp.float32, 'ln3_g': _jnp.float32, 'ln3_b': _jnp.float32}
MOMENT_SCALE = {'ffn1_w_gu': 1.708237e-02, 'ffn1_w_down': 5.569689e-02, 'ln1_g': 2.083665e+00, 'ln1_b': 1.048765e+00, 'w_in': 5.987131e-02, 'sc_conv_w': 8.860983e-02, 'attn_sink': 2.800108e-04, 'cc_conv_w': 5.717667e-02, 'cc_conv_b': 3.259683e-01, 'cc_ln_g': 1.194727e-01, 'cc_ln_b': 1.768099e-01, 'w_out': 1.181644e-01, 'ln2_g': 2.196229e+00, 'ln2_b': 1.037936e+00, 'ffn2_w_gu': 1.651519e-02, 'ffn2_w_down': 5.404392e-02, 'ln3_g': 4.539912e+01, 'ln3_b': 3.186146e+00}


def _to_microbatches(a, axis):
    t = _jnp.moveaxis(a, axis, 0)
    t = t.reshape((N_MICROBATCH, t.shape[0] // N_MICROBATCH) + t.shape[1:])
    return _jnp.moveaxis(t, 1, axis + 1)


def setup_inputs(seed: int = 0) -> dict:
    inp = _fwd_setup_inputs(seed)
    key = _jax.random.fold_in(_jax.random.key(seed), 7919)
    shape, _ = _output_shape()
    out = dict(inp)
    out["loss_target"] = _jax.random.normal(_jax.random.fold_in(key, 0), shape, _jnp.float32)
    for i, name in enumerate(TWIN_WEIGHTS):
        w = inp[name].astype(_jnp.float32)
        if MOMENT_SCALE is None:
            s = _jnp.sqrt(_jnp.mean(_jnp.square(w)) + 1e-30)
        else:
            s = MOMENT_SCALE[name]
        km, kv = _jax.random.split(_jax.random.fold_in(key, i + 1))
        out[name] = w
        out["m_" + name] = s * _jax.random.normal(km, w.shape, _jnp.float32)
        out["v_" + name] = (s * s) * _jax.random.uniform(kv, w.shape, _jnp.float32, 0.5, 1.5)
    if N_MICROBATCH > 1:
        for name, axis in PER_EXAMPLE_BATCH_AXIS.items():
            out[name] = _to_microbatches(out[name], axis)
    return {'x': out['x'], 'ffn1_w_gu': out['ffn1_w_gu'], 'ffn1_w_down': out['ffn1_w_down'], 'ln1_g': out['ln1_g'], 'ln1_b': out['ln1_b'], 'w_in': out['w_in'], 'sc_conv_w': out['sc_conv_w'], 'attn_sink': out['attn_sink'], 'cc_conv_w': out['cc_conv_w'], 'cc_conv_b': out['cc_conv_b'], 'cc_ln_g': out['cc_ln_g'], 'cc_ln_b': out['cc_ln_b'], 'w_out': out['w_out'], 'ln2_g': out['ln2_g'], 'ln2_b': out['ln2_b'], 'ffn2_w_gu': out['ffn2_w_gu'], 'ffn2_w_down': out['ffn2_w_down'], 'ln3_g': out['ln3_g'], 'ln3_b': out['ln3_b'], 'loss_target': out['loss_target'], 'm_ffn1_w_gu': out['m_ffn1_w_gu'], 'm_ffn1_w_down': out['m_ffn1_w_down'], 'm_ln1_g': out['m_ln1_g'], 'm_ln1_b': out['m_ln1_b'], 'm_w_in': out['m_w_in'], 'm_sc_conv_w': out['m_sc_conv_w'], 'm_attn_sink': out['m_attn_sink'], 'm_cc_conv_w': out['m_cc_conv_w'], 'm_cc_conv_b': out['m_cc_conv_b'], 'm_cc_ln_g': out['m_cc_ln_g'], 'm_cc_ln_b': out['m_cc_ln_b'], 'm_w_out': out['m_w_out'], 'm_ln2_g': out['m_ln2_g'], 'm_ln2_b': out['m_ln2_b'], 'm_ffn2_w_gu': out['m_ffn2_w_gu'], 'm_ffn2_w_down': out['m_ffn2_w_down'], 'm_ln3_g': out['m_ln3_g'], 'm_ln3_b': out['m_ln3_b'], 'v_ffn1_w_gu': out['v_ffn1_w_gu'], 'v_ffn1_w_down': out['v_ffn1_w_down'], 'v_ln1_g': out['v_ln1_g'], 'v_ln1_b': out['v_ln1_b'], 'v_w_in': out['v_w_in'], 'v_sc_conv_w': out['v_sc_conv_w'], 'v_attn_sink': out['v_attn_sink'], 'v_cc_conv_w': out['v_cc_conv_w'], 'v_cc_conv_b': out['v_cc_conv_b'], 'v_cc_ln_g': out['v_cc_ln_g'], 'v_cc_ln_b': out['v_cc_ln_b'], 'v_w_out': out['v_w_out'], 'v_ln2_g': out['v_ln2_g'], 'v_ln2_b': out['v_ln2_b'], 'v_ffn2_w_gu': out['v_ffn2_w_gu'], 'v_ffn2_w_down': out['v_ffn2_w_down'], 'v_ln3_g': out['v_ln3_g'], 'v_ln3_b': out['v_ln3_b']}


def _loss(weights, diff, rest, loss_target):
    with _jax.named_scope("forward"):
        args = {**rest, TWIN_DIFF_INPUT: diff, **{k: w.astype(_WEIGHT_DTYPES[k]) for k, w in weights.items()}}
        y = _forward(args)
    with _jax.named_scope("loss_head"):
        err = _jnp.square(y.astype(_jnp.float32) - loss_target)
        return 0.5 * _jnp.sum(_jnp.mean(err, axis=-1)) if err.ndim else 0.5 * err


def _adamw(w, g, m, v):
    m = ADAM_B1 * m + (1.0 - ADAM_B1) * g
    v = ADAM_B2 * v + (1.0 - ADAM_B2) * _jnp.square(g)
    m_hat = m / (1.0 - ADAM_B1 ** ADAM_STEP)
    v_hat = v / (1.0 - ADAM_B2 ** ADAM_STEP)
    delta = -ADAM_LR * (m_hat / (_jnp.sqrt(v_hat) + ADAM_EPS) + ADAM_WD * w)
    return delta, m, v


def reference(x, ffn1_w_gu, ffn1_w_down, ln1_g, ln1_b, w_in, sc_conv_w, attn_sink, cc_conv_w, cc_conv_b, cc_ln_g, cc_ln_b, w_out, ln2_g, ln2_b, ffn2_w_gu, ffn2_w_down, ln3_g, ln3_b, loss_target, m_ffn1_w_gu, m_ffn1_w_down, m_ln1_g, m_ln1_b, m_w_in, m_sc_conv_w, m_attn_sink, m_cc_conv_w, m_cc_conv_b, m_cc_ln_g, m_cc_ln_b, m_w_out, m_ln2_g, m_ln2_b, m_ffn2_w_gu, m_ffn2_w_down, m_ln3_g, m_ln3_b, v_ffn1_w_gu, v_ffn1_w_down, v_ln1_g, v_ln1_b, v_w_in, v_sc_conv_w, v_attn_sink, v_cc_conv_w, v_cc_conv_b, v_cc_ln_g, v_cc_ln_b, v_w_out, v_ln2_g, v_ln2_b, v_ffn2_w_gu, v_ffn2_w_down, v_ln3_g, v_ln3_b):
    given = dict(x=x, ffn1_w_gu=ffn1_w_gu, ffn1_w_down=ffn1_w_down, ln1_g=ln1_g, ln1_b=ln1_b, w_in=w_in, sc_conv_w=sc_conv_w, attn_sink=attn_sink, cc_conv_w=cc_conv_w, cc_conv_b=cc_conv_b, cc_ln_g=cc_ln_g, cc_ln_b=cc_ln_b, w_out=w_out, ln2_g=ln2_g, ln2_b=ln2_b, ffn2_w_gu=ffn2_w_gu, ffn2_w_down=ffn2_w_down, ln3_g=ln3_g, ln3_b=ln3_b, loss_target=loss_target, m_ffn1_w_gu=m_ffn1_w_gu, m_ffn1_w_down=m_ffn1_w_down, m_ln1_g=m_ln1_g, m_ln1_b=m_ln1_b, m_w_in=m_w_in, m_sc_conv_w=m_sc_conv_w, m_attn_sink=m_attn_sink, m_cc_conv_w=m_cc_conv_w, m_cc_conv_b=m_cc_conv_b, m_cc_ln_g=m_cc_ln_g, m_cc_ln_b=m_cc_ln_b, m_w_out=m_w_out, m_ln2_g=m_ln2_g, m_ln2_b=m_ln2_b, m_ffn2_w_gu=m_ffn2_w_gu, m_ffn2_w_down=m_ffn2_w_down, m_ln3_g=m_ln3_g, m_ln3_b=m_ln3_b, v_ffn1_w_gu=v_ffn1_w_gu, v_ffn1_w_down=v_ffn1_w_down, v_ln1_g=v_ln1_g, v_ln1_b=v_ln1_b, v_w_in=v_w_in, v_sc_conv_w=v_sc_conv_w, v_attn_sink=v_attn_sink, v_cc_conv_w=v_cc_conv_w, v_cc_conv_b=v_cc_conv_b, v_cc_ln_g=v_cc_ln_g, v_cc_ln_b=v_cc_ln_b, v_w_out=v_w_out, v_ln2_g=v_ln2_g, v_ln2_b=v_ln2_b, v_ffn2_w_gu=v_ffn2_w_gu, v_ffn2_w_down=v_ffn2_w_down, v_ln3_g=v_ln3_g, v_ln3_b=v_ln3_b)
    weights = {n: given[n] for n in TWIN_WEIGHTS}
    shared = {n: given[n] for n in SHARED_INPUTS}
    per_example = {n: given[n] for n in ['x']}
    grad_fn = _jax.value_and_grad(_loss, argnums=(0, 1))

    def one_microbatch(ex, loss_target):
        ex = dict(ex)
        diff = ex.pop(TWIN_DIFF_INPUT)
        return grad_fn(weights, diff, {**shared, **ex}, loss_target)

    if N_MICROBATCH == 1:
        loss, (grad_w, grad_x) = one_microbatch(per_example, given["loss_target"])
    else:
        def body(carry, xs):
            loss_sum, grad_sum = carry
            l_k, (gw_k, gx_k) = one_microbatch(xs[0], xs[1])
            with _jax.named_scope("update"):
                return (loss_sum + l_k, _jax.tree.map(_jnp.add, grad_sum, gw_k)), gx_k

        init = (_jnp.zeros((), _jnp.float32), _jax.tree.map(_jnp.zeros_like, weights))
        (loss, grad_w), grad_x = _jax.lax.scan(body, init, (per_example, given["loss_target"]))
    with _jax.named_scope("update"):
        delta_w, new_m, new_v = {}, {}, {}
        for n in TWIN_WEIGHTS:
            delta_w[n], new_m[n], new_v[n] = _adamw(weights[n], grad_w[n], given["m_" + n], given["v_" + n])
    return (loss, grad_x, *[grad_w[n] for n in TWIN_WEIGHTS], *[delta_w[n] for n in TWIN_WEIGHTS],
            *[new_m[n] for n in TWIN_WEIGHTS], *[new_v[n] for n in TWIN_WEIGHTS])
```

```python
import functools

import numpy as np
import jax
import jax.numpy as jnp
from jax import lax
from jax.experimental import pallas as pl
from jax.experimental.pallas import tpu as pltpu

F32 = jnp.float32
MXU_DTYPE = jnp.bfloat16

D_MODEL = 1024
D_FF = 2816
N_SHARD = 4
D_IN = 2048
GU_SHARD = 2 * D_FF // N_SHARD
FF_CHUNK = GU_SHARD
N_CHUNK = D_FF // FF_CHUNK
IN_SHARD = D_IN // N_SHARD
OUT_SHARD = D_MODEL // N_SHARD
HEAD_DIM = 64
N_Q_HEADS = 8
BLOCK = 128
SC_W = 3
CC_W = 31
D_CONV = 256
HALO = 16
LN_EPS = 1e-5
ALPHA = (2.0 * 2) ** 0.25
NEG = -1e30
ROPE_THETA = 10000.0
ADAM_LR, ADAM_B1, ADAM_B2, ADAM_EPS, ADAM_WD, ADAM_STEP = 0.001, 0.9, 0.999, 1e-08, 0.01, 10

TM = 512
TMB = 256
TMC = 256
VMEM_LIMIT = 56 * 1024 * 1024
MESH = pl.DeviceIdType.MESH
ANY = pl.BlockSpec(memory_space=pl.ANY)


def _cparams(*sem):
    return pltpu.CompilerParams(dimension_semantics=sem, vmem_limit_bytes=VMEM_LIMIT)


def _dot(a, b):
    return jnp.dot(a, b, preferred_element_type=F32)


def _dot_nt(a, b):
    return lax.dot_general(a, b, (((1,), (1,)), ((), ())), preferred_element_type=F32)


def _dot_tn(a, b):
    return lax.dot_general(a, b, (((0,), (0,)), ((), ())), preferred_element_type=F32)


def _mx(a):
    return a.astype(MXU_DTYPE)


def _mean(a):
    return jnp.mean(a, axis=-1, keepdims=True)


def _ln_stats(r):
    xc = r - _mean(r)
    rstd = lax.rsqrt(_mean(xc * xc) + LN_EPS)
    return xc * rstd, rstd


def _ln_bwd(dy, xh, rstd, gamma):
    dxh = dy * gamma
    return rstd * (dxh - _mean(dxh) - xh * _mean(dxh * xh))


def _colsum(a):
    return jnp.sum(a, axis=0, keepdims=True)


def _sigmoid(a):
    return 1.0 / (1.0 + jnp.exp(-a))


def _ffn_fwd(x, wgu, wd, gamma, beta, l):
    t = x.shape[0]
    nc = N_CHUNK

    def body(x_ref, wg_ref, wu_ref, wd_ref, g_ref, b_ref, y_ref, r_ref, gu_ref, xb_s, acc_s):
        c = pl.program_id(1)

        @pl.when(c == 0)
        def _():
            xb_s[...] = _mx(x_ref[...])
            acc_s[...] = jnp.zeros_like(acc_s)

        xb = xb_s[...]
        hg = _dot(xb, wg_ref[...])
        hu = _dot(xb, wu_ref[...])
        gu_ref[0] = _mx(hg)
        gu_ref[1] = _mx(hu)
        a = (hg * _sigmoid(hg)) * hu
        acc_s[...] += _dot(_mx(a), wd_ref[...])

        @pl.when(c == nc - 1)
        def _():
            r = ALPHA * x_ref[...] + 0.5 * acc_s[...]
            xh, _ = _ln_stats(r)
            r_ref[...] = r
            y_ref[...] = xh * g_ref[...] + b_ref[...]

    row = pl.BlockSpec((TM, D_MODEL), lambda i, c: (i, 0))
    vec = pl.BlockSpec((1, D_MODEL), lambda i, c: (0, 0))
    return pl.pallas_call(
        body, name="ffn_fwd", grid=(t // TM, nc),
        in_specs=[row,
                  pl.BlockSpec((None, None, D_MODEL, FF_CHUNK), lambda i, c: (l, c, 0, 0)),
                  pl.BlockSpec((None, None, D_MODEL, FF_CHUNK), lambda i, c: (l, N_CHUNK + c, 0, 0)),
                  pl.BlockSpec((None, FF_CHUNK, D_MODEL), lambda i, c: (l, c, 0)),
                  vec, vec],
        out_specs=[row, row, pl.BlockSpec((2, TM, FF_CHUNK), lambda i, c: (0, i, c))],
        out_shape=[jax.ShapeDtypeStruct((t, D_MODEL), F32), jax.ShapeDtypeStruct((t, D_MODEL), F32),
                   jax.ShapeDtypeStruct((2, t, D_FF), MXU_DTYPE)],
        scratch_shapes=[pltpu.VMEM((TM, D_MODEL), MXU_DTYPE), pltpu.VMEM((TM, D_MODEL), F32)],
        compiler_params=_cparams("parallel", "arbitrary"),
    )(x, wgu, wgu, wd, gamma, beta)


def _ffn_bwd(dy, r, gu, wgu, wd, gamma, l):
    t = dy.shape[0]
    nc = N_CHUNK

    def body(dy_ref, r_ref, gu_ref, wg_ref, wu_ref, wd_ref, g_ref,
             dx_ref, dh_ref, a_ref, do_ref, dgb_ref, do_s, dx_s):
        i = pl.program_id(0)
        c = pl.program_id(1)

        @pl.when((i == 0) & (c == 0))
        def _():
            dgb_ref[...] = jnp.zeros_like(dgb_ref)

        @pl.when(c == 0)
        def _():
            xh, rstd = _ln_stats(r_ref[...])
            dy = dy_ref[...]
            dr = _ln_bwd(dy, xh, rstd, g_ref[...])
            dob = _mx(0.5 * dr)
            do_s[...] = dob
            do_ref[...] = dob
            dx_s[...] = ALPHA * dr
            dgb_ref[0:1, :] += _colsum(dy * xh)
            dgb_ref[1:2, :] += _colsum(dy)

        dob = do_s[...]
        da = _dot_nt(dob, wd_ref[...])
        g = gu_ref[0].astype(F32)
        u = gu_ref[1].astype(F32)
        s = _sigmoid(g)
        sil = g * s
        a_ref[...] = _mx(sil * u)
        dg = _mx(da * u * (s * (1.0 + g * (1.0 - s))))
        du = _mx(da * sil)
        dh_ref[0] = dg
        dh_ref[1] = du
        dx_s[...] += _dot_nt(dg, wg_ref[...]) + _dot_nt(du, wu_ref[...])

        @pl.when(c == nc - 1)
        def _():
            dx_ref[...] = dx_s[...]

    row = pl.BlockSpec((TMB, D_MODEL), lambda i, c: (i, 0))
    hid = pl.BlockSpec((2, TMB, FF_CHUNK), lambda i, c: (0, i, c))
    return pl.pallas_call(
        body, name="ffn_bwd", grid=(t // TMB, nc),
        in_specs=[row, row, hid,
                  pl.BlockSpec((None, None, D_MODEL, FF_CHUNK), lambda i, c: (l, c, 0, 0)),
                  pl.BlockSpec((None, None, D_MODEL, FF_CHUNK), lambda i, c: (l, N_CHUNK + c, 0, 0)),
                  pl.BlockSpec((None, FF_CHUNK, D_MODEL), lambda i, c: (l, c, 0)),
                  pl.BlockSpec((1, D_MODEL), lambda i, c: (0, 0))],
        out_specs=[row, hid, pl.BlockSpec((TMB, FF_CHUNK), lambda i, c: (i, c)), row,
                   pl.BlockSpec((8, D_MODEL), lambda i, c: (0, 0))],
        out_shape=[jax.ShapeDtypeStruct((t, D_MODEL), F32), jax.ShapeDtypeStruct((2, t, D_FF), MXU_DTYPE),
                   jax.ShapeDtypeStruct((t, D_FF), MXU_DTYPE), jax.ShapeDtypeStruct((t, D_MODEL), MXU_DTYPE),
                   jax.ShapeDtypeStruct((8, D_MODEL), F32)],
        scratch_shapes=[pltpu.VMEM((TMB, D_MODEL), MXU_DTYPE), pltpu.VMEM((TMB, D_MODEL), F32)],
        compiler_params=_cparams("arbitrary", "arbitrary"),
    )(dy, r, gu, wgu, wgu, wd, gamma)


def _mm_tn(name, a, b, a_spec, b_spec, out_spec, out_shape, acc_shape, grid, prev=None):
    nk = grid[-1]

    def body(*refs):
        a_ref, b_ref = refs[0], refs[1]
        o_ref, acc = refs[-2], refs[-1]
        k = pl.program_id(len(grid) - 1)

        @pl.when(k == 0)
        def _():
            acc[...] = jnp.zeros_like(acc)

        acc[...] += _dot_tn(_mx(a_ref[...]), _mx(b_ref[...]))

        @pl.when(k == nk - 1)
        def _():
            o_ref[...] = acc[...]

    ins, specs, alias = [a, b], [a_spec, b_spec], {}
    if prev is not None:
        ins.append(prev)
        specs.append(ANY)
        alias = {2: 0}
    sem = ("parallel",) * (len(grid) - 1) + ("arbitrary",)
    return pl.pallas_call(
        body, name=name, grid=grid, in_specs=specs, out_specs=out_spec,
        out_shape=jax.ShapeDtypeStruct(out_shape, F32), input_output_aliases=alias,
        scratch_shapes=[pltpu.VMEM(acc_shape, F32)], compiler_params=_cparams(*sem),
    )(*ins)


def _in_proj(x, w_in, l):
    t = x.shape[0]

    def body(x_ref, w_ref, z_ref):
        z_ref[...] = _dot(_mx(x_ref[...]), w_ref[...])

    return pl.pallas_call(
        body, name="in_proj", grid=(t // TM, N_SHARD),
        in_specs=[pl.BlockSpec((TM, D_MODEL), lambda i, j: (i, 0)),
                  pl.BlockSpec((None, None, D_MODEL, IN_SHARD), lambda i, j: (l, j, 0, 0))],
        out_specs=pl.BlockSpec((TM, IN_SHARD), lambda i, j: (i, j)),
        out_shape=jax.ShapeDtypeStruct((t, D_IN), F32),
        compiler_params=_cparams("parallel", "arbitrary"),
    )(x, w_in)


def _in_proj_bwd(dz, dx_res, w_in, l):
    t = dz.shape[0]

    def body(dz_ref, res_ref, w_ref, dx_ref, acc):
        j = pl.program_id(1)

        @pl.when(j == 0)
        def _():
            acc[...] = res_ref[...]

        acc[...] += _dot_nt(dz_ref[...], w_ref[...])

        @pl.when(j == N_SHARD - 1)
        def _():
            dx_ref[...] = acc[...]

    row = pl.BlockSpec((TM, D_MODEL), lambda i, j: (i, 0))
    return pl.pallas_call(
        body, name="in_proj_bwd", grid=(t // TM, N_SHARD),
        in_specs=[pl.BlockSpec((TM, IN_SHARD), lambda i, j: (i, j)), row,
                  pl.BlockSpec((None, None, D_MODEL, IN_SHARD), lambda i, j: (l, j, 0, 0))],
        out_specs=row, out_shape=jax.ShapeDtypeStruct((t, D_MODEL), F32),
        scratch_shapes=[pltpu.VMEM((TM, D_MODEL), F32)],
        compiler_params=_cparams("parallel", "arbitrary"),
    )(dz, dx_res, w_in)


def _out_proj(ycat, x, w_out, gamma, beta, l):
    t = x.shape[0]

    def body(yc_ref, x_ref, w_ref, g_ref, b_ref, y_ref, r_ref):
        r = ALPHA * x_ref[...] + _dot(yc_ref[...], w_ref[...])
        xh, _ = _ln_stats(r)
        r_ref[...] = r
        y_ref[...] = xh * g_ref[...] + b_ref[...]

    row = pl.BlockSpec((TM, D_MODEL), lambda i: (i, 0))
    vec = pl.BlockSpec((1, D_MODEL), lambda i: (0, 0))
    return pl.pallas_call(
        body, name="out_proj", grid=(t // TM,),
        in_specs=[row, row, pl.BlockSpec((None, D_MODEL, D_MODEL), lambda i: (l, 0, 0)), vec, vec],
        out_specs=[row, row],
        out_shape=[jax.ShapeDtypeStruct((t, D_MODEL), F32)] * 2,
        compiler_params=_cparams("parallel"),
    )(ycat, x, w_out, gamma, beta)


def _out_proj_bwd(dy, r, w_out, gamma, l):
    t = dy.shape[0]

    def body(dy_ref, r_ref, w_ref, g_ref, res_ref, dm_ref, dyc_ref, dgb_ref):
        @pl.when(pl.program_id(0) == 0)
        def _():
            dgb_ref[...] = jnp.zeros_like(dgb_ref)

        xh, rstd = _ln_stats(r_ref[...])
        dy = dy_ref[...]
        dr = _ln_bwd(dy, xh, rstd, g_ref[...])
        res_ref[...] = ALPHA * dr
        dm = _mx(dr)
        dm_ref[...] = dm
        dyc_ref[...] = _dot_nt(dm, w_ref[...])
        dgb_ref[0:1, :] += _colsum(dy * xh)
        dgb_ref[1:2, :] += _colsum(dy)

    row = pl.BlockSpec((TM, D_MODEL), lambda i: (i, 0))
    return pl.pallas_call(
        body, name="out_proj_bwd", grid=(t // TM,),
        in_specs=[row, row, pl.BlockSpec((None, D_MODEL, D_MODEL), lambda i: (l, 0, 0)),
                  pl.BlockSpec((1, D_MODEL), lambda i: (0, 0))],
        out_specs=[row, row, row, pl.BlockSpec((8, D_MODEL), lambda i: (0, 0))],
        out_shape=[jax.ShapeDtypeStruct((t, D_MODEL), F32), jax.ShapeDtypeStruct((t, D_MODEL), MXU_DTYPE),
                   jax.ShapeDtypeStruct((t, D_MODEL), F32), jax.ShapeDtypeStruct((8, D_MODEL), F32)],
        compiler_params=_cparams("arbitrary"),
    )(dy, r, w_out, gamma)


def _halo_specs(t, width, col):
    per = TMC // HALO
    last = t // HALO - 1
    return [pl.BlockSpec((HALO, width), lambda i: (jnp.maximum(i * per - 1, 0), col)),
            pl.BlockSpec((TMC, width), lambda i: (i, col)),
            pl.BlockSpec((HALO, width), lambda i: (jnp.minimum((i + 1) * per, last), col))]


def _extend(refs, i, nt):
    p_ref, c_ref, n_ref = refs
    p = jnp.where(i > 0, p_ref[...].astype(F32), 0.0)
    n = jnp.where(i < nt - 1, n_ref[...].astype(F32), 0.0)
    return jnp.concatenate([p, c_ref[...].astype(F32), n], axis=0)


def _conv_fwd(z, sc_w, cc_w, cc_cb, cc_g, cc_b):
    t = z.shape[0]
    nt = t // TMC

    def body(*refs):
        b_ref = refs[0]
        c3, h3, a3, g3 = refs[1:4], refs[4:7], refs[7:10], refs[10:13]
        scw_ref, ccw_ref, cb_ref, lg_ref, lb_ref = refs[13:18]
        ysc_ref, ycc_ref, u2_ref, e_s = refs[18:22]
        i = pl.program_id(0)
        e_s[...] = _extend(c3, i, nt) * _extend(h3, i, nt)
        cv = jnp.zeros((TMC, D_CONV), F32)
        for k in range(SC_W):
            cv += scw_ref[k:k + 1, :] * e_s[pl.ds(HALO + k - 1, TMC), :]
        ysc_ref[...] = _mx(b_ref[...] * cv)
        e_s[...] = _extend(a3, i, nt) * _sigmoid(_extend(g3, i, nt))
        u2 = jnp.zeros((TMC, D_CONV), F32) + cb_ref[...]
        for k in range(CC_W):
            u2 += ccw_ref[k:k + 1, :] * e_s[pl.ds(HALO + k - 15, TMC), :]
        u2_ref[...] = u2
        xh, _ = _ln_stats(u2)
        n = xh * lg_ref[...] + lb_ref[...]
        ycc_ref[...] = _mx(n * _sigmoid(n))

    tile = pl.BlockSpec((TMC, D_CONV), lambda i: (i, 0))
    vec = pl.BlockSpec((1, D_CONV), lambda i: (0, 0))
    in_specs = ([pl.BlockSpec((TMC, D_CONV), lambda i: (i, 0))] + _halo_specs(t, D_CONV, 1) + _halo_specs(t, D_CONV, 2)
                + _halo_specs(t, D_CONV, 6) + _halo_specs(t, D_CONV, 7)
                + [pl.BlockSpec((SC_W, D_CONV), lambda i: (0, 0)), pl.BlockSpec((CC_W, D_CONV), lambda i: (0, 0)),
                   vec, vec, vec])
    return pl.pallas_call(
        body, name="conv_fwd", grid=(nt,), in_specs=in_specs, out_specs=[tile, tile, tile],
        out_shape=[jax.ShapeDtypeStruct((t, D_CONV), MXU_DTYPE), jax.ShapeDtypeStruct((t, D_CONV), MXU_DTYPE),
                   jax.ShapeDtypeStruct((t, D_CONV), F32)],
        scratch_shapes=[pltpu.VMEM((TMC + 2 * HALO, D_CONV), F32)],
        compiler_params=_cparams("parallel"),
    )(*([z] * 13), sc_w, cc_w, cc_cb, cc_g, cc_b)


ROW_CCW, ROW_CCB, ROW_CCG, ROW_CCBETA, ROW_SCW, CONV_ROWS = 0, 31, 32, 33, 34, 40


def _conv_bwd(z, dycat, u2, sc_w, cc_w, cc_g, cc_b):
    t = z.shape[0]
    nt = t // TMC

    def body(*refs):
        b3, c3, h3, a3, g3 = refs[0:3], refs[3:6], refs[6:9], refs[9:12], refs[12:15]
        dys3, dyc3, u3 = refs[15:18], refs[18:21], refs[21:24]
        scw_ref, ccw_ref, lg_ref, lb_ref = refs[24:28]
        dsc_ref, dcc_ref, sm_ref, e_s, f_s = refs[28:33]
        i = pl.program_id(0)

        @pl.when(i == 0)
        def _():
            sm_ref[...] = jnp.zeros_like(sm_ref)

        cur = pl.ds(HALO, TMC)
        e_s[...] = _extend(c3, i, nt) * _extend(h3, i, nt)
        f_s[...] = _extend(dys3, i, nt) * _extend(b3, i, nt)
        cv = jnp.zeros((TMC, D_CONV), F32)
        dp = jnp.zeros((TMC, D_CONV), F32)
        dcv = f_s[cur, :]
        for k in range(SC_W):
            win = e_s[pl.ds(HALO + k - 1, TMC), :]
            cv += scw_ref[k:k + 1, :] * win
            dp += scw_ref[k:k + 1, :] * f_s[pl.ds(HALO - k + 1, TMC), :]
            sm_ref[ROW_SCW + k:ROW_SCW + k + 1, :] += _colsum(dcv * win)
        dsc_ref[:, 0:D_CONV] = _mx(dys3[1][...] * cv)
        dsc_ref[:, D_CONV:2 * D_CONV] = _mx(dp * h3[1][...])
        dsc_ref[:, 2 * D_CONV:3 * D_CONV] = _mx(dp * c3[1][...])
        xh, rstd = _ln_stats(_extend(u3, i, nt))
        n = xh * lg_ref[...] + lb_ref[...]
        sg = _sigmoid(n)
        dn = _extend(dyc3, i, nt) * (sg * (1.0 + n * (1.0 - sg)))
        f_s[...] = _ln_bwd(dn, xh, rstd, lg_ref[...])
        sm_ref[ROW_CCG:ROW_CCG + 1, :] += _colsum((dn * xh)[HALO:HALO + TMC])
        sm_ref[ROW_CCBETA:ROW_CCBETA + 1, :] += _colsum(dn[HALO:HALO + TMC])
        sig_g = _sigmoid(_extend(g3, i, nt))
        e_s[...] = _extend(a3, i, nt) * sig_g
        du2 = f_s[cur, :]
        sm_ref[ROW_CCB:ROW_CCB + 1, :] += _colsum(du2)
        duu = jnp.zeros((TMC, D_CONV), F32)
        for k in range(CC_W):
            duu += ccw_ref[k:k + 1, :] * f_s[pl.ds(HALO + 15 - k, TMC), :]
            sm_ref[ROW_CCW + k:ROW_CCW + k + 1, :] += _colsum(du2 * e_s[pl.ds(HALO + k - 15, TMC), :])
        sgc = sig_g[HALO:HALO + TMC]
        dcc_ref[:, 0:D_CONV] = _mx(duu * sgc)
        dcc_ref[:, D_CONV:2 * D_CONV] = _mx(duu * a3[1][...] * sgc * (1.0 - sgc))

    vec = pl.BlockSpec((1, D_CONV), lambda i: (0, 0))
    in_specs = []
    for col in (0, 1, 2, 6, 7):
        in_specs += _halo_specs(t, D_CONV, col)
    in_specs += _halo_specs(t, D_CONV, 0) + _halo_specs(t, D_CONV, 3) + _halo_specs(t, D_CONV, 0)
    in_specs += [pl.BlockSpec((SC_W, D_CONV), lambda i: (0, 0)), pl.BlockSpec((CC_W, D_CONV), lambda i: (0, 0)), vec, vec]
    return pl.pallas_call(
        body, name="conv_bwd", grid=(nt,), in_specs=in_specs,
        out_specs=[pl.BlockSpec((TMC, 3 * D_CONV), lambda i: (i, 0)), pl.BlockSpec((TMC, 2 * D_CONV), lambda i: (i, 0)),
                   pl.BlockSpec((CONV_ROWS, D_CONV), lambda i: (0, 0))],
        out_shape=[jax.ShapeDtypeStruct((t, 3 * D_CONV), MXU_DTYPE), jax.ShapeDtypeStruct((t, 2 * D_CONV), MXU_DTYPE),
                   jax.ShapeDtypeStruct((CONV_ROWS, D_CONV), F32)],
        scratch_shapes=[pltpu.VMEM((TMC + 2 * HALO, D_CONV), F32)] * 2,
        compiler_params=_cparams("arbitrary"),
    )(*([z] * 15), *([dycat] * 6), *([u2] * 3), sc_w, cc_w, cc_g, cc_b)


def _lane(shape):
    return lax.broadcasted_iota(jnp.int32, shape, 1)


def _swap_halves(x):
    w = x.shape[1]
    lo = (_lane(x.shape) % HEAD_DIM) < HEAD_DIM // 2
    return jnp.where(lo, pltpu.roll(x, w - HEAD_DIM // 2, 1), pltpu.roll(x, HEAD_DIM // 2, 1))


def _both_halves(x, first):
    sw = pltpu.roll(x, HEAD_DIM, 1)
    lo = _lane(x.shape) < HEAD_DIM
    return jnp.where(lo, x, sw) if first else jnp.where(lo, sw, x)


def _fold_halves(first_src, second_src):
    lo = _lane(first_src.shape) < HEAD_DIM
    return jnp.where(lo, first_src + pltpu.roll(first_src, HEAD_DIM, 1), second_src + pltpu.roll(second_src, HEAD_DIM, 1))


def _attn_prep(z, cos, sin):
    t = z.shape[0]

    def body(qa_ref, qb_ref, k_ref, v_ref, cos_ref, sin_ref, qs_ref, kf_ref, vf_ref):
        cs, sn = cos_ref[...], sin_ref[...]

        def rope(x):
            return x * cs + _swap_halves(x) * sn

        for tt in range(4):
            src = qa_ref if tt < 2 else qb_ref
            q = src[:, (tt % 2) * BLOCK:(tt % 2 + 1) * BLOCK]
            qs_ref[:, tt * BLOCK:(tt + 1) * BLOCK] = _mx(rope(q) * (HEAD_DIM ** -0.5))
        kr = rope(k_ref[...])
        kf_ref[:, 0:BLOCK] = _mx(_both_halves(kr, True))
        kf_ref[:, BLOCK:2 * BLOCK] = _mx(_both_halves(kr, False))
        v = v_ref[...]
        vf_ref[:, 0:BLOCK] = _mx(_both_halves(v, True))
        vf_ref[:, BLOCK:2 * BLOCK] = _mx(_both_halves(v, False))

    def col(width, j):
        return pl.BlockSpec((TM, width), lambda i: (i, j))

    return pl.pallas_call(
        body, name="attn_prep", grid=(t // TM,),
        in_specs=[col(256, 3), col(256, 4), col(128, 10), col(128, 11), col(128, 0), col(128, 0)],
        out_specs=[col(512, 0), col(256, 0), col(256, 0)],
        out_shape=[jax.ShapeDtypeStruct((t, 512), MXU_DTYPE), jax.ShapeDtypeStruct((t, 256), MXU_DTYPE),
                   jax.ShapeDtypeStruct((t, 256), MXU_DTYPE)],
        compiler_params=_cparams("parallel"),
    )(z, z, z, z, cos, sin)


def _attn_prep_bwd(dqs, dkf, dvf, cos, sin):
    t = dqs.shape[0]

    def body(dqs_ref, dkf_ref, dvf_ref, cos_ref, sin_ref, dz_ref):
        cs, sn = cos_ref[...], sin_ref[...]

        def rope_bwd(d):
            return d * cs + _swap_halves(d * sn)

        for tt in range(4):
            d = dqs_ref[:, tt * BLOCK:(tt + 1) * BLOCK] * (HEAD_DIM ** -0.5)
            dz_ref[:, tt * BLOCK:(tt + 1) * BLOCK] = _mx(rope_bwd(d))
        dkr = _fold_halves(dkf_ref[:, 0:BLOCK], dkf_ref[:, BLOCK:2 * BLOCK])
        dz_ref[:, 4 * BLOCK:5 * BLOCK] = _mx(rope_bwd(dkr))
        dz_ref[:, 5 * BLOCK:6 * BLOCK] = _mx(_fold_halves(dvf_ref[:, 0:BLOCK], dvf_ref[:, BLOCK:2 * BLOCK]))

    def col(width):
        return pl.BlockSpec((TM, width), lambda i: (i, 0))

    return pl.pallas_call(
        body, name="attn_prep_bwd", grid=(t // TM,),
        in_specs=[col(512), col(256), col(256), col(128), col(128)],
        out_specs=col(768), out_shape=jax.ShapeDtypeStruct((t, 768), MXU_DTYPE),
        compiler_params=_cparams("parallel"),
    )(dqs, dkf, dvf, cos, sin)


def _nbr_specs(nb, width, col):
    return [pl.BlockSpec((BLOCK, width), lambda n: (jnp.maximum(n - 1, 0), col)),
            pl.BlockSpec((BLOCK, width), lambda n: (n, col)),
            pl.BlockSpec((BLOCK, width), lambda n: (jnp.minimum(n + 1, nb - 1), col))]


def _band_masks(n, nb):
    row = lax.broadcasted_iota(jnp.int32, (BLOCK, BLOCK), 0)
    col = lax.broadcasted_iota(jnp.int32, (BLOCK, BLOCK), 1)
    return (col >= row) & (n > 0), (col <= row) & (n < nb - 1), col < HEAD_DIM


def _tile(ref, j):
    return ref[:, j * BLOCK:(j + 1) * BLOCK]


def _attn_fwd(qs, kf, vf, sink):
    t = qs.shape[0]
    nb = t // BLOCK

    def body(q_ref, kp_ref, kc_ref, kn_ref, vp_ref, vc_ref, vn_ref, sink_ref, o_ref, l_ref):
        n = pl.program_id(0)
        m_prev, m_next, lo = _band_masks(n, nb)
        for tt in range(4):
            kv = tt // 2
            q = _tile(q_ref, tt)
            halves = []
            for s in range(2):
                h = 2 * tt + s
                qm = jnp.where(lo if s == 0 else ~lo, q, jnp.zeros_like(q))
                sp = jnp.where(m_prev, _dot_nt(qm, _tile(kp_ref, kv)), NEG)
                sc = _dot_nt(qm, _tile(kc_ref, kv))
                sn = jnp.where(m_next, _dot_nt(qm, _tile(kn_ref, kv)), NEG)
                sk = sink_ref[h]
                m = jnp.maximum(jnp.max(jnp.maximum(jnp.maximum(sp, sc), sn), axis=-1, keepdims=True), sk)
                pp, pc, pn = jnp.exp(sp - m), jnp.exp(sc - m), jnp.exp(sn - m)
                den = jnp.sum(pp + pc + pn, axis=-1, keepdims=True) + jnp.exp(sk - m)
                o = _dot(_mx(pp), _tile(vp_ref, kv)) + _dot(_mx(pc), _tile(vc_ref, kv)) + _dot(_mx(pn), _tile(vn_ref, kv))
                halves.append(o / den)
                l_ref[:, h * BLOCK:(h + 1) * BLOCK] = jnp.broadcast_to(m + jnp.log(den), (BLOCK, BLOCK))
            o_ref[:, tt * BLOCK:(tt + 1) * BLOCK] = jnp.where(lo, halves[0], halves[1])

    return pl.pallas_call(
        body, name="attn_fwd", grid=(nb,),
        in_specs=[pl.BlockSpec((BLOCK, 512), lambda n: (n, 0))] + _nbr_specs(nb, 256, 0) + _nbr_specs(nb, 256, 0)
        + [pl.BlockSpec(memory_space=pltpu.SMEM)],
        out_specs=[pl.BlockSpec((BLOCK, 512), lambda n: (n, 0)), pl.BlockSpec((BLOCK, 1024), lambda n: (n, 0))],
        out_shape=[jax.ShapeDtypeStruct((t, 512), F32), jax.ShapeDtypeStruct((t, 1024), F32)],
        compiler_params=_cparams("parallel"),
    )(qs, kf, kf, kf, vf, vf, vf, sink)


def _attn_bwd_q(qs, kf, vf, dycat, o, lse, sink):
    t = qs.shape[0]
    nb = t // BLOCK

    def body(q_ref, kp_ref, kc_ref, kn_ref, vp_ref, vc_ref, vn_ref, doa_ref, dob_ref, o_ref, l_ref, sink_ref,
             dq_ref, d_ref, ds_ref):
        n = pl.program_id(0)

        @pl.when(n == 0)
        def _():
            ds_ref[...] = jnp.zeros_like(ds_ref)

        m_prev, m_next, lo = _band_masks(n, nb)
        for tt in range(4):
            kv = tt // 2
            q = _tile(q_ref, tt)
            do = _tile(doa_ref if tt < 2 else dob_ref, tt % 2)
            doo = do * _tile(o_ref, tt)
            halves = []
            for s in range(2):
                h = 2 * tt + s
                msk = lo if s == 0 else ~lo
                qm = jnp.where(msk, q, jnp.zeros_like(q))
                dom = _mx(jnp.where(msk, do, 0.0))
                dh = jnp.sum(jnp.where(msk, doo, 0.0), axis=-1, keepdims=True)
                lh = _tile(l_ref, h)
                acc = jnp.zeros((BLOCK, BLOCK), F32)
                for k_ref, v_ref, valid in ((kp_ref, vp_ref, m_prev), (kc_ref, vc_ref, None), (kn_ref, vn_ref, m_next)):
                    k, v = _tile(k_ref, kv), _tile(v_ref, kv)
                    sc = _dot_nt(qm, k)
                    if valid is not None:
                        sc = jnp.where(valid, sc, NEG)
                    p = jnp.exp(sc - lh)
                    dsc = p * (_dot_nt(dom, v) - dh)
                    acc += _dot(_mx(dsc), k)
                halves.append(acc)
                d_ref[:, h * BLOCK:(h + 1) * BLOCK] = jnp.broadcast_to(dh, (BLOCK, BLOCK))
                ds_ref[h:h + 1, :] -= _colsum(jnp.exp(sink_ref[h] - lh) * dh)
            dq_ref[:, tt * BLOCK:(tt + 1) * BLOCK] = jnp.where(lo, halves[0], halves[1])

    return pl.pallas_call(
        body, name="attn_bwd_q", grid=(nb,),
        in_specs=[pl.BlockSpec((BLOCK, 512), lambda n: (n, 0))] + _nbr_specs(nb, 256, 0) + _nbr_specs(nb, 256, 0)
        + [pl.BlockSpec((BLOCK, 256), lambda n: (n, 1)), pl.BlockSpec((BLOCK, 256), lambda n: (n, 2)),
           pl.BlockSpec((BLOCK, 512), lambda n: (n, 0)), pl.BlockSpec((BLOCK, 1024), lambda n: (n, 0)),
           pl.BlockSpec(memory_space=pltpu.SMEM)],
        out_specs=[pl.BlockSpec((BLOCK, 512), lambda n: (n, 0)), pl.BlockSpec((BLOCK, 1024), lambda n: (n, 0)),
                   pl.BlockSpec((8, BLOCK), lambda n: (0, 0))],
        out_shape=[jax.ShapeDtypeStruct((t, 512), F32), jax.ShapeDtypeStruct((t, 1024), F32),
                   jax.ShapeDtypeStruct((8, BLOCK), F32)],
        compiler_params=_cparams("arbitrary"),
    )(qs, kf, kf, kf, vf, vf, vf, dycat, dycat, o, lse, sink)


def _attn_bwd_kv(qs, kf, vf, dycat, lse, dsum):
    t = qs.shape[0]
    nb = t // BLOCK

    def body(*refs):
        q3, doa3, dob3, l3, d3 = refs[0:3], refs[3:6], refs[6:9], refs[9:12], refs[12:15]
        k_ref, v_ref, dk_ref, dv_ref = refs[15:19]
        j = pl.program_id(0)
        row = lax.broadcasted_iota(jnp.int32, (BLOCK, BLOCK), 0)
        col = lax.broadcasted_iota(jnp.int32, (BLOCK, BLOCK), 1)
        lo = col < HEAD_DIM
        valid = ((col <= row) & (j > 0), None, (col >= row) & (j < nb - 1))
        for kv in range(2):
            k, v = _tile(k_ref, kv), _tile(v_ref, kv)
            dk = jnp.zeros((BLOCK, BLOCK), F32)
            dv = jnp.zeros((BLOCK, BLOCK), F32)
            for tt in (2 * kv, 2 * kv + 1):
                for s in range(2):
                    h = 2 * tt + s
                    msk = lo if s == 0 else ~lo
                    for b in range(3):
                        q = _tile(q3[b], tt)
                        qm = jnp.where(msk, q, jnp.zeros_like(q))
                        do = _tile(doa3[b] if tt < 2 else dob3[b], tt % 2)
                        dom = _mx(jnp.where(msk, do, 0.0))
                        sc = _dot_nt(qm, k)
                        if valid[b] is not None:
                            sc = jnp.where(valid[b], sc, NEG)
                        p = jnp.exp(sc - _tile(l3[b], h))
                        dsc = p * (_dot_nt(dom, v) - _tile(d3[b], h))
                        dv += _dot_tn(_mx(p), dom)
                        dk += _dot_tn(_mx(dsc), qm)
            dk_ref[:, kv * BLOCK:(kv + 1) * BLOCK] = dk
            dv_ref[:, kv * BLOCK:(kv + 1) * BLOCK] = dv

    cur = pl.BlockSpec((BLOCK, 256), lambda n: (n, 0))
    return pl.pallas_call(
        body, name="attn_bwd_kv", grid=(nb,),
        in_specs=_nbr_specs(nb, 512, 0) + _nbr_specs(nb, 256, 1) + _nbr_specs(nb, 256, 2)
        + _nbr_specs(nb, 1024, 0) + _nbr_specs(nb, 1024, 0) + [cur, cur],
        out_specs=[cur, cur], out_shape=[jax.ShapeDtypeStruct((t, 256), F32)] * 2,
        compiler_params=_cparams("parallel"),
    )(*([qs] * 3), *([dycat] * 6), *([lse] * 3), *([dsum] * 3), kf, vf)


def _loss_head(y, target):
    t = y.shape[0]

    def body(y_ref, t_ref, l_ref, dy_ref):
        @pl.when(pl.program_id(0) == 0)
        def _():
            l_ref[...] = jnp.zeros_like(l_ref)

        e = y_ref[...] - t_ref[...]
        dy_ref[...] = e / D_MODEL
        l_ref[...] += 0.5 * jnp.sum(_mean(e * e))

    row = pl.BlockSpec((TM, D_MODEL), lambda i: (i, 0))
    return pl.pallas_call(
        body, name="loss_head", grid=(t // TM,), in_specs=[row, row],
        out_specs=[pl.BlockSpec((8, 128), lambda i: (0, 0)), row],
        out_shape=[jax.ShapeDtypeStruct((8, 128), F32), jax.ShapeDtypeStruct((t, D_MODEL), F32)],
        compiler_params=_cparams("arbitrary"),
    )(y, target)


def _adamw(name, w, g, m, v, rows):
    n, width = w.shape

    def body(w_ref, g_ref, m_ref, v_ref, d_ref, nm_ref, nv_ref):
        g = g_ref[...]
        m = ADAM_B1 * m_ref[...] + (1.0 - ADAM_B1) * g
        v = ADAM_B2 * v_ref[...] + (1.0 - ADAM_B2) * jnp.square(g)
        m_hat = m / (1.0 - ADAM_B1 ** ADAM_STEP)
        v_hat = v / (1.0 - ADAM_B2 ** ADAM_STEP)
        d_ref[...] = -ADAM_LR * (m_hat / (jnp.sqrt(v_hat) + ADAM_EPS) + ADAM_WD * w_ref[...])
        nm_ref[...] = m
        nv_ref[...] = v

    spec = pl.BlockSpec((rows, width), lambda i: (i, 0))
    return pl.pallas_call(
        body, name=name, grid=(n // rows,), in_specs=[spec] * 4, out_specs=[spec] * 3,
        out_shape=[jax.ShapeDtypeStruct((n, width), F32)] * 3, compiler_params=_cparams("parallel"),
    )(w, g, m, v)


def _place():
    x, y, c = lax.axis_index("x"), lax.axis_index("y"), lax.axis_index("c")
    chips = [(1 - x, y), (x, 1 - y), (1 - x, 1 - y)]
    return x, y, c, chips


def _gather_weights(shards):
    na = len(shards)

    def body(*refs):
        src, dst = refs[:na], refs[na:2 * na]
        send, recv, fsend, frecv, lsem = refs[2 * na:]
        x, y, c, chips = _place()
        mine = 2 * x + y
        local = []
        for a in range(na):
            for l in range(2):
                cp = pltpu.make_async_copy(src[a].at[l], dst[a].at[l, mine], lsem.at[2 * a + l])
                cp.start()
                local.append(cp)

        def ici(a, k, shard, to):
            return pltpu.make_async_remote_copy(
                src_ref=src[a].at[c], dst_ref=dst[a].at[c, shard], send_sem=send.at[3 * a + k], recv_sem=recv.at[3 * a + k],
                device_id=to, device_id_type=MESH)

        def d2d(a, k, layer, shard):
            return pltpu.make_async_remote_copy(
                src_ref=dst[a].at[layer, shard], dst_ref=dst[a].at[layer, shard], send_sem=fsend.at[3 * a + k],
                recv_sem=frecv.at[3 * a + k], device_id=(x, y, 1 - c), device_id_type=MESH)

        sends = []
        for a in range(na):
            for k, (cx, cy) in enumerate(chips):
                cp = ici(a, k, mine, (cx, cy, c))
                cp.start()
                sends.append(cp)
        for a in range(na):
            for k, (cx, cy) in enumerate(chips):
                ici(a, k, 2 * cx + cy, (cx, cy, c)).wait_recv()
                cp = d2d(a, k, c, 2 * cx + cy)
                cp.start()
                sends.append(cp)
        for a in range(na):
            for k, (cx, cy) in enumerate(chips):
                d2d(a, k, 1 - c, 2 * cx + cy).wait_recv()
        for cp in sends:
            cp.wait_send()
        for cp in local:
            cp.wait()

    return pl.pallas_call(
        body, name="gather_weights", in_specs=[ANY] * na, out_specs=[ANY] * na,
        out_shape=[jax.ShapeDtypeStruct((2, N_SHARD) + s.shape[1:], s.dtype) for s in shards],
        scratch_shapes=[pltpu.SemaphoreType.DMA((3 * na,))] * 4 + [pltpu.SemaphoreType.DMA((2 * na,))],
    )(*shards)


def _pair_exchange(parts):
    na = len(parts)

    def body(*refs):
        src, dst = refs[:na], refs[na:2 * na]
        send, recv = refs[2 * na:]
        x, y, c, _ = _place()
        cps = []
        for a in range(na):
            cp = pltpu.make_async_remote_copy(
                src_ref=src[a].at[1 - c], dst_ref=dst[a], send_sem=send.at[a], recv_sem=recv.at[a],
                device_id=(x, y, 1 - c), device_id_type=MESH)
            cp.start()
            cps.append(cp)
        for cp in cps:
            cp.wait()

    return pl.pallas_call(
        body, name="pair_exchange", in_specs=[ANY] * na, out_specs=[ANY] * na,
        out_shape=[jax.ShapeDtypeStruct(p.shape[1:], p.dtype) for p in parts],
        scratch_shapes=[pltpu.SemaphoreType.DMA((na,))] * 2,
    )(*parts)


def _chip_exchange(sums):
    na = len(sums)

    def body(*refs):
        src, dst = refs[:na], refs[na:2 * na]
        send, recv = refs[2 * na:]
        x, y, c, chips = _place()
        cps = []
        for a in range(na):
            for k, (cx, cy) in enumerate(chips):
                cp = pltpu.make_async_remote_copy(
                    src_ref=src[a].at[2 * cx + cy], dst_ref=dst[a].at[k], send_sem=send.at[3 * a + k],
                    recv_sem=recv.at[3 * a + k], device_id=(cx, cy, c), device_id_type=MESH)
                cp.start()
                cps.append(cp)
        for cp in cps:
            cp.wait()

    return pl.pallas_call(
        body, name="chip_exchange", in_specs=[ANY] * na, out_specs=[ANY] * na,
        out_shape=[jax.ShapeDtypeStruct((3,) + s.shape[1:], s.dtype) for s in sums],
        scratch_shapes=[pltpu.SemaphoreType.DMA((3 * na,))] * 2,
    )(*sums)


def _pair_share(halves):
    na = len(halves)

    def body(*refs):
        dst = refs[na:2 * na]
        send, recv = refs[2 * na:]
        x, y, c, _ = _place()
        cps = []
        for a in range(na):
            cp = pltpu.make_async_remote_copy(
                src_ref=dst[a].at[c], dst_ref=dst[a].at[c], send_sem=send.at[a], recv_sem=recv.at[a],
                device_id=(x, y, 1 - c), device_id_type=MESH)
            cp.start()
            cps.append(cp)
        for a in range(na):
            cps[a].wait_send()
            pltpu.make_async_remote_copy(
                src_ref=dst[a].at[1 - c], dst_ref=dst[a].at[1 - c], send_sem=send.at[a], recv_sem=recv.at[a],
                device_id=(x, y, 1 - c), device_id_type=MESH).wait_recv()

    return pl.pallas_call(
        body, name="pair_share", in_specs=[ANY] * na, out_specs=[ANY] * na,
        out_shape=[jax.ShapeDtypeStruct(h.shape, h.dtype) for h in halves],
        input_output_aliases={a: a for a in range(na)},
        scratch_shapes=[pltpu.SemaphoreType.DMA((na,))] * 2,
    )(*halves)


def _pair_sum(name, part, got, rows):
    _, _, r, w = part.shape
    c = lax.axis_index("c").astype(jnp.int32).reshape(1)

    def body(c_ref, p_ref, g_ref, o_ref):
        o_ref[...] = _mx(p_ref[...] + g_ref[...])

    spec = pl.BlockSpec((None, rows, w), lambda j, i, c_ref: (j, i, 0))
    return pl.pallas_call(
        body, name=name, out_shape=jax.ShapeDtypeStruct((N_SHARD, r, w), MXU_DTYPE),
        grid_spec=pltpu.PrefetchScalarGridSpec(
            num_scalar_prefetch=1, grid=(N_SHARD, r // rows),
            in_specs=[pl.BlockSpec((None, None, rows, w), lambda j, i, c_ref: (c_ref[0], j, i, 0)), spec],
            out_specs=spec),
        compiler_params=_cparams("parallel", "parallel"),
    )(c, part, got)


def _chip_sum(name, part, got, others, rows):
    _, _, r, w = part.shape
    cj = jnp.stack([lax.axis_index("c"), 2 * lax.axis_index("x") + lax.axis_index("y")]).astype(jnp.int32)

    def body(cj_ref, p_ref, g_ref, o_ref, out_ref):
        acc = p_ref[...] + g_ref[...]
        for k in range(3):
            acc += o_ref[k].astype(F32)
        out_ref[...] = acc

    return pl.pallas_call(
        body, name=name, out_shape=jax.ShapeDtypeStruct((2, r, w), F32),
        grid_spec=pltpu.PrefetchScalarGridSpec(
            num_scalar_prefetch=1, grid=(r // rows,),
            in_specs=[pl.BlockSpec((None, None, rows, w), lambda i, cj: (cj[0], cj[1], i, 0)),
                      pl.BlockSpec((None, rows, w), lambda i, cj: (cj[1], i, 0)),
                      pl.BlockSpec((3, rows, w), lambda i, cj: (0, i, 0))],
            out_specs=pl.BlockSpec((None, rows, w), lambda i, cj: (cj[0], i, 0))),
        compiler_params=_cparams("parallel"),
    )(cj, part, got, others)


SMALL_ROWS = 40


def _sum_small(part):
    def body(p_ref, o_ref, land, send, recv):
        x, y, c, _ = _place()
        me = 4 * x + 2 * y + c
        cps = []
        for r in range(1, 8):
            cp = pltpu.make_async_remote_copy(
                src_ref=p_ref, dst_ref=land.at[r], send_sem=send.at[r], recv_sem=recv.at[r],
                device_id=(x ^ (r >> 2), y ^ ((r >> 1) & 1), c ^ (r & 1)), device_id_type=MESH)
            cp.start()
            cps.append(cp)
        land[0] = p_ref[...]
        for cp in cps:
            cp.wait()
        acc = land[me]
        for e in range(1, 8):
            acc += land[me ^ e]
        o_ref[...] = acc

    return pl.pallas_call(
        body, name="sum_small", in_specs=[pl.BlockSpec(memory_space=pltpu.VMEM)],
        out_specs=pl.BlockSpec(memory_space=pltpu.VMEM), out_shape=jax.ShapeDtypeStruct(part.shape, F32),
        scratch_shapes=[pltpu.VMEM((8,) + part.shape, F32), pltpu.SemaphoreType.DMA((8,)), pltpu.SemaphoreType.DMA((8,))],
    )(part)


BIG = ("ffn1_w_gu", "ffn1_w_down", "w_in", "w_out", "ffn2_w_gu", "ffn2_w_down")
SMALL = ("ln1_g", "ln1_b", "ln2_g", "ln2_b", "ln3_g", "ln3_b", "attn_sink", "cc_conv_b", "cc_ln_g", "cc_ln_b",
         "sc_conv_w", "cc_conv_w")
NAMES = ("ffn1_w_gu", "ffn1_w_down", "ln1_g", "ln1_b", "w_in", "sc_conv_w", "attn_sink", "cc_conv_w", "cc_conv_b",
         "cc_ln_g", "cc_ln_b", "w_out", "ln2_g", "ln2_b", "ffn2_w_gu", "ffn2_w_down", "ln3_g", "ln3_b")
SUM_ROWS = {"ffn1_w_gu": 256, "ffn1_w_down": 352, "w_in": 256, "w_out": 256, "ffn2_w_gu": 256, "ffn2_w_down": 352}


def _rope_tables(t):
    half = HEAD_DIM // 2
    inv_freq = ROPE_THETA ** (-jnp.arange(half, dtype=F32) / half)
    ang = jnp.arange(t).astype(F32)[:, None] * inv_freq[None, :]
    cos, sin = jnp.cos(ang), jnp.sin(ang)
    return jnp.tile(jnp.concatenate([cos, cos], axis=1), (1, 2)), jnp.tile(jnp.concatenate([-sin, sin], axis=1), (1, 2))


def _pack_small(vals):
    flat = jnp.concatenate([vals[n].reshape(-1) for n in SMALL])
    return jnp.pad(flat, (0, SMALL_ROWS * D_MODEL - flat.shape[0])).reshape(SMALL_ROWS, D_MODEL)


def _unpack_small(packed, shapes):
    flat, out, at = packed.reshape(-1), {}, 0
    for n in SMALL:
        size = int(np.prod(shapes[n]))
        out[n] = flat[at:at + size].reshape(shapes[n])
        at += size
    return out


def kernel(x, ffn1_w_gu, ffn1_w_down, ln1_g, ln1_b, w_in, sc_conv_w, attn_sink, cc_conv_w, cc_conv_b, cc_ln_g, cc_ln_b, w_out, ln2_g, ln2_b, ffn2_w_gu, ffn2_w_down, ln3_g, ln3_b, loss_target, m_ffn1_w_gu, m_ffn1_w_down, m_ln1_g, m_ln1_b, m_w_in, m_sc_conv_w, m_attn_sink, m_cc_conv_w, m_cc_conv_b, m_cc_ln_g, m_cc_ln_b, m_w_out, m_ln2_g, m_ln2_b, m_ffn2_w_gu, m_ffn2_w_down, m_ln3_g, m_ln3_b, v_ffn1_w_gu, v_ffn1_w_down, v_ln1_g, v_ln1_b, v_w_in, v_sc_conv_w, v_attn_sink, v_cc_conv_w, v_cc_conv_b, v_cc_ln_g, v_cc_ln_b, v_w_out, v_ln2_g, v_ln2_b, v_ffn2_w_gu, v_ffn2_w_down, v_ln3_g, v_ln3_b):
    given = dict(locals())
    w = {n: given[n] for n in NAMES}
    mom = {n: given["m_" + n] for n in NAMES}
    var = {n: given["v_" + n] for n in NAMES}
    x0 = x[0]
    target = loss_target[0]
    t = x0.shape[0]
    chip = 2 * lax.axis_index("x") + lax.axis_index("y")

    conv_shard = jnp.pad(jnp.concatenate([sc_conv_w, cc_conv_w], axis=1), ((0, 0), (0, 6), (0, 64)))
    full = _gather_weights([_mx(w[n]) for n in BIG] + [conv_shard])
    wgu1, wd1, win, wout, wgu2, wd2, conv_full = full
    wout = wout.reshape(2, D_MODEL, D_MODEL)
    wd1, wd2 = wd1.reshape(2, D_FF, D_MODEL), wd2.reshape(2, D_FF, D_MODEL)
    conv_full = jnp.transpose(conv_full[:, :, :SC_W + CC_W, :64], (0, 2, 1, 3)).reshape(2, SC_W + CC_W, D_CONV)
    sc_full, cc_full = conv_full[:, :SC_W], conv_full[:, SC_W:]
    cos, sin = _rope_tables(t)

    def vec(a, l):
        return a[l][None, :]

    acts = []
    h = x0
    for l in range(2):
        y1, r1, gu1 = _ffn_fwd(h, wgu1, wd1, vec(ln1_g, l), vec(ln1_b, l), l)
        z = _in_proj(y1, win, l)
        ysc, ycc, u2 = _conv_fwd(z, sc_full[l], cc_full[l], vec(cc_conv_b, l), vec(cc_ln_g, l), vec(cc_ln_b, l))
        qs, kf, vf = _attn_prep(z, cos, sin)
        o, lse = _attn_fwd(qs, kf, vf, attn_sink[l])
        ycat = jnp.concatenate([ysc, _mx(o), ycc], axis=1)
        y2, r2 = _out_proj(ycat, y1, wout, vec(ln2_g, l), vec(ln2_b, l), l)
        y3, r3, gu2 = _ffn_fwd(y2, wgu2, wd2, vec(ln3_g, l), vec(ln3_b, l), l)
        acts.append(dict(x=h, y1=y1, r1=r1, gu1=gu1, z=z, u2=u2, qs=qs, kf=kf, vf=vf, o=o, lse=lse, ycat=ycat,
                         y2=y2, r2=r2, gu2=gu2, r3=r3))
        h = y3
    loss_rows, dy = _loss_head(h, target)
    loss = lax.psum(loss_rows[0, 0], ("x", "y", "c"))

    part = {n: None for n in BIG}
    small = [None, None]

    def ffn_grads(which, dy, r, gu, xin, wgu, wd, gamma, l):
        dx, dh, a, do, dgb = _ffn_bwd(dy, r, gu, wgu, wd, gamma, l)
        part[which + "_w_gu"] = _mm_tn(
            which + "_dwgu", xin, dh, pl.BlockSpec((TM, D_MODEL), lambda n, k: (k, 0)),
            pl.BlockSpec((None, TM, FF_CHUNK), lambda n, k: (n // N_CHUNK, k, n % N_CHUNK)),
            pl.BlockSpec((None, None, D_MODEL, FF_CHUNK), lambda n, k: (l, n, 0, 0)),
            (2, N_SHARD, D_MODEL, GU_SHARD), (D_MODEL, FF_CHUNK), (2 * N_CHUNK, t // TM), part[which + "_w_gu"])
        part[which + "_w_down"] = _mm_tn(
            which + "_dwd", a, do, pl.BlockSpec((TM, FF_CHUNK), lambda n, k: (k, n)),
            pl.BlockSpec((TM, D_MODEL), lambda n, k: (k, 0)),
            pl.BlockSpec((None, FF_CHUNK, D_MODEL), lambda n, k: (l, n, 0)),
            (2, D_FF, D_MODEL), (FF_CHUNK, D_MODEL), (N_CHUNK, t // TM), part[which + "_w_down"])
        return dx, dgb

    for l in (1, 0):
        s = acts[l]
        dy, dgb3 = ffn_grads("ffn2", dy, s["r3"], s["gu2"], s["y2"], wgu2, wd2, vec(ln3_g, l), l)
        res, dm, dycat, dgb2 = _out_proj_bwd(dy, s["r2"], wout, vec(ln2_g, l), l)
        part["w_out"] = _mm_tn(
            "dwout", s["ycat"], dm, pl.BlockSpec((TM, OUT_SHARD), lambda n, k: (k, n)),
            pl.BlockSpec((TM, D_MODEL), lambda n, k: (k, 0)),
            pl.BlockSpec((None, None, OUT_SHARD, D_MODEL), lambda n, k: (l, n, 0, 0)),
            (2, N_SHARD, OUT_SHARD, D_MODEL), (OUT_SHARD, D_MODEL), (N_SHARD, t // TM), part["w_out"])
        dz_sc, dz_cc, dconv = _conv_bwd(s["z"], dycat, s["u2"], sc_full[l], cc_full[l], vec(cc_ln_g, l), vec(cc_ln_b, l))
        dqs, dsum, dsink = _attn_bwd_q(s["qs"], s["kf"], s["vf"], dycat, s["o"], s["lse"], attn_sink[l])
        dkf, dvf = _attn_bwd_kv(s["qs"], s["kf"], s["vf"], dycat, s["lse"], dsum)
        dz_att = _attn_prep_bwd(dqs, dkf, dvf, cos, sin)
        dz = jnp.concatenate([dz_sc, dz_att, dz_cc], axis=1)
        part["w_in"] = _mm_tn(
            "dwin", s["y1"], dz, pl.BlockSpec((TM, D_MODEL), lambda n, k: (k, 0)),
            pl.BlockSpec((TM, IN_SHARD), lambda n, k: (k, n)),
            pl.BlockSpec((None, None, D_MODEL, IN_SHARD), lambda n, k: (l, n, 0, 0)),
            (2, N_SHARD, D_MODEL, IN_SHARD), (D_MODEL, IN_SHARD), (N_SHARD, t // TM), part["w_in"])
        dy = _in_proj_bwd(dz, res, win, l)
        dy, dgb1 = ffn_grads("ffn1", dy, s["r1"], s["gu1"], s["x"], wgu1, wd1, vec(ln1_g, l), l)
        small[l] = dict(ln1_g=dgb1[0], ln1_b=dgb1[1], ln2_g=dgb2[0], ln2_b=dgb2[1], ln3_g=dgb3[0], ln3_b=dgb3[1],
                        attn_sink=dsink[:, 0], cc_conv_b=dconv[ROW_CCB], cc_ln_g=dconv[ROW_CCG],
                        cc_ln_b=dconv[ROW_CCBETA], sc_conv_w=dconv[ROW_SCW:ROW_SCW + SC_W],
                        cc_conv_w=dconv[ROW_CCW:ROW_CCW + CC_W])
    grad_x = dy[None]

    for n in ("ffn1_w_down", "ffn2_w_down"):
        part[n] = part[n].reshape(2, N_SHARD, D_FF // N_SHARD, D_MODEL)
    parts = [part[n] for n in BIG]
    got = _pair_exchange(parts)
    sums = [_pair_sum("pair_sum_" + n, p, g, SUM_ROWS[n]) for n, p, g in zip(BIG, parts, got)]
    others = _chip_exchange(sums)
    halves = [_chip_sum("chip_sum_" + n, p, g, o, SUM_ROWS[n]) for n, p, g, o in zip(BIG, parts, got, others)]
    grads = dict(zip(BIG, _pair_share(halves)))
    for n in BIG:
        grads[n] = grads[n].reshape(w[n].shape)

    small_full = {n: jnp.stack([small[0][n], small[1][n]]) for n in SMALL}
    small_sum = _unpack_small(_sum_small(_pack_small(small_full)), {n: small_full[n].shape for n in SMALL})
    for n in SMALL:
        g = small_sum[n]
        if n in ("sc_conv_w", "cc_conv_w"):
            g = lax.dynamic_slice_in_dim(g, chip * 64, 64, axis=2)
        grads[n] = g

    delta, new_m, new_v = {}, {}, {}
    for n in BIG:
        shape = w[n].shape
        two_d = (shape[0] * shape[1], shape[2])
        outs = _adamw("adamw_" + n, w[n].reshape(two_d), grads[n].reshape(two_d), mom[n].reshape(two_d),
                      var[n].reshape(two_d), 128)
        delta[n], new_m[n], new_v[n] = [a.reshape(shape) for a in outs]
    shapes = {n: w[n].shape for n in SMALL}
    outs = _adamw("adamw_small", _pack_small({n: w[n] for n in SMALL}), _pack_small({n: grads[n] for n in SMALL}),
                  _pack_small({n: mom[n] for n in SMALL}), _pack_small({n: var[n] for n in SMALL}), 8)
    for d, packed in zip((delta, new_m, new_v), outs):
        d.update(_unpack_small(packed, shapes))

    return (loss, grad_x, *[grads[n] for n in NAMES], *[delta[n] for n in NAMES], *[new_m[n] for n in NAMES],
            *[new_v[n] for n in NAMES])
```

```python
import functools

import numpy as np
import jax
import jax.numpy as jnp
from jax import lax
from jax.experimental import pallas as pl
from jax.experimental.pallas import tpu as pltpu

F32 = jnp.float32
MXU_DTYPE = jnp.bfloat16

D_MODEL = 1024
D_FF = 2816
N_SHARD = 4
D_IN = 2048
GU_SHARD = 2 * D_FF // N_SHARD
FF_CHUNK = GU_SHARD
N_CHUNK = D_FF // FF_CHUNK
IN_SHARD = D_IN // N_SHARD
OUT_SHARD = D_MODEL // N_SHARD
HEAD_DIM = 64
N_Q_HEADS = 8
BLOCK = 128
SC_W = 3
CC_W = 31
D_CONV = 256
HALO = 16
LN_EPS = 1e-5
ALPHA = (2.0 * 2) ** 0.25
NEG = -1e30
ROPE_THETA = 10000.0
ADAM_LR, ADAM_B1, ADAM_B2, ADAM_EPS, ADAM_WD, ADAM_STEP = 0.001, 0.9, 0.999, 1e-08, 0.01, 10

TM = 512
TMC = 256
VMEM_LIMIT = 56 * 1024 * 1024
MESH = pl.DeviceIdType.MESH
ANY = pl.BlockSpec(memory_space=pl.ANY)


def _cparams(*sem):
    return pltpu.CompilerParams(dimension_semantics=sem, vmem_limit_bytes=VMEM_LIMIT)


def _dot(a, b):
    return jnp.dot(a, b, preferred_element_type=F32)


def _dot_nt(a, b):
    return lax.dot_general(a, b, (((1,), (1,)), ((), ())), preferred_element_type=F32)


def _dot_tn(a, b):
    return lax.dot_general(a, b, (((0,), (0,)), ((), ())), preferred_element_type=F32)


def _mx(a):
    return a.astype(MXU_DTYPE)


def _mean(a):
    return jnp.mean(a, axis=-1, keepdims=True)


def _ln_stats(r):
    xc = r - _mean(r)
    rstd = lax.rsqrt(_mean(xc * xc) + LN_EPS)
    return xc * rstd, rstd


def _ln_bwd(dy, xh, rstd, gamma):
    dxh = dy * gamma
    return rstd * (dxh - _mean(dxh) - xh * _mean(dxh * xh))


def _colsum(a):
    return jnp.sum(a, axis=0, keepdims=True)


def _sigmoid(a):
    return 1.0 / (1.0 + jnp.exp(-a))


def _ffn_fwd(x, wgu, wd, gamma, beta, l):
    t = x.shape[0]
    nc = N_CHUNK

    def body(x_ref, wg_ref, wu_ref, wd_ref, g_ref, b_ref, y_ref, r_ref, gu_ref, xb_s, acc_s):
        c = pl.program_id(1)

        @pl.when(c == 0)
        def _():
            xb_s[...] = _mx(x_ref[...])
            acc_s[...] = jnp.zeros_like(acc_s)

        xb = xb_s[...]
        hg = _dot(xb, wg_ref[...])
        hu = _dot(xb, wu_ref[...])
        gu_ref[0] = _mx(hg)
        gu_ref[1] = _mx(hu)
        a = (hg * _sigmoid(hg)) * hu
        acc_s[...] += _dot(_mx(a), wd_ref[...])

        @pl.when(c == nc - 1)
        def _():
            r = ALPHA * x_ref[...] + 0.5 * acc_s[...]
            xh, _ = _ln_stats(r)
            r_ref[...] = r
            y_ref[...] = xh * g_ref[...] + b_ref[...]

    row = pl.BlockSpec((TM, D_MODEL), lambda i, c: (i, 0))
    vec = pl.BlockSpec((1, D_MODEL), lambda i, c: (0, 0))
    return pl.pallas_call(
        body, name="ffn_fwd", grid=(t // TM, nc),
        in_specs=[row,
                  pl.BlockSpec((None, None, D_MODEL, FF_CHUNK), lambda i, c: (l, c, 0, 0)),
                  pl.BlockSpec((None, None, D_MODEL, FF_CHUNK), lambda i, c: (l, N_CHUNK + c, 0, 0)),
                  pl.BlockSpec((None, FF_CHUNK, D_MODEL), lambda i, c: (l, c, 0)),
                  vec, vec],
        out_specs=[row, row, pl.BlockSpec((2, TM, FF_CHUNK), lambda i, c: (0, i, c))],
        out_shape=[jax.ShapeDtypeStruct((t, D_MODEL), F32), jax.ShapeDtypeStruct((t, D_MODEL), F32),
                   jax.ShapeDtypeStruct((2, t, D_FF), MXU_DTYPE)],
        scratch_shapes=[pltpu.VMEM((TM, D_MODEL), MXU_DTYPE), pltpu.VMEM((TM, D_MODEL), F32)],
        compiler_params=_cparams("parallel", "arbitrary"),
    )(x, wgu, wgu, wd, gamma, beta)


def _ffn_bwd(dy, r, gu, wgu, wd, gamma, l):
    t = dy.shape[0]
    nc = N_CHUNK

    def norm_body(dy_ref, r_ref, g_ref, res_ref, do_ref, dgb_ref):
        @pl.when(pl.program_id(0) == 0)
        def _():
            dgb_ref[...] = jnp.zeros_like(dgb_ref)

        xh, rstd = _ln_stats(r_ref[...])
        dy = dy_ref[...]
        dr = _ln_bwd(dy, xh, rstd, g_ref[...])
        do_ref[...] = _mx(0.5 * dr)
        res_ref[...] = ALPHA * dr
        dgb_ref[0:1, :] += _colsum(dy * xh)
        dgb_ref[1:2, :] += _colsum(dy)

    row1 = pl.BlockSpec((TM, D_MODEL), lambda i: (i, 0))
    res, do, dgb = pl.pallas_call(
        norm_body, name="ffn_bwd_norm", grid=(t // TM,),
        in_specs=[row1, row1, pl.BlockSpec((1, D_MODEL), lambda i: (0, 0))],
        out_specs=[row1, row1, pl.BlockSpec((8, D_MODEL), lambda i: (0, 0))],
        out_shape=[jax.ShapeDtypeStruct((t, D_MODEL), F32), jax.ShapeDtypeStruct((t, D_MODEL), MXU_DTYPE),
                   jax.ShapeDtypeStruct((8, D_MODEL), F32)],
        compiler_params=_cparams("arbitrary"),
    )(dy, r, gamma)

    def hidden_body(do_ref, gu_ref, wd_ref, dh_ref, a_ref):
        da = _dot_nt(do_ref[...], wd_ref[...])
        g = gu_ref[0].astype(F32)
        u = gu_ref[1].astype(F32)
        s = _sigmoid(g)
        sil = g * s
        a_ref[...] = _mx(sil * u)
        dh_ref[0] = _mx(da * u * (s * (1.0 + g * (1.0 - s))))
        dh_ref[1] = _mx(da * sil)

    hid = pl.BlockSpec((2, TM, FF_CHUNK), lambda c, i: (0, i, c))
    dh, a = pl.pallas_call(
        hidden_body, name="ffn_bwd_hidden", grid=(nc, t // TM),
        in_specs=[pl.BlockSpec((TM, D_MODEL), lambda c, i: (i, 0)), hid,
                  pl.BlockSpec((None, FF_CHUNK, D_MODEL), lambda c, i: (l, c, 0))],
        out_specs=[hid, pl.BlockSpec((TM, FF_CHUNK), lambda c, i: (i, c))],
        out_shape=[jax.ShapeDtypeStruct((2, t, D_FF), MXU_DTYPE), jax.ShapeDtypeStruct((t, D_FF), MXU_DTYPE)],
        compiler_params=_cparams("parallel", "parallel"),
    )(do, gu, wd)

    def input_body(res_ref, dh_ref, w_ref, dx_ref):
        acc = res_ref[...]
        for j in range(N_SHARD):
            part = dh_ref[j // N_CHUNK][:, (j % N_CHUNK) * FF_CHUNK:(j % N_CHUNK + 1) * FF_CHUNK]
            acc += _dot_nt(part, w_ref[j])
        dx_ref[...] = acc

    dx = pl.pallas_call(
        input_body, name="ffn_bwd_input", grid=(t // TM,),
        in_specs=[row1, pl.BlockSpec((2, TM, D_FF), lambda i: (0, i, 0)),
                  pl.BlockSpec((None, N_SHARD, D_MODEL, GU_SHARD), lambda i: (l, 0, 0, 0))],
        out_specs=row1,
        out_shape=jax.ShapeDtypeStruct((t, D_MODEL), F32),
        compiler_params=_cparams("parallel"),
    )(res, dh, wgu)
    return dx, dh, a, do, dgb


def _mm_tn(name, a, b, a_spec, b_spec, out_spec, out_shape, acc_shape, grid, prev=None):
    nk = grid[-1]

    def body(*refs):
        a_ref, b_ref = refs[0], refs[1]
        o_ref, acc = refs[-2], refs[-1]
        k = pl.program_id(len(grid) - 1)

        @pl.when(k == 0)
        def _():
            acc[...] = jnp.zeros_like(acc)

        acc[...] += _dot_tn(_mx(a_ref[...]), _mx(b_ref[...]))

        @pl.when(k == nk - 1)
        def _():
            o_ref[...] = acc[...]

    ins, specs, alias = [a, b], [a_spec, b_spec], {}
    if prev is not None:
        ins.append(prev)
        specs.append(ANY)
        alias = {2: 0}
    sem = ("parallel",) * (len(grid) - 1) + ("arbitrary",)
    return pl.pallas_call(
        body, name=name, grid=grid, in_specs=specs, out_specs=out_spec,
        out_shape=jax.ShapeDtypeStruct(out_shape, F32), input_output_aliases=alias,
        scratch_shapes=[pltpu.VMEM(acc_shape, F32)], compiler_params=_cparams(*sem),
    )(*ins)


def _in_proj(x, w_in, l):
    t = x.shape[0]

    def body(x_ref, w_ref, z_ref):
        z_ref[...] = _dot(_mx(x_ref[...]), w_ref[...])

    return pl.pallas_call(
        body, name="in_proj", grid=(t // TM, N_SHARD),
        in_specs=[pl.BlockSpec((TM, D_MODEL), lambda i, j: (i, 0)),
                  pl.BlockSpec((None, None, D_MODEL, IN_SHARD), lambda i, j: (l, j, 0, 0))],
        out_specs=pl.BlockSpec((TM, IN_SHARD), lambda i, j: (i, j)),
        out_shape=jax.ShapeDtypeStruct((t, D_IN), F32),
        compiler_params=_cparams("parallel", "arbitrary"),
    )(x, w_in)


def _in_proj_bwd(dz, dx_res, w_in, l):
    t = dz.shape[0]

    def body(dz_ref, res_ref, w_ref, dx_ref, acc):
        j = pl.program_id(1)

        @pl.when(j == 0)
        def _():
            acc[...] = res_ref[...]

        acc[...] += _dot_nt(dz_ref[...], w_ref[...])

        @pl.when(j == N_SHARD - 1)
        def _():
            dx_ref[...] = acc[...]

    row = pl.BlockSpec((TM, D_MODEL), lambda i, j: (i, 0))
    return pl.pallas_call(
        body, name="in_proj_bwd", grid=(t // TM, N_SHARD),
        in_specs=[pl.BlockSpec((TM, IN_SHARD), lambda i, j: (i, j)), row,
                  pl.BlockSpec((None, None, D_MODEL, IN_SHARD), lambda i, j: (l, j, 0, 0))],
        out_specs=row, out_shape=jax.ShapeDtypeStruct((t, D_MODEL), F32),
        scratch_shapes=[pltpu.VMEM((TM, D_MODEL), F32)],
        compiler_params=_cparams("parallel", "arbitrary"),
    )(dz, dx_res, w_in)


def _out_proj(ycat, x, w_out, gamma, beta, l):
    t = x.shape[0]

    def body(yc_ref, x_ref, w_ref, g_ref, b_ref, y_ref, r_ref):
        r = ALPHA * x_ref[...] + _dot(yc_ref[...], w_ref[...])
        xh, _ = _ln_stats(r)
        r_ref[...] = r
        y_ref[...] = xh * g_ref[...] + b_ref[...]

    row = pl.BlockSpec((TM, D_MODEL), lambda i: (i, 0))
    vec = pl.BlockSpec((1, D_MODEL), lambda i: (0, 0))
    return pl.pallas_call(
        body, name="out_proj", grid=(t // TM,),
        in_specs=[row, row, pl.BlockSpec((None, D_MODEL, D_MODEL), lambda i: (l, 0, 0)), vec, vec],
        out_specs=[row, row],
        out_shape=[jax.ShapeDtypeStruct((t, D_MODEL), F32)] * 2,
        compiler_params=_cparams("parallel"),
    )(ycat, x, w_out, gamma, beta)


def _out_proj_bwd(dy, r, w_out, gamma, l):
    t = dy.shape[0]

    def body(dy_ref, r_ref, w_ref, g_ref, res_ref, dm_ref, dyc_ref, dgb_ref):
        @pl.when(pl.program_id(0) == 0)
        def _():
            dgb_ref[...] = jnp.zeros_like(dgb_ref)

        xh, rstd = _ln_stats(r_ref[...])
        dy = dy_ref[...]
        dr = _ln_bwd(dy, xh, rstd, g_ref[...])
        res_ref[...] = ALPHA * dr
        dm = _mx(dr)
        dm_ref[...] = dm
        dyc_ref[...] = _dot_nt(dm, w_ref[...])
        dgb_ref[0:1, :] += _colsum(dy * xh)
        dgb_ref[1:2, :] += _colsum(dy)

    row = pl.BlockSpec((TM, D_MODEL), lambda i: (i, 0))
    return pl.pallas_call(
        body, name="out_proj_bwd", grid=(t // TM,),
        in_specs=[row, row, pl.BlockSpec((None, D_MODEL, D_MODEL), lambda i: (l, 0, 0)),
                  pl.BlockSpec((1, D_MODEL), lambda i: (0, 0))],
        out_specs=[row, row, row, pl.BlockSpec((8, D_MODEL), lambda i: (0, 0))],
        out_shape=[jax.ShapeDtypeStruct((t, D_MODEL), F32), jax.ShapeDtypeStruct((t, D_MODEL), MXU_DTYPE),
                   jax.ShapeDtypeStruct((t, D_MODEL), F32), jax.ShapeDtypeStruct((8, D_MODEL), F32)],
        compiler_params=_cparams("arbitrary"),
    )(dy, r, w_out, gamma)


def _halo_specs(t, width, col):
    per = TMC // HALO
    last = t // HALO - 1
    return [pl.BlockSpec((HALO, width), lambda i: (jnp.maximum(i * per - 1, 0), col)),
            pl.BlockSpec((TMC, width), lambda i: (i, col)),
            pl.BlockSpec((HALO, width), lambda i: (jnp.minimum((i + 1) * per, last), col))]


def _extend(refs, i, nt):
    p_ref, c_ref, n_ref = refs
    p = jnp.where(i > 0, p_ref[...].astype(F32), 0.0)
    n = jnp.where(i < nt - 1, n_ref[...].astype(F32), 0.0)
    return jnp.concatenate([p, c_ref[...].astype(F32), n], axis=0)


def _conv_fwd(z, sc_w, cc_w, cc_cb, cc_g, cc_b):
    t = z.shape[0]
    nt = t // TMC

    def body(*refs):
        b_ref = refs[0]
        c3, h3, a3, g3 = refs[1:4], refs[4:7], refs[7:10], refs[10:13]
        scw_ref, ccw_ref, cb_ref, lg_ref, lb_ref = refs[13:18]
        ysc_ref, ycc_ref, u2_ref, e_s = refs[18:22]
        i = pl.program_id(0)
        e_s[...] = _extend(c3, i, nt) * _extend(h3, i, nt)
        cv = jnp.zeros((TMC, D_CONV), F32)
        for k in range(SC_W):
            cv += scw_ref[k:k + 1, :] * e_s[pl.ds(HALO + k - 1, TMC), :]
        ysc_ref[...] = _mx(b_ref[...] * cv)
        e_s[...] = _extend(a3, i, nt) * _sigmoid(_extend(g3, i, nt))
        u2 = jnp.zeros((TMC, D_CONV), F32) + cb_ref[...]
        for k in range(CC_W):
            u2 += ccw_ref[k:k + 1, :] * e_s[pl.ds(HALO + k - 15, TMC), :]
        u2_ref[...] = u2
        xh, _ = _ln_stats(u2)
        n = xh * lg_ref[...] + lb_ref[...]
        ycc_ref[...] = _mx(n * _sigmoid(n))

    tile = pl.BlockSpec((TMC, D_CONV), lambda i: (i, 0))
    vec = pl.BlockSpec((1, D_CONV), lambda i: (0, 0))
    in_specs = ([pl.BlockSpec((TMC, D_CONV), lambda i: (i, 0))] + _halo_specs(t, D_CONV, 1) + _halo_specs(t, D_CONV, 2)
                + _halo_specs(t, D_CONV, 6) + _halo_specs(t, D_CONV, 7)
                + [pl.BlockSpec((SC_W, D_CONV), lambda i: (0, 0)), pl.BlockSpec((CC_W, D_CONV), lambda i: (0, 0)),
                   vec, vec, vec])
    return pl.pallas_call(
        body, name="conv_fwd", grid=(nt,), in_specs=in_specs, out_specs=[tile, tile, tile],
        out_shape=[jax.ShapeDtypeStruct((t, D_CONV), MXU_DTYPE), jax.ShapeDtypeStruct((t, D_CONV), MXU_DTYPE),
                   jax.ShapeDtypeStruct((t, D_CONV), F32)],
        scratch_shapes=[pltpu.VMEM((TMC + 2 * HALO, D_CONV), F32)],
        compiler_params=_cparams("parallel"),
    )(*([z] * 13), sc_w, cc_w, cc_cb, cc_g, cc_b)


ROW_CCW, ROW_CCB, ROW_CCG, ROW_CCBETA, ROW_SCW, CONV_ROWS = 0, 31, 32, 33, 34, 40


def _conv_bwd(z, dycat, u2, sc_w, cc_w, cc_g, cc_b):
    t = z.shape[0]
    nt = t // TMC

    def body(*refs):
        b3, c3, h3, a3, g3 = refs[0:3], refs[3:6], refs[6:9], refs[9:12], refs[12:15]
        dys3, dyc3, u3 = refs[15:18], refs[18:21], refs[21:24]
        scw_ref, ccw_ref, lg_ref, lb_ref = refs[24:28]
        dsc_ref, dcc_ref, sm_ref, e_s, f_s = refs[28:33]
        i = pl.program_id(0)

        @pl.when(i == 0)
        def _():
            sm_ref[...] = jnp.zeros_like(sm_ref)

        cur = pl.ds(HALO, TMC)
        e_s[...] = _extend(c3, i, nt) * _extend(h3, i, nt)
        f_s[...] = _extend(dys3, i, nt) * _extend(b3, i, nt)
        cv = jnp.zeros((TMC, D_CONV), F32)
        dp = jnp.zeros((TMC, D_CONV), F32)
        dcv = f_s[cur, :]
        for k in range(SC_W):
            win = e_s[pl.ds(HALO + k - 1, TMC), :]
            cv += scw_ref[k:k + 1, :] * win
            dp += scw_ref[k:k + 1, :] * f_s[pl.ds(HALO - k + 1, TMC), :]
            sm_ref[ROW_SCW + k:ROW_SCW + k + 1, :] += _colsum(dcv * win)
        dsc_ref[:, 0:D_CONV] = _mx(dys3[1][...] * cv)
        dsc_ref[:, D_CONV:2 * D_CONV] = _mx(dp * h3[1][...])
        dsc_ref[:, 2 * D_CONV:3 * D_CONV] = _mx(dp * c3[1][...])
        xh, rstd = _ln_stats(_extend(u3, i, nt))
        n = xh * lg_ref[...] + lb_ref[...]
        sg = _sigmoid(n)
        dn = _extend(dyc3, i, nt) * (sg * (1.0 + n * (1.0 - sg)))
        f_s[...] = _ln_bwd(dn, xh, rstd, lg_ref[...])
        sm_ref[ROW_CCG:ROW_CCG + 1, :] += _colsum((dn * xh)[HALO:HALO + TMC])
        sm_ref[ROW_CCBETA:ROW_CCBETA + 1, :] += _colsum(dn[HALO:HALO + TMC])
        sig_g = _sigmoid(_extend(g3, i, nt))
        e_s[...] = _extend(a3, i, nt) * sig_g
        du2 = f_s[cur, :]
        sm_ref[ROW_CCB:ROW_CCB + 1, :] += _colsum(du2)
        duu = jnp.zeros((TMC, D_CONV), F32)
        for k in range(CC_W):
            duu += ccw_ref[k:k + 1, :] * f_s[pl.ds(HALO + 15 - k, TMC), :]
            sm_ref[ROW_CCW + k:ROW_CCW + k + 1, :] += _colsum(du2 * e_s[pl.ds(HALO + k - 15, TMC), :])
        sgc = sig_g[HALO:HALO + TMC]
        dcc_ref[:, 0:D_CONV] = _mx(duu * sgc)
        dcc_ref[:, D_CONV:2 * D_CONV] = _mx(duu * a3[1][...] * sgc * (1.0 - sgc))

    vec = pl.BlockSpec((1, D_CONV), lambda i: (0, 0))
    in_specs = []
    for col in (0, 1, 2, 6, 7):
        in_specs += _halo_specs(t, D_CONV, col)
    in_specs += _halo_specs(t, D_CONV, 0) + _halo_specs(t, D_CONV, 3) + _halo_specs(t, D_CONV, 0)
    in_specs += [pl.BlockSpec((SC_W, D_CONV), lambda i: (0, 0)), pl.BlockSpec((CC_W, D_CONV), lambda i: (0, 0)), vec, vec]
    return pl.pallas_call(
        body, name="conv_bwd", grid=(nt,), in_specs=in_specs,
        out_specs=[pl.BlockSpec((TMC, 3 * D_CONV), lambda i: (i, 0)), pl.BlockSpec((TMC, 2 * D_CONV), lambda i: (i, 0)),
                   pl.BlockSpec((CONV_ROWS, D_CONV), lambda i: (0, 0))],
        out_shape=[jax.ShapeDtypeStruct((t, 3 * D_CONV), MXU_DTYPE), jax.ShapeDtypeStruct((t, 2 * D_CONV), MXU_DTYPE),
                   jax.ShapeDtypeStruct((CONV_ROWS, D_CONV), F32)],
        scratch_shapes=[pltpu.VMEM((TMC + 2 * HALO, D_CONV), F32)] * 2,
        compiler_params=_cparams("arbitrary"),
    )(*([z] * 15), *([dycat] * 6), *([u2] * 3), sc_w, cc_w, cc_g, cc_b)


def _lane(shape):
    return lax.broadcasted_iota(jnp.int32, shape, 1)


def _swap_halves(x):
    w = x.shape[1]
    lo = (_lane(x.shape) % HEAD_DIM) < HEAD_DIM // 2
    return jnp.where(lo, pltpu.roll(x, w - HEAD_DIM // 2, 1), pltpu.roll(x, HEAD_DIM // 2, 1))


def _half(shape, g):
    lane = _lane(shape)
    return lane < HEAD_DIM if g == 0 else lane >= HEAD_DIM


GROUP_ROWS = 4 * BLOCK


def _stack_heads(tiles, out_ref, nblk):
    for tt in range(4):
        g = tt // 2
        for slot in range(2):
            s = 2 * (tt % 2) + slot
            piece = tiles[tt] if slot == g else pltpu.roll(tiles[tt], HEAD_DIM, 1)
            piece = jnp.where(_half(piece.shape, g), piece, 0.0).astype(out_ref.dtype)
            for b in range(nblk):
                at = GROUP_ROWS * b + BLOCK * s
                out_ref[g, at:at + BLOCK, :] = piece[BLOCK * b:BLOCK * (b + 1)]


def _unstack_heads(ref, nblk):
    tiles = []
    for tt in range(4):
        g = tt // 2
        tile = None
        for slot in range(2):
            s = 2 * (tt % 2) + slot
            rows = [ref[g, GROUP_ROWS * b + BLOCK * s:GROUP_ROWS * b + BLOCK * (s + 1), :] for b in range(nblk)]
            piece = rows[0] if nblk == 1 else jnp.concatenate(rows, axis=0)
            if slot != g:
                piece = pltpu.roll(piece, HEAD_DIM, 1)
            tile = piece if tile is None else tile + piece
        tiles.append(tile)
    return tiles


def _attn_prep(z, cos, sin):
    t = z.shape[0]
    nblk = TM // BLOCK

    def body(qa_ref, qb_ref, k_ref, v_ref, cos_ref, sin_ref, qst_ref, kr_ref, vb_ref):
        cs, sn = cos_ref[...], sin_ref[...]

        def rope(x):
            return x * cs + _swap_halves(x) * sn

        tiles = []
        for tt in range(4):
            src = qa_ref if tt < 2 else qb_ref
            tiles.append(rope(src[:, (tt % 2) * BLOCK:(tt % 2 + 1) * BLOCK]) * (HEAD_DIM ** -0.5))
        _stack_heads(tiles, qst_ref, nblk)
        kr_ref[...] = _mx(rope(k_ref[...]))
        vb_ref[...] = _mx(v_ref[...])

    def col(width, j):
        return pl.BlockSpec((TM, width), lambda i: (i, j))

    return pl.pallas_call(
        body, name="attn_prep", grid=(t // TM,),
        in_specs=[col(256, 3), col(256, 4), col(128, 10), col(128, 11), col(128, 0), col(128, 0)],
        out_specs=[pl.BlockSpec((2, 4 * TM, BLOCK), lambda i: (0, i, 0)), col(128, 0), col(128, 0)],
        out_shape=[jax.ShapeDtypeStruct((2, 4 * t, BLOCK), MXU_DTYPE), jax.ShapeDtypeStruct((t, BLOCK), MXU_DTYPE),
                   jax.ShapeDtypeStruct((t, BLOCK), MXU_DTYPE)],
        compiler_params=_cparams("parallel"),
    )(z, z, z, z, cos, sin)


def _attn_dprep(dycat, ost, lst):
    t = dycat.shape[0]
    nblk = TM // BLOCK

    def body(da_ref, db_ref, o_ref, l_ref, dost_ref, ld_ref, st_s):
        tiles = []
        for tt in range(4):
            src = da_ref if tt < 2 else db_ref
            tiles.append(src[:, (tt % 2) * BLOCK:(tt % 2 + 1) * BLOCK])
        _stack_heads(tiles, st_s, nblk)
        for g in range(2):
            do = st_s[g]
            dost_ref[g] = _mx(do)
            dsum = jnp.sum(do * o_ref[g], axis=-1, keepdims=True)
            ld_ref[g] = jnp.where(_lane(do.shape) < HEAD_DIM, l_ref[g], dsum)

    stacked = pl.BlockSpec((2, 4 * TM, BLOCK), lambda i: (0, i, 0))
    return pl.pallas_call(
        body, name="attn_dprep", grid=(t // TM,),
        in_specs=[pl.BlockSpec((TM, 256), lambda i: (i, 1)), pl.BlockSpec((TM, 256), lambda i: (i, 2)), stacked, stacked],
        out_specs=[stacked, stacked],
        out_shape=[jax.ShapeDtypeStruct((2, 4 * t, BLOCK), MXU_DTYPE), jax.ShapeDtypeStruct((2, 4 * t, BLOCK), F32)],
        scratch_shapes=[pltpu.VMEM((2, 4 * TM, BLOCK), F32)],
        compiler_params=_cparams("parallel"),
    )(dycat, dycat, ost, lst)


def _attn_prep_bwd(dqst, dk, dv, cos, sin):
    t = dk.shape[0]
    nblk = TM // BLOCK

    def body(dq_ref, dk_ref, dv_ref, cos_ref, sin_ref, dz_ref):
        cs, sn = cos_ref[...], sin_ref[...]

        def rope_bwd(d):
            return d * cs + _swap_halves(d * sn)

        for tt, tile in enumerate(_unstack_heads(dq_ref, nblk)):
            dz_ref[:, tt * BLOCK:(tt + 1) * BLOCK] = _mx(rope_bwd(tile * (HEAD_DIM ** -0.5)))
        dz_ref[:, 4 * BLOCK:5 * BLOCK] = _mx(rope_bwd(dk_ref[...]))
        dz_ref[:, 5 * BLOCK:6 * BLOCK] = _mx(dv_ref[...])

    def col(width):
        return pl.BlockSpec((TM, width), lambda i: (i, 0))

    return pl.pallas_call(
        body, name="attn_prep_bwd", grid=(t // TM,),
        in_specs=[pl.BlockSpec((2, 4 * TM, BLOCK), lambda i: (0, i, 0)), col(128), col(128), col(128), col(128)],
        out_specs=col(768), out_shape=jax.ShapeDtypeStruct((t, 768), MXU_DTYPE),
        compiler_params=_cparams("parallel"),
    )(dqst, dk, dv, cos, sin)


def _nbr_specs(nb, width, col):
    return [pl.BlockSpec((BLOCK, width), lambda n: (jnp.maximum(n - 1, 0), col)),
            pl.BlockSpec((BLOCK, width), lambda n: (n, col)),
            pl.BlockSpec((BLOCK, width), lambda n: (jnp.minimum(n + 1, nb - 1), col))]


def _stacked_specs(nb):
    return [pl.BlockSpec((2, GROUP_ROWS, BLOCK), lambda n: (0, jnp.maximum(n - 1, 0), 0)),
            pl.BlockSpec((2, GROUP_ROWS, BLOCK), lambda n: (0, n, 0)),
            pl.BlockSpec((2, GROUP_ROWS, BLOCK), lambda n: (0, jnp.minimum(n + 1, nb - 1), 0))]


def _query_index():
    row = lax.broadcasted_iota(jnp.int32, (GROUP_ROWS, BLOCK), 0)
    return row & (BLOCK - 1), lax.broadcasted_iota(jnp.int32, (GROUP_ROWS, BLOCK), 1)


def _sink_column(sink_ref, g):
    band = lax.broadcasted_iota(jnp.int32, (GROUP_ROWS, 1), 0) // BLOCK
    col = jnp.zeros((GROUP_ROWS, 1), F32) + sink_ref[4 * g]
    for s in range(1, 4):
        col = jnp.where(band == s, sink_ref[4 * g + s], col)
    return col


def _attn_fwd(qst, kr, vb, sink):
    t = kr.shape[0]
    nb = t // BLOCK

    def body(q_ref, kp_ref, kc_ref, kn_ref, vp_ref, vc_ref, vn_ref, sink_ref, o_ref, ost_ref, lst_ref):
        n = pl.program_id(0)
        qi, kj = _query_index()
        m_prev, m_next = (kj >= qi) & (n > 0), (kj <= qi) & (n < nb - 1)
        nat = [None] * 4
        for g in range(2):
            q = q_ref[g]
            sp = jnp.where(m_prev, _dot_nt(q, kp_ref[...]), NEG)
            sc = _dot_nt(q, kc_ref[...])
            sn = jnp.where(m_next, _dot_nt(q, kn_ref[...]), NEG)
            sk = _sink_column(sink_ref, g)
            m = jnp.maximum(jnp.max(jnp.maximum(jnp.maximum(sp, sc), sn), axis=-1, keepdims=True), sk)
            pp, pc, pn = jnp.exp(sp - m), jnp.exp(sc - m), jnp.exp(sn - m)
            den = jnp.sum(pp + pc + pn, axis=-1, keepdims=True) + jnp.exp(sk - m)
            o = (_dot(_mx(pp), vp_ref[...]) + _dot(_mx(pc), vc_ref[...]) + _dot(_mx(pn), vn_ref[...])) / den
            o = jnp.where(_half(o.shape, g), o, 0.0)
            ost_ref[g] = o
            lst_ref[g] = jnp.broadcast_to(m + jnp.log(den), (GROUP_ROWS, BLOCK))
            for s in range(4):
                tt, slot = 2 * g + s // 2, s % 2
                piece = o[BLOCK * s:BLOCK * (s + 1)]
                if slot != g:
                    piece = pltpu.roll(piece, HEAD_DIM, 1)
                nat[tt] = piece if nat[tt] is None else nat[tt] + piece
        for tt in range(4):
            o_ref[:, tt * BLOCK:(tt + 1) * BLOCK] = _mx(nat[tt])

    stacked = pl.BlockSpec((2, GROUP_ROWS, BLOCK), lambda n: (0, n, 0))
    return pl.pallas_call(
        body, name="attn_fwd", grid=(nb,),
        in_specs=[stacked] + _nbr_specs(nb, BLOCK, 0) + _nbr_specs(nb, BLOCK, 0) + [pl.BlockSpec(memory_space=pltpu.SMEM)],
        out_specs=[pl.BlockSpec((BLOCK, 512), lambda n: (n, 0)), stacked, stacked],
        out_shape=[jax.ShapeDtypeStruct((t, 512), MXU_DTYPE), jax.ShapeDtypeStruct((2, 4 * t, BLOCK), F32),
                   jax.ShapeDtypeStruct((2, 4 * t, BLOCK), F32)],
        compiler_params=_cparams("parallel"),
    )(qst, kr, kr, kr, vb, vb, vb, sink)


def _lse_and_dsum(ld):
    return ld[:, 0:1], pltpu.roll(ld, HEAD_DIM, 1)[:, 0:1]


def _attn_bwd_q(qst, kr, vb, dost, ld, sink):
    t = kr.shape[0]
    nb = t // BLOCK

    def body(q_ref, kp_ref, kc_ref, kn_ref, vp_ref, vc_ref, vn_ref, do_ref, ld_ref, sink_ref, dq_ref, ds_ref):
        n = pl.program_id(0)

        @pl.when(n == 0)
        def _():
            ds_ref[...] = jnp.zeros_like(ds_ref)

        qi, kj = _query_index()
        m_prev, m_next = (kj >= qi) & (n > 0), (kj <= qi) & (n < nb - 1)
        for g in range(2):
            q, do = q_ref[g], do_ref[g]
            lse, dsum = _lse_and_dsum(ld_ref[g])
            acc = jnp.zeros((GROUP_ROWS, BLOCK), F32)
            for k_ref, v_ref, valid in ((kp_ref, vp_ref, m_prev), (kc_ref, vc_ref, None), (kn_ref, vn_ref, m_next)):
                sc = _dot_nt(q, k_ref[...])
                if valid is not None:
                    sc = jnp.where(valid, sc, NEG)
                p = jnp.exp(sc - lse)
                dsc = p * (_dot_nt(do, v_ref[...]) - dsum)
                acc += _dot(_mx(dsc), k_ref[...])
            dq_ref[g] = jnp.where(_half(acc.shape, g), acc, 0.0)
            dsk = jnp.exp(_sink_column(sink_ref, g) - lse) * dsum
            for s in range(4):
                h = 4 * g + s
                ds_ref[h:h + 1, :] -= jnp.sum(dsk[BLOCK * s:BLOCK * (s + 1)], axis=0, keepdims=True)

    stacked = pl.BlockSpec((2, GROUP_ROWS, BLOCK), lambda n: (0, n, 0))
    return pl.pallas_call(
        body, name="attn_bwd_q", grid=(nb,),
        in_specs=[stacked] + _nbr_specs(nb, BLOCK, 0) + _nbr_specs(nb, BLOCK, 0)
        + [stacked, stacked, pl.BlockSpec(memory_space=pltpu.SMEM)],
        out_specs=[stacked, pl.BlockSpec((8, BLOCK), lambda n: (0, 0))],
        out_shape=[jax.ShapeDtypeStruct((2, 4 * t, BLOCK), F32), jax.ShapeDtypeStruct((8, BLOCK), F32)],
        compiler_params=_cparams("arbitrary"),
    )(qst, kr, kr, kr, vb, vb, vb, dost, ld, sink)


def _attn_bwd_kv(qst, kr, vb, dost, ld):
    t = kr.shape[0]
    nb = t // BLOCK

    def body(*refs):
        q3, do3, ld3 = refs[0:3], refs[3:6], refs[6:9]
        k_ref, v_ref, dk_ref, dv_ref = refs[9:13]
        j = pl.program_id(0)
        qi, kj = _query_index()
        valid = ((kj <= qi) & (j > 0), None, (kj >= qi) & (j < nb - 1))
        k, v = k_ref[...], v_ref[...]
        dk = jnp.zeros((BLOCK, BLOCK), F32)
        dv = jnp.zeros((BLOCK, BLOCK), F32)
        for g in range(2):
            for b in range(3):
                q, do = q3[b][g], do3[b][g]
                lse, dsum = _lse_and_dsum(ld3[b][g])
                sc = _dot_nt(q, k)
                if valid[b] is not None:
                    sc = jnp.where(valid[b], sc, NEG)
                p = jnp.exp(sc - lse)
                dsc = p * (_dot_nt(do, v) - dsum)
                dv += _dot_tn(_mx(p), do)
                dk += _dot_tn(_mx(dsc), q)
        dk_ref[...] = dk
        dv_ref[...] = dv

    cur = pl.BlockSpec((BLOCK, BLOCK), lambda n: (n, 0))
    return pl.pallas_call(
        body, name="attn_bwd_kv", grid=(nb,),
        in_specs=_stacked_specs(nb) * 3 + [cur, cur],
        out_specs=[cur, cur], out_shape=[jax.ShapeDtypeStruct((t, BLOCK), F32)] * 2,
        compiler_params=_cparams("parallel"),
    )(*([qst] * 3), *([dost] * 3), *([ld] * 3), kr, vb)


def _loss_head(y, target):
    t = y.shape[0]

    def body(y_ref, t_ref, l_ref, dy_ref):
        @pl.when(pl.program_id(0) == 0)
        def _():
            l_ref[...] = jnp.zeros_like(l_ref)

        e = y_ref[...] - t_ref[...]
        dy_ref[...] = e / D_MODEL
        l_ref[...] += 0.5 * jnp.sum(_mean(e * e))

    row = pl.BlockSpec((TM, D_MODEL), lambda i: (i, 0))
    return pl.pallas_call(
        body, name="loss_head", grid=(t // TM,), in_specs=[row, row],
        out_specs=[pl.BlockSpec((8, 128), lambda i: (0, 0)), row],
        out_shape=[jax.ShapeDtypeStruct((8, 128), F32), jax.ShapeDtypeStruct((t, D_MODEL), F32)],
        compiler_params=_cparams("arbitrary"),
    )(y, target)


def _adamw(name, w, g, m, v, rows):
    n, width = w.shape

    def body(w_ref, g_ref, m_ref, v_ref, d_ref, nm_ref, nv_ref):
        g = g_ref[...]
        m = ADAM_B1 * m_ref[...] + (1.0 - ADAM_B1) * g
        v = ADAM_B2 * v_ref[...] + (1.0 - ADAM_B2) * jnp.square(g)
        m_hat = m / (1.0 - ADAM_B1 ** ADAM_STEP)
        v_hat = v / (1.0 - ADAM_B2 ** ADAM_STEP)
        d_ref[...] = -ADAM_LR * (m_hat / (jnp.sqrt(v_hat) + ADAM_EPS) + ADAM_WD * w_ref[...])
        nm_ref[...] = m
        nv_ref[...] = v

    spec = pl.BlockSpec((rows, width), lambda i: (i, 0))
    return pl.pallas_call(
        body, name=name, grid=(n // rows,), in_specs=[spec] * 4, out_specs=[spec] * 3,
        out_shape=[jax.ShapeDtypeStruct((n, width), F32)] * 3, compiler_params=_cparams("parallel"),
    )(w, g, m, v)


def _place():
    x, y, c = lax.axis_index("x"), lax.axis_index("y"), lax.axis_index("c")
    chips = [(1 - x, y), (x, 1 - y), (1 - x, 1 - y)]
    return x, y, c, chips


def _gather_weights(shards):
    na = len(shards)

    def body(*refs):
        src, dst = refs[:na], refs[na:2 * na]
        send, recv, fsend, frecv, lsem = refs[2 * na:]
        x, y, c, chips = _place()
        mine = 2 * x + y
        local = []
        for a in range(na):
            for l in range(2):
                cp = pltpu.make_async_copy(src[a].at[l], dst[a].at[l, mine], lsem.at[2 * a + l])
                cp.start()
                local.append(cp)

        def ici(a, k, shard, to):
            return pltpu.make_async_remote_copy(
                src_ref=src[a].at[c], dst_ref=dst[a].at[c, shard], send_sem=send.at[3 * a + k], recv_sem=recv.at[3 * a + k],
                device_id=to, device_id_type=MESH)

        def d2d(a, k, layer, shard):
            return pltpu.make_async_remote_copy(
                src_ref=dst[a].at[layer, shard], dst_ref=dst[a].at[layer, shard], send_sem=fsend.at[3 * a + k],
                recv_sem=frecv.at[3 * a + k], device_id=(x, y, 1 - c), device_id_type=MESH)

        sends = []
        for a in range(na):
            for k, (cx, cy) in enumerate(chips):
                cp = ici(a, k, mine, (cx, cy, c))
                cp.start()
                sends.append(cp)
        for a in range(na):
            for k, (cx, cy) in enumerate(chips):
                ici(a, k, 2 * cx + cy, (cx, cy, c)).wait_recv()
                cp = d2d(a, k, c, 2 * cx + cy)
                cp.start()
                sends.append(cp)
        for a in range(na):
            for k, (cx, cy) in enumerate(chips):
                d2d(a, k, 1 - c, 2 * cx + cy).wait_recv()
        for cp in sends:
            cp.wait_send()
        for cp in local:
            cp.wait()

    return pl.pallas_call(
        body, name="gather_weights", in_specs=[ANY] * na, out_specs=[ANY] * na,
        out_shape=[jax.ShapeDtypeStruct((2, N_SHARD) + s.shape[1:], s.dtype) for s in shards],
        scratch_shapes=[pltpu.SemaphoreType.DMA((3 * na,))] * 4 + [pltpu.SemaphoreType.DMA((2 * na,))],
    )(*shards)


def _pair_exchange(parts):
    na = len(parts)

    def body(*refs):
        src, dst = refs[:na], refs[na:2 * na]
        send, recv = refs[2 * na:]
        x, y, c, _ = _place()
        cps = []
        for a in range(na):
            cp = pltpu.make_async_remote_copy(
                src_ref=src[a].at[1 - c], dst_ref=dst[a], send_sem=send.at[a], recv_sem=recv.at[a],
                device_id=(x, y, 1 - c), device_id_type=MESH)
            cp.start()
            cps.append(cp)
        for cp in cps:
            cp.wait()

    return pl.pallas_call(
        body, name="pair_exchange", in_specs=[ANY] * na, out_specs=[ANY] * na,
        out_shape=[jax.ShapeDtypeStruct(p.shape[1:], p.dtype) for p in parts],
        scratch_shapes=[pltpu.SemaphoreType.DMA((na,))] * 2,
    )(*parts)


def _chip_exchange(sums):
    na = len(sums)

    def body(*refs):
        src, dst = refs[:na], refs[na:2 * na]
        send, recv = refs[2 * na:]
        x, y, c, chips = _place()
        cps = []
        for a in range(na):
            for k, (cx, cy) in enumerate(chips):
                cp = pltpu.make_async_remote_copy(
                    src_ref=src[a].at[2 * cx + cy], dst_ref=dst[a].at[k], send_sem=send.at[3 * a + k],
                    recv_sem=recv.at[3 * a + k], device_id=(cx, cy, c), device_id_type=MESH)
                cp.start()
                cps.append(cp)
        for cp in cps:
            cp.wait()

    return pl.pallas_call(
        body, name="chip_exchange", in_specs=[ANY] * na, out_specs=[ANY] * na,
        out_shape=[jax.ShapeDtypeStruct((3,) + s.shape[1:], s.dtype) for s in sums],
        scratch_shapes=[pltpu.SemaphoreType.DMA((3 * na,))] * 2,
    )(*sums)


def _pair_share(halves):
    na = len(halves)

    def body(*refs):
        dst = refs[na:2 * na]
        send, recv = refs[2 * na:]
        x, y, c, _ = _place()
        cps = []
        for a in range(na):
            cp = pltpu.make_async_remote_copy(
                src_ref=dst[a].at[c], dst_ref=dst[a].at[c], send_sem=send.at[a], recv_sem=recv.at[a],
                device_id=(x, y, 1 - c), device_id_type=MESH)
            cp.start()
            cps.append(cp)
        for a in range(na):
            cps[a].wait_send()
            pltpu.make_async_remote_copy(
                src_ref=dst[a].at[1 - c], dst_ref=dst[a].at[1 - c], send_sem=send.at[a], recv_sem=recv.at[a],
                device_id=(x, y, 1 - c), device_id_type=MESH).wait_recv()

    return pl.pallas_call(
        body, name="pair_share", in_specs=[ANY] * na, out_specs=[ANY] * na,
        out_shape=[jax.ShapeDtypeStruct(h.shape, h.dtype) for h in halves],
        input_output_aliases={a: a for a in range(na)},
        scratch_shapes=[pltpu.SemaphoreType.DMA((na,))] * 2,
    )(*halves)


def _pair_sum(name, part, got, rows):
    _, _, r, w = part.shape
    c = lax.axis_index("c").astype(jnp.int32).reshape(1)

    def body(c_ref, p_ref, g_ref, o_ref):
        o_ref[...] = _mx(p_ref[...] + g_ref[...])

    spec = pl.BlockSpec((None, rows, w), lambda j, i, c_ref: (j, i, 0))
    return pl.pallas_call(
        body, name=name, out_shape=jax.ShapeDtypeStruct((N_SHARD, r, w), MXU_DTYPE),
        grid_spec=pltpu.PrefetchScalarGridSpec(
            num_scalar_prefetch=1, grid=(N_SHARD, r // rows),
            in_specs=[pl.BlockSpec((None, None, rows, w), lambda j, i, c_ref: (c_ref[0], j, i, 0)), spec],
            out_specs=spec),
        compiler_params=_cparams("parallel", "parallel"),
    )(c, part, got)


def _chip_sum(name, part, got, others, rows):
    _, _, r, w = part.shape
    cj = jnp.stack([lax.axis_index("c"), 2 * lax.axis_index("x") + lax.axis_index("y")]).astype(jnp.int32)

    def body(cj_ref, p_ref, g_ref, o_ref, out_ref):
        acc = p_ref[...] + g_ref[...]
        for k in range(3):
            acc += o_ref[k].astype(F32)
        out_ref[...] = acc

    return pl.pallas_call(
        body, name=name, out_shape=jax.ShapeDtypeStruct((2, r, w), F32),
        grid_spec=pltpu.PrefetchScalarGridSpec(
            num_scalar_prefetch=1, grid=(r // rows,),
            in_specs=[pl.BlockSpec((None, None, rows, w), lambda i, cj: (cj[0], cj[1], i, 0)),
                      pl.BlockSpec((None, rows, w), lambda i, cj: (cj[1], i, 0)),
                      pl.BlockSpec((3, rows, w), lambda i, cj: (0, i, 0))],
            out_specs=pl.BlockSpec((None, rows, w), lambda i, cj: (cj[0], i, 0))),
        compiler_params=_cparams("parallel"),
    )(cj, part, got, others)


SMALL_ROWS = 40


def _sum_small(part):
    def body(p_ref, o_ref, land, send, recv):
        x, y, c, _ = _place()
        me = 4 * x + 2 * y + c
        cps = []
        for r in range(1, 8):
            cp = pltpu.make_async_remote_copy(
                src_ref=p_ref, dst_ref=land.at[r], send_sem=send.at[r], recv_sem=recv.at[r],
                device_id=(x ^ (r >> 2), y ^ ((r >> 1) & 1), c ^ (r & 1)), device_id_type=MESH)
            cp.start()
            cps.append(cp)
        land[0] = p_ref[...]
        for cp in cps:
            cp.wait()
        acc = land[me]
        for e in range(1, 8):
            acc += land[me ^ e]
        o_ref[...] = acc

    return pl.pallas_call(
        body, name="sum_small", in_specs=[pl.BlockSpec(memory_space=pltpu.VMEM)],
        out_specs=pl.BlockSpec(memory_space=pltpu.VMEM), out_shape=jax.ShapeDtypeStruct(part.shape, F32),
        scratch_shapes=[pltpu.VMEM((8,) + part.shape, F32), pltpu.SemaphoreType.DMA((8,)), pltpu.SemaphoreType.DMA((8,))],
    )(part)


BIG = ("ffn1_w_gu", "ffn1_w_down", "w_in", "w_out", "ffn2_w_gu", "ffn2_w_down")
SMALL = ("ln1_g", "ln1_b", "ln2_g", "ln2_b", "ln3_g", "ln3_b", "attn_sink", "cc_conv_b", "cc_ln_g", "cc_ln_b",
         "sc_conv_w", "cc_conv_w")
NAMES = ("ffn1_w_gu", "ffn1_w_down", "ln1_g", "ln1_b", "w_in", "sc_conv_w", "attn_sink", "cc_conv_w", "cc_conv_b",
         "cc_ln_g", "cc_ln_b", "w_out", "ln2_g", "ln2_b", "ffn2_w_gu", "ffn2_w_down", "ln3_g", "ln3_b")
SUM_ROWS = {"ffn1_w_gu": 256, "ffn1_w_down": 352, "w_in": 256, "w_out": 256, "ffn2_w_gu": 256, "ffn2_w_down": 352}


def _rope_tables(t):
    half = HEAD_DIM // 2
    inv_freq = ROPE_THETA ** (-jnp.arange(half, dtype=F32) / half)
    ang = jnp.arange(t).astype(F32)[:, None] * inv_freq[None, :]
    cos, sin = jnp.cos(ang), jnp.sin(ang)
    return jnp.tile(jnp.concatenate([cos, cos], axis=1), (1, 2)), jnp.tile(jnp.concatenate([-sin, sin], axis=1), (1, 2))


def _pack_small(vals):
    flat = jnp.concatenate([vals[n].reshape(-1) for n in SMALL])
    return jnp.pad(flat, (0, SMALL_ROWS * D_MODEL - flat.shape[0])).reshape(SMALL_ROWS, D_MODEL)


def _unpack_small(packed, shapes):
    flat, out, at = packed.reshape(-1), {}, 0
    for n in SMALL:
        size = int(np.prod(shapes[n]))
        out[n] = flat[at:at + size].reshape(shapes[n])
        at += size
    return out


def kernel(x, ffn1_w_gu, ffn1_w_down, ln1_g, ln1_b, w_in, sc_conv_w, attn_sink, cc_conv_w, cc_conv_b, cc_ln_g, cc_ln_b, w_out, ln2_g, ln2_b, ffn2_w_gu, ffn2_w_down, ln3_g, ln3_b, loss_target, m_ffn1_w_gu, m_ffn1_w_down, m_ln1_g, m_ln1_b, m_w_in, m_sc_conv_w, m_attn_sink, m_cc_conv_w, m_cc_conv_b, m_cc_ln_g, m_cc_ln_b, m_w_out, m_ln2_g, m_ln2_b, m_ffn2_w_gu, m_ffn2_w_down, m_ln3_g, m_ln3_b, v_ffn1_w_gu, v_ffn1_w_down, v_ln1_g, v_ln1_b, v_w_in, v_sc_conv_w, v_attn_sink, v_cc_conv_w, v_cc_conv_b, v_cc_ln_g, v_cc_ln_b, v_w_out, v_ln2_g, v_ln2_b, v_ffn2_w_gu, v_ffn2_w_down, v_ln3_g, v_ln3_b):
    given = dict(locals())
    w = {n: given[n] for n in NAMES}
    mom = {n: given["m_" + n] for n in NAMES}
    var = {n: given["v_" + n] for n in NAMES}
    x0 = x[0]
    target = loss_target[0]
    t = x0.shape[0]
    chip = 2 * lax.axis_index("x") + lax.axis_index("y")

    conv_shard = jnp.pad(jnp.concatenate([sc_conv_w, cc_conv_w], axis=1), ((0, 0), (0, 6), (0, 64)))
    full = _gather_weights([_mx(w[n]) for n in BIG] + [conv_shard])
    wgu1, wd1, win, wout, wgu2, wd2, conv_full = full
    wout = wout.reshape(2, D_MODEL, D_MODEL)
    wd1, wd2 = wd1.reshape(2, D_FF, D_MODEL), wd2.reshape(2, D_FF, D_MODEL)
    conv_full = jnp.transpose(conv_full[:, :, :SC_W + CC_W, :64], (0, 2, 1, 3)).reshape(2, SC_W + CC_W, D_CONV)
    sc_full, cc_full = conv_full[:, :SC_W], conv_full[:, SC_W:]
    cos, sin = _rope_tables(t)

    def vec(a, l):
        return a[l][None, :]

    acts = []
    h = x0
    for l in range(2):
        y1, r1, gu1 = _ffn_fwd(h, wgu1, wd1, vec(ln1_g, l), vec(ln1_b, l), l)
        z = _in_proj(y1, win, l)
        ysc, ycc, u2 = _conv_fwd(z, sc_full[l], cc_full[l], vec(cc_conv_b, l), vec(cc_ln_g, l), vec(cc_ln_b, l))
        qs, kf, vf = _attn_prep(z, cos, sin)
        o_nat, o, lse = _attn_fwd(qs, kf, vf, attn_sink[l])
        ycat = jnp.concatenate([ysc, o_nat, ycc], axis=1)
        y2, r2 = _out_proj(ycat, y1, wout, vec(ln2_g, l), vec(ln2_b, l), l)
        y3, r3, gu2 = _ffn_fwd(y2, wgu2, wd2, vec(ln3_g, l), vec(ln3_b, l), l)
        acts.append(dict(x=h, y1=y1, r1=r1, gu1=gu1, z=z, u2=u2, qs=qs, kf=kf, vf=vf, o=o, lse=lse, ycat=ycat,
                         y2=y2, r2=r2, gu2=gu2, r3=r3))
        h = y3
    loss_rows, dy = _loss_head(h, target)
    loss = lax.psum(loss_rows[0, 0], ("x", "y", "c"))

    part = {n: None for n in BIG}
    small = [None, None]

    def ffn_grads(which, dy, r, gu, xin, wgu, wd, gamma, l):
        dx, dh, a, do, dgb = _ffn_bwd(dy, r, gu, wgu, wd, gamma, l)
        part[which + "_w_gu"] = _mm_tn(
            which + "_dwgu", xin, dh, pl.BlockSpec((TM, D_MODEL), lambda n, k: (k, 0)),
            pl.BlockSpec((None, TM, FF_CHUNK), lambda n, k: (n // N_CHUNK, k, n % N_CHUNK)),
            pl.BlockSpec((None, None, D_MODEL, FF_CHUNK), lambda n, k: (l, n, 0, 0)),
            (2, N_SHARD, D_MODEL, GU_SHARD), (D_MODEL, FF_CHUNK), (2 * N_CHUNK, t // TM), part[which + "_w_gu"])
        part[which + "_w_down"] = _mm_tn(
            which + "_dwd", a, do, pl.BlockSpec((TM, FF_CHUNK), lambda n, k: (k, n)),
            pl.BlockSpec((TM, D_MODEL), lambda n, k: (k, 0)),
            pl.BlockSpec((None, FF_CHUNK, D_MODEL), lambda n, k: (l, n, 0)),
            (2, D_FF, D_MODEL), (FF_CHUNK, D_MODEL), (N_CHUNK, t // TM), part[which + "_w_down"])
        return dx, dgb

    for l in (1, 0):
        s = acts[l]
        dy, dgb3 = ffn_grads("ffn2", dy, s["r3"], s["gu2"], s["y2"], wgu2, wd2, vec(ln3_g, l), l)
        res, dm, dycat, dgb2 = _out_proj_bwd(dy, s["r2"], wout, vec(ln2_g, l), l)
        part["w_out"] = _mm_tn(
            "dwout", s["ycat"], dm, pl.BlockSpec((TM, OUT_SHARD), lambda n, k: (k, n)),
            pl.BlockSpec((TM, D_MODEL), lambda n, k: (k, 0)),
            pl.BlockSpec((None, None, OUT_SHARD, D_MODEL), lambda n, k: (l, n, 0, 0)),
            (2, N_SHARD, OUT_SHARD, D_MODEL), (OUT_SHARD, D_MODEL), (N_SHARD, t // TM), part["w_out"])
        dz_sc, dz_cc, dconv = _conv_bwd(s["z"], dycat, s["u2"], sc_full[l], cc_full[l], vec(cc_ln_g, l), vec(cc_ln_b, l))
        dost, ld = _attn_dprep(dycat, s["o"], s["lse"])
        dqs, dsink = _attn_bwd_q(s["qs"], s["kf"], s["vf"], dost, ld, attn_sink[l])
        dkf, dvf = _attn_bwd_kv(s["qs"], s["kf"], s["vf"], dost, ld)
        dz_att = _attn_prep_bwd(dqs, dkf, dvf, cos, sin)
        dz = jnp.concatenate([dz_sc, dz_att, dz_cc], axis=1)
        part["w_in"] = _mm_tn(
            "dwin", s["y1"], dz, pl.BlockSpec((TM, D_MODEL), lambda n, k: (k, 0)),
            pl.BlockSpec((TM, IN_SHARD), lambda n, k: (k, n)),
            pl.BlockSpec((None, None, D_MODEL, IN_SHARD), lambda n, k: (l, n, 0, 0)),
            (2, N_SHARD, D_MODEL, IN_SHARD), (D_MODEL, IN_SHARD), (N_SHARD, t // TM), part["w_in"])
        dy = _in_proj_bwd(dz, res, win, l)
        dy, dgb1 = ffn_grads("ffn1", dy, s["r1"], s["gu1"], s["x"], wgu1, wd1, vec(ln1_g, l), l)
        small[l] = dict(ln1_g=dgb1[0], ln1_b=dgb1[1], ln2_g=dgb2[0], ln2_b=dgb2[1], ln3_g=dgb3[0], ln3_b=dgb3[1],
                        attn_sink=dsink[:, 0], cc_conv_b=dconv[ROW_CCB], cc_ln_g=dconv[ROW_CCG],
                        cc_ln_b=dconv[ROW_CCBETA], sc_conv_w=dconv[ROW_SCW:ROW_SCW + SC_W],
                        cc_conv_w=dconv[ROW_CCW:ROW_CCW + CC_W])
    grad_x = dy[None]

    for n in ("ffn1_w_down", "ffn2_w_down"):
        part[n] = part[n].reshape(2, N_SHARD, D_FF // N_SHARD, D_MODEL)
    parts = [part[n] for n in BIG]
    got = _pair_exchange(parts)
    sums = [_pair_sum("pair_sum_" + n, p, g, SUM_ROWS[n]) for n, p, g in zip(BIG, parts, got)]
    others = _chip_exchange(sums)
    halves = [_chip_sum("chip_sum_" + n, p, g, o, SUM_ROWS[n]) for n, p, g, o in zip(BIG, parts, got, others)]
    grads = dict(zip(BIG, _pair_share(halves)))
    for n in BIG:
        grads[n] = grads[n].reshape(w[n].shape)

    small_full = {n: jnp.stack([small[0][n], small[1][n]]) for n in SMALL}
    small_sum = _unpack_small(_sum_small(_pack_small(small_full)), {n: small_full[n].shape for n in SMALL})
    for n in SMALL:
        g = small_sum[n]
        if n in ("sc_conv_w", "cc_conv_w"):
            g = lax.dynamic_slice_in_dim(g, chip * 64, 64, axis=2)
        grads[n] = g

    delta, new_m, new_v = {}, {}, {}
    for n in BIG:
        shape = w[n].shape
        two_d = (shape[0] * shape[1], shape[2])
        outs = _adamw("adamw_" + n, w[n].reshape(two_d), grads[n].reshape(two_d), mom[n].reshape(two_d),
                      var[n].reshape(two_d), 128)
        delta[n], new_m[n], new_v[n] = [a.reshape(shape) for a in outs]
    shapes = {n: w[n].shape for n in SMALL}
    outs = _adamw("adamw_small", _pack_small({n: w[n] for n in SMALL}), _pack_small({n: grads[n] for n in SMALL}),
                  _pack_small({n: mom[n] for n in SMALL}), _pack_small({n: var[n] for n in SMALL}), 8)
    for d, packed in zip((delta, new_m, new_v), outs):
        d.update(_unpack_small(packed, shapes))

    return (loss, grad_x, *[grads[n] for n in NAMES], *[delta[n] for n in NAMES], *[new_m[n] for n in NAMES],
            *[new_v[n] for n in NAMES])
```

```python
import functools

import numpy as np
import jax
import jax.numpy as jnp
from jax import lax
from jax.experimental import pallas as pl
from jax.experimental.pallas import tpu as pltpu

F32 = jnp.float32
MXU_DTYPE = jnp.bfloat16

D_MODEL = 1024
D_FF = 2816
N_SHARD = 4
D_IN = 2048
GU_SHARD = 2 * D_FF // N_SHARD
FF_CHUNK = GU_SHARD
N_CHUNK = D_FF // FF_CHUNK
IN_SHARD = D_IN // N_SHARD
OUT_SHARD = D_MODEL // N_SHARD
HEAD_DIM = 64
N_Q_HEADS = 8
BLOCK = 128
SC_W = 3
CC_W = 31
D_CONV = 256
HALO = 16
LN_EPS = 1e-5
ALPHA = (2.0 * 2) ** 0.25
NEG = -1e30
ROPE_THETA = 10000.0
ADAM_LR, ADAM_B1, ADAM_B2, ADAM_EPS, ADAM_WD, ADAM_STEP = 0.001, 0.9, 0.999, 1e-08, 0.01, 10

TM = 512
TMC = 256
VMEM_LIMIT = 56 * 1024 * 1024
MESH = pl.DeviceIdType.MESH
ANY = pl.BlockSpec(memory_space=pl.ANY)


def _cparams(*sem):
    return pltpu.CompilerParams(dimension_semantics=sem, vmem_limit_bytes=VMEM_LIMIT)


def _dot(a, b):
    return jnp.dot(a, b, preferred_element_type=F32)


def _dot_nt(a, b):
    return lax.dot_general(a, b, (((1,), (1,)), ((), ())), preferred_element_type=F32)


def _dot_tn(a, b):
    return lax.dot_general(a, b, (((0,), (0,)), ((), ())), preferred_element_type=F32)


def _mx(a):
    return a.astype(MXU_DTYPE)


def _mean(a):
    return jnp.mean(a, axis=-1, keepdims=True)


def _ln_stats(r):
    xc = r - _mean(r)
    rstd = lax.rsqrt(_mean(xc * xc) + LN_EPS)
    return xc * rstd, rstd


def _ln_bwd(dy, xh, rstd, gamma):
    dxh = dy * gamma
    return rstd * (dxh - _mean(dxh) - xh * _mean(dxh * xh))


def _colsum(a):
    return jnp.sum(a, axis=0, keepdims=True)


def _sigmoid(a):
    return 1.0 / (1.0 + jnp.exp(-a))


def _ffn_fwd(x, wgu, wd, gamma, beta, l):
    t = x.shape[0]
    nc = N_CHUNK

    def body(x_ref, wg_ref, wu_ref, wd_ref, g_ref, b_ref, y_ref, r_ref, gu_ref, xb_s, acc_s):
        c = pl.program_id(1)

        @pl.when(c == 0)
        def _():
            xb_s[...] = _mx(x_ref[...])
            acc_s[...] = jnp.zeros_like(acc_s)

        xb = xb_s[...]
        hg = _dot(xb, wg_ref[...])
        hu = _dot(xb, wu_ref[...])
        gu_ref[0] = _mx(hg)
        gu_ref[1] = _mx(hu)
        a = (hg * _sigmoid(hg)) * hu
        acc_s[...] += _dot(_mx(a), wd_ref[...])

        @pl.when(c == nc - 1)
        def _():
            r = ALPHA * x_ref[...] + 0.5 * acc_s[...]
            xh, _ = _ln_stats(r)
            r_ref[...] = r
            y_ref[...] = xh * g_ref[...] + b_ref[...]

    row = pl.BlockSpec((TM, D_MODEL), lambda i, c: (i, 0))
    vec = pl.BlockSpec((1, D_MODEL), lambda i, c: (0, 0))
    return pl.pallas_call(
        body, name="ffn_fwd", grid=(t // TM, nc),
        in_specs=[row,
                  pl.BlockSpec((None, None, D_MODEL, FF_CHUNK), lambda i, c: (l, c, 0, 0)),
                  pl.BlockSpec((None, None, D_MODEL, FF_CHUNK), lambda i, c: (l, N_CHUNK + c, 0, 0)),
                  pl.BlockSpec((None, FF_CHUNK, D_MODEL), lambda i, c: (l, c, 0)),
                  vec, vec],
        out_specs=[row, row, pl.BlockSpec((2, TM, FF_CHUNK), lambda i, c: (0, i, c))],
        out_shape=[jax.ShapeDtypeStruct((t, D_MODEL), F32), jax.ShapeDtypeStruct((t, D_MODEL), F32),
                   jax.ShapeDtypeStruct((2, t, D_FF), MXU_DTYPE)],
        scratch_shapes=[pltpu.VMEM((TM, D_MODEL), MXU_DTYPE), pltpu.VMEM((TM, D_MODEL), F32)],
        compiler_params=_cparams("parallel", "arbitrary"),
    )(x, wgu, wgu, wd, gamma, beta)


def _ffn_bwd(dy, r, gu, wgu, wd, gamma, l):
    t = dy.shape[0]
    nc = N_CHUNK

    def norm_body(dy_ref, r_ref, g_ref, res_ref, do_ref, dgb_ref):
        @pl.when(pl.program_id(0) == 0)
        def _():
            dgb_ref[...] = jnp.zeros_like(dgb_ref)

        xh, rstd = _ln_stats(r_ref[...])
        dy = dy_ref[...]
        dr = _ln_bwd(dy, xh, rstd, g_ref[...])
        do_ref[...] = _mx(0.5 * dr)
        res_ref[...] = ALPHA * dr
        dgb_ref[0:1, :] += _colsum(dy * xh)
        dgb_ref[1:2, :] += _colsum(dy)

    row1 = pl.BlockSpec((TM, D_MODEL), lambda i: (i, 0))
    res, do, dgb = pl.pallas_call(
        norm_body, name="ffn_bwd_norm", grid=(t // TM,),
        in_specs=[row1, row1, pl.BlockSpec((1, D_MODEL), lambda i: (0, 0))],
        out_specs=[row1, row1, pl.BlockSpec((8, D_MODEL), lambda i: (0, 0))],
        out_shape=[jax.ShapeDtypeStruct((t, D_MODEL), F32), jax.ShapeDtypeStruct((t, D_MODEL), MXU_DTYPE),
                   jax.ShapeDtypeStruct((8, D_MODEL), F32)],
        compiler_params=_cparams("arbitrary"),
    )(dy, r, gamma)

    def hidden_body(do_ref, gu_ref, wd_ref, dh_ref, a_ref):
        da = _dot_nt(do_ref[...], wd_ref[...])
        g = gu_ref[0].astype(F32)
        u = gu_ref[1].astype(F32)
        s = _sigmoid(g)
        sil = g * s
        a_ref[...] = _mx(sil * u)
        dh_ref[0] = _mx(da * u * (s * (1.0 + g * (1.0 - s))))
        dh_ref[1] = _mx(da * sil)

    hid = pl.BlockSpec((2, TM, FF_CHUNK), lambda c, i: (0, i, c))
    dh, a = pl.pallas_call(
        hidden_body, name="ffn_bwd_hidden", grid=(nc, t // TM),
        in_specs=[pl.BlockSpec((TM, D_MODEL), lambda c, i: (i, 0)), hid,
                  pl.BlockSpec((None, FF_CHUNK, D_MODEL), lambda c, i: (l, c, 0))],
        out_specs=[hid, pl.BlockSpec((TM, FF_CHUNK), lambda c, i: (i, c))],
        out_shape=[jax.ShapeDtypeStruct((2, t, D_FF), MXU_DTYPE), jax.ShapeDtypeStruct((t, D_FF), MXU_DTYPE)],
        compiler_params=_cparams("parallel", "parallel"),
    )(do, gu, wd)

    def input_body(res_ref, dh_ref, w_ref, dx_ref):
        acc = res_ref[...]
        for j in range(N_SHARD):
            part = dh_ref[j // N_CHUNK][:, (j % N_CHUNK) * FF_CHUNK:(j % N_CHUNK + 1) * FF_CHUNK]
            acc += _dot_nt(part, w_ref[j])
        dx_ref[...] = acc

    dx = pl.pallas_call(
        input_body, name="ffn_bwd_input", grid=(t // TM,),
        in_specs=[row1, pl.BlockSpec((2, TM, D_FF), lambda i: (0, i, 0)),
                  pl.BlockSpec((None, N_SHARD, D_MODEL, GU_SHARD), lambda i: (l, 0, 0, 0))],
        out_specs=row1,
        out_shape=jax.ShapeDtypeStruct((t, D_MODEL), F32),
        compiler_params=_cparams("parallel"),
    )(res, dh, wgu)
    return dx, dh, a, do, dgb


def _mm_tn(name, a, b, a_spec, b_spec, out_spec, out_shape, acc_shape, grid, prev=None, split=1):
    nk = grid[-1]
    width = acc_shape[1] // split

    def body(*refs):
        a_ref, b_ref = refs[0], refs[1]
        o_ref, acc = refs[-2], refs[-1]
        k = pl.program_id(len(grid) - 1)

        @pl.when(k == 0)
        def _():
            acc[...] = jnp.zeros_like(acc)

        acc[...] += _dot_tn(_mx(a_ref[...]), _mx(b_ref[...]))

        @pl.when(k == nk - 1)
        def _():
            if split == 1:
                o_ref[...] = acc[...]
            else:
                for j in range(split):
                    o_ref[j] = acc[:, j * width:(j + 1) * width]

    ins, specs, alias = [a, b], [a_spec, b_spec], {}
    if prev is not None:
        ins.append(prev)
        specs.append(ANY)
        alias = {2: 0}
    sem = ("parallel",) * (len(grid) - 1) + ("arbitrary",)
    return pl.pallas_call(
        body, name=name, grid=grid, in_specs=specs, out_specs=out_spec,
        out_shape=jax.ShapeDtypeStruct(out_shape, F32), input_output_aliases=alias,
        scratch_shapes=[pltpu.VMEM(acc_shape, F32)], compiler_params=_cparams(*sem),
    )(*ins)


def _in_proj(x, w_in, l):
    t = x.shape[0]

    def body(x_ref, w_ref, z_ref):
        xb = _mx(x_ref[...])
        for j in range(N_SHARD):
            z_ref[:, j * IN_SHARD:(j + 1) * IN_SHARD] = _dot(xb, w_ref[j])

    return pl.pallas_call(
        body, name="in_proj", grid=(t // TM,),
        in_specs=[pl.BlockSpec((TM, D_MODEL), lambda i: (i, 0)),
                  pl.BlockSpec((None, N_SHARD, D_MODEL, IN_SHARD), lambda i: (l, 0, 0, 0))],
        out_specs=pl.BlockSpec((TM, D_IN), lambda i: (i, 0)),
        out_shape=jax.ShapeDtypeStruct((t, D_IN), F32),
        compiler_params=_cparams("parallel"),
    )(x, w_in)


def _in_proj_bwd(dz, dx_res, w_in, l):
    t = dz.shape[0]

    def body(dz_ref, res_ref, w_ref, dx_ref):
        acc = res_ref[...]
        for j in range(N_SHARD):
            acc += _dot_nt(dz_ref[:, j * IN_SHARD:(j + 1) * IN_SHARD], w_ref[j])
        dx_ref[...] = acc

    row = pl.BlockSpec((TM, D_MODEL), lambda i: (i, 0))
    return pl.pallas_call(
        body, name="in_proj_bwd", grid=(t // TM,),
        in_specs=[pl.BlockSpec((TM, D_IN), lambda i: (i, 0)), row,
                  pl.BlockSpec((None, N_SHARD, D_MODEL, IN_SHARD), lambda i: (l, 0, 0, 0))],
        out_specs=row, out_shape=jax.ShapeDtypeStruct((t, D_MODEL), F32),
        compiler_params=_cparams("parallel"),
    )(dz, dx_res, w_in)


def _out_proj(ycat, x, w_out, gamma, beta, l):
    t = x.shape[0]

    def body(yc_ref, x_ref, w_ref, g_ref, b_ref, y_ref, r_ref):
        r = ALPHA * x_ref[...] + _dot(yc_ref[...], w_ref[...])
        xh, _ = _ln_stats(r)
        r_ref[...] = r
        y_ref[...] = xh * g_ref[...] + b_ref[...]

    row = pl.BlockSpec((TM, D_MODEL), lambda i: (i, 0))
    vec = pl.BlockSpec((1, D_MODEL), lambda i: (0, 0))
    return pl.pallas_call(
        body, name="out_proj", grid=(t // TM,),
        in_specs=[row, row, pl.BlockSpec((None, D_MODEL, D_MODEL), lambda i: (l, 0, 0)), vec, vec],
        out_specs=[row, row],
        out_shape=[jax.ShapeDtypeStruct((t, D_MODEL), F32)] * 2,
        compiler_params=_cparams("parallel"),
    )(ycat, x, w_out, gamma, beta)


def _out_proj_bwd(dy, r, w_out, gamma, l):
    t = dy.shape[0]

    def body(dy_ref, r_ref, w_ref, g_ref, res_ref, dm_ref, dyc_ref, dgb_ref):
        @pl.when(pl.program_id(0) == 0)
        def _():
            dgb_ref[...] = jnp.zeros_like(dgb_ref)

        xh, rstd = _ln_stats(r_ref[...])
        dy = dy_ref[...]
        dr = _ln_bwd(dy, xh, rstd, g_ref[...])
        res_ref[...] = ALPHA * dr
        dm = _mx(dr)
        dm_ref[...] = dm
        dyc_ref[...] = _dot_nt(dm, w_ref[...])
        dgb_ref[0:1, :] += _colsum(dy * xh)
        dgb_ref[1:2, :] += _colsum(dy)

    row = pl.BlockSpec((TM, D_MODEL), lambda i: (i, 0))
    return pl.pallas_call(
        body, name="out_proj_bwd", grid=(t // TM,),
        in_specs=[row, row, pl.BlockSpec((None, D_MODEL, D_MODEL), lambda i: (l, 0, 0)),
                  pl.BlockSpec((1, D_MODEL), lambda i: (0, 0))],
        out_specs=[row, row, row, pl.BlockSpec((8, D_MODEL), lambda i: (0, 0))],
        out_shape=[jax.ShapeDtypeStruct((t, D_MODEL), F32), jax.ShapeDtypeStruct((t, D_MODEL), MXU_DTYPE),
                   jax.ShapeDtypeStruct((t, D_MODEL), F32), jax.ShapeDtypeStruct((8, D_MODEL), F32)],
        compiler_params=_cparams("arbitrary"),
    )(dy, r, w_out, gamma)


def _halo_specs(t, width, col):
    per = TMC // HALO
    last = t // HALO - 1
    return [pl.BlockSpec((HALO, width), lambda i: (jnp.maximum(i * per - 1, 0), col)),
            pl.BlockSpec((TMC, width), lambda i: (i, col)),
            pl.BlockSpec((HALO, width), lambda i: (jnp.minimum((i + 1) * per, last), col))]


def _extend(refs, i, nt):
    p_ref, c_ref, n_ref = refs
    p = jnp.where(i > 0, p_ref[...].astype(F32), 0.0)
    n = jnp.where(i < nt - 1, n_ref[...].astype(F32), 0.0)
    return jnp.concatenate([p, c_ref[...].astype(F32), n], axis=0)


def _conv_fwd(z, sc_w, cc_w, cc_cb, cc_g, cc_b):
    t = z.shape[0]
    nt = t // TMC

    def body(*refs):
        b_ref = refs[0]
        c3, h3, a3, g3 = refs[1:4], refs[4:7], refs[7:10], refs[10:13]
        scw_ref, ccw_ref, cb_ref, lg_ref, lb_ref = refs[13:18]
        ysc_ref, ycc_ref, u2_ref, e_s = refs[18:22]
        i = pl.program_id(0)
        e_s[...] = _extend(c3, i, nt) * _extend(h3, i, nt)
        cv = jnp.zeros((TMC, D_CONV), F32)
        for k in range(SC_W):
            cv += scw_ref[k:k + 1, :] * e_s[pl.ds(HALO + k - 1, TMC), :]
        ysc_ref[...] = _mx(b_ref[...] * cv)
        e_s[...] = _extend(a3, i, nt) * _sigmoid(_extend(g3, i, nt))
        u2 = jnp.zeros((TMC, D_CONV), F32) + cb_ref[...]
        for k in range(CC_W):
            u2 += ccw_ref[k:k + 1, :] * e_s[pl.ds(HALO + k - 15, TMC), :]
        u2_ref[...] = u2
        xh, _ = _ln_stats(u2)
        n = xh * lg_ref[...] + lb_ref[...]
        ycc_ref[...] = _mx(n * _sigmoid(n))

    tile = pl.BlockSpec((TMC, D_CONV), lambda i: (i, 0))
    vec = pl.BlockSpec((1, D_CONV), lambda i: (0, 0))
    in_specs = ([pl.BlockSpec((TMC, D_CONV), lambda i: (i, 0))] + _halo_specs(t, D_CONV, 1) + _halo_specs(t, D_CONV, 2)
                + _halo_specs(t, D_CONV, 6) + _halo_specs(t, D_CONV, 7)
                + [pl.BlockSpec((SC_W, D_CONV), lambda i: (0, 0)), pl.BlockSpec((CC_W, D_CONV), lambda i: (0, 0)),
                   vec, vec, vec])
    return pl.pallas_call(
        body, name="conv_fwd", grid=(nt,), in_specs=in_specs, out_specs=[tile, tile, tile],
        out_shape=[jax.ShapeDtypeStruct((t, D_CONV), MXU_DTYPE), jax.ShapeDtypeStruct((t, D_CONV), MXU_DTYPE),
                   jax.ShapeDtypeStruct((t, D_CONV), F32)],
        scratch_shapes=[pltpu.VMEM((TMC + 2 * HALO, D_CONV), F32)],
        compiler_params=_cparams("parallel"),
    )(*([z] * 13), sc_w, cc_w, cc_cb, cc_g, cc_b)


ROW_CCW, ROW_CCB, ROW_CCG, ROW_CCBETA, ROW_SCW, CONV_ROWS = 0, 31, 32, 33, 34, 40


def _conv_bwd(z, dycat, u2, sc_w, cc_w, cc_g, cc_b):
    t = z.shape[0]
    nt = t // TMC

    def body(*refs):
        b3, c3, h3, a3, g3 = refs[0:3], refs[3:6], refs[6:9], refs[9:12], refs[12:15]
        dys3, dyc3, u3 = refs[15:18], refs[18:21], refs[21:24]
        scw_ref, ccw_ref, lg_ref, lb_ref = refs[24:28]
        dsc_ref, dcc_ref, sm_ref, e_s, f_s = refs[28:33]
        i = pl.program_id(0)

        @pl.when(i == 0)
        def _():
            sm_ref[...] = jnp.zeros_like(sm_ref)

        cur = pl.ds(HALO, TMC)
        e_s[...] = _extend(c3, i, nt) * _extend(h3, i, nt)
        f_s[...] = _extend(dys3, i, nt) * _extend(b3, i, nt)
        cv = jnp.zeros((TMC, D_CONV), F32)
        dp = jnp.zeros((TMC, D_CONV), F32)
        dcv = f_s[cur, :]
        for k in range(SC_W):
            win = e_s[pl.ds(HALO + k - 1, TMC), :]
            cv += scw_ref[k:k + 1, :] * win
            dp += scw_ref[k:k + 1, :] * f_s[pl.ds(HALO - k + 1, TMC), :]
            sm_ref[ROW_SCW + k:ROW_SCW + k + 1, :] += _colsum(dcv * win)
        dsc_ref[:, 0:D_CONV] = _mx(dys3[1][...] * cv)
        dsc_ref[:, D_CONV:2 * D_CONV] = _mx(dp * h3[1][...])
        dsc_ref[:, 2 * D_CONV:3 * D_CONV] = _mx(dp * c3[1][...])
        xh, rstd = _ln_stats(_extend(u3, i, nt))
        n = xh * lg_ref[...] + lb_ref[...]
        sg = _sigmoid(n)
        dn = _extend(dyc3, i, nt) * (sg * (1.0 + n * (1.0 - sg)))
        f_s[...] = _ln_bwd(dn, xh, rstd, lg_ref[...])
        sm_ref[ROW_CCG:ROW_CCG + 1, :] += _colsum((dn * xh)[HALO:HALO + TMC])
        sm_ref[ROW_CCBETA:ROW_CCBETA + 1, :] += _colsum(dn[HALO:HALO + TMC])
        sig_g = _sigmoid(_extend(g3, i, nt))
        e_s[...] = _extend(a3, i, nt) * sig_g
        du2 = f_s[cur, :]
        sm_ref[ROW_CCB:ROW_CCB + 1, :] += _colsum(du2)
        duu = jnp.zeros((TMC, D_CONV), F32)
        for k in range(CC_W):
            duu += ccw_ref[k:k + 1, :] * f_s[pl.ds(HALO + 15 - k, TMC), :]
            sm_ref[ROW_CCW + k:ROW_CCW + k + 1, :] += _colsum(du2 * e_s[pl.ds(HALO + k - 15, TMC), :])
        sgc = sig_g[HALO:HALO + TMC]
        dcc_ref[:, 0:D_CONV] = _mx(duu * sgc)
        dcc_ref[:, D_CONV:2 * D_CONV] = _mx(duu * a3[1][...] * sgc * (1.0 - sgc))

    vec = pl.BlockSpec((1, D_CONV), lambda i: (0, 0))
    in_specs = []
    for col in (0, 1, 2, 6, 7):
        in_specs += _halo_specs(t, D_CONV, col)
    in_specs += _halo_specs(t, D_CONV, 0) + _halo_specs(t, D_CONV, 3) + _halo_specs(t, D_CONV, 0)
    in_specs += [pl.BlockSpec((SC_W, D_CONV), lambda i: (0, 0)), pl.BlockSpec((CC_W, D_CONV), lambda i: (0, 0)), vec, vec]
    return pl.pallas_call(
        body, name="conv_bwd", grid=(nt,), in_specs=in_specs,
        out_specs=[pl.BlockSpec((TMC, 3 * D_CONV), lambda i: (i, 0)), pl.BlockSpec((TMC, 2 * D_CONV), lambda i: (i, 0)),
                   pl.BlockSpec((CONV_ROWS, D_CONV), lambda i: (0, 0))],
        out_shape=[jax.ShapeDtypeStruct((t, 3 * D_CONV), MXU_DTYPE), jax.ShapeDtypeStruct((t, 2 * D_CONV), MXU_DTYPE),
                   jax.ShapeDtypeStruct((CONV_ROWS, D_CONV), F32)],
        scratch_shapes=[pltpu.VMEM((TMC + 2 * HALO, D_CONV), F32)] * 2,
        compiler_params=_cparams("arbitrary"),
    )(*([z] * 15), *([dycat] * 6), *([u2] * 3), sc_w, cc_w, cc_g, cc_b)


def _lane(shape):
    return lax.broadcasted_iota(jnp.int32, shape, 1)


def _swap_halves(x):
    w = x.shape[1]
    lo = (_lane(x.shape) % HEAD_DIM) < HEAD_DIM // 2
    return jnp.where(lo, pltpu.roll(x, w - HEAD_DIM // 2, 1), pltpu.roll(x, HEAD_DIM // 2, 1))


def _half(shape, g):
    lane = _lane(shape)
    return lane < HEAD_DIM if g == 0 else lane >= HEAD_DIM


GROUP_ROWS = 4 * BLOCK


def _stack_heads(tiles, out_ref, nblk):
    for tt in range(4):
        g = tt // 2
        for slot in range(2):
            s = 2 * (tt % 2) + slot
            piece = tiles[tt] if slot == g else pltpu.roll(tiles[tt], HEAD_DIM, 1)
            piece = jnp.where(_half(piece.shape, g), piece, 0.0).astype(out_ref.dtype)
            for b in range(nblk):
                at = GROUP_ROWS * b + BLOCK * s
                out_ref[g, at:at + BLOCK, :] = piece[BLOCK * b:BLOCK * (b + 1)]


def _unstack_heads(ref, nblk):
    tiles = []
    for tt in range(4):
        g = tt // 2
        tile = None
        for slot in range(2):
            s = 2 * (tt % 2) + slot
            rows = [ref[g, GROUP_ROWS * b + BLOCK * s:GROUP_ROWS * b + BLOCK * (s + 1), :] for b in range(nblk)]
            piece = rows[0] if nblk == 1 else jnp.concatenate(rows, axis=0)
            if slot != g:
                piece = pltpu.roll(piece, HEAD_DIM, 1)
            tile = piece if tile is None else tile + piece
        tiles.append(tile)
    return tiles


def _attn_prep(z, cos, sin):
    t = z.shape[0]
    nblk = TM // BLOCK

    def body(qa_ref, qb_ref, k_ref, v_ref, cos_ref, sin_ref, qst_ref, kr_ref, vb_ref):
        cs, sn = cos_ref[...], sin_ref[...]

        def rope(x):
            return x * cs + _swap_halves(x) * sn

        tiles = []
        for tt in range(4):
            src = qa_ref if tt < 2 else qb_ref
            tiles.append(rope(src[:, (tt % 2) * BLOCK:(tt % 2 + 1) * BLOCK]) * (HEAD_DIM ** -0.5))
        _stack_heads(tiles, qst_ref, nblk)
        kr_ref[...] = _mx(rope(k_ref[...]))
        vb_ref[...] = _mx(v_ref[...])

    def col(width, j):
        return pl.BlockSpec((TM, width), lambda i: (i, j))

    return pl.pallas_call(
        body, name="attn_prep", grid=(t // TM,),
        in_specs=[col(256, 3), col(256, 4), col(128, 10), col(128, 11), col(128, 0), col(128, 0)],
        out_specs=[pl.BlockSpec((2, 4 * TM, BLOCK), lambda i: (0, i, 0)), col(128, 0), col(128, 0)],
        out_shape=[jax.ShapeDtypeStruct((2, 4 * t, BLOCK), MXU_DTYPE), jax.ShapeDtypeStruct((t, BLOCK), MXU_DTYPE),
                   jax.ShapeDtypeStruct((t, BLOCK), MXU_DTYPE)],
        compiler_params=_cparams("parallel"),
    )(z, z, z, z, cos, sin)


def _attn_dprep(dycat, ost, lst):
    t = dycat.shape[0]
    nblk = TM // BLOCK

    def body(da_ref, db_ref, o_ref, l_ref, dost_ref, ld_ref, st_s):
        tiles = []
        for tt in range(4):
            src = da_ref if tt < 2 else db_ref
            tiles.append(src[:, (tt % 2) * BLOCK:(tt % 2 + 1) * BLOCK])
        _stack_heads(tiles, st_s, nblk)
        for g in range(2):
            do = st_s[g]
            dost_ref[g] = _mx(do)
            dsum = jnp.sum(do * o_ref[g], axis=-1, keepdims=True)
            ld_ref[g] = jnp.where(_lane(do.shape) < HEAD_DIM, l_ref[g], dsum)

    stacked = pl.BlockSpec((2, 4 * TM, BLOCK), lambda i: (0, i, 0))
    return pl.pallas_call(
        body, name="attn_dprep", grid=(t // TM,),
        in_specs=[pl.BlockSpec((TM, 256), lambda i: (i, 1)), pl.BlockSpec((TM, 256), lambda i: (i, 2)), stacked, stacked],
        out_specs=[stacked, stacked],
        out_shape=[jax.ShapeDtypeStruct((2, 4 * t, BLOCK), MXU_DTYPE), jax.ShapeDtypeStruct((2, 4 * t, BLOCK), F32)],
        scratch_shapes=[pltpu.VMEM((2, 4 * TM, BLOCK), F32)],
        compiler_params=_cparams("parallel"),
    )(dycat, dycat, ost, lst)


def _attn_prep_bwd(dqst, dk, dv, cos, sin):
    t = dk.shape[0]
    nblk = TM // BLOCK

    def body(dq_ref, dk_ref, dv_ref, cos_ref, sin_ref, dz_ref):
        cs, sn = cos_ref[...], sin_ref[...]

        def rope_bwd(d):
            return d * cs + _swap_halves(d * sn)

        for tt, tile in enumerate(_unstack_heads(dq_ref, nblk)):
            dz_ref[:, tt * BLOCK:(tt + 1) * BLOCK] = _mx(rope_bwd(tile * (HEAD_DIM ** -0.5)))
        dz_ref[:, 4 * BLOCK:5 * BLOCK] = _mx(rope_bwd(dk_ref[...]))
        dz_ref[:, 5 * BLOCK:6 * BLOCK] = _mx(dv_ref[...])

    def col(width):
        return pl.BlockSpec((TM, width), lambda i: (i, 0))

    return pl.pallas_call(
        body, name="attn_prep_bwd", grid=(t // TM,),
        in_specs=[pl.BlockSpec((2, 4 * TM, BLOCK), lambda i: (0, i, 0)), col(128), col(128), col(128), col(128)],
        out_specs=col(768), out_shape=jax.ShapeDtypeStruct((t, 768), MXU_DTYPE),
        compiler_params=_cparams("parallel"),
    )(dqst, dk, dv, cos, sin)


def _nbr_specs(nb, width, col):
    return [pl.BlockSpec((BLOCK, width), lambda n: (jnp.maximum(n - 1, 0), col)),
            pl.BlockSpec((BLOCK, width), lambda n: (n, col)),
            pl.BlockSpec((BLOCK, width), lambda n: (jnp.minimum(n + 1, nb - 1), col))]


def _stacked_specs(nb):
    return [pl.BlockSpec((2, GROUP_ROWS, BLOCK), lambda n: (0, jnp.maximum(n - 1, 0), 0)),
            pl.BlockSpec((2, GROUP_ROWS, BLOCK), lambda n: (0, n, 0)),
            pl.BlockSpec((2, GROUP_ROWS, BLOCK), lambda n: (0, jnp.minimum(n + 1, nb - 1), 0))]


def _query_index():
    row = lax.broadcasted_iota(jnp.int32, (GROUP_ROWS, BLOCK), 0)
    return row & (BLOCK - 1), lax.broadcasted_iota(jnp.int32, (GROUP_ROWS, BLOCK), 1)


def _sink_column(sink_ref, g):
    band = lax.broadcasted_iota(jnp.int32, (GROUP_ROWS, 1), 0) // BLOCK
    col = jnp.zeros((GROUP_ROWS, 1), F32) + sink_ref[4 * g]
    for s in range(1, 4):
        col = jnp.where(band == s, sink_ref[4 * g + s], col)
    return col


def _attn_fwd(qst, kr, vb, sink):
    t = kr.shape[0]
    nb = t // BLOCK

    def body(q_ref, kp_ref, kc_ref, kn_ref, vp_ref, vc_ref, vn_ref, sink_ref, o_ref, ost_ref, lst_ref):
        n = pl.program_id(0)
        qi, kj = _query_index()
        m_prev, m_next = (kj >= qi) & (n > 0), (kj <= qi) & (n < nb - 1)
        nat = [None] * 4
        for g in range(2):
            q = q_ref[g]
            sp = jnp.where(m_prev, _dot_nt(q, kp_ref[...]), NEG)
            sc = _dot_nt(q, kc_ref[...])
            sn = jnp.where(m_next, _dot_nt(q, kn_ref[...]), NEG)
            sk = _sink_column(sink_ref, g)
            m = jnp.maximum(jnp.max(jnp.maximum(jnp.maximum(sp, sc), sn), axis=-1, keepdims=True), sk)
            pp, pc, pn = jnp.exp(sp - m), jnp.exp(sc - m), jnp.exp(sn - m)
            den = jnp.sum(pp + pc + pn, axis=-1, keepdims=True) + jnp.exp(sk - m)
            o = (_dot(_mx(pp), vp_ref[...]) + _dot(_mx(pc), vc_ref[...]) + _dot(_mx(pn), vn_ref[...])) / den
            o = jnp.where(_half(o.shape, g), o, 0.0)
            ost_ref[g] = o
            lst_ref[g] = jnp.broadcast_to(m + jnp.log(den), (GROUP_ROWS, BLOCK))
            for s in range(4):
                tt, slot = 2 * g + s // 2, s % 2
                piece = o[BLOCK * s:BLOCK * (s + 1)]
                if slot != g:
                    piece = pltpu.roll(piece, HEAD_DIM, 1)
                nat[tt] = piece if nat[tt] is None else nat[tt] + piece
        for tt in range(4):
            o_ref[:, tt * BLOCK:(tt + 1) * BLOCK] = _mx(nat[tt])

    stacked = pl.BlockSpec((2, GROUP_ROWS, BLOCK), lambda n: (0, n, 0))
    return pl.pallas_call(
        body, name="attn_fwd", grid=(nb,),
        in_specs=[stacked] + _nbr_specs(nb, BLOCK, 0) + _nbr_specs(nb, BLOCK, 0) + [pl.BlockSpec(memory_space=pltpu.SMEM)],
        out_specs=[pl.BlockSpec((BLOCK, 512), lambda n: (n, 0)), stacked, stacked],
        out_shape=[jax.ShapeDtypeStruct((t, 512), MXU_DTYPE), jax.ShapeDtypeStruct((2, 4 * t, BLOCK), F32),
                   jax.ShapeDtypeStruct((2, 4 * t, BLOCK), F32)],
        compiler_params=_cparams("parallel"),
    )(qst, kr, kr, kr, vb, vb, vb, sink)


def _lse_and_dsum(ld):
    return ld[:, 0:1], pltpu.roll(ld, HEAD_DIM, 1)[:, 0:1]


def _attn_bwd_q(qst, kr, vb, dost, ld, sink):
    t = kr.shape[0]
    nb = t // BLOCK

    def body(q_ref, kp_ref, kc_ref, kn_ref, vp_ref, vc_ref, vn_ref, do_ref, ld_ref, sink_ref, dq_ref, ds_ref):
        n = pl.program_id(0)

        @pl.when(n == 0)
        def _():
            ds_ref[...] = jnp.zeros_like(ds_ref)

        qi, kj = _query_index()
        m_prev, m_next = (kj >= qi) & (n > 0), (kj <= qi) & (n < nb - 1)
        for g in range(2):
            q, do = q_ref[g], do_ref[g]
            lse, dsum = _lse_and_dsum(ld_ref[g])
            acc = jnp.zeros((GROUP_ROWS, BLOCK), F32)
            for k_ref, v_ref, valid in ((kp_ref, vp_ref, m_prev), (kc_ref, vc_ref, None), (kn_ref, vn_ref, m_next)):
                sc = _dot_nt(q, k_ref[...])
                if valid is not None:
                    sc = jnp.where(valid, sc, NEG)
                p = jnp.exp(sc - lse)
                dsc = p * (_dot_nt(do, v_ref[...]) - dsum)
                acc += _dot(_mx(dsc), k_ref[...])
            dq_ref[g] = jnp.where(_half(acc.shape, g), acc, 0.0)
            dsk = jnp.exp(_sink_column(sink_ref, g) - lse) * dsum
            for s in range(4):
                h = 4 * g + s
                ds_ref[h:h + 1, :] -= jnp.sum(dsk[BLOCK * s:BLOCK * (s + 1)], axis=0, keepdims=True)

    stacked = pl.BlockSpec((2, GROUP_ROWS, BLOCK), lambda n: (0, n, 0))
    return pl.pallas_call(
        body, name="attn_bwd_q", grid=(nb,),
        in_specs=[stacked] + _nbr_specs(nb, BLOCK, 0) + _nbr_specs(nb, BLOCK, 0)
        + [stacked, stacked, pl.BlockSpec(memory_space=pltpu.SMEM)],
        out_specs=[stacked, pl.BlockSpec((8, BLOCK), lambda n: (0, 0))],
        out_shape=[jax.ShapeDtypeStruct((2, 4 * t, BLOCK), F32), jax.ShapeDtypeStruct((8, BLOCK), F32)],
        compiler_params=_cparams("arbitrary"),
    )(qst, kr, kr, kr, vb, vb, vb, dost, ld, sink)


def _attn_bwd_kv(qst, kr, vb, dost, ld):
    t = kr.shape[0]
    nb = t // BLOCK

    def body(*refs):
        q3, do3, ld3 = refs[0:3], refs[3:6], refs[6:9]
        k_ref, v_ref, dk_ref, dv_ref = refs[9:13]
        j = pl.program_id(0)
        qi, kj = _query_index()
        valid = ((kj <= qi) & (j > 0), None, (kj >= qi) & (j < nb - 1))
        k, v = k_ref[...], v_ref[...]
        dk = jnp.zeros((BLOCK, BLOCK), F32)
        dv = jnp.zeros((BLOCK, BLOCK), F32)
        for g in range(2):
            for b in range(3):
                q, do = q3[b][g], do3[b][g]
                lse, dsum = _lse_and_dsum(ld3[b][g])
                sc = _dot_nt(q, k)
                if valid[b] is not None:
                    sc = jnp.where(valid[b], sc, NEG)
                p = jnp.exp(sc - lse)
                dsc = p * (_dot_nt(do, v) - dsum)
                dv += _dot_tn(_mx(p), do)
                dk += _dot_tn(_mx(dsc), q)
        dk_ref[...] = dk
        dv_ref[...] = dv

    cur = pl.BlockSpec((BLOCK, BLOCK), lambda n: (n, 0))
    return pl.pallas_call(
        body, name="attn_bwd_kv", grid=(nb,),
        in_specs=_stacked_specs(nb) * 3 + [cur, cur],
        out_specs=[cur, cur], out_shape=[jax.ShapeDtypeStruct((t, BLOCK), F32)] * 2,
        compiler_params=_cparams("parallel"),
    )(*([qst] * 3), *([dost] * 3), *([ld] * 3), kr, vb)


def _loss_head(y, target):
    t = y.shape[0]

    def body(y_ref, t_ref, l_ref, dy_ref):
        @pl.when(pl.program_id(0) == 0)
        def _():
            l_ref[...] = jnp.zeros_like(l_ref)

        e = y_ref[...] - t_ref[...]
        dy_ref[...] = e / D_MODEL
        l_ref[...] += 0.5 * jnp.sum(_mean(e * e))

    row = pl.BlockSpec((TM, D_MODEL), lambda i: (i, 0))
    return pl.pallas_call(
        body, name="loss_head", grid=(t // TM,), in_specs=[row, row],
        out_specs=[pl.BlockSpec((8, 128), lambda i: (0, 0)), row],
        out_shape=[jax.ShapeDtypeStruct((8, 128), F32), jax.ShapeDtypeStruct((t, D_MODEL), F32)],
        compiler_params=_cparams("arbitrary"),
    )(y, target)


def _adamw(name, w, g, m, v, rows):
    n, width = w.shape

    def body(w_ref, g_ref, m_ref, v_ref, d_ref, nm_ref, nv_ref):
        g = g_ref[...]
        m = ADAM_B1 * m_ref[...] + (1.0 - ADAM_B1) * g
        v = ADAM_B2 * v_ref[...] + (1.0 - ADAM_B2) * jnp.square(g)
        m_hat = m / (1.0 - ADAM_B1 ** ADAM_STEP)
        v_hat = v / (1.0 - ADAM_B2 ** ADAM_STEP)
        d_ref[...] = -ADAM_LR * (m_hat / (jnp.sqrt(v_hat) + ADAM_EPS) + ADAM_WD * w_ref[...])
        nm_ref[...] = m
        nv_ref[...] = v

    spec = pl.BlockSpec((rows, width), lambda i: (i, 0))
    return pl.pallas_call(
        body, name=name, grid=(n // rows,), in_specs=[spec] * 4, out_specs=[spec] * 3,
        out_shape=[jax.ShapeDtypeStruct((n, width), F32)] * 3, compiler_params=_cparams("parallel"),
    )(w, g, m, v)


def _place():
    x, y, c = lax.axis_index("x"), lax.axis_index("y"), lax.axis_index("c")
    chips = [(1 - x, y), (x, 1 - y), (1 - x, 1 - y)]
    return x, y, c, chips


def _gather_weights(shards):
    na = len(shards)

    def body(*refs):
        src, dst = refs[:na], refs[na:2 * na]
        send, recv, fsend, frecv, lsem = refs[2 * na:]
        x, y, c, chips = _place()
        mine = 2 * x + y
        local = []
        for a in range(na):
            for l in range(2):
                cp = pltpu.make_async_copy(src[a].at[l], dst[a].at[l, mine], lsem.at[2 * a + l])
                cp.start()
                local.append(cp)

        def ici(a, k, shard, to):
            return pltpu.make_async_remote_copy(
                src_ref=src[a].at[c], dst_ref=dst[a].at[c, shard], send_sem=send.at[3 * a + k], recv_sem=recv.at[3 * a + k],
                device_id=to, device_id_type=MESH)

        def d2d(a, k, layer, shard):
            return pltpu.make_async_remote_copy(
                src_ref=dst[a].at[layer, shard], dst_ref=dst[a].at[layer, shard], send_sem=fsend.at[3 * a + k],
                recv_sem=frecv.at[3 * a + k], device_id=(x, y, 1 - c), device_id_type=MESH)

        sends = []
        for a in range(na):
            for k, (cx, cy) in enumerate(chips):
                cp = ici(a, k, mine, (cx, cy, c))
                cp.start()
                sends.append(cp)
        for a in range(na):
            for k, (cx, cy) in enumerate(chips):
                ici(a, k, 2 * cx + cy, (cx, cy, c)).wait_recv()
                cp = d2d(a, k, c, 2 * cx + cy)
                cp.start()
                sends.append(cp)
        for a in range(na):
            for k, (cx, cy) in enumerate(chips):
                d2d(a, k, 1 - c, 2 * cx + cy).wait_recv()
        for cp in sends:
            cp.wait_send()
        for cp in local:
            cp.wait()

    return pl.pallas_call(
        body, name="gather_weights", in_specs=[ANY] * na, out_specs=[ANY] * na,
        out_shape=[jax.ShapeDtypeStruct((2, N_SHARD) + s.shape[1:], s.dtype) for s in shards],
        scratch_shapes=[pltpu.SemaphoreType.DMA((3 * na,))] * 4 + [pltpu.SemaphoreType.DMA((2 * na,))],
    )(*shards)


def _pair_exchange(parts):
    na = len(parts)

    def body(*refs):
        src, dst = refs[:na], refs[na:2 * na]
        send, recv = refs[2 * na:]
        x, y, c, _ = _place()
        cps = []
        for a in range(na):
            cp = pltpu.make_async_remote_copy(
                src_ref=src[a].at[1 - c], dst_ref=dst[a], send_sem=send.at[a], recv_sem=recv.at[a],
                device_id=(x, y, 1 - c), device_id_type=MESH)
            cp.start()
            cps.append(cp)
        for cp in cps:
            cp.wait()

    return pl.pallas_call(
        body, name="pair_exchange", in_specs=[ANY] * na, out_specs=[ANY] * na,
        out_shape=[jax.ShapeDtypeStruct(p.shape[1:], p.dtype) for p in parts],
        scratch_shapes=[pltpu.SemaphoreType.DMA((na,))] * 2,
    )(*parts)


def _chip_exchange(sums):
    na = len(sums)

    def body(*refs):
        src, dst = refs[:na], refs[na:2 * na]
        send, recv = refs[2 * na:]
        x, y, c, chips = _place()
        cps = []
        for a in range(na):
            for k, (cx, cy) in enumerate(chips):
                cp = pltpu.make_async_remote_copy(
                    src_ref=src[a].at[2 * cx + cy], dst_ref=dst[a].at[k], send_sem=send.at[3 * a + k],
                    recv_sem=recv.at[3 * a + k], device_id=(cx, cy, c), device_id_type=MESH)
                cp.start()
                cps.append(cp)
        for cp in cps:
            cp.wait()

    return pl.pallas_call(
        body, name="chip_exchange", in_specs=[ANY] * na, out_specs=[ANY] * na,
        out_shape=[jax.ShapeDtypeStruct((3,) + s.shape[1:], s.dtype) for s in sums],
        scratch_shapes=[pltpu.SemaphoreType.DMA((3 * na,))] * 2,
    )(*sums)


def _pair_share(halves):
    na = len(halves)

    def body(*refs):
        dst = refs[na:2 * na]
        send, recv = refs[2 * na:]
        x, y, c, _ = _place()
        cps = []
        for a in range(na):
            cp = pltpu.make_async_remote_copy(
                src_ref=dst[a].at[c], dst_ref=dst[a].at[c], send_sem=send.at[a], recv_sem=recv.at[a],
                device_id=(x, y, 1 - c), device_id_type=MESH)
            cp.start()
            cps.append(cp)
        for a in range(na):
            cps[a].wait_send()
            pltpu.make_async_remote_copy(
                src_ref=dst[a].at[1 - c], dst_ref=dst[a].at[1 - c], send_sem=send.at[a], recv_sem=recv.at[a],
                device_id=(x, y, 1 - c), device_id_type=MESH).wait_recv()

    return pl.pallas_call(
        body, name="pair_share", in_specs=[ANY] * na, out_specs=[ANY] * na,
        out_shape=[jax.ShapeDtypeStruct(h.shape, h.dtype) for h in halves],
        input_output_aliases={a: a for a in range(na)},
        scratch_shapes=[pltpu.SemaphoreType.DMA((na,))] * 2,
    )(*halves)


def _pair_sum(name, part, got, rows):
    _, _, r, w = part.shape
    c = lax.axis_index("c").astype(jnp.int32).reshape(1)

    def body(c_ref, p_ref, g_ref, o_ref):
        o_ref[...] = _mx(p_ref[...] + g_ref[...])

    spec = pl.BlockSpec((None, rows, w), lambda j, i, c_ref: (j, i, 0))
    return pl.pallas_call(
        body, name=name, out_shape=jax.ShapeDtypeStruct((N_SHARD, r, w), MXU_DTYPE),
        grid_spec=pltpu.PrefetchScalarGridSpec(
            num_scalar_prefetch=1, grid=(N_SHARD, r // rows),
            in_specs=[pl.BlockSpec((None, None, rows, w), lambda j, i, c_ref: (c_ref[0], j, i, 0)), spec],
            out_specs=spec),
        compiler_params=_cparams("parallel", "parallel"),
    )(c, part, got)


def _chip_sum(name, part, got, others, rows):
    _, _, r, w = part.shape
    cj = jnp.stack([lax.axis_index("c"), 2 * lax.axis_index("x") + lax.axis_index("y")]).astype(jnp.int32)

    def body(cj_ref, p_ref, g_ref, o_ref, out_ref):
        acc = p_ref[...] + g_ref[...]
        for k in range(3):
            acc += o_ref[k].astype(F32)
        out_ref[...] = acc

    return pl.pallas_call(
        body, name=name, out_shape=jax.ShapeDtypeStruct((2, r, w), F32),
        grid_spec=pltpu.PrefetchScalarGridSpec(
            num_scalar_prefetch=1, grid=(r // rows,),
            in_specs=[pl.BlockSpec((None, None, rows, w), lambda i, cj: (cj[0], cj[1], i, 0)),
                      pl.BlockSpec((None, rows, w), lambda i, cj: (cj[1], i, 0)),
                      pl.BlockSpec((3, rows, w), lambda i, cj: (0, i, 0))],
            out_specs=pl.BlockSpec((None, rows, w), lambda i, cj: (cj[0], i, 0))),
        compiler_params=_cparams("parallel"),
    )(cj, part, got, others)


SMALL_ROWS = 40


def _sum_small(part):
    def body(p_ref, o_ref, land, send, recv):
        x, y, c, _ = _place()
        me = 4 * x + 2 * y + c
        cps = []
        for r in range(1, 8):
            cp = pltpu.make_async_remote_copy(
                src_ref=p_ref, dst_ref=land.at[r], send_sem=send.at[r], recv_sem=recv.at[r],
                device_id=(x ^ (r >> 2), y ^ ((r >> 1) & 1), c ^ (r & 1)), device_id_type=MESH)
            cp.start()
            cps.append(cp)
        land[0] = p_ref[...]
        for cp in cps:
            cp.wait()
        acc = land[me]
        for e in range(1, 8):
            acc += land[me ^ e]
        o_ref[...] = acc

    return pl.pallas_call(
        body, name="sum_small", in_specs=[pl.BlockSpec(memory_space=pltpu.VMEM)],
        out_specs=pl.BlockSpec(memory_space=pltpu.VMEM), out_shape=jax.ShapeDtypeStruct(part.shape, F32),
        scratch_shapes=[pltpu.VMEM((8,) + part.shape, F32), pltpu.SemaphoreType.DMA((8,)), pltpu.SemaphoreType.DMA((8,))],
    )(part)


BIG = ("ffn1_w_gu", "ffn1_w_down", "w_in", "w_out", "ffn2_w_gu", "ffn2_w_down")
SMALL = ("ln1_g", "ln1_b", "ln2_g", "ln2_b", "ln3_g", "ln3_b", "attn_sink", "cc_conv_b", "cc_ln_g", "cc_ln_b",
         "sc_conv_w", "cc_conv_w")
NAMES = ("ffn1_w_gu", "ffn1_w_down", "ln1_g", "ln1_b", "w_in", "sc_conv_w", "attn_sink", "cc_conv_w", "cc_conv_b",
         "cc_ln_g", "cc_ln_b", "w_out", "ln2_g", "ln2_b", "ffn2_w_gu", "ffn2_w_down", "ln3_g", "ln3_b")
SUM_ROWS = {"ffn1_w_gu": 256, "ffn1_w_down": 352, "w_in": 256, "w_out": 256, "ffn2_w_gu": 256, "ffn2_w_down": 352}


def _rope_tables(t):
    half = HEAD_DIM // 2
    inv_freq = ROPE_THETA ** (-jnp.arange(half, dtype=F32) / half)
    ang = jnp.arange(t).astype(F32)[:, None] * inv_freq[None, :]
    cos, sin = jnp.cos(ang), jnp.sin(ang)
    return jnp.tile(jnp.concatenate([cos, cos], axis=1), (1, 2)), jnp.tile(jnp.concatenate([-sin, sin], axis=1), (1, 2))


def _pack_small(vals):
    flat = jnp.concatenate([vals[n].reshape(-1) for n in SMALL])
    return jnp.pad(flat, (0, SMALL_ROWS * D_MODEL - flat.shape[0])).reshape(SMALL_ROWS, D_MODEL)


def _unpack_small(packed, shapes):
    flat, out, at = packed.reshape(-1), {}, 0
    for n in SMALL:
        size = int(np.prod(shapes[n]))
        out[n] = flat[at:at + size].reshape(shapes[n])
        at += size
    return out


def kernel(x, ffn1_w_gu, ffn1_w_down, ln1_g, ln1_b, w_in, sc_conv_w, attn_sink, cc_conv_w, cc_conv_b, cc_ln_g, cc_ln_b, w_out, ln2_g, ln2_b, ffn2_w_gu, ffn2_w_down, ln3_g, ln3_b, loss_target, m_ffn1_w_gu, m_ffn1_w_down, m_ln1_g, m_ln1_b, m_w_in, m_sc_conv_w, m_attn_sink, m_cc_conv_w, m_cc_conv_b, m_cc_ln_g, m_cc_ln_b, m_w_out, m_ln2_g, m_ln2_b, m_ffn2_w_gu, m_ffn2_w_down, m_ln3_g, m_ln3_b, v_ffn1_w_gu, v_ffn1_w_down, v_ln1_g, v_ln1_b, v_w_in, v_sc_conv_w, v_attn_sink, v_cc_conv_w, v_cc_conv_b, v_cc_ln_g, v_cc_ln_b, v_w_out, v_ln2_g, v_ln2_b, v_ffn2_w_gu, v_ffn2_w_down, v_ln3_g, v_ln3_b):
    given = dict(locals())
    w = {n: given[n] for n in NAMES}
    mom = {n: given["m_" + n] for n in NAMES}
    var = {n: given["v_" + n] for n in NAMES}
    x0 = x[0]
    target = loss_target[0]
    t = x0.shape[0]
    chip = 2 * lax.axis_index("x") + lax.axis_index("y")

    conv_shard = jnp.pad(jnp.concatenate([sc_conv_w, cc_conv_w], axis=1), ((0, 0), (0, 6), (0, 64)))
    full = _gather_weights([_mx(w[n]) for n in BIG] + [conv_shard])
    wgu1, wd1, win, wout, wgu2, wd2, conv_full = full
    wout = wout.reshape(2, D_MODEL, D_MODEL)
    wd1, wd2 = wd1.reshape(2, D_FF, D_MODEL), wd2.reshape(2, D_FF, D_MODEL)
    conv_full = jnp.transpose(conv_full[:, :, :SC_W + CC_W, :64], (0, 2, 1, 3)).reshape(2, SC_W + CC_W, D_CONV)
    sc_full, cc_full = conv_full[:, :SC_W], conv_full[:, SC_W:]
    cos, sin = _rope_tables(t)

    def vec(a, l):
        return a[l][None, :]

    acts = []
    h = x0
    for l in range(2):
        y1, r1, gu1 = _ffn_fwd(h, wgu1, wd1, vec(ln1_g, l), vec(ln1_b, l), l)
        z = _in_proj(y1, win, l)
        ysc, ycc, u2 = _conv_fwd(z, sc_full[l], cc_full[l], vec(cc_conv_b, l), vec(cc_ln_g, l), vec(cc_ln_b, l))
        qs, kf, vf = _attn_prep(z, cos, sin)
        o_nat, o, lse = _attn_fwd(qs, kf, vf, attn_sink[l])
        ycat = jnp.concatenate([ysc, o_nat, ycc], axis=1)
        y2, r2 = _out_proj(ycat, y1, wout, vec(ln2_g, l), vec(ln2_b, l), l)
        y3, r3, gu2 = _ffn_fwd(y2, wgu2, wd2, vec(ln3_g, l), vec(ln3_b, l), l)
        acts.append(dict(x=h, y1=y1, r1=r1, gu1=gu1, z=z, u2=u2, qs=qs, kf=kf, vf=vf, o=o, lse=lse, ycat=ycat,
                         y2=y2, r2=r2, gu2=gu2, r3=r3))
        h = y3
    loss_rows, dy = _loss_head(h, target)
    loss = lax.psum(loss_rows[0, 0], ("x", "y", "c"))

    part = {n: None for n in BIG}
    small = [None, None]

    def ffn_grads(which, dy, r, gu, xin, wgu, wd, gamma, l):
        dx, dh, a, do, dgb = _ffn_bwd(dy, r, gu, wgu, wd, gamma, l)
        part[which + "_w_gu"] = _mm_tn(
            which + "_dwgu", xin, dh, pl.BlockSpec((TM, D_MODEL), lambda n, k: (k, 0)),
            pl.BlockSpec((None, TM, FF_CHUNK), lambda n, k: (n // N_CHUNK, k, n % N_CHUNK)),
            pl.BlockSpec((None, None, D_MODEL, FF_CHUNK), lambda n, k: (l, n, 0, 0)),
            (2, N_SHARD, D_MODEL, GU_SHARD), (D_MODEL, FF_CHUNK), (2 * N_CHUNK, t // TM), part[which + "_w_gu"])
        part[which + "_w_down"] = _mm_tn(
            which + "_dwd", a, do, pl.BlockSpec((TM, FF_CHUNK), lambda n, k: (k, n)),
            pl.BlockSpec((TM, D_MODEL), lambda n, k: (k, 0)),
            pl.BlockSpec((None, FF_CHUNK, D_MODEL), lambda n, k: (l, n, 0)),
            (2, D_FF, D_MODEL), (FF_CHUNK, D_MODEL), (N_CHUNK, t // TM), part[which + "_w_down"])
        return dx, dgb

    for l in (1, 0):
        s = acts[l]
        dy, dgb3 = ffn_grads("ffn2", dy, s["r3"], s["gu2"], s["y2"], wgu2, wd2, vec(ln3_g, l), l)
        res, dm, dycat, dgb2 = _out_proj_bwd(dy, s["r2"], wout, vec(ln2_g, l), l)
        part["w_out"] = _mm_tn(
            "dwout", s["ycat"], dm, pl.BlockSpec((TM, D_MODEL), lambda n, k: (k, 0)),
            pl.BlockSpec((TM, D_MODEL), lambda n, k: (k, 0)),
            pl.BlockSpec((None, D_MODEL, D_MODEL), lambda n, k: (l, 0, 0)),
            (2, D_MODEL, D_MODEL), (D_MODEL, D_MODEL), (1, t // TM), part["w_out"])
        dz_sc, dz_cc, dconv = _conv_bwd(s["z"], dycat, s["u2"], sc_full[l], cc_full[l], vec(cc_ln_g, l), vec(cc_ln_b, l))
        dost, ld = _attn_dprep(dycat, s["o"], s["lse"])
        dqs, dsink = _attn_bwd_q(s["qs"], s["kf"], s["vf"], dost, ld, attn_sink[l])
        dkf, dvf = _attn_bwd_kv(s["qs"], s["kf"], s["vf"], dost, ld)
        dz_att = _attn_prep_bwd(dqs, dkf, dvf, cos, sin)
        dz = jnp.concatenate([dz_sc, dz_att, dz_cc], axis=1)
        part["w_in"] = _mm_tn(
            "dwin", s["y1"], dz, pl.BlockSpec((TM, D_MODEL), lambda n, k: (k, 0)),
            pl.BlockSpec((TM, D_IN), lambda n, k: (k, 0)),
            pl.BlockSpec((None, N_SHARD, D_MODEL, IN_SHARD), lambda n, k: (l, 0, 0, 0)),
            (2, N_SHARD, D_MODEL, IN_SHARD), (D_MODEL, D_IN), (1, t // TM), part["w_in"], split=N_SHARD)
        dy = _in_proj_bwd(dz, res, win, l)
        dy, dgb1 = ffn_grads("ffn1", dy, s["r1"], s["gu1"], s["x"], wgu1, wd1, vec(ln1_g, l), l)
        small[l] = dict(ln1_g=dgb1[0], ln1_b=dgb1[1], ln2_g=dgb2[0], ln2_b=dgb2[1], ln3_g=dgb3[0], ln3_b=dgb3[1],
                        attn_sink=dsink[:, 0], cc_conv_b=dconv[ROW_CCB], cc_ln_g=dconv[ROW_CCG],
                        cc_ln_b=dconv[ROW_CCBETA], sc_conv_w=dconv[ROW_SCW:ROW_SCW + SC_W],
                        cc_conv_w=dconv[ROW_CCW:ROW_CCW + CC_W])
    grad_x = dy[None]

    for n in ("ffn1_w_down", "ffn2_w_down"):
        part[n] = part[n].reshape(2, N_SHARD, D_FF // N_SHARD, D_MODEL)
    part["w_out"] = part["w_out"].reshape(2, N_SHARD, OUT_SHARD, D_MODEL)
    parts = [part[n] for n in BIG]
    got = _pair_exchange(parts)
    sums = [_pair_sum("pair_sum_" + n, p, g, SUM_ROWS[n]) for n, p, g in zip(BIG, parts, got)]
    others = _chip_exchange(sums)
    halves = [_chip_sum("chip_sum_" + n, p, g, o, SUM_ROWS[n]) for n, p, g, o in zip(BIG, parts, got, others)]
    grads = dict(zip(BIG, _pair_share(halves)))
    for n in BIG:
        grads[n] = grads[n].reshape(w[n].shape)

    small_full = {n: jnp.stack([small[0][n], small[1][n]]) for n in SMALL}
    small_sum = _unpack_small(_sum_small(_pack_small(small_full)), {n: small_full[n].shape for n in SMALL})
    for n in SMALL:
        g = small_sum[n]
        if n in ("sc_conv_w", "cc_conv_w"):
            g = lax.dynamic_slice_in_dim(g, chip * 64, 64, axis=2)
        grads[n] = g

    delta, new_m, new_v = {}, {}, {}
    for n in BIG:
        shape = w[n].shape
        two_d = (shape[0] * shape[1], shape[2])
        outs = _adamw("adamw_" + n, w[n].reshape(two_d), grads[n].reshape(two_d), mom[n].reshape(two_d),
                      var[n].reshape(two_d), 128)
        delta[n], new_m[n], new_v[n] = [a.reshape(shape) for a in outs]
    shapes = {n: w[n].shape for n in SMALL}
    outs = _adamw("adamw_small", _pack_small({n: w[n] for n in SMALL}), _pack_small({n: grads[n] for n in SMALL}),
                  _pack_small({n: mom[n] for n in SMALL}), _pack_small({n: var[n] for n in SMALL}), 8)
    for d, packed in zip((delta, new_m, new_v), outs):
        d.update(_unpack_small(packed, shapes))

    return (loss, grad_x, *[grads[n] for n in NAMES], *[delta[n] for n in NAMES], *[new_m[n] for n in NAMES],
            *[new_v[n] for n in NAMES])
```

```python
import functools

import numpy as np
import jax
import jax.numpy as jnp
from jax import lax
from jax.experimental import pallas as pl
from jax.experimental.pallas import tpu as pltpu

F32 = jnp.float32
MXU_DTYPE = jnp.bfloat16

D_MODEL = 1024
D_FF = 2816
N_SHARD = 4
D_IN = 2048
GU_SHARD = 2 * D_FF // N_SHARD
FF_CHUNK = GU_SHARD
N_CHUNK = D_FF // FF_CHUNK
IN_SHARD = D_IN // N_SHARD
OUT_SHARD = D_MODEL // N_SHARD
HEAD_DIM = 64
N_Q_HEADS = 8
BLOCK = 128
SC_W = 3
CC_W = 31
D_CONV = 256
HALO = 16
LN_EPS = 1e-5
ALPHA = (2.0 * 2) ** 0.25
NEG = -1e30
ROPE_THETA = 10000.0
ADAM_LR, ADAM_B1, ADAM_B2, ADAM_EPS, ADAM_WD, ADAM_STEP = 0.001, 0.9, 0.999, 1e-08, 0.01, 10

TM = 512
TMC = 256
VMEM_LIMIT = 56 * 1024 * 1024
MESH = pl.DeviceIdType.MESH
ANY = pl.BlockSpec(memory_space=pl.ANY)


def _cparams(*sem):
    return pltpu.CompilerParams(dimension_semantics=sem, vmem_limit_bytes=VMEM_LIMIT)


def _dot(a, b):
    return jnp.dot(a, b, preferred_element_type=F32)


def _dot_nt(a, b):
    return lax.dot_general(a, b, (((1,), (1,)), ((), ())), preferred_element_type=F32)


def _dot_tn(a, b):
    return lax.dot_general(a, b, (((0,), (0,)), ((), ())), preferred_element_type=F32)


def _mx(a):
    return a.astype(MXU_DTYPE)


def _mean(a):
    return jnp.mean(a, axis=-1, keepdims=True)


def _ln_stats(r):
    xc = r - _mean(r)
    rstd = lax.rsqrt(_mean(xc * xc) + LN_EPS)
    return xc * rstd, rstd


def _ln_bwd(dy, xh, rstd, gamma):
    dxh = dy * gamma
    return rstd * (dxh - _mean(dxh) - xh * _mean(dxh * xh))


def _colsum(a):
    return jnp.sum(a, axis=0, keepdims=True)


def _sigmoid(a):
    return 1.0 / (1.0 + jnp.exp(-a))


def _call(body, args, *, name, grid, in_specs, out_specs, out_shape, scratch, sem, rider=None):
    if rider is None:
        outs = pl.pallas_call(
            body, name=name, grid=grid, in_specs=in_specs, out_specs=out_specs, out_shape=out_shape,
            scratch_shapes=scratch, compiler_params=_cparams(*sem))(*args)
        return list(outs), []
    n_in, n_out, n_sc = len(in_specs), len(out_specs), len(scratch)
    r_in, r_out = len(rider.ins), len(rider.outs)

    def carrying(*refs):
        cuts = np.cumsum([0, n_in, r_in, n_out, r_out, n_sc])
        ins, rins, outs, routs, scr = [refs[a:b] for a, b in zip(cuts[:-1], cuts[1:])]
        rsems = refs[cuts[-1]:]
        first = functools.reduce(jnp.logical_and, [pl.program_id(d) == 0 for d in range(len(grid))])
        last = functools.reduce(jnp.logical_and, [pl.program_id(d) == grid[d] - 1 for d in range(len(grid))])

        @pl.when(first)
        def _():
            rider.start(rins, routs, rsems)

        body(*ins, *outs, *scr)

        @pl.when(last)
        def _():
            rider.finish(rins, routs, rsems)

    outs = pl.pallas_call(
        carrying, name=name, grid=grid, in_specs=list(in_specs) + [ANY] * r_in,
        out_specs=list(out_specs) + [ANY] * r_out, out_shape=list(out_shape) + list(rider.outs),
        scratch_shapes=list(scratch) + list(rider.sems), compiler_params=_cparams(*(("arbitrary",) * len(grid))),
    )(*args, *rider.ins)
    return list(outs[:n_out]), list(outs[n_out:])


def _ffn_fwd(name, x, wgu, wd, gamma, beta, l, rider=None):
    t = x.shape[0]
    nc = N_CHUNK

    def body(x_ref, wg_ref, wu_ref, wd_ref, g_ref, b_ref, y_ref, r_ref, gu_ref, xb_s, acc_s):
        c = pl.program_id(1)

        @pl.when(c == 0)
        def _():
            xb_s[...] = _mx(x_ref[...])
            acc_s[...] = jnp.zeros_like(acc_s)

        xb = xb_s[...]
        hg = _dot(xb, wg_ref[...])
        hu = _dot(xb, wu_ref[...])
        gu_ref[0] = _mx(hg)
        gu_ref[1] = _mx(hu)
        a = (hg * _sigmoid(hg)) * hu
        acc_s[...] += _dot(_mx(a), wd_ref[...])

        @pl.when(c == nc - 1)
        def _():
            r = ALPHA * x_ref[...] + 0.5 * acc_s[...]
            xh, _ = _ln_stats(r)
            r_ref[...] = r
            y_ref[...] = xh * g_ref[...] + b_ref[...]

    row = pl.BlockSpec((TM, D_MODEL), lambda i, c: (i, 0))
    vec = pl.BlockSpec((1, D_MODEL), lambda i, c: (0, 0))
    return _call(
        body, (x, wgu, wgu, wd, gamma, beta), name=name, grid=(t // TM, nc),
        in_specs=[row,
                  pl.BlockSpec((None, None, D_MODEL, FF_CHUNK), lambda i, c: (l, c, 0, 0)),
                  pl.BlockSpec((None, None, D_MODEL, FF_CHUNK), lambda i, c: (l, N_CHUNK + c, 0, 0)),
                  pl.BlockSpec((None, FF_CHUNK, D_MODEL), lambda i, c: (l, c, 0)),
                  vec, vec],
        out_specs=[row, row, pl.BlockSpec((2, TM, FF_CHUNK), lambda i, c: (0, i, c))],
        out_shape=[jax.ShapeDtypeStruct((t, D_MODEL), F32), jax.ShapeDtypeStruct((t, D_MODEL), F32),
                   jax.ShapeDtypeStruct((2, t, D_FF), MXU_DTYPE)],
        scratch=[pltpu.VMEM((TM, D_MODEL), MXU_DTYPE), pltpu.VMEM((TM, D_MODEL), F32)],
        sem=("parallel", "arbitrary"), rider=rider)


def _ffn_bwd(dy, r, gu, wgu, wd, gamma, l):
    t = dy.shape[0]
    nc = N_CHUNK

    def norm_body(dy_ref, r_ref, g_ref, res_ref, do_ref, dgb_ref):
        @pl.when(pl.program_id(0) == 0)
        def _():
            dgb_ref[...] = jnp.zeros_like(dgb_ref)

        xh, rstd = _ln_stats(r_ref[...])
        dy = dy_ref[...]
        dr = _ln_bwd(dy, xh, rstd, g_ref[...])
        do_ref[...] = _mx(0.5 * dr)
        res_ref[...] = ALPHA * dr
        dgb_ref[0:1, :] += _colsum(dy * xh)
        dgb_ref[1:2, :] += _colsum(dy)

    row1 = pl.BlockSpec((TM, D_MODEL), lambda i: (i, 0))
    res, do, dgb = pl.pallas_call(
        norm_body, name="ffn_bwd_norm", grid=(t // TM,),
        in_specs=[row1, row1, pl.BlockSpec((1, D_MODEL), lambda i: (0, 0))],
        out_specs=[row1, row1, pl.BlockSpec((8, D_MODEL), lambda i: (0, 0))],
        out_shape=[jax.ShapeDtypeStruct((t, D_MODEL), F32), jax.ShapeDtypeStruct((t, D_MODEL), MXU_DTYPE),
                   jax.ShapeDtypeStruct((8, D_MODEL), F32)],
        compiler_params=_cparams("arbitrary"),
    )(dy, r, gamma)

    def hidden_body(do_ref, gu_ref, wd_ref, dh_ref, a_ref):
        da = _dot_nt(do_ref[...], wd_ref[...])
        g = gu_ref[0].astype(F32)
        u = gu_ref[1].astype(F32)
        s = _sigmoid(g)
        sil = g * s
        a_ref[...] = _mx(sil * u)
        dh_ref[0] = _mx(da * u * (s * (1.0 + g * (1.0 - s))))
        dh_ref[1] = _mx(da * sil)

    hid = pl.BlockSpec((2, TM, FF_CHUNK), lambda c, i: (0, i, c))
    dh, a = pl.pallas_call(
        hidden_body, name="ffn_bwd_hidden", grid=(nc, t // TM),
        in_specs=[pl.BlockSpec((TM, D_MODEL), lambda c, i: (i, 0)), hid,
                  pl.BlockSpec((None, FF_CHUNK, D_MODEL), lambda c, i: (l, c, 0))],
        out_specs=[hid, pl.BlockSpec((TM, FF_CHUNK), lambda c, i: (i, c))],
        out_shape=[jax.ShapeDtypeStruct((2, t, D_FF), MXU_DTYPE), jax.ShapeDtypeStruct((t, D_FF), MXU_DTYPE)],
        compiler_params=_cparams("parallel", "parallel"),
    )(do, gu, wd)

    def input_body(res_ref, dh_ref, w_ref, dx_ref):
        acc = res_ref[...]
        for j in range(N_SHARD):
            part = dh_ref[j // N_CHUNK][:, (j % N_CHUNK) * FF_CHUNK:(j % N_CHUNK + 1) * FF_CHUNK]
            acc += _dot_nt(part, w_ref[j])
        dx_ref[...] = acc

    dx = pl.pallas_call(
        input_body, name="ffn_bwd_input", grid=(t // TM,),
        in_specs=[row1, pl.BlockSpec((2, TM, D_FF), lambda i: (0, i, 0)),
                  pl.BlockSpec((None, N_SHARD, D_MODEL, GU_SHARD), lambda i: (l, 0, 0, 0))],
        out_specs=row1,
        out_shape=jax.ShapeDtypeStruct((t, D_MODEL), F32),
        compiler_params=_cparams("parallel"),
    )(res, dh, wgu)
    return dx, dh, a, do, dgb


def _mm_tn(name, a, b, a_spec, b_spec, out_spec, out_shape, acc_shape, grid, rider=None, split=1):
    nk = grid[-1]
    width = acc_shape[1] // split

    def body(*refs):
        a_ref, b_ref = refs[0], refs[1]
        o_ref, acc = refs[-2], refs[-1]
        k = pl.program_id(len(grid) - 1)

        @pl.when(k == 0)
        def _():
            acc[...] = jnp.zeros_like(acc)

        acc[...] += _dot_tn(_mx(a_ref[...]), _mx(b_ref[...]))

        @pl.when(k == nk - 1)
        def _():
            if split == 1:
                o_ref[...] = acc[...]
            else:
                for j in range(split):
                    o_ref[j] = acc[:, j * width:(j + 1) * width]

    sem = ("parallel",) * (len(grid) - 1) + ("arbitrary",)
    (out,), got = _call(
        body, (a, b), name=name, grid=grid, in_specs=[a_spec, b_spec], out_specs=[out_spec],
        out_shape=[jax.ShapeDtypeStruct(out_shape, F32)], scratch=[pltpu.VMEM(acc_shape, F32)], sem=sem, rider=rider)
    return out, got


def _in_proj(x, w_in, l):
    t = x.shape[0]

    def body(x_ref, w_ref, z_ref):
        xb = _mx(x_ref[...])
        for j in range(N_SHARD):
            z_ref[:, j * IN_SHARD:(j + 1) * IN_SHARD] = _dot(xb, w_ref[j])

    return pl.pallas_call(
        body, name="in_proj", grid=(t // TM,),
        in_specs=[pl.BlockSpec((TM, D_MODEL), lambda i: (i, 0)),
                  pl.BlockSpec((None, N_SHARD, D_MODEL, IN_SHARD), lambda i: (l, 0, 0, 0))],
        out_specs=pl.BlockSpec((TM, D_IN), lambda i: (i, 0)),
        out_shape=jax.ShapeDtypeStruct((t, D_IN), F32),
        compiler_params=_cparams("parallel"),
    )(x, w_in)


def _in_proj_bwd(dz, dx_res, w_in, l):
    t = dz.shape[0]

    def body(dz_ref, res_ref, w_ref, dx_ref):
        acc = res_ref[...]
        for j in range(N_SHARD):
            acc += _dot_nt(dz_ref[:, j * IN_SHARD:(j + 1) * IN_SHARD], w_ref[j])
        dx_ref[...] = acc

    row = pl.BlockSpec((TM, D_MODEL), lambda i: (i, 0))
    return pl.pallas_call(
        body, name="in_proj_bwd", grid=(t // TM,),
        in_specs=[pl.BlockSpec((TM, D_IN), lambda i: (i, 0)), row,
                  pl.BlockSpec((None, N_SHARD, D_MODEL, IN_SHARD), lambda i: (l, 0, 0, 0))],
        out_specs=row, out_shape=jax.ShapeDtypeStruct((t, D_MODEL), F32),
        compiler_params=_cparams("parallel"),
    )(dz, dx_res, w_in)


def _out_proj(ycat, x, w_out, gamma, beta, l):
    t = x.shape[0]

    def body(yc_ref, x_ref, w_ref, g_ref, b_ref, y_ref, r_ref):
        r = ALPHA * x_ref[...] + _dot(yc_ref[...], w_ref[...])
        xh, _ = _ln_stats(r)
        r_ref[...] = r
        y_ref[...] = xh * g_ref[...] + b_ref[...]

    row = pl.BlockSpec((TM, D_MODEL), lambda i: (i, 0))
    vec = pl.BlockSpec((1, D_MODEL), lambda i: (0, 0))
    return pl.pallas_call(
        body, name="out_proj", grid=(t // TM,),
        in_specs=[row, row, pl.BlockSpec((None, D_MODEL, D_MODEL), lambda i: (l, 0, 0)), vec, vec],
        out_specs=[row, row],
        out_shape=[jax.ShapeDtypeStruct((t, D_MODEL), F32)] * 2,
        compiler_params=_cparams("parallel"),
    )(ycat, x, w_out, gamma, beta)


def _out_proj_bwd(dy, r, w_out, gamma, l):
    t = dy.shape[0]

    def body(dy_ref, r_ref, w_ref, g_ref, res_ref, dm_ref, dyc_ref, dgb_ref):
        @pl.when(pl.program_id(0) == 0)
        def _():
            dgb_ref[...] = jnp.zeros_like(dgb_ref)

        xh, rstd = _ln_stats(r_ref[...])
        dy = dy_ref[...]
        dr = _ln_bwd(dy, xh, rstd, g_ref[...])
        res_ref[...] = ALPHA * dr
        dm = _mx(dr)
        dm_ref[...] = dm
        dyc_ref[...] = _dot_nt(dm, w_ref[...])
        dgb_ref[0:1, :] += _colsum(dy * xh)
        dgb_ref[1:2, :] += _colsum(dy)

    row = pl.BlockSpec((TM, D_MODEL), lambda i: (i, 0))
    return pl.pallas_call(
        body, name="out_proj_bwd", grid=(t // TM,),
        in_specs=[row, row, pl.BlockSpec((None, D_MODEL, D_MODEL), lambda i: (l, 0, 0)),
                  pl.BlockSpec((1, D_MODEL), lambda i: (0, 0))],
        out_specs=[row, row, row, pl.BlockSpec((8, D_MODEL), lambda i: (0, 0))],
        out_shape=[jax.ShapeDtypeStruct((t, D_MODEL), F32), jax.ShapeDtypeStruct((t, D_MODEL), MXU_DTYPE),
                   jax.ShapeDtypeStruct((t, D_MODEL), F32), jax.ShapeDtypeStruct((8, D_MODEL), F32)],
        compiler_params=_cparams("arbitrary"),
    )(dy, r, w_out, gamma)


def _halo_specs(t, width, col):
    per = TMC // HALO
    last = t // HALO - 1
    return [pl.BlockSpec((HALO, width), lambda i: (jnp.maximum(i * per - 1, 0), col)),
            pl.BlockSpec((TMC, width), lambda i: (i, col)),
            pl.BlockSpec((HALO, width), lambda i: (jnp.minimum((i + 1) * per, last), col))]


def _extend(refs, i, nt):
    p_ref, c_ref, n_ref = refs
    p = jnp.where(i > 0, p_ref[...].astype(F32), 0.0)
    n = jnp.where(i < nt - 1, n_ref[...].astype(F32), 0.0)
    return jnp.concatenate([p, c_ref[...].astype(F32), n], axis=0)


def _conv_fwd(z, sc_w, cc_w, cc_cb, cc_g, cc_b):
    t = z.shape[0]
    nt = t // TMC

    def body(*refs):
        b_ref = refs[0]
        c3, h3, a3, g3 = refs[1:4], refs[4:7], refs[7:10], refs[10:13]
        scw_ref, ccw_ref, cb_ref, lg_ref, lb_ref = refs[13:18]
        ysc_ref, ycc_ref, u2_ref, e_s = refs[18:22]
        i = pl.program_id(0)
        e_s[...] = _extend(c3, i, nt) * _extend(h3, i, nt)
        cv = jnp.zeros((TMC, D_CONV), F32)
        for k in range(SC_W):
            cv += scw_ref[k:k + 1, :] * e_s[pl.ds(HALO + k - 1, TMC), :]
        ysc_ref[...] = _mx(b_ref[...] * cv)
        e_s[...] = _extend(a3, i, nt) * _sigmoid(_extend(g3, i, nt))
        u2 = jnp.zeros((TMC, D_CONV), F32) + cb_ref[...]
        for k in range(CC_W):
            u2 += ccw_ref[k:k + 1, :] * e_s[pl.ds(HALO + k - 15, TMC), :]
        u2_ref[...] = u2
        xh, _ = _ln_stats(u2)
        n = xh * lg_ref[...] + lb_ref[...]
        ycc_ref[...] = _mx(n * _sigmoid(n))

    tile = pl.BlockSpec((TMC, D_CONV), lambda i: (i, 0))
    vec = pl.BlockSpec((1, D_CONV), lambda i: (0, 0))
    in_specs = ([pl.BlockSpec((TMC, D_CONV), lambda i: (i, 0))] + _halo_specs(t, D_CONV, 1) + _halo_specs(t, D_CONV, 2)
                + _halo_specs(t, D_CONV, 6) + _halo_specs(t, D_CONV, 7)
                + [pl.BlockSpec((SC_W, D_CONV), lambda i: (0, 0)), pl.BlockSpec((CC_W, D_CONV), lambda i: (0, 0)),
                   vec, vec, vec])
    return pl.pallas_call(
        body, name="conv_fwd", grid=(nt,), in_specs=in_specs, out_specs=[tile, tile, tile],
        out_shape=[jax.ShapeDtypeStruct((t, D_CONV), MXU_DTYPE), jax.ShapeDtypeStruct((t, D_CONV), MXU_DTYPE),
                   jax.ShapeDtypeStruct((t, D_CONV), F32)],
        scratch_shapes=[pltpu.VMEM((TMC + 2 * HALO, D_CONV), F32)],
        compiler_params=_cparams("parallel"),
    )(*([z] * 13), sc_w, cc_w, cc_cb, cc_g, cc_b)


ROW_CCW, ROW_CCB, ROW_CCG, ROW_CCBETA, ROW_SCW, CONV_ROWS = 0, 31, 32, 33, 34, 40


def _conv_bwd(z, dycat, u2, sc_w, cc_w, cc_g, cc_b):
    t = z.shape[0]
    nt = t // TMC

    def body(*refs):
        b3, c3, h3, a3, g3 = refs[0:3], refs[3:6], refs[6:9], refs[9:12], refs[12:15]
        dys3, dyc3, u3 = refs[15:18], refs[18:21], refs[21:24]
        scw_ref, ccw_ref, lg_ref, lb_ref = refs[24:28]
        dsc_ref, dcc_ref, sm_ref, e_s, f_s = refs[28:33]
        i = pl.program_id(0)

        @pl.when(i == 0)
        def _():
            sm_ref[...] = jnp.zeros_like(sm_ref)

        cur = pl.ds(HALO, TMC)
        e_s[...] = _extend(c3, i, nt) * _extend(h3, i, nt)
        f_s[...] = _extend(dys3, i, nt) * _extend(b3, i, nt)
        cv = jnp.zeros((TMC, D_CONV), F32)
        dp = jnp.zeros((TMC, D_CONV), F32)
        dcv = f_s[cur, :]
        for k in range(SC_W):
            win = e_s[pl.ds(HALO + k - 1, TMC), :]
            cv += scw_ref[k:k + 1, :] * win
            dp += scw_ref[k:k + 1, :] * f_s[pl.ds(HALO - k + 1, TMC), :]
            sm_ref[ROW_SCW + k:ROW_SCW + k + 1, :] += _colsum(dcv * win)
        dsc_ref[:, 0:D_CONV] = _mx(dys3[1][...] * cv)
        dsc_ref[:, D_CONV:2 * D_CONV] = _mx(dp * h3[1][...])
        dsc_ref[:, 2 * D_CONV:3 * D_CONV] = _mx(dp * c3[1][...])
        xh, rstd = _ln_stats(_extend(u3, i, nt))
        n = xh * lg_ref[...] + lb_ref[...]
        sg = _sigmoid(n)
        dn = _extend(dyc3, i, nt) * (sg * (1.0 + n * (1.0 - sg)))
        f_s[...] = _ln_bwd(dn, xh, rstd, lg_ref[...])
        sm_ref[ROW_CCG:ROW_CCG + 1, :] += _colsum((dn * xh)[HALO:HALO + TMC])
        sm_ref[ROW_CCBETA:ROW_CCBETA + 1, :] += _colsum(dn[HALO:HALO + TMC])
        sig_g = _sigmoid(_extend(g3, i, nt))
        e_s[...] = _extend(a3, i, nt) * sig_g
        du2 = f_s[cur, :]
        sm_ref[ROW_CCB:ROW_CCB + 1, :] += _colsum(du2)
        duu = jnp.zeros((TMC, D_CONV), F32)
        for k in range(CC_W):
            duu += ccw_ref[k:k + 1, :] * f_s[pl.ds(HALO + 15 - k, TMC), :]
            sm_ref[ROW_CCW + k:ROW_CCW + k + 1, :] += _colsum(du2 * e_s[pl.ds(HALO + k - 15, TMC), :])
        sgc = sig_g[HALO:HALO + TMC]
        dcc_ref[:, 0:D_CONV] = _mx(duu * sgc)
        dcc_ref[:, D_CONV:2 * D_CONV] = _mx(duu * a3[1][...] * sgc * (1.0 - sgc))

    vec = pl.BlockSpec((1, D_CONV), lambda i: (0, 0))
    in_specs = []
    for col in (0, 1, 2, 6, 7):
        in_specs += _halo_specs(t, D_CONV, col)
    in_specs += _halo_specs(t, D_CONV, 0) + _halo_specs(t, D_CONV, 3) + _halo_specs(t, D_CONV, 0)
    in_specs += [pl.BlockSpec((SC_W, D_CONV), lambda i: (0, 0)), pl.BlockSpec((CC_W, D_CONV), lambda i: (0, 0)), vec, vec]
    return pl.pallas_call(
        body, name="conv_bwd", grid=(nt,), in_specs=in_specs,
        out_specs=[pl.BlockSpec((TMC, 3 * D_CONV), lambda i: (i, 0)), pl.BlockSpec((TMC, 2 * D_CONV), lambda i: (i, 0)),
                   pl.BlockSpec((CONV_ROWS, D_CONV), lambda i: (0, 0))],
        out_shape=[jax.ShapeDtypeStruct((t, 3 * D_CONV), MXU_DTYPE), jax.ShapeDtypeStruct((t, 2 * D_CONV), MXU_DTYPE),
                   jax.ShapeDtypeStruct((CONV_ROWS, D_CONV), F32)],
        scratch_shapes=[pltpu.VMEM((TMC + 2 * HALO, D_CONV), F32)] * 2,
        compiler_params=_cparams("arbitrary"),
    )(*([z] * 15), *([dycat] * 6), *([u2] * 3), sc_w, cc_w, cc_g, cc_b)


def _lane(shape):
    return lax.broadcasted_iota(jnp.int32, shape, 1)


def _swap_halves(x):
    w = x.shape[1]
    lo = (_lane(x.shape) % HEAD_DIM) < HEAD_DIM // 2
    return jnp.where(lo, pltpu.roll(x, w - HEAD_DIM // 2, 1), pltpu.roll(x, HEAD_DIM // 2, 1))


def _half(shape, g):
    lane = _lane(shape)
    return lane < HEAD_DIM if g == 0 else lane >= HEAD_DIM


GROUP_ROWS = 4 * BLOCK


def _stack_heads(tiles, out_ref, nblk):
    for tt in range(4):
        g = tt // 2
        for slot in range(2):
            s = 2 * (tt % 2) + slot
            piece = tiles[tt] if slot == g else pltpu.roll(tiles[tt], HEAD_DIM, 1)
            piece = jnp.where(_half(piece.shape, g), piece, 0.0).astype(out_ref.dtype)
            for b in range(nblk):
                at = GROUP_ROWS * b + BLOCK * s
                out_ref[g, at:at + BLOCK, :] = piece[BLOCK * b:BLOCK * (b + 1)]


def _unstack_heads(ref, nblk):
    tiles = []
    for tt in range(4):
        g = tt // 2
        tile = None
        for slot in range(2):
            s = 2 * (tt % 2) + slot
            rows = [ref[g, GROUP_ROWS * b + BLOCK * s:GROUP_ROWS * b + BLOCK * (s + 1), :] for b in range(nblk)]
            piece = rows[0] if nblk == 1 else jnp.concatenate(rows, axis=0)
            if slot != g:
                piece = pltpu.roll(piece, HEAD_DIM, 1)
            tile = piece if tile is None else tile + piece
        tiles.append(tile)
    return tiles


def _attn_prep(z, cos, sin):
    t = z.shape[0]
    nblk = TM // BLOCK

    def body(qa_ref, qb_ref, k_ref, v_ref, cos_ref, sin_ref, qst_ref, kr_ref, vb_ref):
        cs, sn = cos_ref[...], sin_ref[...]

        def rope(x):
            return x * cs + _swap_halves(x) * sn

        tiles = []
        for tt in range(4):
            src = qa_ref if tt < 2 else qb_ref
            tiles.append(rope(src[:, (tt % 2) * BLOCK:(tt % 2 + 1) * BLOCK]) * (HEAD_DIM ** -0.5))
        _stack_heads(tiles, qst_ref, nblk)
        kr_ref[...] = _mx(rope(k_ref[...]))
        vb_ref[...] = _mx(v_ref[...])

    def col(width, j):
        return pl.BlockSpec((TM, width), lambda i: (i, j))

    return pl.pallas_call(
        body, name="attn_prep", grid=(t // TM,),
        in_specs=[col(256, 3), col(256, 4), col(128, 10), col(128, 11), col(128, 0), col(128, 0)],
        out_specs=[pl.BlockSpec((2, 4 * TM, BLOCK), lambda i: (0, i, 0)), col(128, 0), col(128, 0)],
        out_shape=[jax.ShapeDtypeStruct((2, 4 * t, BLOCK), MXU_DTYPE), jax.ShapeDtypeStruct((t, BLOCK), MXU_DTYPE),
                   jax.ShapeDtypeStruct((t, BLOCK), MXU_DTYPE)],
        compiler_params=_cparams("parallel"),
    )(z, z, z, z, cos, sin)


def _attn_dprep(dycat, ost, lst):
    t = dycat.shape[0]
    nblk = TM // BLOCK

    def body(da_ref, db_ref, o_ref, l_ref, dost_ref, ld_ref, st_s):
        tiles = []
        for tt in range(4):
            src = da_ref if tt < 2 else db_ref
            tiles.append(src[:, (tt % 2) * BLOCK:(tt % 2 + 1) * BLOCK])
        _stack_heads(tiles, st_s, nblk)
        for g in range(2):
            do = st_s[g]
            dost_ref[g] = _mx(do)
            dsum = jnp.sum(do * o_ref[g], axis=-1, keepdims=True)
            ld_ref[g] = jnp.where(_lane(do.shape) < HEAD_DIM, l_ref[g], dsum)

    stacked = pl.BlockSpec((2, 4 * TM, BLOCK), lambda i: (0, i, 0))
    return pl.pallas_call(
        body, name="attn_dprep", grid=(t // TM,),
        in_specs=[pl.BlockSpec((TM, 256), lambda i: (i, 1)), pl.BlockSpec((TM, 256), lambda i: (i, 2)), stacked, stacked],
        out_specs=[stacked, stacked],
        out_shape=[jax.ShapeDtypeStruct((2, 4 * t, BLOCK), MXU_DTYPE), jax.ShapeDtypeStruct((2, 4 * t, BLOCK), F32)],
        scratch_shapes=[pltpu.VMEM((2, 4 * TM, BLOCK), F32)],
        compiler_params=_cparams("parallel"),
    )(dycat, dycat, ost, lst)


def _attn_prep_bwd(dqst, dk, dv, cos, sin):
    t = dk.shape[0]
    nblk = TM // BLOCK

    def body(dq_ref, dk_ref, dv_ref, cos_ref, sin_ref, dz_ref):
        cs, sn = cos_ref[...], sin_ref[...]

        def rope_bwd(d):
            return d * cs + _swap_halves(d * sn)

        for tt, tile in enumerate(_unstack_heads(dq_ref, nblk)):
            dz_ref[:, tt * BLOCK:(tt + 1) * BLOCK] = _mx(rope_bwd(tile * (HEAD_DIM ** -0.5)))
        dz_ref[:, 4 * BLOCK:5 * BLOCK] = _mx(rope_bwd(dk_ref[...]))
        dz_ref[:, 5 * BLOCK:6 * BLOCK] = _mx(dv_ref[...])

    def col(width):
        return pl.BlockSpec((TM, width), lambda i: (i, 0))

    return pl.pallas_call(
        body, name="attn_prep_bwd", grid=(t // TM,),
        in_specs=[pl.BlockSpec((2, 4 * TM, BLOCK), lambda i: (0, i, 0)), col(128), col(128), col(128), col(128)],
        out_specs=col(768), out_shape=jax.ShapeDtypeStruct((t, 768), MXU_DTYPE),
        compiler_params=_cparams("parallel"),
    )(dqst, dk, dv, cos, sin)


def _nbr_specs(nb, width, col):
    return [pl.BlockSpec((BLOCK, width), lambda n: (jnp.maximum(n - 1, 0), col)),
            pl.BlockSpec((BLOCK, width), lambda n: (n, col)),
            pl.BlockSpec((BLOCK, width), lambda n: (jnp.minimum(n + 1, nb - 1), col))]


def _stacked_specs(nb):
    return [pl.BlockSpec((2, GROUP_ROWS, BLOCK), lambda n: (0, jnp.maximum(n - 1, 0), 0)),
            pl.BlockSpec((2, GROUP_ROWS, BLOCK), lambda n: (0, n, 0)),
            pl.BlockSpec((2, GROUP_ROWS, BLOCK), lambda n: (0, jnp.minimum(n + 1, nb - 1), 0))]


def _query_index():
    row = lax.broadcasted_iota(jnp.int32, (GROUP_ROWS, BLOCK), 0)
    return row & (BLOCK - 1), lax.broadcasted_iota(jnp.int32, (GROUP_ROWS, BLOCK), 1)


def _sink_column(sink_ref, g):
    band = lax.broadcasted_iota(jnp.int32, (GROUP_ROWS, 1), 0) // BLOCK
    col = jnp.zeros((GROUP_ROWS, 1), F32) + sink_ref[4 * g]
    for s in range(1, 4):
        col = jnp.where(band == s, sink_ref[4 * g + s], col)
    return col


def _attn_fwd(qst, kr, vb, sink):
    t = kr.shape[0]
    nb = t // BLOCK

    def body(q_ref, kp_ref, kc_ref, kn_ref, vp_ref, vc_ref, vn_ref, sink_ref, o_ref, ost_ref, lst_ref):
        n = pl.program_id(0)
        qi, kj = _query_index()
        m_prev, m_next = (kj >= qi) & (n > 0), (kj <= qi) & (n < nb - 1)
        nat = [None] * 4
        for g in range(2):
            q = q_ref[g]
            sp = jnp.where(m_prev, _dot_nt(q, kp_ref[...]), NEG)
            sc = _dot_nt(q, kc_ref[...])
            sn = jnp.where(m_next, _dot_nt(q, kn_ref[...]), NEG)
            sk = _sink_column(sink_ref, g)
            m = jnp.maximum(jnp.max(jnp.maximum(jnp.maximum(sp, sc), sn), axis=-1, keepdims=True), sk)
            pp, pc, pn = jnp.exp(sp - m), jnp.exp(sc - m), jnp.exp(sn - m)
            den = jnp.sum(pp + pc + pn, axis=-1, keepdims=True) + jnp.exp(sk - m)
            o = (_dot(_mx(pp), vp_ref[...]) + _dot(_mx(pc), vc_ref[...]) + _dot(_mx(pn), vn_ref[...])) / den
            o = jnp.where(_half(o.shape, g), o, 0.0)
            ost_ref[g] = o
            lst_ref[g] = jnp.broadcast_to(m + jnp.log(den), (GROUP_ROWS, BLOCK))
            for s in range(4):
                tt, slot = 2 * g + s // 2, s % 2
                piece = o[BLOCK * s:BLOCK * (s + 1)]
                if slot != g:
                    piece = pltpu.roll(piece, HEAD_DIM, 1)
                nat[tt] = piece if nat[tt] is None else nat[tt] + piece
        for tt in range(4):
            o_ref[:, tt * BLOCK:(tt + 1) * BLOCK] = _mx(nat[tt])

    stacked = pl.BlockSpec((2, GROUP_ROWS, BLOCK), lambda n: (0, n, 0))
    return pl.pallas_call(
        body, name="attn_fwd", grid=(nb,),
        in_specs=[stacked] + _nbr_specs(nb, BLOCK, 0) + _nbr_specs(nb, BLOCK, 0) + [pl.BlockSpec(memory_space=pltpu.SMEM)],
        out_specs=[pl.BlockSpec((BLOCK, 512), lambda n: (n, 0)), stacked, stacked],
        out_shape=[jax.ShapeDtypeStruct((t, 512), MXU_DTYPE), jax.ShapeDtypeStruct((2, 4 * t, BLOCK), F32),
                   jax.ShapeDtypeStruct((2, 4 * t, BLOCK), F32)],
        compiler_params=_cparams("parallel"),
    )(qst, kr, kr, kr, vb, vb, vb, sink)


def _lse_and_dsum(ld):
    return ld[:, 0:1], pltpu.roll(ld, HEAD_DIM, 1)[:, 0:1]


def _attn_bwd_q(qst, kr, vb, dost, ld, sink):
    t = kr.shape[0]
    nb = t // BLOCK

    def body(q_ref, kp_ref, kc_ref, kn_ref, vp_ref, vc_ref, vn_ref, do_ref, ld_ref, sink_ref, dq_ref, ds_ref):
        n = pl.program_id(0)

        @pl.when(n == 0)
        def _():
            ds_ref[...] = jnp.zeros_like(ds_ref)

        qi, kj = _query_index()
        m_prev, m_next = (kj >= qi) & (n > 0), (kj <= qi) & (n < nb - 1)
        for g in range(2):
            q, do = q_ref[g], do_ref[g]
            lse, dsum = _lse_and_dsum(ld_ref[g])
            acc = jnp.zeros((GROUP_ROWS, BLOCK), F32)
            for k_ref, v_ref, valid in ((kp_ref, vp_ref, m_prev), (kc_ref, vc_ref, None), (kn_ref, vn_ref, m_next)):
                sc = _dot_nt(q, k_ref[...])
                if valid is not None:
                    sc = jnp.where(valid, sc, NEG)
                p = jnp.exp(sc - lse)
                dsc = p * (_dot_nt(do, v_ref[...]) - dsum)
                acc += _dot(_mx(dsc), k_ref[...])
            dq_ref[g] = jnp.where(_half(acc.shape, g), acc, 0.0)
            dsk = jnp.exp(_sink_column(sink_ref, g) - lse) * dsum
            for s in range(4):
                h = 4 * g + s
                ds_ref[h:h + 1, :] -= jnp.sum(dsk[BLOCK * s:BLOCK * (s + 1)], axis=0, keepdims=True)

    stacked = pl.BlockSpec((2, GROUP_ROWS, BLOCK), lambda n: (0, n, 0))
    return pl.pallas_call(
        body, name="attn_bwd_q", grid=(nb,),
        in_specs=[stacked] + _nbr_specs(nb, BLOCK, 0) + _nbr_specs(nb, BLOCK, 0)
        + [stacked, stacked, pl.BlockSpec(memory_space=pltpu.SMEM)],
        out_specs=[stacked, pl.BlockSpec((8, BLOCK), lambda n: (0, 0))],
        out_shape=[jax.ShapeDtypeStruct((2, 4 * t, BLOCK), F32), jax.ShapeDtypeStruct((8, BLOCK), F32)],
        compiler_params=_cparams("arbitrary"),
    )(qst, kr, kr, kr, vb, vb, vb, dost, ld, sink)


def _attn_bwd_kv(qst, kr, vb, dost, ld):
    t = kr.shape[0]
    nb = t // BLOCK

    def body(*refs):
        q3, do3, ld3 = refs[0:3], refs[3:6], refs[6:9]
        k_ref, v_ref, dk_ref, dv_ref = refs[9:13]
        j = pl.program_id(0)
        qi, kj = _query_index()
        valid = ((kj <= qi) & (j > 0), None, (kj >= qi) & (j < nb - 1))
        k, v = k_ref[...], v_ref[...]
        dk = jnp.zeros((BLOCK, BLOCK), F32)
        dv = jnp.zeros((BLOCK, BLOCK), F32)
        for g in range(2):
            for b in range(3):
                q, do = q3[b][g], do3[b][g]
                lse, dsum = _lse_and_dsum(ld3[b][g])
                sc = _dot_nt(q, k)
                if valid[b] is not None:
                    sc = jnp.where(valid[b], sc, NEG)
                p = jnp.exp(sc - lse)
                dsc = p * (_dot_nt(do, v) - dsum)
                dv += _dot_tn(_mx(p), do)
                dk += _dot_tn(_mx(dsc), q)
        dk_ref[...] = dk
        dv_ref[...] = dv

    cur = pl.BlockSpec((BLOCK, BLOCK), lambda n: (n, 0))
    return pl.pallas_call(
        body, name="attn_bwd_kv", grid=(nb,),
        in_specs=_stacked_specs(nb) * 3 + [cur, cur],
        out_specs=[cur, cur], out_shape=[jax.ShapeDtypeStruct((t, BLOCK), F32)] * 2,
        compiler_params=_cparams("parallel"),
    )(*([qst] * 3), *([dost] * 3), *([ld] * 3), kr, vb)


def _loss_head(y, target):
    t = y.shape[0]

    def body(y_ref, t_ref, l_ref, dy_ref):
        @pl.when(pl.program_id(0) == 0)
        def _():
            l_ref[...] = jnp.zeros_like(l_ref)

        e = y_ref[...] - t_ref[...]
        dy_ref[...] = e / D_MODEL
        l_ref[...] += 0.5 * jnp.sum(_mean(e * e))

    row = pl.BlockSpec((TM, D_MODEL), lambda i: (i, 0))
    return pl.pallas_call(
        body, name="loss_head", grid=(t // TM,), in_specs=[row, row],
        out_specs=[pl.BlockSpec((8, 128), lambda i: (0, 0)), row],
        out_shape=[jax.ShapeDtypeStruct((8, 128), F32), jax.ShapeDtypeStruct((t, D_MODEL), F32)],
        compiler_params=_cparams("arbitrary"),
    )(y, target)


def _adamw(name, w, g, m, v, rows):
    n, width = w.shape

    def body(w_ref, g_ref, m_ref, v_ref, d_ref, nm_ref, nv_ref):
        g = g_ref[...]
        m = ADAM_B1 * m_ref[...] + (1.0 - ADAM_B1) * g
        v = ADAM_B2 * v_ref[...] + (1.0 - ADAM_B2) * jnp.square(g)
        m_hat = m / (1.0 - ADAM_B1 ** ADAM_STEP)
        v_hat = v / (1.0 - ADAM_B2 ** ADAM_STEP)
        d_ref[...] = -ADAM_LR * (m_hat / (jnp.sqrt(v_hat) + ADAM_EPS) + ADAM_WD * w_ref[...])
        nm_ref[...] = m
        nv_ref[...] = v

    spec = pl.BlockSpec((rows, width), lambda i: (i, 0))
    return pl.pallas_call(
        body, name=name, grid=(n // rows,), in_specs=[spec] * 4, out_specs=[spec] * 3,
        out_shape=[jax.ShapeDtypeStruct((n, width), F32)] * 3, compiler_params=_cparams("parallel"),
    )(w, g, m, v)


def _place():
    x, y, c = lax.axis_index("x"), lax.axis_index("y"), lax.axis_index("c")
    chips = [(1 - x, y), (x, 1 - y), (1 - x, 1 - y)]
    return x, y, c, chips


class _Gather:
    def __init__(self, shards):
        na = len(shards)
        self.ins = list(shards)
        self.outs = [jax.ShapeDtypeStruct((N_SHARD,) + s.shape, s.dtype) for s in shards]
        self.sems = [pltpu.SemaphoreType.DMA((3 * na,))] * 4 + [pltpu.SemaphoreType.DMA((na,))]

    def _copies(self, src, dst, sems):
        send, recv, fsend, frecv, lsem = sems
        x, y, c, chips = _place()
        mine = 2 * x + y

        def local(a):
            return pltpu.make_async_copy(src[a], dst[a].at[mine], lsem.at[a])

        def ici(a, k, shard):
            cx, cy = chips[k]
            return pltpu.make_async_remote_copy(
                src_ref=src[a].at[c], dst_ref=dst[a].at[shard, c], send_sem=send.at[3 * a + k], recv_sem=recv.at[3 * a + k],
                device_id=(cx, cy, c), device_id_type=MESH)

        def d2d(a, k, half):
            cx, cy = chips[k]
            block = dst[a].at[2 * cx + cy, half]
            return pltpu.make_async_remote_copy(
                src_ref=block, dst_ref=block, send_sem=fsend.at[3 * a + k], recv_sem=frecv.at[3 * a + k],
                device_id=(x, y, 1 - c), device_id_type=MESH)

        return local, ici, d2d, mine, c, chips

    def start(self, src, dst, sems):
        local, ici, _, mine, _, _ = self._copies(src, dst, sems)
        for a in range(len(src)):
            local(a).start()
            for k in range(3):
                ici(a, k, mine).start()

    def finish(self, src, dst, sems):
        local, ici, d2d, mine, c, chips = self._copies(src, dst, sems)
        for a in range(len(src)):
            for k, (cx, cy) in enumerate(chips):
                ici(a, k, 2 * cx + cy).wait_recv()
                d2d(a, k, c).start()
        for a in range(len(src)):
            for k in range(3):
                d2d(a, k, 1 - c).wait_recv()
        for a in range(len(src)):
            for k in range(3):
                ici(a, k, mine).wait_send()
                d2d(a, k, c).wait_send()
            local(a).wait()


class _PairExchange:
    def __init__(self, parts):
        self.ins = list(parts)
        self.outs = [jax.ShapeDtypeStruct((N_SHARD,) + p.shape[2:], p.dtype) for p in parts]
        self.sems = [pltpu.SemaphoreType.DMA((len(parts),))] * 2

    def _copy(self, a, src, dst, sems):
        x, y, c, _ = _place()
        return pltpu.make_async_remote_copy(
            src_ref=src[a].at[:, 1 - c], dst_ref=dst[a], send_sem=sems[0].at[a], recv_sem=sems[1].at[a],
            device_id=(x, y, 1 - c), device_id_type=MESH)

    def start(self, src, dst, sems):
        for a in range(len(src)):
            self._copy(a, src, dst, sems).start()

    def finish(self, src, dst, sems):
        for a in range(len(src)):
            self._copy(a, src, dst, sems).wait()


class _ChipExchange:
    def __init__(self, sums):
        self.ins = list(sums)
        self.outs = [jax.ShapeDtypeStruct((3,) + s.shape[1:], s.dtype) for s in sums]
        self.sems = [pltpu.SemaphoreType.DMA((3 * len(sums),))] * 2

    def _copy(self, a, k, src, dst, sems):
        _, _, c, chips = _place()
        cx, cy = chips[k]
        return pltpu.make_async_remote_copy(
            src_ref=src[a].at[2 * cx + cy], dst_ref=dst[a].at[k], send_sem=sems[0].at[3 * a + k],
            recv_sem=sems[1].at[3 * a + k], device_id=(cx, cy, c), device_id_type=MESH)

    def start(self, src, dst, sems):
        for a in range(len(src)):
            for k in range(3):
                self._copy(a, k, src, dst, sems).start()

    def finish(self, src, dst, sems):
        for a in range(len(src)):
            for k in range(3):
                self._copy(a, k, src, dst, sems).wait()


def _run(name, rider):
    n_in, n_out = len(rider.ins), len(rider.outs)

    def body(*refs):
        ins, outs, sems = refs[:n_in], refs[n_in:n_in + n_out], refs[n_in + n_out:]
        rider.start(ins, outs, sems)
        rider.finish(ins, outs, sems)

    return list(pl.pallas_call(
        body, name=name, in_specs=[ANY] * n_in, out_specs=[ANY] * n_out, out_shape=rider.outs,
        scratch_shapes=rider.sems)(*rider.ins))


def _pair_share(halves):
    na = len(halves)

    def body(*refs):
        dst = refs[na:2 * na]
        send, recv = refs[2 * na:]
        x, y, c, _ = _place()
        cps = []
        for a in range(na):
            cp = pltpu.make_async_remote_copy(
                src_ref=dst[a].at[:, c], dst_ref=dst[a].at[:, c], send_sem=send.at[a], recv_sem=recv.at[a],
                device_id=(x, y, 1 - c), device_id_type=MESH)
            cp.start()
            cps.append(cp)
        for a in range(na):
            cps[a].wait_send()
            pltpu.make_async_remote_copy(
                src_ref=dst[a].at[:, 1 - c], dst_ref=dst[a].at[:, 1 - c], send_sem=send.at[a], recv_sem=recv.at[a],
                device_id=(x, y, 1 - c), device_id_type=MESH).wait_recv()

    return pl.pallas_call(
        body, name="pair_share", in_specs=[ANY] * na, out_specs=[ANY] * na,
        out_shape=[jax.ShapeDtypeStruct(h.shape, h.dtype) for h in halves],
        input_output_aliases={a: a for a in range(na)},
        scratch_shapes=[pltpu.SemaphoreType.DMA((na,))] * 2,
    )(*halves)


def _sum_rows(r):
    return r if r <= 352 else 256


def _pair_sum(name, part, got):
    _, _, r, w = part.shape
    rows = _sum_rows(r)
    c = lax.axis_index("c").astype(jnp.int32).reshape(1)

    def body(c_ref, p_ref, g_ref, o_ref):
        o_ref[...] = _mx(p_ref[...] + g_ref[...])

    spec = pl.BlockSpec((None, rows, w), lambda j, i, c_ref: (j, i, 0))
    return pl.pallas_call(
        body, name=name, out_shape=jax.ShapeDtypeStruct((N_SHARD, r, w), MXU_DTYPE),
        grid_spec=pltpu.PrefetchScalarGridSpec(
            num_scalar_prefetch=1, grid=(N_SHARD, r // rows),
            in_specs=[pl.BlockSpec((None, None, rows, w), lambda j, i, c_ref: (j, c_ref[0], i, 0)), spec],
            out_specs=spec),
        compiler_params=_cparams("parallel", "parallel"),
    )(c, part, got)


def _chip_sum(name, part, got, others, l, prev):
    _, _, r, w = part.shape
    rows = _sum_rows(r)
    cj = jnp.stack([lax.axis_index("c"), 2 * lax.axis_index("x") + lax.axis_index("y")]).astype(jnp.int32)

    def body(cj_ref, p_ref, g_ref, o_ref, *rest):
        acc = p_ref[...] + g_ref[...]
        for k in range(3):
            acc += o_ref[k].astype(F32)
        rest[-1][...] = acc

    ins, specs, alias = [cj, part, got, others], [], {}
    if prev is not None:
        ins.append(prev)
        specs.append(ANY)
        alias = {4: 0}
    return pl.pallas_call(
        body, name=name, out_shape=jax.ShapeDtypeStruct((2, 2, r, w), F32), input_output_aliases=alias,
        grid_spec=pltpu.PrefetchScalarGridSpec(
            num_scalar_prefetch=1, grid=(r // rows,),
            in_specs=[pl.BlockSpec((None, None, rows, w), lambda i, cj: (cj[1], cj[0], i, 0)),
                      pl.BlockSpec((None, rows, w), lambda i, cj: (cj[1], i, 0)),
                      pl.BlockSpec((3, rows, w), lambda i, cj: (0, i, 0))] + specs,
            out_specs=pl.BlockSpec((None, None, rows, w), lambda i, cj: (l, cj[0], i, 0))),
        compiler_params=_cparams("parallel"),
    )(*ins)


SMALL_ROWS = 40


def _sum_small(part):
    def body(p_ref, o_ref, land, send, recv):
        x, y, c, _ = _place()
        me = 4 * x + 2 * y + c
        cps = []
        for r in range(1, 8):
            cp = pltpu.make_async_remote_copy(
                src_ref=p_ref, dst_ref=land.at[r], send_sem=send.at[r], recv_sem=recv.at[r],
                device_id=(x ^ (r >> 2), y ^ ((r >> 1) & 1), c ^ (r & 1)), device_id_type=MESH)
            cp.start()
            cps.append(cp)
        land[0] = p_ref[...]
        for cp in cps:
            cp.wait()
        acc = land[me]
        for e in range(1, 8):
            acc += land[me ^ e]
        o_ref[...] = acc

    return pl.pallas_call(
        body, name="sum_small", in_specs=[pl.BlockSpec(memory_space=pltpu.VMEM)],
        out_specs=pl.BlockSpec(memory_space=pltpu.VMEM), out_shape=jax.ShapeDtypeStruct(part.shape, F32),
        scratch_shapes=[pltpu.VMEM((8,) + part.shape, F32), pltpu.SemaphoreType.DMA((8,)), pltpu.SemaphoreType.DMA((8,))],
    )(part)


BIG = ("ffn1_w_gu", "ffn1_w_down", "w_in", "w_out", "ffn2_w_gu", "ffn2_w_down")
SMALL = ("ln1_g", "ln1_b", "ln2_g", "ln2_b", "ln3_g", "ln3_b", "attn_sink", "cc_conv_b", "cc_ln_g", "cc_ln_b",
         "sc_conv_w", "cc_conv_w")
NAMES = ("ffn1_w_gu", "ffn1_w_down", "ln1_g", "ln1_b", "w_in", "sc_conv_w", "attn_sink", "cc_conv_w", "cc_conv_b",
         "cc_ln_g", "cc_ln_b", "w_out", "ln2_g", "ln2_b", "ffn2_w_gu", "ffn2_w_down", "ln3_g", "ln3_b")


def _rope_tables(t):
    half = HEAD_DIM // 2
    inv_freq = ROPE_THETA ** (-jnp.arange(half, dtype=F32) / half)
    ang = jnp.arange(t).astype(F32)[:, None] * inv_freq[None, :]
    cos, sin = jnp.cos(ang), jnp.sin(ang)
    return jnp.tile(jnp.concatenate([cos, cos], axis=1), (1, 2)), jnp.tile(jnp.concatenate([-sin, sin], axis=1), (1, 2))


def _pack_small(vals):
    flat = jnp.concatenate([vals[n].reshape(-1) for n in SMALL])
    return jnp.pad(flat, (0, SMALL_ROWS * D_MODEL - flat.shape[0])).reshape(SMALL_ROWS, D_MODEL)


def _unpack_small(packed, shapes):
    flat, out, at = packed.reshape(-1), {}, 0
    for n in SMALL:
        size = int(np.prod(shapes[n]))
        out[n] = flat[at:at + size].reshape(shapes[n])
        at += size
    return out


def kernel(x, ffn1_w_gu, ffn1_w_down, ln1_g, ln1_b, w_in, sc_conv_w, attn_sink, cc_conv_w, cc_conv_b, cc_ln_g, cc_ln_b, w_out, ln2_g, ln2_b, ffn2_w_gu, ffn2_w_down, ln3_g, ln3_b, loss_target, m_ffn1_w_gu, m_ffn1_w_down, m_ln1_g, m_ln1_b, m_w_in, m_sc_conv_w, m_attn_sink, m_cc_conv_w, m_cc_conv_b, m_cc_ln_g, m_cc_ln_b, m_w_out, m_ln2_g, m_ln2_b, m_ffn2_w_gu, m_ffn2_w_down, m_ln3_g, m_ln3_b, v_ffn1_w_gu, v_ffn1_w_down, v_ln1_g, v_ln1_b, v_w_in, v_sc_conv_w, v_attn_sink, v_cc_conv_w, v_cc_conv_b, v_cc_ln_g, v_cc_ln_b, v_w_out, v_ln2_g, v_ln2_b, v_ffn2_w_gu, v_ffn2_w_down, v_ln3_g, v_ln3_b):
    given = dict(locals())
    w = {n: given[n] for n in NAMES}
    mom = {n: given["m_" + n] for n in NAMES}
    var = {n: given["v_" + n] for n in NAMES}
    x0 = x[0]
    target = loss_target[0]
    t = x0.shape[0]
    chip = 2 * lax.axis_index("x") + lax.axis_index("y")

    conv_shard = jnp.pad(jnp.concatenate([sc_conv_w, cc_conv_w], axis=1), ((0, 0), (0, 14), (0, 64)))
    local = {n: _mx(w[n]) for n in BIG}
    local["conv"] = conv_shard
    full = [{}, {}]

    def gather(l, names):
        return _Gather([local[n][l].reshape(2, local[n].shape[1] // 2, local[n].shape[2]) for n in names])

    def land(l, names, arrays):
        for n, a in zip(names, arrays):
            full[l][n] = a.reshape(1, N_SHARD, 2 * a.shape[2], a.shape[3])

    def weights(l):
        f = full[l]
        conv = jnp.transpose(f["conv"][0, :, :SC_W + CC_W, :64], (1, 0, 2)).reshape(SC_W + CC_W, D_CONV)
        return dict(wgu1=f["ffn1_w_gu"], wd1=f["ffn1_w_down"].reshape(1, D_FF, D_MODEL), win=f["w_in"],
                    wout=f["w_out"].reshape(1, D_MODEL, D_MODEL), wgu2=f["ffn2_w_gu"],
                    wd2=f["ffn2_w_down"].reshape(1, D_FF, D_MODEL), sc=conv[:SC_W], cc=conv[SC_W:])

    first = ("ffn1_w_gu", "ffn1_w_down")
    mixer = ("w_in", "w_out", "conv")
    second = ("ffn2_w_gu", "ffn2_w_down")
    land(0, first, _run("gather_first", gather(0, first)))
    cos, sin = _rope_tables(t)

    def vec(a, l):
        return a[l][None, :]

    acts = []
    h = x0
    for l in range(2):
        ahead = (0, mixer + second) if l == 0 else (1, second)
        (y1, r1, gu1), got = _ffn_fwd("ffn_fwd_a%d" % l, h, full[l]["ffn1_w_gu"], full[l]["ffn1_w_down"].reshape(1, D_FF, D_MODEL),
                                      vec(ln1_g, l), vec(ln1_b, l), 0, gather(*ahead))
        land(*ahead, got)
        wl = weights(l)
        z = _in_proj(y1, wl["win"], 0)
        ysc, ycc, u2 = _conv_fwd(z, wl["sc"], wl["cc"], vec(cc_conv_b, l), vec(cc_ln_g, l), vec(cc_ln_b, l))
        qs, kf, vf = _attn_prep(z, cos, sin)
        o_nat, o, lse = _attn_fwd(qs, kf, vf, attn_sink[l])
        ycat = jnp.concatenate([ysc, o_nat, ycc], axis=1)
        y2, r2 = _out_proj(ycat, y1, wl["wout"], vec(ln2_g, l), vec(ln2_b, l), 0)
        ahead = (1, first + mixer) if l == 0 else None
        (y3, r3, gu2), got = _ffn_fwd("ffn_fwd_b%d" % l, y2, wl["wgu2"], wl["wd2"], vec(ln3_g, l), vec(ln3_b, l), 0,
                                      gather(*ahead) if ahead else None)
        if ahead:
            land(*ahead, got)
        acts.append(dict(x=h, y1=y1, r1=r1, gu1=gu1, z=z, u2=u2, qs=qs, kf=kf, vf=vf, o=o, lse=lse, ycat=ycat,
                         y2=y2, r2=r2, gu2=gu2, r3=r3, w=wl))
        h = y3
    loss_rows, dy = _loss_head(h, target)
    loss = lax.psum(loss_rows[0, 0], ("x", "y", "c"))

    upper = ("ffn2_w_gu", "ffn2_w_down", "w_out")
    lower = ("w_in", "ffn1_w_gu", "ffn1_w_down")
    part = [{}, {}]
    small = [None, None]
    stage = {}
    reduced = {n: None for n in BIG}
    row = pl.BlockSpec((TM, D_MODEL), lambda n, k: (k, 0))

    def halves(a, r):
        return a.reshape(N_SHARD, 2, r // 2, a.shape[-1])

    def pair_rider(l, names):
        return _PairExchange([part[l][n] for n in names])

    def after_pair(l, names, got):
        stage[l, names] = (got, [_pair_sum("pair_sum_%s_%d" % (n, l), part[l][n], g) for n, g in zip(names, got)])

    def chip_rider(l, names):
        return _ChipExchange(stage[l, names][1])

    def after_chip(l, names, others):
        for n, g, o in zip(names, stage[l, names][0], others):
            reduced[n] = _chip_sum("chip_sum_%s_%d" % (n, l), part[l][n], g, o, l, reduced[n])

    def ffn_grads(which, dy, r, gu, xin, wgu, wd, gamma, l, rider_gu=None, between=None, rider_d=None):
        dx, dh, a, do, dgb = _ffn_bwd(dy, r, gu, wgu, wd, gamma, 0)
        out, got_gu = _mm_tn(
            "%s_dwgu_%d" % (which, l), xin, dh, row,
            pl.BlockSpec((None, TM, FF_CHUNK), lambda n, k: (n // N_CHUNK, k, n % N_CHUNK)),
            pl.BlockSpec((None, D_MODEL, FF_CHUNK), lambda n, k: (n, 0, 0)),
            (N_SHARD, D_MODEL, GU_SHARD), (D_MODEL, FF_CHUNK), (2 * N_CHUNK, t // TM), rider_gu)
        part[l][which + "_w_gu"] = halves(out, D_MODEL)
        rider_d = between(got_gu) if between else rider_d
        out, got_d = _mm_tn(
            "%s_dwd_%d" % (which, l), a, do, pl.BlockSpec((TM, FF_CHUNK), lambda n, k: (k, n)), row,
            pl.BlockSpec((FF_CHUNK, D_MODEL), lambda n, k: (n, 0)),
            (D_FF, D_MODEL), (FF_CHUNK, D_MODEL), (N_CHUNK, t // TM), rider_d)
        part[l][which + "_w_down"] = halves(out, D_FF // N_SHARD)
        return dx, dgb, got_gu, got_d

    for l in (1, 0):
        s = acts[l]
        wl = s["w"]
        if l == 0:
            def between(got):
                after_pair(1, lower, got)
                return chip_rider(1, lower)

            dy, dgb3, _, got = ffn_grads("ffn2", dy, s["r3"], s["gu2"], s["y2"], wl["wgu2"], wl["wd2"], vec(ln3_g, l), l,
                                         pair_rider(1, lower), between)
            after_chip(1, lower, got)
        else:
            dy, dgb3, _, _ = ffn_grads("ffn2", dy, s["r3"], s["gu2"], s["y2"], wl["wgu2"], wl["wd2"], vec(ln3_g, l), l)
        res, dm, dycat, dgb2 = _out_proj_bwd(dy, s["r2"], wl["wout"], vec(ln2_g, l), 0)
        out, _ = _mm_tn("dwout_%d" % l, s["ycat"], dm, row, row, pl.BlockSpec((D_MODEL, D_MODEL), lambda n, k: (0, 0)),
                        (D_MODEL, D_MODEL), (D_MODEL, D_MODEL), (1, t // TM))
        part[l]["w_out"] = halves(out, OUT_SHARD)
        dz_sc, dz_cc, dconv = _conv_bwd(s["z"], dycat, s["u2"], wl["sc"], wl["cc"], vec(cc_ln_g, l), vec(cc_ln_b, l))
        dost, ld = _attn_dprep(dycat, s["o"], s["lse"])
        dqs, dsink = _attn_bwd_q(s["qs"], s["kf"], s["vf"], dost, ld, attn_sink[l])
        dkf, dvf = _attn_bwd_kv(s["qs"], s["kf"], s["vf"], dost, ld)
        dz_att = _attn_prep_bwd(dqs, dkf, dvf, cos, sin)
        dz = jnp.concatenate([dz_sc, dz_att, dz_cc], axis=1)
        out, got = _mm_tn(
            "dwin_%d" % l, s["y1"], dz, row, pl.BlockSpec((TM, D_IN), lambda n, k: (k, 0)),
            pl.BlockSpec((N_SHARD, D_MODEL, IN_SHARD), lambda n, k: (0, 0, 0)),
            (N_SHARD, D_MODEL, IN_SHARD), (D_MODEL, D_IN), (1, t // TM), pair_rider(l, upper), split=N_SHARD)
        part[l]["w_in"] = halves(out, D_MODEL)
        after_pair(l, upper, got)
        dy = _in_proj_bwd(dz, res, wl["win"], 0)
        dy, dgb1, got, _ = ffn_grads("ffn1", dy, s["r1"], s["gu1"], s["x"], wl["wgu1"], wl["wd1"], vec(ln1_g, l), l,
                                     chip_rider(l, upper))
        after_chip(l, upper, got)
        small[l] = dict(ln1_g=dgb1[0], ln1_b=dgb1[1], ln2_g=dgb2[0], ln2_b=dgb2[1], ln3_g=dgb3[0], ln3_b=dgb3[1],
                        attn_sink=dsink[:, 0], cc_conv_b=dconv[ROW_CCB], cc_ln_g=dconv[ROW_CCG],
                        cc_ln_b=dconv[ROW_CCBETA], sc_conv_w=dconv[ROW_SCW:ROW_SCW + SC_W],
                        cc_conv_w=dconv[ROW_CCW:ROW_CCW + CC_W])
    grad_x = dy[None]

    after_pair(0, lower, _run("pair_exchange_last", pair_rider(0, lower)))
    after_chip(0, lower, _run("chip_exchange_last", chip_rider(0, lower)))
    grads = dict(zip(BIG, _pair_share([reduced[n] for n in BIG])))
    for n in BIG:
        grads[n] = grads[n].reshape(w[n].shape)

    small_full = {n: jnp.stack([small[0][n], small[1][n]]) for n in SMALL}
    small_sum = _unpack_small(_sum_small(_pack_small(small_full)), {n: small_full[n].shape for n in SMALL})
    for n in SMALL:
        g = small_sum[n]
        if n in ("sc_conv_w", "cc_conv_w"):
            g = lax.dynamic_slice_in_dim(g, chip * 64, 64, axis=2)
        grads[n] = g

    delta, new_m, new_v = {}, {}, {}
    for n in BIG:
        shape = w[n].shape
        two_d = (shape[0] * shape[1], shape[2])
        outs = _adamw("adamw_" + n, w[n].reshape(two_d), grads[n].reshape(two_d), mom[n].reshape(two_d),
                      var[n].reshape(two_d), 128)
        delta[n], new_m[n], new_v[n] = [a.reshape(shape) for a in outs]
    shapes = {n: w[n].shape for n in SMALL}
    outs = _adamw("adamw_small", _pack_small({n: w[n] for n in SMALL}), _pack_small({n: grads[n] for n in SMALL}),
                  _pack_small({n: mom[n] for n in SMALL}), _pack_small({n: var[n] for n in SMALL}), 8)
    for d, packed in zip((delta, new_m, new_v), outs):
        d.update(_unpack_small(packed, shapes))

    return (loss, grad_x, *[grads[n] for n in NAMES], *[delta[n] for n in NAMES], *[new_m[n] for n in NAMES],
            *[new_v[n] for n in NAMES])
```

```python
import functools

import numpy as np
import jax
import jax.numpy as jnp
from jax import lax
from jax.experimental import pallas as pl
from jax.experimental.pallas import tpu as pltpu

F32 = jnp.float32
MXU_DTYPE = jnp.bfloat16

D_MODEL = 1024
D_FF = 2816
N_SHARD = 4
D_IN = 2048
GU_SHARD = 2 * D_FF // N_SHARD
FF_CHUNK = GU_SHARD
N_CHUNK = D_FF // FF_CHUNK
IN_SHARD = D_IN // N_SHARD
OUT_SHARD = D_MODEL // N_SHARD
HEAD_DIM = 64
N_Q_HEADS = 8
BLOCK = 128
SC_W = 3
CC_W = 31
D_CONV = 256
HALO = 16
LN_EPS = 1e-5
ALPHA = (2.0 * 2) ** 0.25
NEG = -1e30
ROPE_THETA = 10000.0
ADAM_LR, ADAM_B1, ADAM_B2, ADAM_EPS, ADAM_WD, ADAM_STEP = 0.001, 0.9, 0.999, 1e-08, 0.01, 10

TM = 512
TMC = 256
VMEM_LIMIT = 56 * 1024 * 1024
MESH = pl.DeviceIdType.MESH
ANY = pl.BlockSpec(memory_space=pl.ANY)


def _cparams(*sem):
    return pltpu.CompilerParams(dimension_semantics=sem, vmem_limit_bytes=VMEM_LIMIT)


def _dot(a, b):
    return jnp.dot(a, b, preferred_element_type=F32)


def _dot_nt(a, b):
    return lax.dot_general(a, b, (((1,), (1,)), ((), ())), preferred_element_type=F32)


def _dot_tn(a, b):
    return lax.dot_general(a, b, (((0,), (0,)), ((), ())), preferred_element_type=F32)


def _mx(a):
    return a.astype(MXU_DTYPE)


def _mean(a):
    return jnp.mean(a, axis=-1, keepdims=True)


def _ln_stats(r):
    xc = r - _mean(r)
    rstd = lax.rsqrt(_mean(xc * xc) + LN_EPS)
    return xc * rstd, rstd


def _ln_bwd(dy, xh, rstd, gamma):
    dxh = dy * gamma
    return rstd * (dxh - _mean(dxh) - xh * _mean(dxh * xh))


def _colsum(a):
    return jnp.sum(a, axis=0, keepdims=True)


def _sigmoid(a):
    return 1.0 / (1.0 + jnp.exp(-a))


def _call(body, args, *, name, grid, in_specs, out_specs, out_shape, scratch, sem, rider=None):
    if rider is None:
        outs = pl.pallas_call(
            body, name=name, grid=grid, in_specs=in_specs, out_specs=out_specs, out_shape=out_shape,
            scratch_shapes=scratch, compiler_params=_cparams(*sem))(*args)
        return list(outs), []
    n_in, n_out, n_sc = len(in_specs), len(out_specs), len(scratch)
    r_in, r_out = len(rider.ins), len(rider.outs)

    def carrying(*refs):
        cuts = np.cumsum([0, n_in, r_in, n_out, r_out, n_sc])
        ins, rins, outs, routs, scr = [refs[a:b] for a, b in zip(cuts[:-1], cuts[1:])]
        rsems = refs[cuts[-1]:]
        first = functools.reduce(jnp.logical_and, [pl.program_id(d) == 0 for d in range(len(grid))])
        last = functools.reduce(jnp.logical_and, [pl.program_id(d) == grid[d] - 1 for d in range(len(grid))])

        @pl.when(first)
        def _():
            rider.start(rins, routs, rsems)

        body(*ins, *outs, *scr)

        @pl.when(last)
        def _():
            rider.finish(rins, routs, rsems)

    outs = pl.pallas_call(
        carrying, name=name, grid=grid, in_specs=list(in_specs) + [ANY] * r_in,
        out_specs=list(out_specs) + [ANY] * r_out, out_shape=list(out_shape) + list(rider.outs),
        scratch_shapes=list(scratch) + list(rider.sems), compiler_params=_cparams(*(("arbitrary",) * len(grid))),
    )(*args, *rider.ins)
    return list(outs[:n_out]), list(outs[n_out:])


def _ffn_fwd(name, x, wgu, wd, gamma, beta, l, rider=None):
    t = x.shape[0]
    nc = N_CHUNK

    def body(x_ref, wg_ref, wu_ref, wd_ref, g_ref, b_ref, y_ref, r_ref, gu_ref, xb_s, acc_s):
        c = pl.program_id(1)

        @pl.when(c == 0)
        def _():
            xb_s[...] = _mx(x_ref[...])
            acc_s[...] = jnp.zeros_like(acc_s)

        xb = xb_s[...]
        hg = _dot(xb, wg_ref[...])
        hu = _dot(xb, wu_ref[...])
        gu_ref[0] = _mx(hg)
        gu_ref[1] = _mx(hu)
        a = (hg * _sigmoid(hg)) * hu
        acc_s[...] += _dot(_mx(a), wd_ref[...])

        @pl.when(c == nc - 1)
        def _():
            r = ALPHA * x_ref[...] + 0.5 * acc_s[...]
            xh, _ = _ln_stats(r)
            r_ref[...] = r
            y_ref[...] = xh * g_ref[...] + b_ref[...]

    row = pl.BlockSpec((TM, D_MODEL), lambda i, c: (i, 0))
    vec = pl.BlockSpec((1, D_MODEL), lambda i, c: (0, 0))
    return _call(
        body, (x, wgu, wgu, wd, gamma, beta), name=name, grid=(t // TM, nc),
        in_specs=[row,
                  pl.BlockSpec((None, None, D_MODEL, FF_CHUNK), lambda i, c: (l, c, 0, 0)),
                  pl.BlockSpec((None, None, D_MODEL, FF_CHUNK), lambda i, c: (l, N_CHUNK + c, 0, 0)),
                  pl.BlockSpec((None, FF_CHUNK, D_MODEL), lambda i, c: (l, c, 0)),
                  vec, vec],
        out_specs=[row, row, pl.BlockSpec((2, TM, FF_CHUNK), lambda i, c: (0, i, c))],
        out_shape=[jax.ShapeDtypeStruct((t, D_MODEL), F32), jax.ShapeDtypeStruct((t, D_MODEL), F32),
                   jax.ShapeDtypeStruct((2, t, D_FF), MXU_DTYPE)],
        scratch=[pltpu.VMEM((TM, D_MODEL), MXU_DTYPE), pltpu.VMEM((TM, D_MODEL), F32)],
        sem=("parallel", "arbitrary"), rider=rider)


def _norm_bwd_tail(dy, r_ref, g_ref, res_ref, do_ref, dgb_ref):
    @pl.when(pl.program_id(0) == 0)
    def _():
        dgb_ref[...] = jnp.zeros_like(dgb_ref)

    xh, rstd = _ln_stats(r_ref[...])
    dr = _ln_bwd(dy, xh, rstd, g_ref[...])
    do_ref[...] = _mx(0.5 * dr)
    res_ref[...] = ALPHA * dr
    dgb_ref[0:1, :] += _colsum(dy * xh)
    dgb_ref[1:2, :] += _colsum(dy)


def _norm_tail_specs(t):
    row = pl.BlockSpec((TM, D_MODEL), lambda i: (i, 0))
    return ([row, pl.BlockSpec((1, D_MODEL), lambda i: (0, 0))],
            [row, row, pl.BlockSpec((8, D_MODEL), lambda i: (0, 0))],
            [jax.ShapeDtypeStruct((t, D_MODEL), F32), jax.ShapeDtypeStruct((t, D_MODEL), MXU_DTYPE),
             jax.ShapeDtypeStruct((8, D_MODEL), F32)])


def _ffn_bwd(res, do, gu, wgu, wd, l, tail=None, rider=None):
    t = res.shape[0]
    nc = N_CHUNK
    row1 = pl.BlockSpec((TM, D_MODEL), lambda i: (i, 0))

    def hidden_body(do_ref, gu_ref, wd_ref, dh_ref, a_ref):
        da = _dot_nt(do_ref[...], wd_ref[...])
        g = gu_ref[0].astype(F32)
        u = gu_ref[1].astype(F32)
        s = _sigmoid(g)
        sil = g * s
        a_ref[...] = _mx(sil * u)
        dh_ref[0] = _mx(da * u * (s * (1.0 + g * (1.0 - s))))
        dh_ref[1] = _mx(da * sil)

    hid = pl.BlockSpec((2, TM, FF_CHUNK), lambda c, i: (0, i, c))
    (dh, a), got = _call(
        hidden_body, (do, gu, wd), name="ffn_bwd_hidden" if rider is None else "ffn_bwd_hidden_carry", grid=(nc, t // TM),
        in_specs=[pl.BlockSpec((TM, D_MODEL), lambda c, i: (i, 0)), hid,
                  pl.BlockSpec((None, FF_CHUNK, D_MODEL), lambda c, i: (l, c, 0))],
        out_specs=[hid, pl.BlockSpec((TM, FF_CHUNK), lambda c, i: (i, c))],
        out_shape=[jax.ShapeDtypeStruct((2, t, D_FF), MXU_DTYPE), jax.ShapeDtypeStruct((t, D_FF), MXU_DTYPE)],
        scratch=[], sem=("parallel", "parallel"), rider=rider)

    def input_body(res_ref, dh_ref, w_ref, *rest):
        acc = res_ref[...]
        for j in range(N_SHARD):
            part = dh_ref[j // N_CHUNK][:, (j % N_CHUNK) * FF_CHUNK:(j % N_CHUNK + 1) * FF_CHUNK]
            acc += _dot_nt(part, w_ref[j])
        if tail is None:
            rest[0][...] = acc
        else:
            _norm_bwd_tail(acc, *rest)

    in_specs = [row1, pl.BlockSpec((2, TM, D_FF), lambda i: (0, i, 0)),
                pl.BlockSpec((None, N_SHARD, D_MODEL, GU_SHARD), lambda i: (l, 0, 0, 0))]
    if tail is None:
        dx = pl.pallas_call(
            input_body, name="ffn_bwd_input", grid=(t // TM,), in_specs=in_specs, out_specs=row1,
            out_shape=jax.ShapeDtypeStruct((t, D_MODEL), F32), compiler_params=_cparams("parallel"),
        )(res, dh, wgu)
    else:
        tail_in, tail_out, tail_shape = _norm_tail_specs(t)
        dx = pl.pallas_call(
            input_body, name="ffn_bwd_input_norm", grid=(t // TM,), in_specs=in_specs + tail_in, out_specs=tail_out,
            out_shape=tail_shape, compiler_params=_cparams("arbitrary"),
        )(res, dh, wgu, *tail)
    return dx, dh, a, got


def _mm_tn(name, a, b, a_spec, b_spec, out_spec, out_shape, acc_shape, grid, rider=None, split=1):
    nk = grid[-1]
    width = acc_shape[1] // split

    def body(*refs):
        a_ref, b_ref = refs[0], refs[1]
        o_ref, acc = refs[-2], refs[-1]
        k = pl.program_id(len(grid) - 1)

        @pl.when(k == 0)
        def _():
            acc[...] = jnp.zeros_like(acc)

        acc[...] += _dot_tn(_mx(a_ref[...]), _mx(b_ref[...]))

        @pl.when(k == nk - 1)
        def _():
            if split == 1:
                o_ref[...] = acc[...]
            else:
                for j in range(split):
                    o_ref[j] = acc[:, j * width:(j + 1) * width]

    sem = ("parallel",) * (len(grid) - 1) + ("arbitrary",)
    (out,), got = _call(
        body, (a, b), name=name, grid=grid, in_specs=[a_spec, b_spec], out_specs=[out_spec],
        out_shape=[jax.ShapeDtypeStruct(out_shape, F32)], scratch=[pltpu.VMEM(acc_shape, F32)], sem=sem, rider=rider)
    return out, got


def _in_proj(x, w_in, l):
    t = x.shape[0]

    def body(x_ref, w_ref, z_ref):
        xb = _mx(x_ref[...])
        for j in range(N_SHARD):
            z_ref[:, j * IN_SHARD:(j + 1) * IN_SHARD] = _dot(xb, w_ref[j])

    return pl.pallas_call(
        body, name="in_proj", grid=(t // TM,),
        in_specs=[pl.BlockSpec((TM, D_MODEL), lambda i: (i, 0)),
                  pl.BlockSpec((None, N_SHARD, D_MODEL, IN_SHARD), lambda i: (l, 0, 0, 0))],
        out_specs=pl.BlockSpec((TM, D_IN), lambda i: (i, 0)),
        out_shape=jax.ShapeDtypeStruct((t, D_IN), F32),
        compiler_params=_cparams("parallel"),
    )(x, w_in)


def _in_proj_bwd(dz, dx_res, w_in, l, r, gamma):
    t = dz.shape[0]

    def body(dz_ref, res_ref, w_ref, *tail):
        acc = res_ref[...]
        for j in range(N_SHARD):
            acc += _dot_nt(dz_ref[:, j * IN_SHARD:(j + 1) * IN_SHARD], w_ref[j])
        _norm_bwd_tail(acc, *tail)

    row = pl.BlockSpec((TM, D_MODEL), lambda i: (i, 0))
    tail_in, tail_out, tail_shape = _norm_tail_specs(t)
    return pl.pallas_call(
        body, name="in_proj_bwd", grid=(t // TM,),
        in_specs=[pl.BlockSpec((TM, D_IN), lambda i: (i, 0)), row,
                  pl.BlockSpec((None, N_SHARD, D_MODEL, IN_SHARD), lambda i: (l, 0, 0, 0))] + tail_in,
        out_specs=tail_out, out_shape=tail_shape, compiler_params=_cparams("arbitrary"),
    )(dz, dx_res, w_in, r, gamma)


def _out_proj(ycat, x, w_out, gamma, beta, l):
    t = x.shape[0]

    def body(yc_ref, x_ref, w_ref, g_ref, b_ref, y_ref, r_ref):
        r = ALPHA * x_ref[...] + _dot(yc_ref[...], w_ref[...])
        xh, _ = _ln_stats(r)
        r_ref[...] = r
        y_ref[...] = xh * g_ref[...] + b_ref[...]

    row = pl.BlockSpec((TM, D_MODEL), lambda i: (i, 0))
    vec = pl.BlockSpec((1, D_MODEL), lambda i: (0, 0))
    return pl.pallas_call(
        body, name="out_proj", grid=(t // TM,),
        in_specs=[row, row, pl.BlockSpec((None, D_MODEL, D_MODEL), lambda i: (l, 0, 0)), vec, vec],
        out_specs=[row, row],
        out_shape=[jax.ShapeDtypeStruct((t, D_MODEL), F32)] * 2,
        compiler_params=_cparams("parallel"),
    )(ycat, x, w_out, gamma, beta)


def _out_proj_bwd(dy, r, w_out, gamma, l):
    t = dy.shape[0]

    def body(dy_ref, r_ref, w_ref, g_ref, res_ref, dm_ref, dyc_ref, dgb_ref):
        @pl.when(pl.program_id(0) == 0)
        def _():
            dgb_ref[...] = jnp.zeros_like(dgb_ref)

        xh, rstd = _ln_stats(r_ref[...])
        dy = dy_ref[...]
        dr = _ln_bwd(dy, xh, rstd, g_ref[...])
        res_ref[...] = ALPHA * dr
        dm = _mx(dr)
        dm_ref[...] = dm
        dyc_ref[...] = _dot_nt(dm, w_ref[...])
        dgb_ref[0:1, :] += _colsum(dy * xh)
        dgb_ref[1:2, :] += _colsum(dy)

    row = pl.BlockSpec((TM, D_MODEL), lambda i: (i, 0))
    return pl.pallas_call(
        body, name="out_proj_bwd", grid=(t // TM,),
        in_specs=[row, row, pl.BlockSpec((None, D_MODEL, D_MODEL), lambda i: (l, 0, 0)),
                  pl.BlockSpec((1, D_MODEL), lambda i: (0, 0))],
        out_specs=[row, row, row, pl.BlockSpec((8, D_MODEL), lambda i: (0, 0))],
        out_shape=[jax.ShapeDtypeStruct((t, D_MODEL), F32), jax.ShapeDtypeStruct((t, D_MODEL), MXU_DTYPE),
                   jax.ShapeDtypeStruct((t, D_MODEL), F32), jax.ShapeDtypeStruct((8, D_MODEL), F32)],
        compiler_params=_cparams("arbitrary"),
    )(dy, r, w_out, gamma)


def _halo_specs(t, width, col):
    per = TMC // HALO
    last = t // HALO - 1
    return [pl.BlockSpec((HALO, width), lambda i: (jnp.maximum(i * per - 1, 0), col)),
            pl.BlockSpec((TMC, width), lambda i: (i, col)),
            pl.BlockSpec((HALO, width), lambda i: (jnp.minimum((i + 1) * per, last), col))]


def _extend(refs, i, nt):
    p_ref, c_ref, n_ref = refs
    p = jnp.where(i > 0, p_ref[...].astype(F32), 0.0)
    n = jnp.where(i < nt - 1, n_ref[...].astype(F32), 0.0)
    return jnp.concatenate([p, c_ref[...].astype(F32), n], axis=0)


def _conv_fwd(z, sc_w, cc_w, cc_cb, cc_g, cc_b):
    t = z.shape[0]
    nt = t // TMC

    def body(*refs):
        b_ref = refs[0]
        c3, h3, a3, g3 = refs[1:4], refs[4:7], refs[7:10], refs[10:13]
        scw_ref, ccw_ref, cb_ref, lg_ref, lb_ref = refs[13:18]
        ysc_ref, ycc_ref, u2_ref, e_s = refs[18:22]
        i = pl.program_id(0)
        e_s[...] = _extend(c3, i, nt) * _extend(h3, i, nt)
        cv = jnp.zeros((TMC, D_CONV), F32)
        for k in range(SC_W):
            cv += scw_ref[k:k + 1, :] * e_s[pl.ds(HALO + k - 1, TMC), :]
        ysc_ref[...] = _mx(b_ref[...] * cv)
        e_s[...] = _extend(a3, i, nt) * _sigmoid(_extend(g3, i, nt))
        u2 = jnp.zeros((TMC, D_CONV), F32) + cb_ref[...]
        for k in range(CC_W):
            u2 += ccw_ref[k:k + 1, :] * e_s[pl.ds(HALO + k - 15, TMC), :]
        u2_ref[...] = u2
        xh, _ = _ln_stats(u2)
        n = xh * lg_ref[...] + lb_ref[...]
        ycc_ref[...] = _mx(n * _sigmoid(n))

    tile = pl.BlockSpec((TMC, D_CONV), lambda i: (i, 0))
    vec = pl.BlockSpec((1, D_CONV), lambda i: (0, 0))
    in_specs = ([pl.BlockSpec((TMC, D_CONV), lambda i: (i, 0))] + _halo_specs(t, D_CONV, 1) + _halo_specs(t, D_CONV, 2)
                + _halo_specs(t, D_CONV, 6) + _halo_specs(t, D_CONV, 7)
                + [pl.BlockSpec((SC_W, D_CONV), lambda i: (0, 0)), pl.BlockSpec((CC_W, D_CONV), lambda i: (0, 0)),
                   vec, vec, vec])
    return pl.pallas_call(
        body, name="conv_fwd", grid=(nt,), in_specs=in_specs, out_specs=[tile, tile, tile],
        out_shape=[jax.ShapeDtypeStruct((t, D_CONV), MXU_DTYPE), jax.ShapeDtypeStruct((t, D_CONV), MXU_DTYPE),
                   jax.ShapeDtypeStruct((t, D_CONV), F32)],
        scratch_shapes=[pltpu.VMEM((TMC + 2 * HALO, D_CONV), F32)],
        compiler_params=_cparams("parallel"),
    )(*([z] * 13), sc_w, cc_w, cc_cb, cc_g, cc_b)


ROW_CCW, ROW_CCB, ROW_CCG, ROW_CCBETA, ROW_SCW, CONV_ROWS = 0, 31, 32, 33, 34, 40


def _conv_bwd(z, dycat, u2, sc_w, cc_w, cc_g, cc_b):
    t = z.shape[0]
    nt = t // TMC

    def body(*refs):
        b3, c3, h3, a3, g3 = refs[0:3], refs[3:6], refs[6:9], refs[9:12], refs[12:15]
        dys3, dyc3, u3 = refs[15:18], refs[18:21], refs[21:24]
        scw_ref, ccw_ref, lg_ref, lb_ref = refs[24:28]
        dsc_ref, dcc_ref, sm_ref, e_s, f_s = refs[28:33]
        i = pl.program_id(0)

        @pl.when(i == 0)
        def _():
            sm_ref[...] = jnp.zeros_like(sm_ref)

        cur = pl.ds(HALO, TMC)
        e_s[...] = _extend(c3, i, nt) * _extend(h3, i, nt)
        f_s[...] = _extend(dys3, i, nt) * _extend(b3, i, nt)
        cv = jnp.zeros((TMC, D_CONV), F32)
        dp = jnp.zeros((TMC, D_CONV), F32)
        dcv = f_s[cur, :]
        for k in range(SC_W):
            win = e_s[pl.ds(HALO + k - 1, TMC), :]
            cv += scw_ref[k:k + 1, :] * win
            dp += scw_ref[k:k + 1, :] * f_s[pl.ds(HALO - k + 1, TMC), :]
            sm_ref[ROW_SCW + k:ROW_SCW + k + 1, :] += _colsum(dcv * win)
        dsc_ref[:, 0:D_CONV] = _mx(dys3[1][...] * cv)
        dsc_ref[:, D_CONV:2 * D_CONV] = _mx(dp * h3[1][...])
        dsc_ref[:, 2 * D_CONV:3 * D_CONV] = _mx(dp * c3[1][...])
        xh, rstd = _ln_stats(_extend(u3, i, nt))
        n = xh * lg_ref[...] + lb_ref[...]
        sg = _sigmoid(n)
        dn = _extend(dyc3, i, nt) * (sg * (1.0 + n * (1.0 - sg)))
        f_s[...] = _ln_bwd(dn, xh, rstd, lg_ref[...])
        sm_ref[ROW_CCG:ROW_CCG + 1, :] += _colsum((dn * xh)[HALO:HALO + TMC])
        sm_ref[ROW_CCBETA:ROW_CCBETA + 1, :] += _colsum(dn[HALO:HALO + TMC])
        sig_g = _sigmoid(_extend(g3, i, nt))
        e_s[...] = _extend(a3, i, nt) * sig_g
        du2 = f_s[cur, :]
        sm_ref[ROW_CCB:ROW_CCB + 1, :] += _colsum(du2)
        duu = jnp.zeros((TMC, D_CONV), F32)
        for k in range(CC_W):
            duu += ccw_ref[k:k + 1, :] * f_s[pl.ds(HALO + 15 - k, TMC), :]
            sm_ref[ROW_CCW + k:ROW_CCW + k + 1, :] += _colsum(du2 * e_s[pl.ds(HALO + k - 15, TMC), :])
        sgc = sig_g[HALO:HALO + TMC]
        dcc_ref[:, 0:D_CONV] = _mx(duu * sgc)
        dcc_ref[:, D_CONV:2 * D_CONV] = _mx(duu * a3[1][...] * sgc * (1.0 - sgc))

    vec = pl.BlockSpec((1, D_CONV), lambda i: (0, 0))
    in_specs = []
    for col in (0, 1, 2, 6, 7):
        in_specs += _halo_specs(t, D_CONV, col)
    in_specs += _halo_specs(t, D_CONV, 0) + _halo_specs(t, D_CONV, 3) + _halo_specs(t, D_CONV, 0)
    in_specs += [pl.BlockSpec((SC_W, D_CONV), lambda i: (0, 0)), pl.BlockSpec((CC_W, D_CONV), lambda i: (0, 0)), vec, vec]
    return pl.pallas_call(
        body, name="conv_bwd", grid=(nt,), in_specs=in_specs,
        out_specs=[pl.BlockSpec((TMC, 3 * D_CONV), lambda i: (i, 0)), pl.BlockSpec((TMC, 2 * D_CONV), lambda i: (i, 0)),
                   pl.BlockSpec((CONV_ROWS, D_CONV), lambda i: (0, 0))],
        out_shape=[jax.ShapeDtypeStruct((t, 3 * D_CONV), MXU_DTYPE), jax.ShapeDtypeStruct((t, 2 * D_CONV), MXU_DTYPE),
                   jax.ShapeDtypeStruct((CONV_ROWS, D_CONV), F32)],
        scratch_shapes=[pltpu.VMEM((TMC + 2 * HALO, D_CONV), F32)] * 2,
        compiler_params=_cparams("arbitrary"),
    )(*([z] * 15), *([dycat] * 6), *([u2] * 3), sc_w, cc_w, cc_g, cc_b)


def _lane(shape):
    return lax.broadcasted_iota(jnp.int32, shape, 1)


def _swap_halves(x):
    w = x.shape[1]
    lo = (_lane(x.shape) % HEAD_DIM) < HEAD_DIM // 2
    return jnp.where(lo, pltpu.roll(x, w - HEAD_DIM // 2, 1), pltpu.roll(x, HEAD_DIM // 2, 1))


def _half(shape, g):
    lane = _lane(shape)
    return lane < HEAD_DIM if g == 0 else lane >= HEAD_DIM


GROUP_ROWS = 4 * BLOCK


def _stack_heads(tiles, out_ref, nblk):
    for tt in range(4):
        g = tt // 2
        for slot in range(2):
            s = 2 * (tt % 2) + slot
            piece = tiles[tt] if slot == g else pltpu.roll(tiles[tt], HEAD_DIM, 1)
            piece = jnp.where(_half(piece.shape, g), piece, 0.0).astype(out_ref.dtype)
            for b in range(nblk):
                at = GROUP_ROWS * b + BLOCK * s
                out_ref[g, at:at + BLOCK, :] = piece[BLOCK * b:BLOCK * (b + 1)]


def _unstack_heads(ref, nblk):
    tiles = []
    for tt in range(4):
        g = tt // 2
        tile = None
        for slot in range(2):
            s = 2 * (tt % 2) + slot
            rows = [ref[g, GROUP_ROWS * b + BLOCK * s:GROUP_ROWS * b + BLOCK * (s + 1), :] for b in range(nblk)]
            piece = rows[0] if nblk == 1 else jnp.concatenate(rows, axis=0)
            if slot != g:
                piece = pltpu.roll(piece, HEAD_DIM, 1)
            tile = piece if tile is None else tile + piece
        tiles.append(tile)
    return tiles


def _attn_prep(z, cos, sin):
    t = z.shape[0]
    nblk = TM // BLOCK

    def body(qa_ref, qb_ref, k_ref, v_ref, cos_ref, sin_ref, qst_ref, kr_ref, vb_ref):
        cs, sn = cos_ref[...], sin_ref[...]

        def rope(x):
            return x * cs + _swap_halves(x) * sn

        tiles = []
        for tt in range(4):
            src = qa_ref if tt < 2 else qb_ref
            tiles.append(rope(src[:, (tt % 2) * BLOCK:(tt % 2 + 1) * BLOCK]) * (HEAD_DIM ** -0.5))
        _stack_heads(tiles, qst_ref, nblk)
        kr_ref[...] = _mx(rope(k_ref[...]))
        vb_ref[...] = _mx(v_ref[...])

    def col(width, j):
        return pl.BlockSpec((TM, width), lambda i: (i, j))

    return pl.pallas_call(
        body, name="attn_prep", grid=(t // TM,),
        in_specs=[col(256, 3), col(256, 4), col(128, 10), col(128, 11), col(128, 0), col(128, 0)],
        out_specs=[pl.BlockSpec((2, 4 * TM, BLOCK), lambda i: (0, i, 0)), col(128, 0), col(128, 0)],
        out_shape=[jax.ShapeDtypeStruct((2, 4 * t, BLOCK), MXU_DTYPE), jax.ShapeDtypeStruct((t, BLOCK), MXU_DTYPE),
                   jax.ShapeDtypeStruct((t, BLOCK), MXU_DTYPE)],
        compiler_params=_cparams("parallel"),
    )(z, z, z, z, cos, sin)


def _attn_dprep(dycat, ost, lst):
    t = dycat.shape[0]
    nblk = TM // BLOCK

    def body(da_ref, db_ref, o_ref, l_ref, dost_ref, ld_ref, st_s):
        tiles = []
        for tt in range(4):
            src = da_ref if tt < 2 else db_ref
            tiles.append(src[:, (tt % 2) * BLOCK:(tt % 2 + 1) * BLOCK])
        _stack_heads(tiles, st_s, nblk)
        for g in range(2):
            do = st_s[g]
            dost_ref[g] = _mx(do)
            dsum = jnp.sum(do * o_ref[g], axis=-1, keepdims=True)
            ld_ref[g] = jnp.where(_lane(do.shape) < HEAD_DIM, l_ref[g], dsum)

    stacked = pl.BlockSpec((2, 4 * TM, BLOCK), lambda i: (0, i, 0))
    return pl.pallas_call(
        body, name="attn_dprep", grid=(t // TM,),
        in_specs=[pl.BlockSpec((TM, 256), lambda i: (i, 1)), pl.BlockSpec((TM, 256), lambda i: (i, 2)), stacked, stacked],
        out_specs=[stacked, stacked],
        out_shape=[jax.ShapeDtypeStruct((2, 4 * t, BLOCK), MXU_DTYPE), jax.ShapeDtypeStruct((2, 4 * t, BLOCK), F32)],
        scratch_shapes=[pltpu.VMEM((2, 4 * TM, BLOCK), F32)],
        compiler_params=_cparams("parallel"),
    )(dycat, dycat, ost, lst)


def _attn_prep_bwd(dqst, dk, dv, cos, sin):
    t = dk.shape[0]
    nblk = TM // BLOCK

    def body(dq_ref, dk_ref, dv_ref, cos_ref, sin_ref, dz_ref):
        cs, sn = cos_ref[...], sin_ref[...]

        def rope_bwd(d):
            return d * cs + _swap_halves(d * sn)

        for tt, tile in enumerate(_unstack_heads(dq_ref, nblk)):
            dz_ref[:, tt * BLOCK:(tt + 1) * BLOCK] = _mx(rope_bwd(tile * (HEAD_DIM ** -0.5)))
        dz_ref[:, 4 * BLOCK:5 * BLOCK] = _mx(rope_bwd(dk_ref[...]))
        dz_ref[:, 5 * BLOCK:6 * BLOCK] = _mx(dv_ref[...])

    def col(width):
        return pl.BlockSpec((TM, width), lambda i: (i, 0))

    return pl.pallas_call(
        body, name="attn_prep_bwd", grid=(t // TM,),
        in_specs=[pl.BlockSpec((2, 4 * TM, BLOCK), lambda i: (0, i, 0)), col(128), col(128), col(128), col(128)],
        out_specs=col(768), out_shape=jax.ShapeDtypeStruct((t, 768), MXU_DTYPE),
        compiler_params=_cparams("parallel"),
    )(dqst, dk, dv, cos, sin)


def _nbr_specs(nb, width, col):
    return [pl.BlockSpec((BLOCK, width), lambda n: (jnp.maximum(n - 1, 0), col)),
            pl.BlockSpec((BLOCK, width), lambda n: (n, col)),
            pl.BlockSpec((BLOCK, width), lambda n: (jnp.minimum(n + 1, nb - 1), col))]


def _stacked_specs(nb):
    return [pl.BlockSpec((2, GROUP_ROWS, BLOCK), lambda n: (0, jnp.maximum(n - 1, 0), 0)),
            pl.BlockSpec((2, GROUP_ROWS, BLOCK), lambda n: (0, n, 0)),
            pl.BlockSpec((2, GROUP_ROWS, BLOCK), lambda n: (0, jnp.minimum(n + 1, nb - 1), 0))]


def _query_index():
    row = lax.broadcasted_iota(jnp.int32, (GROUP_ROWS, BLOCK), 0)
    return row & (BLOCK - 1), lax.broadcasted_iota(jnp.int32, (GROUP_ROWS, BLOCK), 1)


def _sink_column(sink_ref, g):
    band = lax.broadcasted_iota(jnp.int32, (GROUP_ROWS, 1), 0) // BLOCK
    col = jnp.zeros((GROUP_ROWS, 1), F32) + sink_ref[4 * g]
    for s in range(1, 4):
        col = jnp.where(band == s, sink_ref[4 * g + s], col)
    return col


def _attn_fwd(qst, kr, vb, sink):
    t = kr.shape[0]
    nb = t // BLOCK

    def body(q_ref, kp_ref, kc_ref, kn_ref, vp_ref, vc_ref, vn_ref, sink_ref, o_ref, ost_ref, lst_ref):
        n = pl.program_id(0)
        qi, kj = _query_index()
        m_prev, m_next = (kj >= qi) & (n > 0), (kj <= qi) & (n < nb - 1)
        nat = [None] * 4
        for g in range(2):
            q = q_ref[g]
            sp = jnp.where(m_prev, _dot_nt(q, kp_ref[...]), NEG)
            sc = _dot_nt(q, kc_ref[...])
            sn = jnp.where(m_next, _dot_nt(q, kn_ref[...]), NEG)
            sk = _sink_column(sink_ref, g)
            m = jnp.maximum(jnp.max(jnp.maximum(jnp.maximum(sp, sc), sn), axis=-1, keepdims=True), sk)
            pp, pc, pn = jnp.exp(sp - m), jnp.exp(sc - m), jnp.exp(sn - m)
            den = jnp.sum(pp + pc + pn, axis=-1, keepdims=True) + jnp.exp(sk - m)
            o = (_dot(_mx(pp), vp_ref[...]) + _dot(_mx(pc), vc_ref[...]) + _dot(_mx(pn), vn_ref[...])) / den
            o = jnp.where(_half(o.shape, g), o, 0.0)
            ost_ref[g] = o
            lst_ref[g] = jnp.broadcast_to(m + jnp.log(den), (GROUP_ROWS, BLOCK))
            for s in range(4):
                tt, slot = 2 * g + s // 2, s % 2
                piece = o[BLOCK * s:BLOCK * (s + 1)]
                if slot != g:
                    piece = pltpu.roll(piece, HEAD_DIM, 1)
                nat[tt] = piece if nat[tt] is None else nat[tt] + piece
        for tt in range(4):
            o_ref[:, tt * BLOCK:(tt + 1) * BLOCK] = _mx(nat[tt])

    stacked = pl.BlockSpec((2, GROUP_ROWS, BLOCK), lambda n: (0, n, 0))
    return pl.pallas_call(
        body, name="attn_fwd", grid=(nb,),
        in_specs=[stacked] + _nbr_specs(nb, BLOCK, 0) + _nbr_specs(nb, BLOCK, 0) + [pl.BlockSpec(memory_space=pltpu.SMEM)],
        out_specs=[pl.BlockSpec((BLOCK, 512), lambda n: (n, 0)), stacked, stacked],
        out_shape=[jax.ShapeDtypeStruct((t, 512), MXU_DTYPE), jax.ShapeDtypeStruct((2, 4 * t, BLOCK), F32),
                   jax.ShapeDtypeStruct((2, 4 * t, BLOCK), F32)],
        compiler_params=_cparams("parallel"),
    )(qst, kr, kr, kr, vb, vb, vb, sink)


def _lse_and_dsum(ld):
    return ld[:, 0:1], pltpu.roll(ld, HEAD_DIM, 1)[:, 0:1]


def _attn_bwd_q(qst, kr, vb, dost, ld, sink):
    t = kr.shape[0]
    nb = t // BLOCK

    def body(q_ref, kp_ref, kc_ref, kn_ref, vp_ref, vc_ref, vn_ref, do_ref, ld_ref, sink_ref, dq_ref, ds_ref):
        n = pl.program_id(0)

        @pl.when(n == 0)
        def _():
            ds_ref[...] = jnp.zeros_like(ds_ref)

        qi, kj = _query_index()
        m_prev, m_next = (kj >= qi) & (n > 0), (kj <= qi) & (n < nb - 1)
        for g in range(2):
            q, do = q_ref[g], do_ref[g]
            lse, dsum = _lse_and_dsum(ld_ref[g])
            acc = jnp.zeros((GROUP_ROWS, BLOCK), F32)
            for k_ref, v_ref, valid in ((kp_ref, vp_ref, m_prev), (kc_ref, vc_ref, None), (kn_ref, vn_ref, m_next)):
                sc = _dot_nt(q, k_ref[...])
                if valid is not None:
                    sc = jnp.where(valid, sc, NEG)
                p = jnp.exp(sc - lse)
                dsc = p * (_dot_nt(do, v_ref[...]) - dsum)
                acc += _dot(_mx(dsc), k_ref[...])
            dq_ref[g] = jnp.where(_half(acc.shape, g), acc, 0.0)
            dsk = jnp.exp(_sink_column(sink_ref, g) - lse) * dsum
            for s in range(4):
                h = 4 * g + s
                ds_ref[h:h + 1, :] -= jnp.sum(dsk[BLOCK * s:BLOCK * (s + 1)], axis=0, keepdims=True)

    stacked = pl.BlockSpec((2, GROUP_ROWS, BLOCK), lambda n: (0, n, 0))
    return pl.pallas_call(
        body, name="attn_bwd_q", grid=(nb,),
        in_specs=[stacked] + _nbr_specs(nb, BLOCK, 0) + _nbr_specs(nb, BLOCK, 0)
        + [stacked, stacked, pl.BlockSpec(memory_space=pltpu.SMEM)],
        out_specs=[stacked, pl.BlockSpec((8, BLOCK), lambda n: (0, 0))],
        out_shape=[jax.ShapeDtypeStruct((2, 4 * t, BLOCK), F32), jax.ShapeDtypeStruct((8, BLOCK), F32)],
        compiler_params=_cparams("arbitrary"),
    )(qst, kr, kr, kr, vb, vb, vb, dost, ld, sink)


def _attn_bwd_kv(qst, kr, vb, dost, ld):
    t = kr.shape[0]
    nb = t // BLOCK

    def body(*refs):
        q3, do3, ld3 = refs[0:3], refs[3:6], refs[6:9]
        k_ref, v_ref, dk_ref, dv_ref = refs[9:13]
        j = pl.program_id(0)
        qi, kj = _query_index()
        valid = ((kj <= qi) & (j > 0), None, (kj >= qi) & (j < nb - 1))
        k, v = k_ref[...], v_ref[...]
        dk = jnp.zeros((BLOCK, BLOCK), F32)
        dv = jnp.zeros((BLOCK, BLOCK), F32)
        for g in range(2):
            for b in range(3):
                q, do = q3[b][g], do3[b][g]
                lse, dsum = _lse_and_dsum(ld3[b][g])
                sc = _dot_nt(q, k)
                if valid[b] is not None:
                    sc = jnp.where(valid[b], sc, NEG)
                p = jnp.exp(sc - lse)
                dsc = p * (_dot_nt(do, v) - dsum)
                dv += _dot_tn(_mx(p), do)
                dk += _dot_tn(_mx(dsc), q)
        dk_ref[...] = dk
        dv_ref[...] = dv

    cur = pl.BlockSpec((BLOCK, BLOCK), lambda n: (n, 0))
    return pl.pallas_call(
        body, name="attn_bwd_kv", grid=(nb,),
        in_specs=_stacked_specs(nb) * 3 + [cur, cur],
        out_specs=[cur, cur], out_shape=[jax.ShapeDtypeStruct((t, BLOCK), F32)] * 2,
        compiler_params=_cparams("parallel"),
    )(*([qst] * 3), *([dost] * 3), *([ld] * 3), kr, vb)


def _loss_head(y, target, r, gamma):
    t = y.shape[0]

    def body(y_ref, t_ref, r_ref, g_ref, l_ref, res_ref, do_ref, dgb_ref):
        @pl.when(pl.program_id(0) == 0)
        def _():
            l_ref[...] = jnp.zeros_like(l_ref)

        e = y_ref[...] - t_ref[...]
        l_ref[...] += 0.5 * jnp.sum(_mean(e * e))
        _norm_bwd_tail(e / D_MODEL, r_ref, g_ref, res_ref, do_ref, dgb_ref)

    row = pl.BlockSpec((TM, D_MODEL), lambda i: (i, 0))
    tail_in, tail_out, tail_shape = _norm_tail_specs(t)
    return pl.pallas_call(
        body, name="loss_head", grid=(t // TM,), in_specs=[row, row] + tail_in,
        out_specs=[pl.BlockSpec((8, 128), lambda i: (0, 0))] + tail_out,
        out_shape=[jax.ShapeDtypeStruct((8, 128), F32)] + tail_shape,
        compiler_params=_cparams("arbitrary"),
    )(y, target, r, gamma)


def _adamw(name, w, g, m, v, rows):
    n, width = w.shape

    def body(w_ref, g_ref, m_ref, v_ref, d_ref, nm_ref, nv_ref):
        g = g_ref[...]
        m = ADAM_B1 * m_ref[...] + (1.0 - ADAM_B1) * g
        v = ADAM_B2 * v_ref[...] + (1.0 - ADAM_B2) * jnp.square(g)
        m_hat = m / (1.0 - ADAM_B1 ** ADAM_STEP)
        v_hat = v / (1.0 - ADAM_B2 ** ADAM_STEP)
        d_ref[...] = -ADAM_LR * (m_hat / (jnp.sqrt(v_hat) + ADAM_EPS) + ADAM_WD * w_ref[...])
        nm_ref[...] = m
        nv_ref[...] = v

    spec = pl.BlockSpec((rows, width), lambda i: (i, 0))
    return pl.pallas_call(
        body, name=name, grid=(n // rows,), in_specs=[spec] * 4, out_specs=[spec] * 3,
        out_shape=[jax.ShapeDtypeStruct((n, width), F32)] * 3, compiler_params=_cparams("parallel"),
    )(w, g, m, v)


def _place():
    x, y, c = lax.axis_index("x"), lax.axis_index("y"), lax.axis_index("c")
    chips = [(1 - x, y), (x, 1 - y), (1 - x, 1 - y)]
    return x, y, c, chips


class _Gather:
    def __init__(self, shards):
        na = len(shards)
        self.ins = list(shards)
        self.outs = [jax.ShapeDtypeStruct((N_SHARD,) + s.shape, s.dtype) for s in shards]
        self.sems = [pltpu.SemaphoreType.DMA((3 * na,))] * 4 + [pltpu.SemaphoreType.DMA((na,))]

    def _copies(self, src, dst, sems):
        send, recv, fsend, frecv, lsem = sems
        x, y, c, chips = _place()
        mine = 2 * x + y

        def local(a):
            return pltpu.make_async_copy(src[a], dst[a].at[mine], lsem.at[a])

        def ici(a, k, shard):
            cx, cy = chips[k]
            return pltpu.make_async_remote_copy(
                src_ref=src[a].at[c], dst_ref=dst[a].at[shard, c], send_sem=send.at[3 * a + k], recv_sem=recv.at[3 * a + k],
                device_id=(cx, cy, c), device_id_type=MESH)

        def d2d(a, k, half):
            cx, cy = chips[k]
            block = dst[a].at[2 * cx + cy, half]
            return pltpu.make_async_remote_copy(
                src_ref=block, dst_ref=block, send_sem=fsend.at[3 * a + k], recv_sem=frecv.at[3 * a + k],
                device_id=(x, y, 1 - c), device_id_type=MESH)

        return local, ici, d2d, mine, c, chips

    def start(self, src, dst, sems):
        local, ici, _, mine, _, _ = self._copies(src, dst, sems)
        for a in range(len(src)):
            local(a).start()
            for k in range(3):
                ici(a, k, mine).start()

    def finish(self, src, dst, sems):
        local, ici, d2d, mine, c, chips = self._copies(src, dst, sems)
        for a in range(len(src)):
            for k, (cx, cy) in enumerate(chips):
                ici(a, k, 2 * cx + cy).wait_recv()
                d2d(a, k, c).start()
        for a in range(len(src)):
            for k in range(3):
                d2d(a, k, 1 - c).wait_recv()
        for a in range(len(src)):
            for k in range(3):
                ici(a, k, mine).wait_send()
                d2d(a, k, c).wait_send()
            local(a).wait()


class _PairExchange:
    def __init__(self, parts):
        self.ins = list(parts)
        self.outs = [jax.ShapeDtypeStruct((N_SHARD,) + p.shape[2:], p.dtype) for p in parts]
        self.sems = [pltpu.SemaphoreType.DMA((len(parts),))] * 2

    def _copy(self, a, src, dst, sems):
        x, y, c, _ = _place()
        return pltpu.make_async_remote_copy(
            src_ref=src[a].at[:, 1 - c], dst_ref=dst[a], send_sem=sems[0].at[a], recv_sem=sems[1].at[a],
            device_id=(x, y, 1 - c), device_id_type=MESH)

    def start(self, src, dst, sems):
        for a in range(len(src)):
            self._copy(a, src, dst, sems).start()

    def finish(self, src, dst, sems):
        for a in range(len(src)):
            self._copy(a, src, dst, sems).wait()


class _ChipExchange:
    def __init__(self, sums):
        self.ins = list(sums)
        self.outs = [jax.ShapeDtypeStruct((3,) + s.shape[1:], s.dtype) for s in sums]
        self.sems = [pltpu.SemaphoreType.DMA((3 * len(sums),))] * 2

    def _copy(self, a, k, src, dst, sems):
        _, _, c, chips = _place()
        cx, cy = chips[k]
        return pltpu.make_async_remote_copy(
            src_ref=src[a].at[2 * cx + cy], dst_ref=dst[a].at[k], send_sem=sems[0].at[3 * a + k],
            recv_sem=sems[1].at[3 * a + k], device_id=(cx, cy, c), device_id_type=MESH)

    def start(self, src, dst, sems):
        for a in range(len(src)):
            for k in range(3):
                self._copy(a, k, src, dst, sems).start()

    def finish(self, src, dst, sems):
        for a in range(len(src)):
            for k in range(3):
                self._copy(a, k, src, dst, sems).wait()


class _Both:
    def __init__(self, a, b):
        self.a, self.b = a, b
        self.ins, self.outs, self.sems = a.ins + b.ins, a.outs + b.outs, a.sems + b.sems

    def _each(self, method, ins, outs, sems):
        a = self.a
        getattr(a, method)(ins[:len(a.ins)], outs[:len(a.outs)], sems[:len(a.sems)])
        getattr(self.b, method)(ins[len(a.ins):], outs[len(a.outs):], sems[len(a.sems):])

    def start(self, ins, outs, sems):
        self._each("start", ins, outs, sems)

    def finish(self, ins, outs, sems):
        self._each("finish", ins, outs, sems)


def _run(name, rider):
    n_in, n_out = len(rider.ins), len(rider.outs)

    def body(*refs):
        ins, outs, sems = refs[:n_in], refs[n_in:n_in + n_out], refs[n_in + n_out:]
        rider.start(ins, outs, sems)
        rider.finish(ins, outs, sems)

    return list(pl.pallas_call(
        body, name=name, in_specs=[ANY] * n_in, out_specs=[ANY] * n_out, out_shape=rider.outs,
        scratch_shapes=rider.sems)(*rider.ins))


def _pair_share(halves):
    na = len(halves)

    def body(*refs):
        dst = refs[na:2 * na]
        send, recv = refs[2 * na:]
        x, y, c, _ = _place()
        cps = []
        for a in range(na):
            cp = pltpu.make_async_remote_copy(
                src_ref=dst[a].at[:, c], dst_ref=dst[a].at[:, c], send_sem=send.at[a], recv_sem=recv.at[a],
                device_id=(x, y, 1 - c), device_id_type=MESH)
            cp.start()
            cps.append(cp)
        for a in range(na):
            cps[a].wait_send()
            pltpu.make_async_remote_copy(
                src_ref=dst[a].at[:, 1 - c], dst_ref=dst[a].at[:, 1 - c], send_sem=send.at[a], recv_sem=recv.at[a],
                device_id=(x, y, 1 - c), device_id_type=MESH).wait_recv()

    return pl.pallas_call(
        body, name="pair_share", in_specs=[ANY] * na, out_specs=[ANY] * na,
        out_shape=[jax.ShapeDtypeStruct(h.shape, h.dtype) for h in halves],
        input_output_aliases={a: a for a in range(na)},
        scratch_shapes=[pltpu.SemaphoreType.DMA((na,))] * 2,
    )(*halves)


def _sum_rows(r):
    return r if r <= 352 else 256


def _pair_sum(name, part, got):
    _, _, r, w = part.shape
    rows = _sum_rows(r)
    c = lax.axis_index("c").astype(jnp.int32).reshape(1)

    def body(c_ref, p_ref, g_ref, o_ref):
        o_ref[...] = _mx(p_ref[...] + g_ref[...])

    spec = pl.BlockSpec((None, rows, w), lambda j, i, c_ref: (j, i, 0))
    return pl.pallas_call(
        body, name=name, out_shape=jax.ShapeDtypeStruct((N_SHARD, r, w), MXU_DTYPE),
        grid_spec=pltpu.PrefetchScalarGridSpec(
            num_scalar_prefetch=1, grid=(N_SHARD, r // rows),
            in_specs=[pl.BlockSpec((None, None, rows, w), lambda j, i, c_ref: (j, c_ref[0], i, 0)), spec],
            out_specs=spec),
        compiler_params=_cparams("parallel", "parallel"),
    )(c, part, got)


def _chip_sum(name, part, got, others, l, prev):
    _, _, r, w = part.shape
    rows = _sum_rows(r)
    cj = jnp.stack([lax.axis_index("c"), 2 * lax.axis_index("x") + lax.axis_index("y")]).astype(jnp.int32)

    def body(cj_ref, p_ref, g_ref, o_ref, *rest):
        acc = p_ref[...] + g_ref[...]
        for k in range(3):
            acc += o_ref[k].astype(F32)
        rest[-1][...] = acc

    ins, specs, alias = [cj, part, got, others], [], {}
    if prev is not None:
        ins.append(prev)
        specs.append(ANY)
        alias = {4: 0}
    return pl.pallas_call(
        body, name=name, out_shape=jax.ShapeDtypeStruct((2, 2, r, w), F32), input_output_aliases=alias,
        grid_spec=pltpu.PrefetchScalarGridSpec(
            num_scalar_prefetch=1, grid=(r // rows,),
            in_specs=[pl.BlockSpec((None, None, rows, w), lambda i, cj: (cj[1], cj[0], i, 0)),
                      pl.BlockSpec((None, rows, w), lambda i, cj: (cj[1], i, 0)),
                      pl.BlockSpec((3, rows, w), lambda i, cj: (0, i, 0))] + specs,
            out_specs=pl.BlockSpec((None, None, rows, w), lambda i, cj: (l, cj[0], i, 0))),
        compiler_params=_cparams("parallel"),
    )(*ins)


SMALL_ROWS = 40


def _sum_small(part):
    def body(p_ref, o_ref, land, send, recv):
        x, y, c, _ = _place()
        me = 4 * x + 2 * y + c
        cps = []
        for r in range(1, 8):
            cp = pltpu.make_async_remote_copy(
                src_ref=p_ref, dst_ref=land.at[r], send_sem=send.at[r], recv_sem=recv.at[r],
                device_id=(x ^ (r >> 2), y ^ ((r >> 1) & 1), c ^ (r & 1)), device_id_type=MESH)
            cp.start()
            cps.append(cp)
        land[0] = p_ref[...]
        for cp in cps:
            cp.wait()
        acc = land[me]
        for e in range(1, 8):
            acc += land[me ^ e]
        o_ref[...] = acc

    return pl.pallas_call(
        body, name="sum_small", in_specs=[pl.BlockSpec(memory_space=pltpu.VMEM)],
        out_specs=pl.BlockSpec(memory_space=pltpu.VMEM), out_shape=jax.ShapeDtypeStruct(part.shape, F32),
        scratch_shapes=[pltpu.VMEM((8,) + part.shape, F32), pltpu.SemaphoreType.DMA((8,)), pltpu.SemaphoreType.DMA((8,))],
    )(part)


BIG = ("ffn1_w_gu", "ffn1_w_down", "w_in", "w_out", "ffn2_w_gu", "ffn2_w_down")
SMALL = ("ln1_g", "ln1_b", "ln2_g", "ln2_b", "ln3_g", "ln3_b", "attn_sink", "cc_conv_b", "cc_ln_g", "cc_ln_b",
         "sc_conv_w", "cc_conv_w")
NAMES = ("ffn1_w_gu", "ffn1_w_down", "ln1_g", "ln1_b", "w_in", "sc_conv_w", "attn_sink", "cc_conv_w", "cc_conv_b",
         "cc_ln_g", "cc_ln_b", "w_out", "ln2_g", "ln2_b", "ffn2_w_gu", "ffn2_w_down", "ln3_g", "ln3_b")


def _rope_tables(t):
    half = HEAD_DIM // 2
    inv_freq = ROPE_THETA ** (-jnp.arange(half, dtype=F32) / half)
    ang = jnp.arange(t).astype(F32)[:, None] * inv_freq[None, :]
    cos, sin = jnp.cos(ang), jnp.sin(ang)
    return jnp.tile(jnp.concatenate([cos, cos], axis=1), (1, 2)), jnp.tile(jnp.concatenate([-sin, sin], axis=1), (1, 2))


def _pack_small(vals):
    flat = jnp.concatenate([vals[n].reshape(-1) for n in SMALL])
    return jnp.pad(flat, (0, SMALL_ROWS * D_MODEL - flat.shape[0])).reshape(SMALL_ROWS, D_MODEL)


def _unpack_small(packed, shapes):
    flat, out, at = packed.reshape(-1), {}, 0
    for n in SMALL:
        size = int(np.prod(shapes[n]))
        out[n] = flat[at:at + size].reshape(shapes[n])
        at += size
    return out


def kernel(x, ffn1_w_gu, ffn1_w_down, ln1_g, ln1_b, w_in, sc_conv_w, attn_sink, cc_conv_w, cc_conv_b, cc_ln_g, cc_ln_b, w_out, ln2_g, ln2_b, ffn2_w_gu, ffn2_w_down, ln3_g, ln3_b, loss_target, m_ffn1_w_gu, m_ffn1_w_down, m_ln1_g, m_ln1_b, m_w_in, m_sc_conv_w, m_attn_sink, m_cc_conv_w, m_cc_conv_b, m_cc_ln_g, m_cc_ln_b, m_w_out, m_ln2_g, m_ln2_b, m_ffn2_w_gu, m_ffn2_w_down, m_ln3_g, m_ln3_b, v_ffn1_w_gu, v_ffn1_w_down, v_ln1_g, v_ln1_b, v_w_in, v_sc_conv_w, v_attn_sink, v_cc_conv_w, v_cc_conv_b, v_cc_ln_g, v_cc_ln_b, v_w_out, v_ln2_g, v_ln2_b, v_ffn2_w_gu, v_ffn2_w_down, v_ln3_g, v_ln3_b):
    given = dict(locals())
    w = {n: given[n] for n in NAMES}
    mom = {n: given["m_" + n] for n in NAMES}
    var = {n: given["v_" + n] for n in NAMES}
    x0 = x[0]
    target = loss_target[0]
    t = x0.shape[0]
    chip = 2 * lax.axis_index("x") + lax.axis_index("y")

    conv_shard = jnp.pad(jnp.concatenate([sc_conv_w, cc_conv_w], axis=1), ((0, 0), (0, 14), (0, 64)))
    local = {n: _mx(w[n]) for n in BIG}
    local["conv"] = conv_shard
    full = [{}, {}]

    def gather(l, names):
        return _Gather([local[n][l].reshape(2, local[n].shape[1] // 2, local[n].shape[2]) for n in names])

    def land(l, names, arrays):
        for n, a in zip(names, arrays):
            full[l][n] = a.reshape(1, N_SHARD, 2 * a.shape[2], a.shape[3])

    def weights(l):
        f = full[l]
        conv = jnp.transpose(f["conv"][0, :, :SC_W + CC_W, :64], (1, 0, 2)).reshape(SC_W + CC_W, D_CONV)
        return dict(wgu1=f["ffn1_w_gu"], wd1=f["ffn1_w_down"].reshape(1, D_FF, D_MODEL), win=f["w_in"],
                    wout=f["w_out"].reshape(1, D_MODEL, D_MODEL), wgu2=f["ffn2_w_gu"],
                    wd2=f["ffn2_w_down"].reshape(1, D_FF, D_MODEL), sc=conv[:SC_W], cc=conv[SC_W:])

    first = ("ffn1_w_gu", "ffn1_w_down")
    mixer = ("w_in", "w_out", "conv")
    second = ("ffn2_w_gu", "ffn2_w_down")
    land(0, first, _run("gather_first", gather(0, first)))
    cos, sin = _rope_tables(t)

    def vec(a, l):
        return a[l][None, :]

    acts = []
    h = x0
    for l in range(2):
        ahead = (0, mixer + second) if l == 0 else (1, second)
        (y1, r1, gu1), got = _ffn_fwd("ffn_fwd_a%d" % l, h, full[l]["ffn1_w_gu"], full[l]["ffn1_w_down"].reshape(1, D_FF, D_MODEL),
                                      vec(ln1_g, l), vec(ln1_b, l), 0, gather(*ahead))
        land(*ahead, got)
        wl = weights(l)
        z = _in_proj(y1, wl["win"], 0)
        ysc, ycc, u2 = _conv_fwd(z, wl["sc"], wl["cc"], vec(cc_conv_b, l), vec(cc_ln_g, l), vec(cc_ln_b, l))
        qs, kf, vf = _attn_prep(z, cos, sin)
        o_nat, o, lse = _attn_fwd(qs, kf, vf, attn_sink[l])
        ycat = jnp.concatenate([ysc, o_nat, ycc], axis=1)
        y2, r2 = _out_proj(ycat, y1, wl["wout"], vec(ln2_g, l), vec(ln2_b, l), 0)
        ahead = (1, first + mixer) if l == 0 else None
        (y3, r3, gu2), got = _ffn_fwd("ffn_fwd_b%d" % l, y2, wl["wgu2"], wl["wd2"], vec(ln3_g, l), vec(ln3_b, l), 0,
                                      gather(*ahead) if ahead else None)
        if ahead:
            land(*ahead, got)
        acts.append(dict(x=h, y1=y1, r1=r1, gu1=gu1, z=z, u2=u2, qs=qs, kf=kf, vf=vf, o=o, lse=lse, ycat=ycat,
                         y2=y2, r2=r2, gu2=gu2, r3=r3, w=wl))
        h = y3
    loss_rows, res, do, dgb3_next = _loss_head(h, target, acts[1]["r3"], vec(ln3_g, 1))
    loss = lax.psum(loss_rows[0, 0], ("x", "y", "c"))

    upper = ("ffn2_w_gu", "ffn2_w_down", "w_out")
    lower = ("w_in", "ffn1_w_gu", "ffn1_w_down")
    part = [{}, {}]
    small = [None, None]
    stage = {}
    reduced = {n: None for n in BIG}
    row = pl.BlockSpec((TM, D_MODEL), lambda n, k: (k, 0))

    def halves(a, r):
        return a.reshape(N_SHARD, 2, r // 2, a.shape[-1])

    def pair_rider(l, names):
        return _PairExchange([part[l][n] for n in names])

    def after_pair(l, names, got):
        stage[l, names] = (got, [_pair_sum("pair_sum_%s_%d" % (n, l), part[l][n], g) for n, g in zip(names, got)])

    def chip_rider(l, names):
        return _ChipExchange(stage[l, names][1])

    def after_chip(l, names, others):
        for n, g, o in zip(names, stage[l, names][0], others):
            reduced[n] = _chip_sum("chip_sum_%s_%d" % (n, l), part[l][n], g, o, l, reduced[n])

    def ffn_weight_grads(which, l, xin, dh, a, do, rider_gu=None, make_rider_d=None):
        out, got_gu = _mm_tn(
            "%s_dwgu_%d" % (which, l), xin, dh, row,
            pl.BlockSpec((None, TM, FF_CHUNK), lambda n, k: (n // N_CHUNK, k, n % N_CHUNK)),
            pl.BlockSpec((None, D_MODEL, FF_CHUNK), lambda n, k: (n, 0, 0)),
            (N_SHARD, D_MODEL, GU_SHARD), (D_MODEL, FF_CHUNK), (2 * N_CHUNK, t // TM), rider_gu)
        part[l][which + "_w_gu"] = halves(out, D_MODEL)
        rider_d = make_rider_d() if make_rider_d else None
        out, got_d = _mm_tn(
            "%s_dwd_%d" % (which, l), a, do, pl.BlockSpec((TM, FF_CHUNK), lambda n, k: (k, n)), row,
            pl.BlockSpec((FF_CHUNK, D_MODEL), lambda n, k: (n, 0)),
            (D_FF, D_MODEL), (FF_CHUNK, D_MODEL), (N_CHUNK, t // TM), rider_d)
        part[l][which + "_w_down"] = halves(out, D_FF // N_SHARD)
        return got_gu, got_d

    w_in_only, w_gu_only, w_down_only = ("w_in",), ("ffn1_w_gu",), ("ffn1_w_down",)
    for l in (1, 0):
        s = acts[l]
        wl = s["w"]
        dgb3 = dgb3_next
        if l == 0:
            dy, dh, a, got = _ffn_bwd(res, do, s["gu2"], wl["wgu2"], wl["wd2"], 0, rider=pair_rider(1, lower))
            after_pair(1, lower, got)
            got, _ = ffn_weight_grads("ffn2", l, s["y2"], dh, a, do, chip_rider(1, lower))
            after_chip(1, lower, got)
        else:
            dy, dh, a, _ = _ffn_bwd(res, do, s["gu2"], wl["wgu2"], wl["wd2"], 0)
            ffn_weight_grads("ffn2", l, s["y2"], dh, a, do)
        res, dm, dycat, dgb2 = _out_proj_bwd(dy, s["r2"], wl["wout"], vec(ln2_g, l), 0)
        out, _ = _mm_tn("dwout_%d" % l, s["ycat"], dm, row, row, pl.BlockSpec((D_MODEL, D_MODEL), lambda n, k: (0, 0)),
                        (D_MODEL, D_MODEL), (D_MODEL, D_MODEL), (1, t // TM))
        part[l]["w_out"] = halves(out, OUT_SHARD)
        dz_sc, dz_cc, dconv = _conv_bwd(s["z"], dycat, s["u2"], wl["sc"], wl["cc"], vec(cc_ln_g, l), vec(cc_ln_b, l))
        dost, ld = _attn_dprep(dycat, s["o"], s["lse"])
        dqs, dsink = _attn_bwd_q(s["qs"], s["kf"], s["vf"], dost, ld, attn_sink[l])
        dkf, dvf = _attn_bwd_kv(s["qs"], s["kf"], s["vf"], dost, ld)
        dz_att = _attn_prep_bwd(dqs, dkf, dvf, cos, sin)
        dz = jnp.concatenate([dz_sc, dz_att, dz_cc], axis=1)
        out, got = _mm_tn(
            "dwin_%d" % l, s["y1"], dz, row, pl.BlockSpec((TM, D_IN), lambda n, k: (k, 0)),
            pl.BlockSpec((N_SHARD, D_MODEL, IN_SHARD), lambda n, k: (0, 0, 0)),
            (N_SHARD, D_MODEL, IN_SHARD), (D_MODEL, D_IN), (1, t // TM), pair_rider(l, upper), split=N_SHARD)
        part[l]["w_in"] = halves(out, D_MODEL)
        after_pair(l, upper, got)
        res, do, dgb1 = _in_proj_bwd(dz, res, wl["win"], 0, s["r1"], vec(ln1_g, l))
        if l == 1:
            (res0, do0, dgb3_next), dh, a, _ = _ffn_bwd(res, do, s["gu1"], wl["wgu1"], wl["wd1"], 0,
                                                        tail=(acts[0]["r3"], vec(ln3_g, 0)))
            got, _ = ffn_weight_grads("ffn1", l, s["x"], dh, a, do, chip_rider(l, upper))
            after_chip(l, upper, got)
            res, do = res0, do0
        else:
            dy, dh, a, got = _ffn_bwd(res, do, s["gu1"], wl["wgu1"], wl["wd1"], 0, rider=pair_rider(0, w_in_only))
            after_pair(0, w_in_only, got)
            got, got_d = ffn_weight_grads("ffn1", l, s["x"], dh, a, do, _Both(chip_rider(0, upper), chip_rider(0, w_in_only)),
                                          lambda: pair_rider(0, w_gu_only))
            n_upper = len(upper)
            after_chip(0, upper, got[:n_upper])
            after_chip(0, w_in_only, got[n_upper:])
            after_pair(0, w_gu_only, got_d)
        small[l] = dict(ln1_g=dgb1[0], ln1_b=dgb1[1], ln2_g=dgb2[0], ln2_b=dgb2[1], ln3_g=dgb3[0], ln3_b=dgb3[1],
                        attn_sink=dsink[:, 0], cc_conv_b=dconv[ROW_CCB], cc_ln_g=dconv[ROW_CCG],
                        cc_ln_b=dconv[ROW_CCBETA], sc_conv_w=dconv[ROW_SCW:ROW_SCW + SC_W],
                        cc_conv_w=dconv[ROW_CCW:ROW_CCW + CC_W])
    grad_x = dy[None]

    after_pair(0, w_down_only, _run("pair_exchange_last", pair_rider(0, w_down_only)))
    got = _run("chip_exchange_last", _Both(chip_rider(0, w_gu_only), chip_rider(0, w_down_only)))
    after_chip(0, w_gu_only, got[:1])
    after_chip(0, w_down_only, got[1:])
    grads = dict(zip(BIG, _pair_share([reduced[n] for n in BIG])))
    for n in BIG:
        grads[n] = grads[n].reshape(w[n].shape)

    small_full = {n: jnp.stack([small[0][n], small[1][n]]) for n in SMALL}
    small_sum = _unpack_small(_sum_small(_pack_small(small_full)), {n: small_full[n].shape for n in SMALL})
    for n in SMALL:
        g = small_sum[n]
        if n in ("sc_conv_w", "cc_conv_w"):
            g = lax.dynamic_slice_in_dim(g, chip * 64, 64, axis=2)
        grads[n] = g

    delta, new_m, new_v = {}, {}, {}
    for n in BIG:
        shape = w[n].shape
        two_d = (shape[0] * shape[1], shape[2])
        outs = _adamw("adamw_" + n, w[n].reshape(two_d), grads[n].reshape(two_d), mom[n].reshape(two_d),
                      var[n].reshape(two_d), 128)
        delta[n], new_m[n], new_v[n] = [a.reshape(shape) for a in outs]
    shapes = {n: w[n].shape for n in SMALL}
    outs = _adamw("adamw_small", _pack_small({n: w[n] for n in SMALL}), _pack_small({n: grads[n] for n in SMALL}),
                  _pack_small({n: mom[n] for n in SMALL}), _pack_small({n: var[n] for n in SMALL}), 8)
    for d, packed in zip((delta, new_m, new_v), outs):
        d.update(_unpack_small(packed, shapes))

    return (loss, grad_x, *[grads[n] for n in NAMES], *[delta[n] for n in NAMES], *[new_m[n] for n in NAMES],
            *[new_v[n] for n in NAMES])
```

```python
import functools

import numpy as np
import jax
import jax.numpy as jnp
from jax import lax
from jax.experimental import pallas as pl
from jax.experimental.pallas import tpu as pltpu

F32 = jnp.float32
MXU_DTYPE = jnp.bfloat16

D_MODEL = 1024
D_FF = 2816
N_SHARD = 4
D_IN = 2048
GU_SHARD = 2 * D_FF // N_SHARD
FF_CHUNK = GU_SHARD
N_CHUNK = D_FF // FF_CHUNK
IN_SHARD = D_IN // N_SHARD
OUT_SHARD = D_MODEL // N_SHARD
HEAD_DIM = 64
N_Q_HEADS = 8
BLOCK = 128
SC_W = 3
CC_W = 31
D_CONV = 256
HALO = 16
LN_EPS = 1e-5
ALPHA = (2.0 * 2) ** 0.25
NEG = -1e30
ROPE_THETA = 10000.0
ADAM_LR, ADAM_B1, ADAM_B2, ADAM_EPS, ADAM_WD, ADAM_STEP = 0.001, 0.9, 0.999, 1e-08, 0.01, 10

TM = 512
TK = 1024
TMC = 256
VMEM_LIMIT = 56 * 1024 * 1024
MESH = pl.DeviceIdType.MESH
ANY = pl.BlockSpec(memory_space=pl.ANY)


def _cparams(*sem):
    return pltpu.CompilerParams(dimension_semantics=sem, vmem_limit_bytes=VMEM_LIMIT)


def _dot(a, b):
    return jnp.dot(a, b, preferred_element_type=F32)


def _dot_nt(a, b):
    return lax.dot_general(a, b, (((1,), (1,)), ((), ())), preferred_element_type=F32)


def _dot_tn(a, b):
    return lax.dot_general(a, b, (((0,), (0,)), ((), ())), preferred_element_type=F32)


def _mx(a):
    return a.astype(MXU_DTYPE)


def _mean(a):
    return jnp.mean(a, axis=-1, keepdims=True)


def _ln_stats(r):
    xc = r - _mean(r)
    rstd = lax.rsqrt(_mean(xc * xc) + LN_EPS)
    return xc * rstd, rstd


def _ln_bwd(dy, xh, rstd, gamma):
    dxh = dy * gamma
    return rstd * (dxh - _mean(dxh) - xh * _mean(dxh * xh))


def _colsum(a):
    return jnp.sum(a, axis=0, keepdims=True)


def _sigmoid(a):
    return 1.0 / (1.0 + jnp.exp(-a))


def _call(body, args, *, name, grid, in_specs, out_specs, out_shape, scratch, sem, rider=None):
    if rider is None:
        outs = pl.pallas_call(
            body, name=name, grid=grid, in_specs=in_specs, out_specs=out_specs, out_shape=out_shape,
            scratch_shapes=scratch, compiler_params=_cparams(*sem))(*args)
        return list(outs), []
    n_in, n_out, n_sc = len(in_specs), len(out_specs), len(scratch)
    r_in, r_out = len(rider.ins), len(rider.outs)

    def carrying(*refs):
        cuts = np.cumsum([0, n_in, r_in, n_out, r_out, n_sc])
        ins, rins, outs, routs, scr = [refs[a:b] for a, b in zip(cuts[:-1], cuts[1:])]
        rsems = refs[cuts[-1]:]
        first = functools.reduce(jnp.logical_and, [pl.program_id(d) == 0 for d in range(len(grid))])
        last = functools.reduce(jnp.logical_and, [pl.program_id(d) == grid[d] - 1 for d in range(len(grid))])

        @pl.when(first)
        def _():
            rider.start(rins, routs, rsems)

        body(*ins, *outs, *scr)

        @pl.when(last)
        def _():
            rider.finish(rins, routs, rsems)

    outs = pl.pallas_call(
        carrying, name=name, grid=grid, in_specs=list(in_specs) + [ANY] * r_in,
        out_specs=list(out_specs) + [ANY] * r_out, out_shape=list(out_shape) + list(rider.outs),
        scratch_shapes=list(scratch) + list(rider.sems), compiler_params=_cparams(*(("arbitrary",) * len(grid))),
    )(*args, *rider.ins)
    return list(outs[:n_out]), list(outs[n_out:])


def _ffn_fwd(name, x, wgu, wd, gamma, beta, l, rider=None):
    t = x.shape[0]
    nc = N_CHUNK

    def body(x_ref, wg_ref, wu_ref, wd_ref, g_ref, b_ref, y_ref, r_ref, gu_ref, xb_s, acc_s):
        c = pl.program_id(1)

        @pl.when(c == 0)
        def _():
            xb_s[...] = _mx(x_ref[...])
            acc_s[...] = jnp.zeros_like(acc_s)

        xb = xb_s[...]
        hg = _dot(xb, wg_ref[...])
        hu = _dot(xb, wu_ref[...])
        gu_ref[0] = _mx(hg)
        gu_ref[1] = _mx(hu)
        a = (hg * _sigmoid(hg)) * hu
        acc_s[...] += _dot(_mx(a), wd_ref[...])

        @pl.when(c == nc - 1)
        def _():
            r = ALPHA * x_ref[...] + 0.5 * acc_s[...]
            xh, _ = _ln_stats(r)
            r_ref[...] = r
            y_ref[...] = xh * g_ref[...] + b_ref[...]

    row = pl.BlockSpec((TM, D_MODEL), lambda i, c: (i, 0))
    vec = pl.BlockSpec((1, D_MODEL), lambda i, c: (0, 0))
    return _call(
        body, (x, wgu, wgu, wd, gamma, beta), name=name, grid=(t // TM, nc),
        in_specs=[row,
                  pl.BlockSpec((None, None, D_MODEL, FF_CHUNK), lambda i, c: (l, c, 0, 0)),
                  pl.BlockSpec((None, None, D_MODEL, FF_CHUNK), lambda i, c: (l, N_CHUNK + c, 0, 0)),
                  pl.BlockSpec((None, FF_CHUNK, D_MODEL), lambda i, c: (l, c, 0)),
                  vec, vec],
        out_specs=[row, row, pl.BlockSpec((2, TM, FF_CHUNK), lambda i, c: (0, i, c))],
        out_shape=[jax.ShapeDtypeStruct((t, D_MODEL), F32), jax.ShapeDtypeStruct((t, D_MODEL), F32),
                   jax.ShapeDtypeStruct((2, t, D_FF), MXU_DTYPE)],
        scratch=[pltpu.VMEM((TM, D_MODEL), MXU_DTYPE), pltpu.VMEM((TM, D_MODEL), F32)],
        sem=("parallel", "arbitrary"), rider=rider)


def _norm_bwd_tail(dy, r_ref, g_ref, res_ref, do_ref, dgb_ref):
    @pl.when(pl.program_id(0) == 0)
    def _():
        dgb_ref[...] = jnp.zeros_like(dgb_ref)

    xh, rstd = _ln_stats(r_ref[...])
    dr = _ln_bwd(dy, xh, rstd, g_ref[...])
    do_ref[...] = _mx(0.5 * dr)
    res_ref[...] = ALPHA * dr
    dgb_ref[0:1, :] += _colsum(dy * xh)
    dgb_ref[1:2, :] += _colsum(dy)


def _norm_tail_specs(t):
    row = pl.BlockSpec((TM, D_MODEL), lambda i: (i, 0))
    return ([row, pl.BlockSpec((1, D_MODEL), lambda i: (0, 0))],
            [row, row, pl.BlockSpec((8, D_MODEL), lambda i: (0, 0))],
            [jax.ShapeDtypeStruct((t, D_MODEL), F32), jax.ShapeDtypeStruct((t, D_MODEL), MXU_DTYPE),
             jax.ShapeDtypeStruct((8, D_MODEL), F32)])


def _ffn_bwd(res, do, gu, wgu, wd, l, tail=None, rider=None):
    t = res.shape[0]
    nc = N_CHUNK
    row1 = pl.BlockSpec((TM, D_MODEL), lambda i: (i, 0))

    def hidden_body(do_ref, gu_ref, wd_ref, dh_ref, a_ref):
        da = _dot_nt(do_ref[...], wd_ref[...])
        g = gu_ref[0].astype(F32)
        u = gu_ref[1].astype(F32)
        s = _sigmoid(g)
        sil = g * s
        a_ref[...] = _mx(sil * u)
        dh_ref[0] = _mx(da * u * (s * (1.0 + g * (1.0 - s))))
        dh_ref[1] = _mx(da * sil)

    hid = pl.BlockSpec((2, TM, FF_CHUNK), lambda c, i: (0, i, c))
    (dh, a), got = _call(
        hidden_body, (do, gu, wd), name="ffn_bwd_hidden" if rider is None else "ffn_bwd_hidden_carry", grid=(nc, t // TM),
        in_specs=[pl.BlockSpec((TM, D_MODEL), lambda c, i: (i, 0)), hid,
                  pl.BlockSpec((None, FF_CHUNK, D_MODEL), lambda c, i: (l, c, 0))],
        out_specs=[hid, pl.BlockSpec((TM, FF_CHUNK), lambda c, i: (i, c))],
        out_shape=[jax.ShapeDtypeStruct((2, t, D_FF), MXU_DTYPE), jax.ShapeDtypeStruct((t, D_FF), MXU_DTYPE)],
        scratch=[], sem=("parallel", "parallel"), rider=rider)

    def input_body(res_ref, dh_ref, w_ref, *rest):
        acc = res_ref[...]
        for j in range(N_SHARD):
            part = dh_ref[j // N_CHUNK][:, (j % N_CHUNK) * FF_CHUNK:(j % N_CHUNK + 1) * FF_CHUNK]
            acc += _dot_nt(part, w_ref[j])
        if tail is None:
            rest[0][...] = acc
        else:
            _norm_bwd_tail(acc, *rest)

    in_specs = [row1, pl.BlockSpec((2, TM, D_FF), lambda i: (0, i, 0)),
                pl.BlockSpec((None, N_SHARD, D_MODEL, GU_SHARD), lambda i: (l, 0, 0, 0))]
    if tail is None:
        dx = pl.pallas_call(
            input_body, name="ffn_bwd_input", grid=(t // TM,), in_specs=in_specs, out_specs=row1,
            out_shape=jax.ShapeDtypeStruct((t, D_MODEL), F32), compiler_params=_cparams("parallel"),
        )(res, dh, wgu)
    else:
        tail_in, tail_out, tail_shape = _norm_tail_specs(t)
        dx = pl.pallas_call(
            input_body, name="ffn_bwd_input_norm", grid=(t // TM,), in_specs=in_specs + tail_in, out_specs=tail_out,
            out_shape=tail_shape, compiler_params=_cparams("arbitrary"),
        )(res, dh, wgu, *tail)
    return dx, dh, a, got


def _mm_tn(name, a, b, a_spec, b_spec, out_spec, out_shape, acc_shape, grid, rider=None, split=1):
    nk = grid[-1]
    width = acc_shape[1] // split

    def body(*refs):
        a_ref, b_ref = refs[0], refs[1]
        o_ref, acc = refs[-2], refs[-1]
        k = pl.program_id(len(grid) - 1)

        @pl.when(k == 0)
        def _():
            acc[...] = jnp.zeros_like(acc)

        acc[...] += _dot_tn(_mx(a_ref[...]), _mx(b_ref[...]))

        @pl.when(k == nk - 1)
        def _():
            if split == 1:
                o_ref[...] = acc[...]
            else:
                for j in range(split):
                    o_ref[j] = acc[:, j * width:(j + 1) * width]

    sem = ("parallel",) * (len(grid) - 1) + ("arbitrary",)
    (out,), got = _call(
        body, (a, b), name=name, grid=grid, in_specs=[a_spec, b_spec], out_specs=[out_spec],
        out_shape=[jax.ShapeDtypeStruct(out_shape, F32)], scratch=[pltpu.VMEM(acc_shape, F32)], sem=sem, rider=rider)
    return out, got


def _in_proj(x, w_in, l):
    t = x.shape[0]

    def body(x_ref, w_ref, z_ref):
        xb = _mx(x_ref[...])
        for j in range(N_SHARD):
            z_ref[:, j * IN_SHARD:(j + 1) * IN_SHARD] = _dot(xb, w_ref[j])

    return pl.pallas_call(
        body, name="in_proj", grid=(t // TM,),
        in_specs=[pl.BlockSpec((TM, D_MODEL), lambda i: (i, 0)),
                  pl.BlockSpec((None, N_SHARD, D_MODEL, IN_SHARD), lambda i: (l, 0, 0, 0))],
        out_specs=pl.BlockSpec((TM, D_IN), lambda i: (i, 0)),
        out_shape=jax.ShapeDtypeStruct((t, D_IN), F32),
        compiler_params=_cparams("parallel"),
    )(x, w_in)


def _in_proj_bwd(dz, dx_res, w_in, l, r, gamma):
    t = dz.shape[0]

    def body(dz_ref, res_ref, w_ref, *tail):
        acc = res_ref[...]
        for j in range(N_SHARD):
            acc += _dot_nt(dz_ref[:, j * IN_SHARD:(j + 1) * IN_SHARD], w_ref[j])
        _norm_bwd_tail(acc, *tail)

    row = pl.BlockSpec((TM, D_MODEL), lambda i: (i, 0))
    tail_in, tail_out, tail_shape = _norm_tail_specs(t)
    return pl.pallas_call(
        body, name="in_proj_bwd", grid=(t // TM,),
        in_specs=[pl.BlockSpec((TM, D_IN), lambda i: (i, 0)), row,
                  pl.BlockSpec((None, N_SHARD, D_MODEL, IN_SHARD), lambda i: (l, 0, 0, 0))] + tail_in,
        out_specs=tail_out, out_shape=tail_shape, compiler_params=_cparams("arbitrary"),
    )(dz, dx_res, w_in, r, gamma)


def _out_proj(ycat, x, w_out, gamma, beta, l):
    t = x.shape[0]

    def body(yc_ref, x_ref, w_ref, g_ref, b_ref, y_ref, r_ref):
        r = ALPHA * x_ref[...] + _dot(yc_ref[...], w_ref[...])
        xh, _ = _ln_stats(r)
        r_ref[...] = r
        y_ref[...] = xh * g_ref[...] + b_ref[...]

    row = pl.BlockSpec((TM, D_MODEL), lambda i: (i, 0))
    vec = pl.BlockSpec((1, D_MODEL), lambda i: (0, 0))
    return pl.pallas_call(
        body, name="out_proj", grid=(t // TM,),
        in_specs=[row, row, pl.BlockSpec((None, D_MODEL, D_MODEL), lambda i: (l, 0, 0)), vec, vec],
        out_specs=[row, row],
        out_shape=[jax.ShapeDtypeStruct((t, D_MODEL), F32)] * 2,
        compiler_params=_cparams("parallel"),
    )(ycat, x, w_out, gamma, beta)


def _out_proj_bwd(dy, r, w_out, gamma, l):
    t = dy.shape[0]

    def body(dy_ref, r_ref, w_ref, g_ref, res_ref, dm_ref, dyc_ref, dgb_ref):
        @pl.when(pl.program_id(0) == 0)
        def _():
            dgb_ref[...] = jnp.zeros_like(dgb_ref)

        xh, rstd = _ln_stats(r_ref[...])
        dy = dy_ref[...]
        dr = _ln_bwd(dy, xh, rstd, g_ref[...])
        res_ref[...] = ALPHA * dr
        dm = _mx(dr)
        dm_ref[...] = dm
        dyc_ref[...] = _dot_nt(dm, w_ref[...])
        dgb_ref[0:1, :] += _colsum(dy * xh)
        dgb_ref[1:2, :] += _colsum(dy)

    row = pl.BlockSpec((TM, D_MODEL), lambda i: (i, 0))
    return pl.pallas_call(
        body, name="out_proj_bwd", grid=(t // TM,),
        in_specs=[row, row, pl.BlockSpec((None, D_MODEL, D_MODEL), lambda i: (l, 0, 0)),
                  pl.BlockSpec((1, D_MODEL), lambda i: (0, 0))],
        out_specs=[row, row, row, pl.BlockSpec((8, D_MODEL), lambda i: (0, 0))],
        out_shape=[jax.ShapeDtypeStruct((t, D_MODEL), F32), jax.ShapeDtypeStruct((t, D_MODEL), MXU_DTYPE),
                   jax.ShapeDtypeStruct((t, D_MODEL), F32), jax.ShapeDtypeStruct((8, D_MODEL), F32)],
        compiler_params=_cparams("arbitrary"),
    )(dy, r, w_out, gamma)


def _halo_specs(t, width, col):
    per = TMC // HALO
    last = t // HALO - 1
    return [pl.BlockSpec((HALO, width), lambda i: (jnp.maximum(i * per - 1, 0), col)),
            pl.BlockSpec((TMC, width), lambda i: (i, col)),
            pl.BlockSpec((HALO, width), lambda i: (jnp.minimum((i + 1) * per, last), col))]


def _extend(refs, i, nt):
    p_ref, c_ref, n_ref = refs
    p = jnp.where(i > 0, p_ref[...].astype(F32), 0.0)
    n = jnp.where(i < nt - 1, n_ref[...].astype(F32), 0.0)
    return jnp.concatenate([p, c_ref[...].astype(F32), n], axis=0)


def _conv_fwd(z, sc_w, cc_w, cc_cb, cc_g, cc_b):
    t = z.shape[0]
    nt = t // TMC

    def body(*refs):
        b_ref = refs[0]
        c3, h3, a3, g3 = refs[1:4], refs[4:7], refs[7:10], refs[10:13]
        scw_ref, ccw_ref, cb_ref, lg_ref, lb_ref = refs[13:18]
        ysc_ref, ycc_ref, u2_ref, e_s = refs[18:22]
        i = pl.program_id(0)
        e_s[...] = _extend(c3, i, nt) * _extend(h3, i, nt)
        cv = jnp.zeros((TMC, D_CONV), F32)
        for k in range(SC_W):
            cv += scw_ref[k:k + 1, :] * e_s[pl.ds(HALO + k - 1, TMC), :]
        ysc_ref[...] = _mx(b_ref[...] * cv)
        e_s[...] = _extend(a3, i, nt) * _sigmoid(_extend(g3, i, nt))
        u2 = jnp.zeros((TMC, D_CONV), F32) + cb_ref[...]
        for k in range(CC_W):
            u2 += ccw_ref[k:k + 1, :] * e_s[pl.ds(HALO + k - 15, TMC), :]
        u2_ref[...] = u2
        xh, _ = _ln_stats(u2)
        n = xh * lg_ref[...] + lb_ref[...]
        ycc_ref[...] = _mx(n * _sigmoid(n))

    tile = pl.BlockSpec((TMC, D_CONV), lambda i: (i, 0))
    vec = pl.BlockSpec((1, D_CONV), lambda i: (0, 0))
    in_specs = ([pl.BlockSpec((TMC, D_CONV), lambda i: (i, 0))] + _halo_specs(t, D_CONV, 1) + _halo_specs(t, D_CONV, 2)
                + _halo_specs(t, D_CONV, 6) + _halo_specs(t, D_CONV, 7)
                + [pl.BlockSpec((SC_W, D_CONV), lambda i: (0, 0)), pl.BlockSpec((CC_W, D_CONV), lambda i: (0, 0)),
                   vec, vec, vec])
    return pl.pallas_call(
        body, name="conv_fwd", grid=(nt,), in_specs=in_specs, out_specs=[tile, tile, tile],
        out_shape=[jax.ShapeDtypeStruct((t, D_CONV), MXU_DTYPE), jax.ShapeDtypeStruct((t, D_CONV), MXU_DTYPE),
                   jax.ShapeDtypeStruct((t, D_CONV), F32)],
        scratch_shapes=[pltpu.VMEM((TMC + 2 * HALO, D_CONV), F32)],
        compiler_params=_cparams("parallel"),
    )(*([z] * 13), sc_w, cc_w, cc_cb, cc_g, cc_b)


ROW_CCW, ROW_CCB, ROW_CCG, ROW_CCBETA, ROW_SCW, CONV_ROWS = 0, 31, 32, 33, 34, 40


def _conv_bwd(z, dycat, u2, sc_w, cc_w, cc_g, cc_b):
    t = z.shape[0]
    nt = t // TMC

    def body(*refs):
        b3, c3, h3, a3, g3 = refs[0:3], refs[3:6], refs[6:9], refs[9:12], refs[12:15]
        dys3, dyc3, u3 = refs[15:18], refs[18:21], refs[21:24]
        scw_ref, ccw_ref, lg_ref, lb_ref = refs[24:28]
        dsc_ref, dcc_ref, sm_ref, e_s, f_s = refs[28:33]
        i = pl.program_id(0)

        @pl.when(i == 0)
        def _():
            sm_ref[...] = jnp.zeros_like(sm_ref)

        cur = pl.ds(HALO, TMC)
        e_s[...] = _extend(c3, i, nt) * _extend(h3, i, nt)
        f_s[...] = _extend(dys3, i, nt) * _extend(b3, i, nt)
        cv = jnp.zeros((TMC, D_CONV), F32)
        dp = jnp.zeros((TMC, D_CONV), F32)
        dcv = f_s[cur, :]
        for k in range(SC_W):
            win = e_s[pl.ds(HALO + k - 1, TMC), :]
            cv += scw_ref[k:k + 1, :] * win
            dp += scw_ref[k:k + 1, :] * f_s[pl.ds(HALO - k + 1, TMC), :]
            sm_ref[ROW_SCW + k:ROW_SCW + k + 1, :] += _colsum(dcv * win)
        dsc_ref[:, 0:D_CONV] = _mx(dys3[1][...] * cv)
        dsc_ref[:, D_CONV:2 * D_CONV] = _mx(dp * h3[1][...])
        dsc_ref[:, 2 * D_CONV:3 * D_CONV] = _mx(dp * c3[1][...])
        xh, rstd = _ln_stats(_extend(u3, i, nt))
        n = xh * lg_ref[...] + lb_ref[...]
        sg = _sigmoid(n)
        dn = _extend(dyc3, i, nt) * (sg * (1.0 + n * (1.0 - sg)))
        f_s[...] = _ln_bwd(dn, xh, rstd, lg_ref[...])
        sm_ref[ROW_CCG:ROW_CCG + 1, :] += _colsum((dn * xh)[HALO:HALO + TMC])
        sm_ref[ROW_CCBETA:ROW_CCBETA + 1, :] += _colsum(dn[HALO:HALO + TMC])
        sig_g = _sigmoid(_extend(g3, i, nt))
        e_s[...] = _extend(a3, i, nt) * sig_g
        du2 = f_s[cur, :]
        sm_ref[ROW_CCB:ROW_CCB + 1, :] += _colsum(du2)
        duu = jnp.zeros((TMC, D_CONV), F32)
        for k in range(CC_W):
            duu += ccw_ref[k:k + 1, :] * f_s[pl.ds(HALO + 15 - k, TMC), :]
            sm_ref[ROW_CCW + k:ROW_CCW + k + 1, :] += _colsum(du2 * e_s[pl.ds(HALO + k - 15, TMC), :])
        sgc = sig_g[HALO:HALO + TMC]
        dcc_ref[:, 0:D_CONV] = _mx(duu * sgc)
        dcc_ref[:, D_CONV:2 * D_CONV] = _mx(duu * a3[1][...] * sgc * (1.0 - sgc))

    vec = pl.BlockSpec((1, D_CONV), lambda i: (0, 0))
    in_specs = []
    for col in (0, 1, 2, 6, 7):
        in_specs += _halo_specs(t, D_CONV, col)
    in_specs += _halo_specs(t, D_CONV, 0) + _halo_specs(t, D_CONV, 3) + _halo_specs(t, D_CONV, 0)
    in_specs += [pl.BlockSpec((SC_W, D_CONV), lambda i: (0, 0)), pl.BlockSpec((CC_W, D_CONV), lambda i: (0, 0)), vec, vec]
    return pl.pallas_call(
        body, name="conv_bwd", grid=(nt,), in_specs=in_specs,
        out_specs=[pl.BlockSpec((TMC, 3 * D_CONV), lambda i: (i, 0)), pl.BlockSpec((TMC, 2 * D_CONV), lambda i: (i, 0)),
                   pl.BlockSpec((CONV_ROWS, D_CONV), lambda i: (0, 0))],
        out_shape=[jax.ShapeDtypeStruct((t, 3 * D_CONV), MXU_DTYPE), jax.ShapeDtypeStruct((t, 2 * D_CONV), MXU_DTYPE),
                   jax.ShapeDtypeStruct((CONV_ROWS, D_CONV), F32)],
        scratch_shapes=[pltpu.VMEM((TMC + 2 * HALO, D_CONV), F32)] * 2,
        compiler_params=_cparams("arbitrary"),
    )(*([z] * 15), *([dycat] * 6), *([u2] * 3), sc_w, cc_w, cc_g, cc_b)


def _lane(shape):
    return lax.broadcasted_iota(jnp.int32, shape, 1)


def _swap_halves(x):
    w = x.shape[1]
    lo = (_lane(x.shape) % HEAD_DIM) < HEAD_DIM // 2
    return jnp.where(lo, pltpu.roll(x, w - HEAD_DIM // 2, 1), pltpu.roll(x, HEAD_DIM // 2, 1))


def _half(shape, g):
    lane = _lane(shape)
    return lane < HEAD_DIM if g == 0 else lane >= HEAD_DIM


GROUP_ROWS = 4 * BLOCK


def _stack_heads(tiles, out_ref, nblk):
    for tt in range(4):
        g = tt // 2
        for slot in range(2):
            s = 2 * (tt % 2) + slot
            piece = tiles[tt] if slot == g else pltpu.roll(tiles[tt], HEAD_DIM, 1)
            piece = jnp.where(_half(piece.shape, g), piece, 0.0).astype(out_ref.dtype)
            for b in range(nblk):
                at = GROUP_ROWS * b + BLOCK * s
                out_ref[g, at:at + BLOCK, :] = piece[BLOCK * b:BLOCK * (b + 1)]


def _unstack_heads(ref, nblk):
    tiles = []
    for tt in range(4):
        g = tt // 2
        tile = None
        for slot in range(2):
            s = 2 * (tt % 2) + slot
            rows = [ref[g, GROUP_ROWS * b + BLOCK * s:GROUP_ROWS * b + BLOCK * (s + 1), :] for b in range(nblk)]
            piece = rows[0] if nblk == 1 else jnp.concatenate(rows, axis=0)
            if slot != g:
                piece = pltpu.roll(piece, HEAD_DIM, 1)
            tile = piece if tile is None else tile + piece
        tiles.append(tile)
    return tiles


def _attn_prep(z, cos, sin):
    t = z.shape[0]
    nblk = TM // BLOCK

    def body(qa_ref, qb_ref, k_ref, v_ref, cos_ref, sin_ref, qst_ref, kr_ref, vb_ref):
        cs, sn = cos_ref[...], sin_ref[...]

        def rope(x):
            return x * cs + _swap_halves(x) * sn

        tiles = []
        for tt in range(4):
            src = qa_ref if tt < 2 else qb_ref
            tiles.append(rope(src[:, (tt % 2) * BLOCK:(tt % 2 + 1) * BLOCK]) * (HEAD_DIM ** -0.5))
        _stack_heads(tiles, qst_ref, nblk)
        kr_ref[...] = _mx(rope(k_ref[...]))
        vb_ref[...] = _mx(v_ref[...])

    def col(width, j):
        return pl.BlockSpec((TM, width), lambda i: (i, j))

    return pl.pallas_call(
        body, name="attn_prep", grid=(t // TM,),
        in_specs=[col(256, 3), col(256, 4), col(128, 10), col(128, 11), col(128, 0), col(128, 0)],
        out_specs=[pl.BlockSpec((2, 4 * TM, BLOCK), lambda i: (0, i, 0)), col(128, 0), col(128, 0)],
        out_shape=[jax.ShapeDtypeStruct((2, 4 * t, BLOCK), MXU_DTYPE), jax.ShapeDtypeStruct((t, BLOCK), MXU_DTYPE),
                   jax.ShapeDtypeStruct((t, BLOCK), MXU_DTYPE)],
        compiler_params=_cparams("parallel"),
    )(z, z, z, z, cos, sin)


def _attn_dprep(dycat, ost, lst):
    t = dycat.shape[0]
    nblk = TM // BLOCK

    def body(da_ref, db_ref, o_ref, l_ref, dost_ref, ld_ref, st_s):
        tiles = []
        for tt in range(4):
            src = da_ref if tt < 2 else db_ref
            tiles.append(src[:, (tt % 2) * BLOCK:(tt % 2 + 1) * BLOCK])
        _stack_heads(tiles, st_s, nblk)
        for g in range(2):
            do = st_s[g]
            dost_ref[g] = _mx(do)
            dsum = jnp.sum(do * o_ref[g], axis=-1, keepdims=True)
            ld_ref[g] = jnp.where(_lane(do.shape) < HEAD_DIM, l_ref[g], dsum)

    stacked = pl.BlockSpec((2, 4 * TM, BLOCK), lambda i: (0, i, 0))
    return pl.pallas_call(
        body, name="attn_dprep", grid=(t // TM,),
        in_specs=[pl.BlockSpec((TM, 256), lambda i: (i, 1)), pl.BlockSpec((TM, 256), lambda i: (i, 2)), stacked, stacked],
        out_specs=[stacked, stacked],
        out_shape=[jax.ShapeDtypeStruct((2, 4 * t, BLOCK), MXU_DTYPE), jax.ShapeDtypeStruct((2, 4 * t, BLOCK), F32)],
        scratch_shapes=[pltpu.VMEM((2, 4 * TM, BLOCK), F32)],
        compiler_params=_cparams("parallel"),
    )(dycat, dycat, ost, lst)


def _attn_prep_bwd(dqst, dk, dv, cos, sin):
    t = dk.shape[0]
    nblk = TM // BLOCK

    def body(dq_ref, dk_ref, dv_ref, cos_ref, sin_ref, dz_ref):
        cs, sn = cos_ref[...], sin_ref[...]

        def rope_bwd(d):
            return d * cs + _swap_halves(d * sn)

        for tt, tile in enumerate(_unstack_heads(dq_ref, nblk)):
            dz_ref[:, tt * BLOCK:(tt + 1) * BLOCK] = _mx(rope_bwd(tile * (HEAD_DIM ** -0.5)))
        dz_ref[:, 4 * BLOCK:5 * BLOCK] = _mx(rope_bwd(dk_ref[...]))
        dz_ref[:, 5 * BLOCK:6 * BLOCK] = _mx(dv_ref[...])

    def col(width):
        return pl.BlockSpec((TM, width), lambda i: (i, 0))

    return pl.pallas_call(
        body, name="attn_prep_bwd", grid=(t // TM,),
        in_specs=[pl.BlockSpec((2, 4 * TM, BLOCK), lambda i: (0, i, 0)), col(128), col(128), col(128), col(128)],
        out_specs=col(768), out_shape=jax.ShapeDtypeStruct((t, 768), MXU_DTYPE),
        compiler_params=_cparams("parallel"),
    )(dqst, dk, dv, cos, sin)


def _nbr_specs(nb, width, col):
    return [pl.BlockSpec((BLOCK, width), lambda n: (jnp.maximum(n - 1, 0), col)),
            pl.BlockSpec((BLOCK, width), lambda n: (n, col)),
            pl.BlockSpec((BLOCK, width), lambda n: (jnp.minimum(n + 1, nb - 1), col))]


def _query_index():
    row = lax.broadcasted_iota(jnp.int32, (GROUP_ROWS, BLOCK), 0)
    return row & (BLOCK - 1), lax.broadcasted_iota(jnp.int32, (GROUP_ROWS, BLOCK), 1)


def _sink_column(sink_ref, g):
    band = lax.broadcasted_iota(jnp.int32, (GROUP_ROWS, 1), 0) // BLOCK
    col = jnp.zeros((GROUP_ROWS, 1), F32) + sink_ref[4 * g]
    for s in range(1, 4):
        col = jnp.where(band == s, sink_ref[4 * g + s], col)
    return col


def _attn_fwd(qst, kr, vb, sink):
    t = kr.shape[0]
    nb = t // BLOCK

    def body(q_ref, kp_ref, kc_ref, kn_ref, vp_ref, vc_ref, vn_ref, sink_ref, o_ref, ost_ref, lst_ref):
        n = pl.program_id(0)
        qi, kj = _query_index()
        m_prev, m_next = (kj >= qi) & (n > 0), (kj <= qi) & (n < nb - 1)
        nat = [None] * 4
        for g in range(2):
            q = q_ref[g]
            sp = jnp.where(m_prev, _dot_nt(q, kp_ref[...]), NEG)
            sc = _dot_nt(q, kc_ref[...])
            sn = jnp.where(m_next, _dot_nt(q, kn_ref[...]), NEG)
            sk = _sink_column(sink_ref, g)
            m = jnp.maximum(jnp.max(jnp.maximum(jnp.maximum(sp, sc), sn), axis=-1, keepdims=True), sk)
            pp, pc, pn = jnp.exp(sp - m), jnp.exp(sc - m), jnp.exp(sn - m)
            den = jnp.sum(pp + pc + pn, axis=-1, keepdims=True) + jnp.exp(sk - m)
            o = (_dot(_mx(pp), vp_ref[...]) + _dot(_mx(pc), vc_ref[...]) + _dot(_mx(pn), vn_ref[...])) / den
            o = jnp.where(_half(o.shape, g), o, 0.0)
            ost_ref[g] = o
            lst_ref[g] = jnp.broadcast_to(m + jnp.log(den), (GROUP_ROWS, BLOCK))
            for s in range(4):
                tt, slot = 2 * g + s // 2, s % 2
                piece = o[BLOCK * s:BLOCK * (s + 1)]
                if slot != g:
                    piece = pltpu.roll(piece, HEAD_DIM, 1)
                nat[tt] = piece if nat[tt] is None else nat[tt] + piece
        for tt in range(4):
            o_ref[:, tt * BLOCK:(tt + 1) * BLOCK] = _mx(nat[tt])

    stacked = pl.BlockSpec((2, GROUP_ROWS, BLOCK), lambda n: (0, n, 0))
    return pl.pallas_call(
        body, name="attn_fwd", grid=(nb,),
        in_specs=[stacked] + _nbr_specs(nb, BLOCK, 0) + _nbr_specs(nb, BLOCK, 0) + [pl.BlockSpec(memory_space=pltpu.SMEM)],
        out_specs=[pl.BlockSpec((BLOCK, 512), lambda n: (n, 0)), stacked, stacked],
        out_shape=[jax.ShapeDtypeStruct((t, 512), MXU_DTYPE), jax.ShapeDtypeStruct((2, 4 * t, BLOCK), F32),
                   jax.ShapeDtypeStruct((2, 4 * t, BLOCK), F32)],
        compiler_params=_cparams("parallel"),
    )(qst, kr, kr, kr, vb, vb, vb, sink)


def _lse_and_dsum(ld):
    return ld[:, 0:1], pltpu.roll(ld, HEAD_DIM, 1)[:, 0:1]


def _attn_bwd(qst, kr, vb, dost, ld, sink):
    t = kr.shape[0]
    nb = t // BLOCK

    def body(q_ref, kp_ref, kc_ref, kn_ref, vp_ref, vc_ref, vn_ref, do_ref, ld_ref, sink_ref,
             dq_ref, dk_ref, dv_ref, ds_ref):
        n = pl.program_id(0)

        @pl.when(n == 0)
        def _():
            ds_ref[...] = jnp.zeros_like(ds_ref)
            dk_ref[...] = jnp.zeros_like(dk_ref)
            dv_ref[...] = jnp.zeros_like(dv_ref)

        qi, kj = _query_index()
        m_prev, m_next = (kj >= qi) & (n > 0), (kj <= qi) & (n < nb - 1)
        key_rows = [pl.ds(pl.multiple_of(jnp.clip(n - 1 + b, 0, nb - 1) * BLOCK, BLOCK), BLOCK) for b in range(3)]
        for g in range(2):
            q, do = q_ref[g], do_ref[g]
            lse, dsum = _lse_and_dsum(ld_ref[g])
            acc = jnp.zeros((GROUP_ROWS, BLOCK), F32)
            for b, (k_ref, v_ref, valid) in enumerate(((kp_ref, vp_ref, m_prev), (kc_ref, vc_ref, None),
                                                       (kn_ref, vn_ref, m_next))):
                sc = _dot_nt(q, k_ref[...])
                if valid is not None:
                    sc = jnp.where(valid, sc, NEG)
                p = jnp.exp(sc - lse)
                dsc = _mx(p * (_dot_nt(do, v_ref[...]) - dsum))
                acc += _dot(dsc, k_ref[...])
                dv_ref[key_rows[b], :] += _dot_tn(_mx(p), do)
                dk_ref[key_rows[b], :] += _dot_tn(dsc, q)
            dq_ref[g] = jnp.where(_half(acc.shape, g), acc, 0.0)
            dsk = jnp.exp(_sink_column(sink_ref, g) - lse) * dsum
            for s in range(4):
                h = 4 * g + s
                ds_ref[h:h + 1, :] -= jnp.sum(dsk[BLOCK * s:BLOCK * (s + 1)], axis=0, keepdims=True)

    stacked = pl.BlockSpec((2, GROUP_ROWS, BLOCK), lambda n: (0, n, 0))
    whole = pl.BlockSpec((t, BLOCK), lambda n: (0, 0))
    return pl.pallas_call(
        body, name="attn_bwd", grid=(nb,),
        in_specs=[stacked] + _nbr_specs(nb, BLOCK, 0) + _nbr_specs(nb, BLOCK, 0)
        + [stacked, stacked, pl.BlockSpec(memory_space=pltpu.SMEM)],
        out_specs=[stacked, whole, whole, pl.BlockSpec((8, BLOCK), lambda n: (0, 0))],
        out_shape=[jax.ShapeDtypeStruct((2, 4 * t, BLOCK), F32), jax.ShapeDtypeStruct((t, BLOCK), F32),
                   jax.ShapeDtypeStruct((t, BLOCK), F32), jax.ShapeDtypeStruct((8, BLOCK), F32)],
        compiler_params=_cparams("arbitrary"),
    )(qst, kr, kr, kr, vb, vb, vb, dost, ld, sink)


def _loss_head(y, target, r, gamma):
    t = y.shape[0]

    def body(y_ref, t_ref, r_ref, g_ref, l_ref, res_ref, do_ref, dgb_ref):
        @pl.when(pl.program_id(0) == 0)
        def _():
            l_ref[...] = jnp.zeros_like(l_ref)

        e = y_ref[...] - t_ref[...]
        l_ref[...] += 0.5 * jnp.sum(_mean(e * e))
        _norm_bwd_tail(e / D_MODEL, r_ref, g_ref, res_ref, do_ref, dgb_ref)

    row = pl.BlockSpec((TM, D_MODEL), lambda i: (i, 0))
    tail_in, tail_out, tail_shape = _norm_tail_specs(t)
    return pl.pallas_call(
        body, name="loss_head", grid=(t // TM,), in_specs=[row, row] + tail_in,
        out_specs=[pl.BlockSpec((8, 128), lambda i: (0, 0))] + tail_out,
        out_shape=[jax.ShapeDtypeStruct((8, 128), F32)] + tail_shape,
        compiler_params=_cparams("arbitrary"),
    )(y, target, r, gamma)


def _adamw(name, w, g, m, v, rows):
    n, width = w.shape

    def body(w_ref, g_ref, m_ref, v_ref, d_ref, nm_ref, nv_ref):
        g = g_ref[...]
        m = ADAM_B1 * m_ref[...] + (1.0 - ADAM_B1) * g
        v = ADAM_B2 * v_ref[...] + (1.0 - ADAM_B2) * jnp.square(g)
        m_hat = m / (1.0 - ADAM_B1 ** ADAM_STEP)
        v_hat = v / (1.0 - ADAM_B2 ** ADAM_STEP)
        d_ref[...] = -ADAM_LR * (m_hat / (jnp.sqrt(v_hat) + ADAM_EPS) + ADAM_WD * w_ref[...])
        nm_ref[...] = m
        nv_ref[...] = v

    spec = pl.BlockSpec((rows, width), lambda i: (i, 0))
    return pl.pallas_call(
        body, name=name, grid=(n // rows,), in_specs=[spec] * 4, out_specs=[spec] * 3,
        out_shape=[jax.ShapeDtypeStruct((n, width), F32)] * 3, compiler_params=_cparams("parallel"),
    )(w, g, m, v)


def _place():
    x, y, c = lax.axis_index("x"), lax.axis_index("y"), lax.axis_index("c")
    chips = [(1 - x, y), (x, 1 - y), (1 - x, 1 - y)]
    return x, y, c, chips


class _Gather:
    def __init__(self, shards):
        na = len(shards)
        self.ins = list(shards)
        self.outs = [jax.ShapeDtypeStruct((N_SHARD,) + s.shape, s.dtype) for s in shards]
        self.sems = [pltpu.SemaphoreType.DMA((3 * na,))] * 4 + [pltpu.SemaphoreType.DMA((na,))]

    def _copies(self, src, dst, sems):
        send, recv, fsend, frecv, lsem = sems
        x, y, c, chips = _place()
        mine = 2 * x + y

        def local(a):
            return pltpu.make_async_copy(src[a], dst[a].at[mine], lsem.at[a])

        def ici(a, k, shard):
            cx, cy = chips[k]
            return pltpu.make_async_remote_copy(
                src_ref=src[a].at[c], dst_ref=dst[a].at[shard, c], send_sem=send.at[3 * a + k], recv_sem=recv.at[3 * a + k],
                device_id=(cx, cy, c), device_id_type=MESH)

        def d2d(a, k, half):
            cx, cy = chips[k]
            block = dst[a].at[2 * cx + cy, half]
            return pltpu.make_async_remote_copy(
                src_ref=block, dst_ref=block, send_sem=fsend.at[3 * a + k], recv_sem=frecv.at[3 * a + k],
                device_id=(x, y, 1 - c), device_id_type=MESH)

        return local, ici, d2d, mine, c, chips

    def start(self, src, dst, sems):
        local, ici, _, mine, _, _ = self._copies(src, dst, sems)
        for a in range(len(src)):
            local(a).start()
            for k in range(3):
                ici(a, k, mine).start()

    def finish(self, src, dst, sems):
        local, ici, d2d, mine, c, chips = self._copies(src, dst, sems)
        for a in range(len(src)):
            for k, (cx, cy) in enumerate(chips):
                ici(a, k, 2 * cx + cy).wait_recv()
                d2d(a, k, c).start()
        for a in range(len(src)):
            for k in range(3):
                d2d(a, k, 1 - c).wait_recv()
        for a in range(len(src)):
            for k in range(3):
                ici(a, k, mine).wait_send()
                d2d(a, k, c).wait_send()
            local(a).wait()


class _PairExchange:
    def __init__(self, parts):
        self.ins = list(parts)
        self.outs = [jax.ShapeDtypeStruct((N_SHARD,) + p.shape[2:], p.dtype) for p in parts]
        self.sems = [pltpu.SemaphoreType.DMA((len(parts),))] * 2

    def _copy(self, a, src, dst, sems):
        x, y, c, _ = _place()
        return pltpu.make_async_remote_copy(
            src_ref=src[a].at[:, 1 - c], dst_ref=dst[a], send_sem=sems[0].at[a], recv_sem=sems[1].at[a],
            device_id=(x, y, 1 - c), device_id_type=MESH)

    def start(self, src, dst, sems):
        for a in range(len(src)):
            self._copy(a, src, dst, sems).start()

    def finish(self, src, dst, sems):
        for a in range(len(src)):
            self._copy(a, src, dst, sems).wait()


class _ChipExchange:
    def __init__(self, sums):
        self.ins = list(sums)
        self.outs = [jax.ShapeDtypeStruct((3,) + s.shape[1:], s.dtype) for s in sums]
        self.sems = [pltpu.SemaphoreType.DMA((3 * len(sums),))] * 2

    def _copy(self, a, k, src, dst, sems):
        _, _, c, chips = _place()
        cx, cy = chips[k]
        return pltpu.make_async_remote_copy(
            src_ref=src[a].at[2 * cx + cy], dst_ref=dst[a].at[k], send_sem=sems[0].at[3 * a + k],
            recv_sem=sems[1].at[3 * a + k], device_id=(cx, cy, c), device_id_type=MESH)

    def start(self, src, dst, sems):
        for a in range(len(src)):
            for k in range(3):
                self._copy(a, k, src, dst, sems).start()

    def finish(self, src, dst, sems):
        for a in range(len(src)):
            for k in range(3):
                self._copy(a, k, src, dst, sems).wait()


class _Both:
    def __init__(self, a, b):
        self.a, self.b = a, b
        self.ins, self.outs, self.sems = a.ins + b.ins, a.outs + b.outs, a.sems + b.sems

    def _each(self, method, ins, outs, sems):
        a = self.a
        getattr(a, method)(ins[:len(a.ins)], outs[:len(a.outs)], sems[:len(a.sems)])
        getattr(self.b, method)(ins[len(a.ins):], outs[len(a.outs):], sems[len(a.sems):])

    def start(self, ins, outs, sems):
        self._each("start", ins, outs, sems)

    def finish(self, ins, outs, sems):
        self._each("finish", ins, outs, sems)


def _run(name, rider):
    n_in, n_out = len(rider.ins), len(rider.outs)

    def body(*refs):
        ins, outs, sems = refs[:n_in], refs[n_in:n_in + n_out], refs[n_in + n_out:]
        rider.start(ins, outs, sems)
        rider.finish(ins, outs, sems)

    return list(pl.pallas_call(
        body, name=name, in_specs=[ANY] * n_in, out_specs=[ANY] * n_out, out_shape=rider.outs,
        scratch_shapes=rider.sems)(*rider.ins))


def _pair_share(halves):
    na = len(halves)

    def body(*refs):
        dst = refs[na:2 * na]
        send, recv = refs[2 * na:]
        x, y, c, _ = _place()
        cps = []
        for a in range(na):
            cp = pltpu.make_async_remote_copy(
                src_ref=dst[a].at[:, c], dst_ref=dst[a].at[:, c], send_sem=send.at[a], recv_sem=recv.at[a],
                device_id=(x, y, 1 - c), device_id_type=MESH)
            cp.start()
            cps.append(cp)
        for a in range(na):
            cps[a].wait_send()
            pltpu.make_async_remote_copy(
                src_ref=dst[a].at[:, 1 - c], dst_ref=dst[a].at[:, 1 - c], send_sem=send.at[a], recv_sem=recv.at[a],
                device_id=(x, y, 1 - c), device_id_type=MESH).wait_recv()

    return pl.pallas_call(
        body, name="pair_share", in_specs=[ANY] * na, out_specs=[ANY] * na,
        out_shape=[jax.ShapeDtypeStruct(h.shape, h.dtype) for h in halves],
        input_output_aliases={a: a for a in range(na)},
        scratch_shapes=[pltpu.SemaphoreType.DMA((na,))] * 2,
    )(*halves)


def _sum_rows(r):
    return r if r <= 352 else 256


def _pair_sum(name, part, got):
    _, _, r, w = part.shape
    rows = _sum_rows(r)
    c = lax.axis_index("c").astype(jnp.int32).reshape(1)

    def body(c_ref, p_ref, g_ref, o_ref):
        o_ref[...] = _mx(p_ref[...] + g_ref[...])

    spec = pl.BlockSpec((None, rows, w), lambda j, i, c_ref: (j, i, 0))
    return pl.pallas_call(
        body, name=name, out_shape=jax.ShapeDtypeStruct((N_SHARD, r, w), MXU_DTYPE),
        grid_spec=pltpu.PrefetchScalarGridSpec(
            num_scalar_prefetch=1, grid=(N_SHARD, r // rows),
            in_specs=[pl.BlockSpec((None, None, rows, w), lambda j, i, c_ref: (j, c_ref[0], i, 0)), spec],
            out_specs=spec),
        compiler_params=_cparams("parallel", "parallel"),
    )(c, part, got)


def _chip_sum(name, part, got, others, l, prev):
    _, _, r, w = part.shape
    rows = _sum_rows(r)
    cj = jnp.stack([lax.axis_index("c"), 2 * lax.axis_index("x") + lax.axis_index("y")]).astype(jnp.int32)

    def body(cj_ref, p_ref, g_ref, o_ref, *rest):
        acc = p_ref[...] + g_ref[...]
        for k in range(3):
            acc += o_ref[k].astype(F32)
        rest[-1][...] = acc

    ins, specs, alias = [cj, part, got, others], [], {}
    if prev is not None:
        ins.append(prev)
        specs.append(ANY)
        alias = {4: 0}
    return pl.pallas_call(
        body, name=name, out_shape=jax.ShapeDtypeStruct((2, 2, r, w), F32), input_output_aliases=alias,
        grid_spec=pltpu.PrefetchScalarGridSpec(
            num_scalar_prefetch=1, grid=(r // rows,),
            in_specs=[pl.BlockSpec((None, None, rows, w), lambda i, cj: (cj[1], cj[0], i, 0)),
                      pl.BlockSpec((None, rows, w), lambda i, cj: (cj[1], i, 0)),
                      pl.BlockSpec((3, rows, w), lambda i, cj: (0, i, 0))] + specs,
            out_specs=pl.BlockSpec((None, None, rows, w), lambda i, cj: (l, cj[0], i, 0))),
        compiler_params=_cparams("parallel"),
    )(*ins)


SMALL_ROWS = 40


def _sum_small(part):
    def body(p_ref, o_ref, land, send, recv):
        x, y, c, _ = _place()
        me = 4 * x + 2 * y + c
        cps = []
        for r in range(1, 8):
            cp = pltpu.make_async_remote_copy(
                src_ref=p_ref, dst_ref=land.at[r], send_sem=send.at[r], recv_sem=recv.at[r],
                device_id=(x ^ (r >> 2), y ^ ((r >> 1) & 1), c ^ (r & 1)), device_id_type=MESH)
            cp.start()
            cps.append(cp)
        land[0] = p_ref[...]
        for cp in cps:
            cp.wait()
        acc = land[me]
        for e in range(1, 8):
            acc += land[me ^ e]
        o_ref[...] = acc

    return pl.pallas_call(
        body, name="sum_small", in_specs=[pl.BlockSpec(memory_space=pltpu.VMEM)],
        out_specs=pl.BlockSpec(memory_space=pltpu.VMEM), out_shape=jax.ShapeDtypeStruct(part.shape, F32),
        scratch_shapes=[pltpu.VMEM((8,) + part.shape, F32), pltpu.SemaphoreType.DMA((8,)), pltpu.SemaphoreType.DMA((8,))],
    )(part)


BIG = ("ffn1_w_gu", "ffn1_w_down", "w_in", "w_out", "ffn2_w_gu", "ffn2_w_down")
SMALL = ("ln1_g", "ln1_b", "ln2_g", "ln2_b", "ln3_g", "ln3_b", "attn_sink", "cc_conv_b", "cc_ln_g", "cc_ln_b",
         "sc_conv_w", "cc_conv_w")
NAMES = ("ffn1_w_gu", "ffn1_w_down", "ln1_g", "ln1_b", "w_in", "sc_conv_w", "attn_sink", "cc_conv_w", "cc_conv_b",
         "cc_ln_g", "cc_ln_b", "w_out", "ln2_g", "ln2_b", "ffn2_w_gu", "ffn2_w_down", "ln3_g", "ln3_b")


def _rope_tables(t):
    half = HEAD_DIM // 2
    inv_freq = ROPE_THETA ** (-jnp.arange(half, dtype=F32) / half)
    ang = jnp.arange(t).astype(F32)[:, None] * inv_freq[None, :]
    cos, sin = jnp.cos(ang), jnp.sin(ang)
    return jnp.tile(jnp.concatenate([cos, cos], axis=1), (1, 2)), jnp.tile(jnp.concatenate([-sin, sin], axis=1), (1, 2))


def _pack_small(vals):
    flat = jnp.concatenate([vals[n].reshape(-1) for n in SMALL])
    return jnp.pad(flat, (0, SMALL_ROWS * D_MODEL - flat.shape[0])).reshape(SMALL_ROWS, D_MODEL)


def _unpack_small(packed, shapes):
    flat, out, at = packed.reshape(-1), {}, 0
    for n in SMALL:
        size = int(np.prod(shapes[n]))
        out[n] = flat[at:at + size].reshape(shapes[n])
        at += size
    return out


def kernel(x, ffn1_w_gu, ffn1_w_down, ln1_g, ln1_b, w_in, sc_conv_w, attn_sink, cc_conv_w, cc_conv_b, cc_ln_g, cc_ln_b, w_out, ln2_g, ln2_b, ffn2_w_gu, ffn2_w_down, ln3_g, ln3_b, loss_target, m_ffn1_w_gu, m_ffn1_w_down, m_ln1_g, m_ln1_b, m_w_in, m_sc_conv_w, m_attn_sink, m_cc_conv_w, m_cc_conv_b, m_cc_ln_g, m_cc_ln_b, m_w_out, m_ln2_g, m_ln2_b, m_ffn2_w_gu, m_ffn2_w_down, m_ln3_g, m_ln3_b, v_ffn1_w_gu, v_ffn1_w_down, v_ln1_g, v_ln1_b, v_w_in, v_sc_conv_w, v_attn_sink, v_cc_conv_w, v_cc_conv_b, v_cc_ln_g, v_cc_ln_b, v_w_out, v_ln2_g, v_ln2_b, v_ffn2_w_gu, v_ffn2_w_down, v_ln3_g, v_ln3_b):
    given = dict(locals())
    w = {n: given[n] for n in NAMES}
    mom = {n: given["m_" + n] for n in NAMES}
    var = {n: given["v_" + n] for n in NAMES}
    x0 = x[0]
    target = loss_target[0]
    t = x0.shape[0]
    chip = 2 * lax.axis_index("x") + lax.axis_index("y")

    conv_shard = jnp.pad(jnp.concatenate([sc_conv_w, cc_conv_w], axis=1), ((0, 0), (0, 14), (0, 64)))
    local = {n: _mx(w[n]) for n in BIG}
    local["conv"] = conv_shard
    full = [{}, {}]

    def gather(l, names):
        return _Gather([local[n][l].reshape(2, local[n].shape[1] // 2, local[n].shape[2]) for n in names])

    def land(l, names, arrays):
        for n, a in zip(names, arrays):
            full[l][n] = a.reshape(1, N_SHARD, 2 * a.shape[2], a.shape[3])

    def weights(l):
        f = full[l]
        conv = jnp.transpose(f["conv"][0, :, :SC_W + CC_W, :64], (1, 0, 2)).reshape(SC_W + CC_W, D_CONV)
        return dict(wgu1=f["ffn1_w_gu"], wd1=f["ffn1_w_down"].reshape(1, D_FF, D_MODEL), win=f["w_in"],
                    wout=f["w_out"].reshape(1, D_MODEL, D_MODEL), wgu2=f["ffn2_w_gu"],
                    wd2=f["ffn2_w_down"].reshape(1, D_FF, D_MODEL), sc=conv[:SC_W], cc=conv[SC_W:])

    first = ("ffn1_w_gu", "ffn1_w_down")
    mixer = ("w_in", "w_out", "conv")
    second = ("ffn2_w_gu", "ffn2_w_down")
    land(0, first, _run("gather_first", gather(0, first)))
    cos, sin = _rope_tables(t)

    def vec(a, l):
        return a[l][None, :]

    acts = []
    h = x0
    for l in range(2):
        ahead = (0, mixer + second) if l == 0 else (1, second)
        (y1, r1, gu1), got = _ffn_fwd("ffn_fwd_a%d" % l, h, full[l]["ffn1_w_gu"], full[l]["ffn1_w_down"].reshape(1, D_FF, D_MODEL),
                                      vec(ln1_g, l), vec(ln1_b, l), 0, gather(*ahead))
        land(*ahead, got)
        wl = weights(l)
        z = _in_proj(y1, wl["win"], 0)
        ysc, ycc, u2 = _conv_fwd(z, wl["sc"], wl["cc"], vec(cc_conv_b, l), vec(cc_ln_g, l), vec(cc_ln_b, l))
        qs, kf, vf = _attn_prep(z, cos, sin)
        o_nat, o, lse = _attn_fwd(qs, kf, vf, attn_sink[l])
        ycat = jnp.concatenate([ysc, o_nat, ycc], axis=1)
        y2, r2 = _out_proj(ycat, y1, wl["wout"], vec(ln2_g, l), vec(ln2_b, l), 0)
        ahead = (1, first + mixer) if l == 0 else None
        (y3, r3, gu2), got = _ffn_fwd("ffn_fwd_b%d" % l, y2, wl["wgu2"], wl["wd2"], vec(ln3_g, l), vec(ln3_b, l), 0,
                                      gather(*ahead) if ahead else None)
        if ahead:
            land(*ahead, got)
        acts.append(dict(x=h, y1=y1, r1=r1, gu1=gu1, z=z, u2=u2, qs=qs, kf=kf, vf=vf, o=o, lse=lse, ycat=ycat,
                         y2=y2, r2=r2, gu2=gu2, r3=r3, w=wl))
        h = y3
    loss_rows, res, do, dgb3_next = _loss_head(h, target, acts[1]["r3"], vec(ln3_g, 1))
    loss = lax.psum(loss_rows[0, 0], ("x", "y", "c"))

    upper = ("ffn2_w_gu", "ffn2_w_down", "w_out")
    lower = ("w_in", "ffn1_w_gu", "ffn1_w_down")
    part = [{}, {}]
    small = [None, None]
    stage = {}
    reduced = {n: None for n in BIG}
    row = pl.BlockSpec((TM, D_MODEL), lambda n, k: (k, 0))
    deep = pl.BlockSpec((TK, D_MODEL), lambda n, k: (k, 0))

    def halves(a, r):
        return a.reshape(N_SHARD, 2, r // 2, a.shape[-1])

    def pair_rider(l, names):
        return _PairExchange([part[l][n] for n in names])

    def after_pair(l, names, got):
        stage[l, names] = (got, [_pair_sum("pair_sum_%s_%d" % (n, l), part[l][n], g) for n, g in zip(names, got)])

    def chip_rider(l, names):
        return _ChipExchange(stage[l, names][1])

    def after_chip(l, names, others):
        for n, g, o in zip(names, stage[l, names][0], others):
            reduced[n] = _chip_sum("chip_sum_%s_%d" % (n, l), part[l][n], g, o, l, reduced[n])

    def ffn_weight_grads(which, l, xin, dh, a, do, rider_gu=None, make_rider_d=None):
        out, got_gu = _mm_tn(
            "%s_dwgu_%d" % (which, l), xin, dh, deep,
            pl.BlockSpec((None, TK, FF_CHUNK), lambda n, k: (n // N_CHUNK, k, n % N_CHUNK)),
            pl.BlockSpec((None, D_MODEL, FF_CHUNK), lambda n, k: (n, 0, 0)),
            (N_SHARD, D_MODEL, GU_SHARD), (D_MODEL, FF_CHUNK), (2 * N_CHUNK, t // TK), rider_gu)
        part[l][which + "_w_gu"] = halves(out, D_MODEL)
        rider_d = make_rider_d() if make_rider_d else None
        out, got_d = _mm_tn(
            "%s_dwd_%d" % (which, l), a, do, pl.BlockSpec((TK, FF_CHUNK), lambda n, k: (k, n)), deep,
            pl.BlockSpec((FF_CHUNK, D_MODEL), lambda n, k: (n, 0)),
            (D_FF, D_MODEL), (FF_CHUNK, D_MODEL), (N_CHUNK, t // TK), rider_d)
        part[l][which + "_w_down"] = halves(out, D_FF // N_SHARD)
        return got_gu, got_d

    w_in_only, w_gu_only, w_down_only = ("w_in",), ("ffn1_w_gu",), ("ffn1_w_down",)
    for l in (1, 0):
        s = acts[l]
        wl = s["w"]
        dgb3 = dgb3_next
        if l == 0:
            dy, dh, a, got = _ffn_bwd(res, do, s["gu2"], wl["wgu2"], wl["wd2"], 0, rider=pair_rider(1, lower))
            after_pair(1, lower, got)
            got, _ = ffn_weight_grads("ffn2", l, s["y2"], dh, a, do, chip_rider(1, lower))
            after_chip(1, lower, got)
        else:
            dy, dh, a, _ = _ffn_bwd(res, do, s["gu2"], wl["wgu2"], wl["wd2"], 0)
            ffn_weight_grads("ffn2", l, s["y2"], dh, a, do)
        res, dm, dycat, dgb2 = _out_proj_bwd(dy, s["r2"], wl["wout"], vec(ln2_g, l), 0)
        out, _ = _mm_tn("dwout_%d" % l, s["ycat"], dm, row, row, pl.BlockSpec((D_MODEL, D_MODEL), lambda n, k: (0, 0)),
                        (D_MODEL, D_MODEL), (D_MODEL, D_MODEL), (1, t // TM))
        part[l]["w_out"] = halves(out, OUT_SHARD)
        dz_sc, dz_cc, dconv = _conv_bwd(s["z"], dycat, s["u2"], wl["sc"], wl["cc"], vec(cc_ln_g, l), vec(cc_ln_b, l))
        dost, ld = _attn_dprep(dycat, s["o"], s["lse"])
        dqs, dkf, dvf, dsink = _attn_bwd(s["qs"], s["kf"], s["vf"], dost, ld, attn_sink[l])
        dz_att = _attn_prep_bwd(dqs, dkf, dvf, cos, sin)
        dz = jnp.concatenate([dz_sc, dz_att, dz_cc], axis=1)
        out, got = _mm_tn(
            "dwin_%d" % l, s["y1"], dz, row, pl.BlockSpec((TM, D_IN), lambda n, k: (k, 0)),
            pl.BlockSpec((N_SHARD, D_MODEL, IN_SHARD), lambda n, k: (0, 0, 0)),
            (N_SHARD, D_MODEL, IN_SHARD), (D_MODEL, D_IN), (1, t // TM), pair_rider(l, upper), split=N_SHARD)
        part[l]["w_in"] = halves(out, D_MODEL)
        after_pair(l, upper, got)
        res, do, dgb1 = _in_proj_bwd(dz, res, wl["win"], 0, s["r1"], vec(ln1_g, l))
        if l == 1:
            (res0, do0, dgb3_next), dh, a, _ = _ffn_bwd(res, do, s["gu1"], wl["wgu1"], wl["wd1"], 0,
                                                        tail=(acts[0]["r3"], vec(ln3_g, 0)))
            got, _ = ffn_weight_grads("ffn1", l, s["x"], dh, a, do, chip_rider(l, upper))
            after_chip(l, upper, got)
            res, do = res0, do0
        else:
            dy, dh, a, got = _ffn_bwd(res, do, s["gu1"], wl["wgu1"], wl["wd1"], 0, rider=pair_rider(0, w_in_only))
            after_pair(0, w_in_only, got)
            got, got_d = ffn_weight_grads("ffn1", l, s["x"], dh, a, do, _Both(chip_rider(0, upper), chip_rider(0, w_in_only)),
                                          lambda: pair_rider(0, w_gu_only))
            n_upper = len(upper)
            after_chip(0, upper, got[:n_upper])
            after_chip(0, w_in_only, got[n_upper:])
            after_pair(0, w_gu_only, got_d)
        small[l] = dict(ln1_g=dgb1[0], ln1_b=dgb1[1], ln2_g=dgb2[0], ln2_b=dgb2[1], ln3_g=dgb3[0], ln3_b=dgb3[1],
                        attn_sink=dsink[:, 0], cc_conv_b=dconv[ROW_CCB], cc_ln_g=dconv[ROW_CCG],
                        cc_ln_b=dconv[ROW_CCBETA], sc_conv_w=dconv[ROW_SCW:ROW_SCW + SC_W],
                        cc_conv_w=dconv[ROW_CCW:ROW_CCW + CC_W])
    grad_x = dy[None]

    after_pair(0, w_down_only, _run("pair_exchange_last", pair_rider(0, w_down_only)))
    got = _run("chip_exchange_last", _Both(chip_rider(0, w_gu_only), chip_rider(0, w_down_only)))
    after_chip(0, w_gu_only, got[:1])
    after_chip(0, w_down_only, got[1:])
    grads = dict(zip(BIG, _pair_share([reduced[n] for n in BIG])))
    for n in BIG:
        grads[n] = grads[n].reshape(w[n].shape)

    small_full = {n: jnp.stack([small[0][n], small[1][n]]) for n in SMALL}
    small_sum = _unpack_small(_sum_small(_pack_small(small_full)), {n: small_full[n].shape for n in SMALL})
    for n in SMALL:
        g = small_sum[n]
        if n in ("sc_conv_w", "cc_conv_w"):
            g = lax.dynamic_slice_in_dim(g, chip * 64, 64, axis=2)
        grads[n] = g

    delta, new_m, new_v = {}, {}, {}
    for n in BIG:
        shape = w[n].shape
        two_d = (shape[0] * shape[1], shape[2])
        outs = _adamw("adamw_" + n, w[n].reshape(two_d), grads[n].reshape(two_d), mom[n].reshape(two_d),
                      var[n].reshape(two_d), 128)
        delta[n], new_m[n], new_v[n] = [a.reshape(shape) for a in outs]
    shapes = {n: w[n].shape for n in SMALL}
    outs = _adamw("adamw_small", _pack_small({n: w[n] for n in SMALL}), _pack_small({n: grads[n] for n in SMALL}),
                  _pack_small({n: mom[n] for n in SMALL}), _pack_small({n: var[n] for n in SMALL}), 8)
    for d, packed in zip((delta, new_m, new_v), outs):
        d.update(_unpack_small(packed, shapes))

    return (loss, grad_x, *[grads[n] for n in NAMES], *[delta[n] for n in NAMES], *[new_m[n] for n in NAMES],
            *[new_v[n] for n in NAMES])
```

```python
import functools

import numpy as np
import jax
import jax.numpy as jnp
from jax import lax
from jax.experimental import pallas as pl
from jax.experimental.pallas import tpu as pltpu

F32 = jnp.float32
MXU_DTYPE = jnp.bfloat16

D_MODEL = 1024
D_FF = 2816
N_SHARD = 4
D_IN = 2048
GU_SHARD = 2 * D_FF // N_SHARD
FF_CHUNK = GU_SHARD
N_CHUNK = D_FF // FF_CHUNK
IN_SHARD = D_IN // N_SHARD
OUT_SHARD = D_MODEL // N_SHARD
HEAD_DIM = 64
N_Q_HEADS = 8
BLOCK = 128
SC_W = 3
CC_W = 31
D_CONV = 256
HALO = 16
LN_EPS = 1e-5
ALPHA = (2.0 * 2) ** 0.25
NEG = -1e30
ROPE_THETA = 10000.0
ADAM_LR, ADAM_B1, ADAM_B2, ADAM_EPS, ADAM_WD, ADAM_STEP = 0.001, 0.9, 0.999, 1e-08, 0.01, 10

TM = 512
TK = 1024
TMC = 256
VMEM_LIMIT = 56 * 1024 * 1024
MESH = pl.DeviceIdType.MESH
ANY = pl.BlockSpec(memory_space=pl.ANY)


def _cparams(*sem):
    return pltpu.CompilerParams(dimension_semantics=sem, vmem_limit_bytes=VMEM_LIMIT)


def _dot(a, b):
    return jnp.dot(a, b, preferred_element_type=F32)


def _dot_nt(a, b):
    return lax.dot_general(a, b, (((1,), (1,)), ((), ())), preferred_element_type=F32)


def _dot_tn(a, b):
    return lax.dot_general(a, b, (((0,), (0,)), ((), ())), preferred_element_type=F32)


def _mx(a):
    return a.astype(MXU_DTYPE)


def _mean(a):
    return jnp.mean(a, axis=-1, keepdims=True)


def _ln_stats(r):
    xc = r - _mean(r)
    rstd = lax.rsqrt(_mean(xc * xc) + LN_EPS)
    return xc * rstd, rstd


def _ln_bwd(dy, xh, rstd, gamma):
    dxh = dy * gamma
    return rstd * (dxh - _mean(dxh) - xh * _mean(dxh * xh))


def _colsum(a):
    return jnp.sum(a, axis=0, keepdims=True)


def _sigmoid(a):
    return 1.0 / (1.0 + jnp.exp(-a))


def _call(body, args, *, name, grid, in_specs, out_specs, out_shape, scratch, sem, rider=None):
    if rider is None:
        outs = pl.pallas_call(
            body, name=name, grid=grid, in_specs=in_specs, out_specs=out_specs, out_shape=out_shape,
            scratch_shapes=scratch, compiler_params=_cparams(*sem))(*args)
        return list(outs), []
    n_in, n_out, n_sc = len(in_specs), len(out_specs), len(scratch)
    r_in, r_out = len(rider.ins), len(rider.outs)

    def carrying(*refs):
        cuts = np.cumsum([0, n_in, r_in, n_out, r_out, n_sc])
        ins, rins, outs, routs, scr = [refs[a:b] for a, b in zip(cuts[:-1], cuts[1:])]
        rsems = refs[cuts[-1]:]
        first = functools.reduce(jnp.logical_and, [pl.program_id(d) == 0 for d in range(len(grid))])
        last = functools.reduce(jnp.logical_and, [pl.program_id(d) == grid[d] - 1 for d in range(len(grid))])

        @pl.when(first)
        def _():
            rider.start(rins, routs, rsems)

        body(*ins, *outs, *scr)

        @pl.when(last)
        def _():
            rider.finish(rins, routs, rsems)

    outs = pl.pallas_call(
        carrying, name=name, grid=grid, in_specs=list(in_specs) + [ANY] * r_in,
        out_specs=list(out_specs) + [ANY] * r_out, out_shape=list(out_shape) + list(rider.outs),
        scratch_shapes=list(scratch) + list(rider.sems), compiler_params=_cparams(*(("arbitrary",) * len(grid))),
    )(*args, *rider.ins)
    return list(outs[:n_out]), list(outs[n_out:])


def _ffn_fwd(name, x, wgu, wd, gamma, beta, l, rider=None):
    t = x.shape[0]
    nc = N_CHUNK

    def body(x_ref, wg_ref, wu_ref, wd_ref, g_ref, b_ref, y_ref, r_ref, gu_ref, xb_s, acc_s):
        c = pl.program_id(1)

        @pl.when(c == 0)
        def _():
            xb_s[...] = _mx(x_ref[...])
            acc_s[...] = jnp.zeros_like(acc_s)

        xb = xb_s[...]
        hg = _dot(xb, wg_ref[...])
        hu = _dot(xb, wu_ref[...])
        s = _sigmoid(hg)
        sil = hg * s
        gu_ref[0] = _mx(hu)
        gu_ref[1] = _mx(sil)
        gu_ref[2] = _mx(s)
        acc_s[...] += _dot(_mx(sil * hu), wd_ref[...])

        @pl.when(c == nc - 1)
        def _():
            r = ALPHA * x_ref[...] + 0.5 * acc_s[...]
            xh, _ = _ln_stats(r)
            r_ref[...] = r
            y_ref[...] = xh * g_ref[...] + b_ref[...]

    row = pl.BlockSpec((TM, D_MODEL), lambda i, c: (i, 0))
    vec = pl.BlockSpec((1, D_MODEL), lambda i, c: (0, 0))
    return _call(
        body, (x, wgu, wgu, wd, gamma, beta), name=name, grid=(t // TM, nc),
        in_specs=[row,
                  pl.BlockSpec((None, None, D_MODEL, FF_CHUNK), lambda i, c: (l, c, 0, 0)),
                  pl.BlockSpec((None, None, D_MODEL, FF_CHUNK), lambda i, c: (l, N_CHUNK + c, 0, 0)),
                  pl.BlockSpec((None, FF_CHUNK, D_MODEL), lambda i, c: (l, c, 0)),
                  vec, vec],
        out_specs=[row, row, pl.BlockSpec((3, TM, FF_CHUNK), lambda i, c: (0, i, c))],
        out_shape=[jax.ShapeDtypeStruct((t, D_MODEL), F32), jax.ShapeDtypeStruct((t, D_MODEL), F32),
                   jax.ShapeDtypeStruct((3, t, D_FF), MXU_DTYPE)],
        scratch=[pltpu.VMEM((TM, D_MODEL), MXU_DTYPE), pltpu.VMEM((TM, D_MODEL), F32)],
        sem=("parallel", "arbitrary"), rider=rider)


def _norm_bwd_tail(dy, r_ref, g_ref, res_ref, do_ref, dgb_ref):
    @pl.when(pl.program_id(0) == 0)
    def _():
        dgb_ref[...] = jnp.zeros_like(dgb_ref)

    xh, rstd = _ln_stats(r_ref[...])
    dr = _ln_bwd(dy, xh, rstd, g_ref[...])
    do_ref[...] = _mx(0.5 * dr)
    res_ref[...] = ALPHA * dr
    dgb_ref[0:1, :] += _colsum(dy * xh)
    dgb_ref[1:2, :] += _colsum(dy)


def _norm_tail_specs(t):
    row = pl.BlockSpec((TM, D_MODEL), lambda i: (i, 0))
    return ([row, pl.BlockSpec((1, D_MODEL), lambda i: (0, 0))],
            [row, row, pl.BlockSpec((8, D_MODEL), lambda i: (0, 0))],
            [jax.ShapeDtypeStruct((t, D_MODEL), F32), jax.ShapeDtypeStruct((t, D_MODEL), MXU_DTYPE),
             jax.ShapeDtypeStruct((8, D_MODEL), F32)])


def _ffn_bwd(res, do, gu, wgu, wd, l, tail=None, rider=None):
    t = res.shape[0]
    nc = N_CHUNK
    row1 = pl.BlockSpec((TM, D_MODEL), lambda i: (i, 0))

    def hidden_body(do_ref, gu_ref, wd_ref, dh_ref, a_ref):
        da = _dot_nt(do_ref[...], wd_ref[...])
        u = gu_ref[0].astype(F32)
        sil = gu_ref[1].astype(F32)
        a_ref[...] = _mx(sil * u)
        s = gu_ref[2].astype(F32)
        dh_ref[0] = _mx(da * u * (s + sil * (1.0 - s)))
        dh_ref[1] = _mx(da * sil)

    hid = pl.BlockSpec((2, TM, FF_CHUNK), lambda c, i: (0, i, c))
    (dh, a), got = _call(
        hidden_body, (do, gu, wd), name="ffn_bwd_hidden" if rider is None else "ffn_bwd_hidden_carry", grid=(nc, t // TM),
        in_specs=[pl.BlockSpec((TM, D_MODEL), lambda c, i: (i, 0)), pl.BlockSpec((3, TM, FF_CHUNK), lambda c, i: (0, i, c)),
                  pl.BlockSpec((None, FF_CHUNK, D_MODEL), lambda c, i: (l, c, 0))],
        out_specs=[hid, pl.BlockSpec((TM, FF_CHUNK), lambda c, i: (i, c))],
        out_shape=[jax.ShapeDtypeStruct((2, t, D_FF), MXU_DTYPE), jax.ShapeDtypeStruct((t, D_FF), MXU_DTYPE)],
        scratch=[], sem=("parallel", "parallel"), rider=rider)

    def input_body(res_ref, dh_ref, w_ref, *rest):
        acc = res_ref[...]
        for j in range(N_SHARD):
            part = dh_ref[j // N_CHUNK][:, (j % N_CHUNK) * FF_CHUNK:(j % N_CHUNK + 1) * FF_CHUNK]
            acc += _dot_nt(part, w_ref[j])
        if tail is None:
            rest[0][...] = acc
        else:
            _norm_bwd_tail(acc, *rest)

    in_specs = [row1, pl.BlockSpec((2, TM, D_FF), lambda i: (0, i, 0)),
                pl.BlockSpec((None, N_SHARD, D_MODEL, GU_SHARD), lambda i: (l, 0, 0, 0))]
    if tail is None:
        dx = pl.pallas_call(
            input_body, name="ffn_bwd_input", grid=(t // TM,), in_specs=in_specs, out_specs=row1,
            out_shape=jax.ShapeDtypeStruct((t, D_MODEL), F32), compiler_params=_cparams("parallel"),
        )(res, dh, wgu)
    else:
        tail_in, tail_out, tail_shape = _norm_tail_specs(t)
        dx = pl.pallas_call(
            input_body, name="ffn_bwd_input_norm", grid=(t // TM,), in_specs=in_specs + tail_in, out_specs=tail_out,
            out_shape=tail_shape, compiler_params=_cparams("arbitrary"),
        )(res, dh, wgu, *tail)
    return dx, dh, a, got


def _mm_tn(name, a, b, a_spec, b_spec, out_spec, out_shape, acc_shape, grid, rider=None, split=1):
    nk = grid[-1]
    width = acc_shape[1] // split

    def body(*refs):
        a_ref, b_ref = refs[0], refs[1]
        o_ref, acc = refs[-2], refs[-1]
        k = pl.program_id(len(grid) - 1)

        @pl.when(k == 0)
        def _():
            acc[...] = jnp.zeros_like(acc)

        acc[...] += _dot_tn(_mx(a_ref[...]), _mx(b_ref[...]))

        @pl.when(k == nk - 1)
        def _():
            if split == 1:
                o_ref[...] = acc[...]
            else:
                for j in range(split):
                    o_ref[j] = acc[:, j * width:(j + 1) * width]

    sem = ("parallel",) * (len(grid) - 1) + ("arbitrary",)
    (out,), got = _call(
        body, (a, b), name=name, grid=grid, in_specs=[a_spec, b_spec], out_specs=[out_spec],
        out_shape=[jax.ShapeDtypeStruct(out_shape, F32)], scratch=[pltpu.VMEM(acc_shape, F32)], sem=sem, rider=rider)
    return out, got


def _in_proj(x, w_in, l):
    t = x.shape[0]

    def body(x_ref, w_ref, z_ref):
        xb = _mx(x_ref[...])
        for j in range(N_SHARD):
            z_ref[:, j * IN_SHARD:(j + 1) * IN_SHARD] = _dot(xb, w_ref[j])

    return pl.pallas_call(
        body, name="in_proj", grid=(t // TM,),
        in_specs=[pl.BlockSpec((TM, D_MODEL), lambda i: (i, 0)),
                  pl.BlockSpec((None, N_SHARD, D_MODEL, IN_SHARD), lambda i: (l, 0, 0, 0))],
        out_specs=pl.BlockSpec((TM, D_IN), lambda i: (i, 0)),
        out_shape=jax.ShapeDtypeStruct((t, D_IN), F32),
        compiler_params=_cparams("parallel"),
    )(x, w_in)


def _in_proj_bwd(dz, dx_res, w_in, l, r, gamma):
    t = dz.shape[0]

    def body(dz_ref, res_ref, w_ref, *tail):
        acc = res_ref[...]
        for j in range(N_SHARD):
            acc += _dot_nt(dz_ref[:, j * IN_SHARD:(j + 1) * IN_SHARD], w_ref[j])
        _norm_bwd_tail(acc, *tail)

    row = pl.BlockSpec((TM, D_MODEL), lambda i: (i, 0))
    tail_in, tail_out, tail_shape = _norm_tail_specs(t)
    return pl.pallas_call(
        body, name="in_proj_bwd", grid=(t // TM,),
        in_specs=[pl.BlockSpec((TM, D_IN), lambda i: (i, 0)), row,
                  pl.BlockSpec((None, N_SHARD, D_MODEL, IN_SHARD), lambda i: (l, 0, 0, 0))] + tail_in,
        out_specs=tail_out, out_shape=tail_shape, compiler_params=_cparams("arbitrary"),
    )(dz, dx_res, w_in, r, gamma)


def _out_proj(ycat, x, w_out, gamma, beta, l):
    t = x.shape[0]

    def body(yc_ref, x_ref, w_ref, g_ref, b_ref, y_ref, r_ref):
        r = ALPHA * x_ref[...] + _dot(yc_ref[...], w_ref[...])
        xh, _ = _ln_stats(r)
        r_ref[...] = r
        y_ref[...] = xh * g_ref[...] + b_ref[...]

    row = pl.BlockSpec((TM, D_MODEL), lambda i: (i, 0))
    vec = pl.BlockSpec((1, D_MODEL), lambda i: (0, 0))
    return pl.pallas_call(
        body, name="out_proj", grid=(t // TM,),
        in_specs=[row, row, pl.BlockSpec((None, D_MODEL, D_MODEL), lambda i: (l, 0, 0)), vec, vec],
        out_specs=[row, row],
        out_shape=[jax.ShapeDtypeStruct((t, D_MODEL), F32)] * 2,
        compiler_params=_cparams("parallel"),
    )(ycat, x, w_out, gamma, beta)


def _out_proj_bwd(dy, r, w_out, gamma, l):
    t = dy.shape[0]

    def body(dy_ref, r_ref, w_ref, g_ref, res_ref, dm_ref, dyc_ref, dgb_ref):
        @pl.when(pl.program_id(0) == 0)
        def _():
            dgb_ref[...] = jnp.zeros_like(dgb_ref)

        xh, rstd = _ln_stats(r_ref[...])
        dy = dy_ref[...]
        dr = _ln_bwd(dy, xh, rstd, g_ref[...])
        res_ref[...] = ALPHA * dr
        dm = _mx(dr)
        dm_ref[...] = dm
        dyc_ref[...] = _dot_nt(dm, w_ref[...])
        dgb_ref[0:1, :] += _colsum(dy * xh)
        dgb_ref[1:2, :] += _colsum(dy)

    row = pl.BlockSpec((TM, D_MODEL), lambda i: (i, 0))
    return pl.pallas_call(
        body, name="out_proj_bwd", grid=(t // TM,),
        in_specs=[row, row, pl.BlockSpec((None, D_MODEL, D_MODEL), lambda i: (l, 0, 0)),
                  pl.BlockSpec((1, D_MODEL), lambda i: (0, 0))],
        out_specs=[row, row, row, pl.BlockSpec((8, D_MODEL), lambda i: (0, 0))],
        out_shape=[jax.ShapeDtypeStruct((t, D_MODEL), F32), jax.ShapeDtypeStruct((t, D_MODEL), MXU_DTYPE),
                   jax.ShapeDtypeStruct((t, D_MODEL), F32), jax.ShapeDtypeStruct((8, D_MODEL), F32)],
        compiler_params=_cparams("arbitrary"),
    )(dy, r, w_out, gamma)


def _halo_specs(t, width, col):
    per = TMC // HALO
    last = t // HALO - 1
    return [pl.BlockSpec((HALO, width), lambda i: (jnp.maximum(i * per - 1, 0), col)),
            pl.BlockSpec((TMC, width), lambda i: (i, col)),
            pl.BlockSpec((HALO, width), lambda i: (jnp.minimum((i + 1) * per, last), col))]


def _extend(refs, i, nt):
    p_ref, c_ref, n_ref = refs
    p = jnp.where(i > 0, p_ref[...].astype(F32), 0.0)
    n = jnp.where(i < nt - 1, n_ref[...].astype(F32), 0.0)
    return jnp.concatenate([p, c_ref[...].astype(F32), n], axis=0)


def _shifted_copies(src_s, dst8_s):
    n = src_s.shape[0] - 8
    for b in range(8):
        dst8_s[b, 0:n, :] = src_s[pl.ds(b, n), :]


def _window(dst8_s, start):
    return dst8_s[start % 8, pl.ds(start - start % 8, TMC), :]


def _conv_fwd(z, sc_w, cc_w, cc_cb, cc_g, cc_b):
    t = z.shape[0]
    nt = t // TMC

    def body(*refs):
        b_ref = refs[0]
        c3, h3, a3, g3 = refs[1:4], refs[4:7], refs[7:10], refs[10:13]
        scw_ref, ccw_ref, cb_ref, lg_ref, lb_ref = refs[13:18]
        ysc_ref, ycc_ref, u2_ref, e_s, e8_s = refs[18:23]
        i = pl.program_id(0)
        e_s[...] = _extend(c3, i, nt) * _extend(h3, i, nt)
        cv = jnp.zeros((TMC, D_CONV), F32)
        for k in range(SC_W):
            cv += scw_ref[k:k + 1, :] * e_s[pl.ds(HALO + k - 1, TMC), :]
        ysc_ref[...] = _mx(b_ref[...] * cv)
        e_s[...] = _extend(a3, i, nt) * _sigmoid(_extend(g3, i, nt))
        _shifted_copies(e_s, e8_s)
        u2 = jnp.zeros((TMC, D_CONV), F32) + cb_ref[...]
        for k in range(CC_W):
            u2 += ccw_ref[k:k + 1, :] * _window(e8_s, HALO + k - 15)
        u2_ref[...] = u2
        xh, _ = _ln_stats(u2)
        n = xh * lg_ref[...] + lb_ref[...]
        ycc_ref[...] = _mx(n * _sigmoid(n))

    tile = pl.BlockSpec((TMC, D_CONV), lambda i: (i, 0))
    vec = pl.BlockSpec((1, D_CONV), lambda i: (0, 0))
    in_specs = ([pl.BlockSpec((TMC, D_CONV), lambda i: (i, 0))] + _halo_specs(t, D_CONV, 1) + _halo_specs(t, D_CONV, 2)
                + _halo_specs(t, D_CONV, 6) + _halo_specs(t, D_CONV, 7)
                + [pl.BlockSpec((SC_W, D_CONV), lambda i: (0, 0)), pl.BlockSpec((CC_W, D_CONV), lambda i: (0, 0)),
                   vec, vec, vec])
    return pl.pallas_call(
        body, name="conv_fwd", grid=(nt,), in_specs=in_specs, out_specs=[tile, tile, tile],
        out_shape=[jax.ShapeDtypeStruct((t, D_CONV), MXU_DTYPE), jax.ShapeDtypeStruct((t, D_CONV), MXU_DTYPE),
                   jax.ShapeDtypeStruct((t, D_CONV), F32)],
        scratch_shapes=[pltpu.VMEM((TMC + 2 * HALO, D_CONV), F32), pltpu.VMEM((8, TMC + 2 * HALO, D_CONV), F32)],
        compiler_params=_cparams("parallel"),
    )(*([z] * 13), sc_w, cc_w, cc_cb, cc_g, cc_b)


ROW_CCW, ROW_CCB, ROW_CCG, ROW_CCBETA, ROW_SCW, CONV_ROWS = 0, 31, 32, 33, 34, 40


def _conv_bwd(z, dycat, u2, sc_w, cc_w, cc_g, cc_b):
    t = z.shape[0]
    nt = t // TMC

    def body(*refs):
        b3, c3, h3, a3, g3 = refs[0:3], refs[3:6], refs[6:9], refs[9:12], refs[12:15]
        dys3, dyc3, u3 = refs[15:18], refs[18:21], refs[21:24]
        scw_ref, ccw_ref, lg_ref, lb_ref = refs[24:28]
        dsc_ref, dcc_ref, sm_ref, e_s, f_s, e8_s, f8_s = refs[28:35]
        i = pl.program_id(0)

        @pl.when(i == 0)
        def _():
            sm_ref[...] = jnp.zeros_like(sm_ref)

        cur = pl.ds(HALO, TMC)
        e_s[...] = _extend(c3, i, nt) * _extend(h3, i, nt)
        f_s[...] = _extend(dys3, i, nt) * _extend(b3, i, nt)
        cv = jnp.zeros((TMC, D_CONV), F32)
        dp = jnp.zeros((TMC, D_CONV), F32)
        dcv = f_s[cur, :]
        for k in range(SC_W):
            win = e_s[pl.ds(HALO + k - 1, TMC), :]
            cv += scw_ref[k:k + 1, :] * win
            dp += scw_ref[k:k + 1, :] * f_s[pl.ds(HALO - k + 1, TMC), :]
            sm_ref[ROW_SCW + k:ROW_SCW + k + 1, :] += _colsum(dcv * win)
        dsc_ref[:, 0:D_CONV] = _mx(dys3[1][...] * cv)
        dsc_ref[:, D_CONV:2 * D_CONV] = _mx(dp * h3[1][...])
        dsc_ref[:, 2 * D_CONV:3 * D_CONV] = _mx(dp * c3[1][...])
        xh, rstd = _ln_stats(_extend(u3, i, nt))
        n = xh * lg_ref[...] + lb_ref[...]
        sg = _sigmoid(n)
        dn = _extend(dyc3, i, nt) * (sg * (1.0 + n * (1.0 - sg)))
        f_s[...] = _ln_bwd(dn, xh, rstd, lg_ref[...])
        sm_ref[ROW_CCG:ROW_CCG + 1, :] += _colsum((dn * xh)[HALO:HALO + TMC])
        sm_ref[ROW_CCBETA:ROW_CCBETA + 1, :] += _colsum(dn[HALO:HALO + TMC])
        sig_g = _sigmoid(_extend(g3, i, nt))
        e_s[...] = _extend(a3, i, nt) * sig_g
        _shifted_copies(e_s, e8_s)
        _shifted_copies(f_s, f8_s)
        du2 = f_s[cur, :]
        sm_ref[ROW_CCB:ROW_CCB + 1, :] += _colsum(du2)
        duu = jnp.zeros((TMC, D_CONV), F32)
        for k in range(CC_W):
            duu += ccw_ref[k:k + 1, :] * _window(f8_s, HALO + 15 - k)
            sm_ref[ROW_CCW + k:ROW_CCW + k + 1, :] += _colsum(du2 * _window(e8_s, HALO + k - 15))
        sgc = sig_g[HALO:HALO + TMC]
        dcc_ref[:, 0:D_CONV] = _mx(duu * sgc)
        dcc_ref[:, D_CONV:2 * D_CONV] = _mx(duu * a3[1][...] * sgc * (1.0 - sgc))

    vec = pl.BlockSpec((1, D_CONV), lambda i: (0, 0))
    in_specs = []
    for col in (0, 1, 2, 6, 7):
        in_specs += _halo_specs(t, D_CONV, col)
    in_specs += _halo_specs(t, D_CONV, 0) + _halo_specs(t, D_CONV, 3) + _halo_specs(t, D_CONV, 0)
    in_specs += [pl.BlockSpec((SC_W, D_CONV), lambda i: (0, 0)), pl.BlockSpec((CC_W, D_CONV), lambda i: (0, 0)), vec, vec]
    return pl.pallas_call(
        body, name="conv_bwd", grid=(nt,), in_specs=in_specs,
        out_specs=[pl.BlockSpec((TMC, 3 * D_CONV), lambda i: (i, 0)), pl.BlockSpec((TMC, 2 * D_CONV), lambda i: (i, 0)),
                   pl.BlockSpec((CONV_ROWS, D_CONV), lambda i: (0, 0))],
        out_shape=[jax.ShapeDtypeStruct((t, 3 * D_CONV), MXU_DTYPE), jax.ShapeDtypeStruct((t, 2 * D_CONV), MXU_DTYPE),
                   jax.ShapeDtypeStruct((CONV_ROWS, D_CONV), F32)],
        scratch_shapes=[pltpu.VMEM((TMC + 2 * HALO, D_CONV), F32)] * 2
        + [pltpu.VMEM((8, TMC + 2 * HALO, D_CONV), F32)] * 2,
        compiler_params=_cparams("arbitrary"),
    )(*([z] * 15), *([dycat] * 6), *([u2] * 3), sc_w, cc_w, cc_g, cc_b)


def _lane(shape):
    return lax.broadcasted_iota(jnp.int32, shape, 1)


def _swap_halves(x):
    w = x.shape[1]
    lo = (_lane(x.shape) % HEAD_DIM) < HEAD_DIM // 2
    return jnp.where(lo, pltpu.roll(x, w - HEAD_DIM // 2, 1), pltpu.roll(x, HEAD_DIM // 2, 1))


def _half(shape, g):
    lane = _lane(shape)
    return lane < HEAD_DIM if g == 0 else lane >= HEAD_DIM


GROUP_ROWS = 4 * BLOCK


def _stack_heads(tiles, out_ref, nblk):
    for tt in range(4):
        g = tt // 2
        for slot in range(2):
            s = 2 * (tt % 2) + slot
            piece = tiles[tt] if slot == g else pltpu.roll(tiles[tt], HEAD_DIM, 1)
            piece = jnp.where(_half(piece.shape, g), piece, 0.0).astype(out_ref.dtype)
            for b in range(nblk):
                at = GROUP_ROWS * b + BLOCK * s
                out_ref[g, at:at + BLOCK, :] = piece[BLOCK * b:BLOCK * (b + 1)]


def _unstack_heads(ref, nblk):
    tiles = []
    for tt in range(4):
        g = tt // 2
        tile = None
        for slot in range(2):
            s = 2 * (tt % 2) + slot
            rows = [ref[g, GROUP_ROWS * b + BLOCK * s:GROUP_ROWS * b + BLOCK * (s + 1), :] for b in range(nblk)]
            piece = rows[0] if nblk == 1 else jnp.concatenate(rows, axis=0)
            if slot != g:
                piece = pltpu.roll(piece, HEAD_DIM, 1)
            tile = piece if tile is None else tile + piece
        tiles.append(tile)
    return tiles


def _attn_prep(z, cos, sin):
    t = z.shape[0]
    nblk = TM // BLOCK

    def body(qa_ref, qb_ref, k_ref, v_ref, cos_ref, sin_ref, qst_ref, kr_ref, vb_ref):
        cs, sn = cos_ref[...], sin_ref[...]

        def rope(x):
            return x * cs + _swap_halves(x) * sn

        tiles = []
        for tt in range(4):
            src = qa_ref if tt < 2 else qb_ref
            tiles.append(rope(src[:, (tt % 2) * BLOCK:(tt % 2 + 1) * BLOCK]) * (HEAD_DIM ** -0.5))
        _stack_heads(tiles, qst_ref, nblk)
        kr_ref[...] = _mx(rope(k_ref[...]))
        vb_ref[...] = _mx(v_ref[...])

    def col(width, j):
        return pl.BlockSpec((TM, width), lambda i: (i, j))

    return pl.pallas_call(
        body, name="attn_prep", grid=(t // TM,),
        in_specs=[col(256, 3), col(256, 4), col(128, 10), col(128, 11), col(128, 0), col(128, 0)],
        out_specs=[pl.BlockSpec((2, 4 * TM, BLOCK), lambda i: (0, i, 0)), col(128, 0), col(128, 0)],
        out_shape=[jax.ShapeDtypeStruct((2, 4 * t, BLOCK), MXU_DTYPE), jax.ShapeDtypeStruct((t, BLOCK), MXU_DTYPE),
                   jax.ShapeDtypeStruct((t, BLOCK), MXU_DTYPE)],
        compiler_params=_cparams("parallel"),
    )(z, z, z, z, cos, sin)


def _attn_dprep(dycat, ost, lst):
    t = dycat.shape[0]
    nblk = TM // BLOCK

    def body(da_ref, db_ref, o_ref, l_ref, dost_ref, ld_ref, st_s):
        tiles = []
        for tt in range(4):
            src = da_ref if tt < 2 else db_ref
            tiles.append(src[:, (tt % 2) * BLOCK:(tt % 2 + 1) * BLOCK])
        _stack_heads(tiles, st_s, nblk)
        for g in range(2):
            do = st_s[g]
            dost_ref[g] = _mx(do)
            dsum = jnp.sum(do * o_ref[g], axis=-1, keepdims=True)
            ld_ref[g] = jnp.where(_lane(do.shape) < HEAD_DIM, l_ref[g], dsum)

    stacked = pl.BlockSpec((2, 4 * TM, BLOCK), lambda i: (0, i, 0))
    return pl.pallas_call(
        body, name="attn_dprep", grid=(t // TM,),
        in_specs=[pl.BlockSpec((TM, 256), lambda i: (i, 1)), pl.BlockSpec((TM, 256), lambda i: (i, 2)), stacked, stacked],
        out_specs=[stacked, stacked],
        out_shape=[jax.ShapeDtypeStruct((2, 4 * t, BLOCK), MXU_DTYPE), jax.ShapeDtypeStruct((2, 4 * t, BLOCK), F32)],
        scratch_shapes=[pltpu.VMEM((2, 4 * TM, BLOCK), F32)],
        compiler_params=_cparams("parallel"),
    )(dycat, dycat, ost, lst)


def _attn_prep_bwd(dqst, dk, dv, cos, sin):
    t = dk.shape[0]
    nblk = TM // BLOCK

    def body(dq_ref, dk_ref, dv_ref, cos_ref, sin_ref, dz_ref):
        cs, sn = cos_ref[...], sin_ref[...]

        def rope_bwd(d):
            return d * cs + _swap_halves(d * sn)

        for tt, tile in enumerate(_unstack_heads(dq_ref, nblk)):
            dz_ref[:, tt * BLOCK:(tt + 1) * BLOCK] = _mx(rope_bwd(tile * (HEAD_DIM ** -0.5)))
        dz_ref[:, 4 * BLOCK:5 * BLOCK] = _mx(rope_bwd(dk_ref[...]))
        dz_ref[:, 5 * BLOCK:6 * BLOCK] = _mx(dv_ref[...])

    def col(width):
        return pl.BlockSpec((TM, width), lambda i: (i, 0))

    return pl.pallas_call(
        body, name="attn_prep_bwd", grid=(t // TM,),
        in_specs=[pl.BlockSpec((2, 4 * TM, BLOCK), lambda i: (0, i, 0)), col(128), col(128), col(128), col(128)],
        out_specs=col(768), out_shape=jax.ShapeDtypeStruct((t, 768), MXU_DTYPE),
        compiler_params=_cparams("parallel"),
    )(dqst, dk, dv, cos, sin)


def _nbr_specs(nb, width, col):
    return [pl.BlockSpec((BLOCK, width), lambda n: (jnp.maximum(n - 1, 0), col)),
            pl.BlockSpec((BLOCK, width), lambda n: (n, col)),
            pl.BlockSpec((BLOCK, width), lambda n: (jnp.minimum(n + 1, nb - 1), col))]


def _query_index():
    row = lax.broadcasted_iota(jnp.int32, (GROUP_ROWS, BLOCK), 0)
    return row & (BLOCK - 1), lax.broadcasted_iota(jnp.int32, (GROUP_ROWS, BLOCK), 1)


def _sink_column(sink_ref, g):
    band = lax.broadcasted_iota(jnp.int32, (GROUP_ROWS, 1), 0) // BLOCK
    col = jnp.zeros((GROUP_ROWS, 1), F32) + sink_ref[4 * g]
    for s in range(1, 4):
        col = jnp.where(band == s, sink_ref[4 * g + s], col)
    return col


def _attn_fwd(qst, kr, vb, sink):
    t = kr.shape[0]
    nb = t // BLOCK

    def body(q_ref, kp_ref, kc_ref, kn_ref, vp_ref, vc_ref, vn_ref, sink_ref, o_ref, ost_ref, lst_ref):
        n = pl.program_id(0)
        qi, kj = _query_index()
        m_prev, m_next = (kj >= qi) & (n > 0), (kj <= qi) & (n < nb - 1)
        nat = [None] * 4
        for g in range(2):
            q = q_ref[g]
            sp = jnp.where(m_prev, _dot_nt(q, kp_ref[...]), NEG)
            sc = _dot_nt(q, kc_ref[...])
            sn = jnp.where(m_next, _dot_nt(q, kn_ref[...]), NEG)
            sk = _sink_column(sink_ref, g)
            m = jnp.maximum(jnp.max(jnp.maximum(jnp.maximum(sp, sc), sn), axis=-1, keepdims=True), sk)
            pp, pc, pn = jnp.exp(sp - m), jnp.exp(sc - m), jnp.exp(sn - m)
            den = jnp.sum(pp + pc + pn, axis=-1, keepdims=True) + jnp.exp(sk - m)
            o = (_dot(_mx(pp), vp_ref[...]) + _dot(_mx(pc), vc_ref[...]) + _dot(_mx(pn), vn_ref[...])) / den
            o = jnp.where(_half(o.shape, g), o, 0.0)
            ost_ref[g] = o
            lst_ref[g] = jnp.broadcast_to(m + jnp.log(den), (GROUP_ROWS, BLOCK))
            for s in range(4):
                tt, slot = 2 * g + s // 2, s % 2
                piece = o[BLOCK * s:BLOCK * (s + 1)]
                if slot != g:
                    piece = pltpu.roll(piece, HEAD_DIM, 1)
                nat[tt] = piece if nat[tt] is None else nat[tt] + piece
        for tt in range(4):
            o_ref[:, tt * BLOCK:(tt + 1) * BLOCK] = _mx(nat[tt])

    stacked = pl.BlockSpec((2, GROUP_ROWS, BLOCK), lambda n: (0, n, 0))
    return pl.pallas_call(
        body, name="attn_fwd", grid=(nb,),
        in_specs=[stacked] + _nbr_specs(nb, BLOCK, 0) + _nbr_specs(nb, BLOCK, 0) + [pl.BlockSpec(memory_space=pltpu.SMEM)],
        out_specs=[pl.BlockSpec((BLOCK, 512), lambda n: (n, 0)), stacked, stacked],
        out_shape=[jax.ShapeDtypeStruct((t, 512), MXU_DTYPE), jax.ShapeDtypeStruct((2, 4 * t, BLOCK), F32),
                   jax.ShapeDtypeStruct((2, 4 * t, BLOCK), F32)],
        compiler_params=_cparams("parallel"),
    )(qst, kr, kr, kr, vb, vb, vb, sink)


def _lse_and_dsum(ld):
    return ld[:, 0:1], pltpu.roll(ld, HEAD_DIM, 1)[:, 0:1]


def _attn_bwd(qst, kr, vb, dost, ld, sink):
    t = kr.shape[0]
    nb = t // BLOCK

    def body(q_ref, kp_ref, kc_ref, kn_ref, vp_ref, vc_ref, vn_ref, do_ref, ld_ref, sink_ref,
             dq_ref, dk_ref, dv_ref, ds_ref):
        n = pl.program_id(0)

        @pl.when(n == 0)
        def _():
            ds_ref[...] = jnp.zeros_like(ds_ref)
            dk_ref[...] = jnp.zeros_like(dk_ref)
            dv_ref[...] = jnp.zeros_like(dv_ref)

        qi, kj = _query_index()
        m_prev, m_next = (kj >= qi) & (n > 0), (kj <= qi) & (n < nb - 1)
        key_rows = [pl.ds(pl.multiple_of(jnp.clip(n - 1 + b, 0, nb - 1) * BLOCK, BLOCK), BLOCK) for b in range(3)]
        for g in range(2):
            q, do = q_ref[g], do_ref[g]
            lse, dsum = _lse_and_dsum(ld_ref[g])
            acc = jnp.zeros((GROUP_ROWS, BLOCK), F32)
            for b, (k_ref, v_ref, valid) in enumerate(((kp_ref, vp_ref, m_prev), (kc_ref, vc_ref, None),
                                                       (kn_ref, vn_ref, m_next))):
                sc = _dot_nt(q, k_ref[...])
                if valid is not None:
                    sc = jnp.where(valid, sc, NEG)
                p = jnp.exp(sc - lse)
                dsc = _mx(p * (_dot_nt(do, v_ref[...]) - dsum))
                acc += _dot(dsc, k_ref[...])
                dv_ref[key_rows[b], :] += _dot_tn(_mx(p), do)
                dk_ref[key_rows[b], :] += _dot_tn(dsc, q)
            dq_ref[g] = jnp.where(_half(acc.shape, g), acc, 0.0)
            dsk = jnp.exp(_sink_column(sink_ref, g) - lse) * dsum
            for s in range(4):
                h = 4 * g + s
                ds_ref[h:h + 1, :] -= jnp.sum(dsk[BLOCK * s:BLOCK * (s + 1)], axis=0, keepdims=True)

    stacked = pl.BlockSpec((2, GROUP_ROWS, BLOCK), lambda n: (0, n, 0))
    whole = pl.BlockSpec((t, BLOCK), lambda n: (0, 0))
    return pl.pallas_call(
        body, name="attn_bwd", grid=(nb,),
        in_specs=[stacked] + _nbr_specs(nb, BLOCK, 0) + _nbr_specs(nb, BLOCK, 0)
        + [stacked, stacked, pl.BlockSpec(memory_space=pltpu.SMEM)],
        out_specs=[stacked, whole, whole, pl.BlockSpec((8, BLOCK), lambda n: (0, 0))],
        out_shape=[jax.ShapeDtypeStruct((2, 4 * t, BLOCK), F32), jax.ShapeDtypeStruct((t, BLOCK), F32),
                   jax.ShapeDtypeStruct((t, BLOCK), F32), jax.ShapeDtypeStruct((8, BLOCK), F32)],
        compiler_params=_cparams("arbitrary"),
    )(qst, kr, kr, kr, vb, vb, vb, dost, ld, sink)


def _loss_head(y, target, r, gamma):
    t = y.shape[0]

    def body(y_ref, t_ref, r_ref, g_ref, l_ref, res_ref, do_ref, dgb_ref):
        @pl.when(pl.program_id(0) == 0)
        def _():
            l_ref[...] = jnp.zeros_like(l_ref)

        e = y_ref[...] - t_ref[...]
        l_ref[...] += 0.5 * jnp.sum(_mean(e * e))
        _norm_bwd_tail(e / D_MODEL, r_ref, g_ref, res_ref, do_ref, dgb_ref)

    row = pl.BlockSpec((TM, D_MODEL), lambda i: (i, 0))
    tail_in, tail_out, tail_shape = _norm_tail_specs(t)
    return pl.pallas_call(
        body, name="loss_head", grid=(t // TM,), in_specs=[row, row] + tail_in,
        out_specs=[pl.BlockSpec((8, 128), lambda i: (0, 0))] + tail_out,
        out_shape=[jax.ShapeDtypeStruct((8, 128), F32)] + tail_shape,
        compiler_params=_cparams("arbitrary"),
    )(y, target, r, gamma)


def _adamw(name, w, g, m, v, rows):
    n, width = w.shape

    def body(w_ref, g_ref, m_ref, v_ref, d_ref, nm_ref, nv_ref):
        g = g_ref[...]
        m = ADAM_B1 * m_ref[...] + (1.0 - ADAM_B1) * g
        v = ADAM_B2 * v_ref[...] + (1.0 - ADAM_B2) * jnp.square(g)
        m_hat = m / (1.0 - ADAM_B1 ** ADAM_STEP)
        v_hat = v / (1.0 - ADAM_B2 ** ADAM_STEP)
        d_ref[...] = -ADAM_LR * (m_hat / (jnp.sqrt(v_hat) + ADAM_EPS) + ADAM_WD * w_ref[...])
        nm_ref[...] = m
        nv_ref[...] = v

    spec = pl.BlockSpec((rows, width), lambda i: (i, 0))
    return pl.pallas_call(
        body, name=name, grid=(n // rows,), in_specs=[spec] * 4, out_specs=[spec] * 3,
        out_shape=[jax.ShapeDtypeStruct((n, width), F32)] * 3, compiler_params=_cparams("parallel"),
    )(w, g, m, v)


def _place():
    x, y, c = lax.axis_index("x"), lax.axis_index("y"), lax.axis_index("c")
    chips = [(1 - x, y), (x, 1 - y), (1 - x, 1 - y)]
    return x, y, c, chips


class _Gather:
    def __init__(self, shards):
        na = len(shards)
        self.ins = list(shards)
        self.outs = [jax.ShapeDtypeStruct((N_SHARD,) + s.shape, s.dtype) for s in shards]
        self.sems = [pltpu.SemaphoreType.DMA((3 * na,))] * 4 + [pltpu.SemaphoreType.DMA((na,))]

    def _copies(self, src, dst, sems):
        send, recv, fsend, frecv, lsem = sems
        x, y, c, chips = _place()
        mine = 2 * x + y

        def local(a):
            return pltpu.make_async_copy(src[a], dst[a].at[mine], lsem.at[a])

        def ici(a, k, shard):
            cx, cy = chips[k]
            return pltpu.make_async_remote_copy(
                src_ref=src[a].at[c], dst_ref=dst[a].at[shard, c], send_sem=send.at[3 * a + k], recv_sem=recv.at[3 * a + k],
                device_id=(cx, cy, c), device_id_type=MESH)

        def d2d(a, k, half):
            cx, cy = chips[k]
            block = dst[a].at[2 * cx + cy, half]
            return pltpu.make_async_remote_copy(
                src_ref=block, dst_ref=block, send_sem=fsend.at[3 * a + k], recv_sem=frecv.at[3 * a + k],
                device_id=(x, y, 1 - c), device_id_type=MESH)

        return local, ici, d2d, mine, c, chips

    def start(self, src, dst, sems):
        local, ici, _, mine, _, _ = self._copies(src, dst, sems)
        for a in range(len(src)):
            local(a).start()
            for k in range(3):
                ici(a, k, mine).start()

    def finish(self, src, dst, sems):
        local, ici, d2d, mine, c, chips = self._copies(src, dst, sems)
        for a in range(len(src)):
            for k, (cx, cy) in enumerate(chips):
                ici(a, k, 2 * cx + cy).wait_recv()
                d2d(a, k, c).start()
        for a in range(len(src)):
            for k in range(3):
                d2d(a, k, 1 - c).wait_recv()
        for a in range(len(src)):
            for k in range(3):
                ici(a, k, mine).wait_send()
                d2d(a, k, c).wait_send()
            local(a).wait()


class _PairExchange:
    def __init__(self, parts):
        self.ins = list(parts)
        self.outs = [jax.ShapeDtypeStruct((N_SHARD,) + p.shape[2:], p.dtype) for p in parts]
        self.sems = [pltpu.SemaphoreType.DMA((len(parts),))] * 2

    def _copy(self, a, src, dst, sems):
        x, y, c, _ = _place()
        return pltpu.make_async_remote_copy(
            src_ref=src[a].at[:, 1 - c], dst_ref=dst[a], send_sem=sems[0].at[a], recv_sem=sems[1].at[a],
            device_id=(x, y, 1 - c), device_id_type=MESH)

    def start(self, src, dst, sems):
        for a in range(len(src)):
            self._copy(a, src, dst, sems).start()

    def finish(self, src, dst, sems):
        for a in range(len(src)):
            self._copy(a, src, dst, sems).wait()


class _ChipExchange:
    def __init__(self, sums):
        self.ins = list(sums)
        self.outs = [jax.ShapeDtypeStruct((3,) + s.shape[1:], s.dtype) for s in sums]
        self.sems = [pltpu.SemaphoreType.DMA((3 * len(sums),))] * 2

    def _copy(self, a, k, src, dst, sems):
        _, _, c, chips = _place()
        cx, cy = chips[k]
        return pltpu.make_async_remote_copy(
            src_ref=src[a].at[2 * cx + cy], dst_ref=dst[a].at[k], send_sem=sems[0].at[3 * a + k],
            recv_sem=sems[1].at[3 * a + k], device_id=(cx, cy, c), device_id_type=MESH)

    def start(self, src, dst, sems):
        for a in range(len(src)):
            for k in range(3):
                self._copy(a, k, src, dst, sems).start()

    def finish(self, src, dst, sems):
        for a in range(len(src)):
            for k in range(3):
                self._copy(a, k, src, dst, sems).wait()


class _Both:
    def __init__(self, a, b):
        self.a, self.b = a, b
        self.ins, self.outs, self.sems = a.ins + b.ins, a.outs + b.outs, a.sems + b.sems

    def _each(self, method, ins, outs, sems):
        a = self.a
        getattr(a, method)(ins[:len(a.ins)], outs[:len(a.outs)], sems[:len(a.sems)])
        getattr(self.b, method)(ins[len(a.ins):], outs[len(a.outs):], sems[len(a.sems):])

    def start(self, ins, outs, sems):
        self._each("start", ins, outs, sems)

    def finish(self, ins, outs, sems):
        self._each("finish", ins, outs, sems)


def _run(name, rider):
    n_in, n_out = len(rider.ins), len(rider.outs)

    def body(*refs):
        ins, outs, sems = refs[:n_in], refs[n_in:n_in + n_out], refs[n_in + n_out:]
        rider.start(ins, outs, sems)
        rider.finish(ins, outs, sems)

    return list(pl.pallas_call(
        body, name=name, in_specs=[ANY] * n_in, out_specs=[ANY] * n_out, out_shape=rider.outs,
        scratch_shapes=rider.sems)(*rider.ins))


def _pair_share(halves):
    na = len(halves)

    def body(*refs):
        dst = refs[na:2 * na]
        send, recv = refs[2 * na:]
        x, y, c, _ = _place()
        cps = []
        for a in range(na):
            cp = pltpu.make_async_remote_copy(
                src_ref=dst[a].at[:, c], dst_ref=dst[a].at[:, c], send_sem=send.at[a], recv_sem=recv.at[a],
                device_id=(x, y, 1 - c), device_id_type=MESH)
            cp.start()
            cps.append(cp)
        for a in range(na):
            cps[a].wait_send()
            pltpu.make_async_remote_copy(
                src_ref=dst[a].at[:, 1 - c], dst_ref=dst[a].at[:, 1 - c], send_sem=send.at[a], recv_sem=recv.at[a],
                device_id=(x, y, 1 - c), device_id_type=MESH).wait_recv()

    return pl.pallas_call(
        body, name="pair_share", in_specs=[ANY] * na, out_specs=[ANY] * na,
        out_shape=[jax.ShapeDtypeStruct(h.shape, h.dtype) for h in halves],
        input_output_aliases={a: a for a in range(na)},
        scratch_shapes=[pltpu.SemaphoreType.DMA((na,))] * 2,
    )(*halves)


def _sum_rows(r):
    return r if r <= 352 else 256


def _pair_sum(name, part, got):
    _, _, r, w = part.shape
    rows = _sum_rows(r)
    c = lax.axis_index("c").astype(jnp.int32).reshape(1)

    def body(c_ref, p_ref, g_ref, o_ref):
        o_ref[...] = _mx(p_ref[...] + g_ref[...])

    spec = pl.BlockSpec((None, rows, w), lambda j, i, c_ref: (j, i, 0))
    return pl.pallas_call(
        body, name=name, out_shape=jax.ShapeDtypeStruct((N_SHARD, r, w), MXU_DTYPE),
        grid_spec=pltpu.PrefetchScalarGridSpec(
            num_scalar_prefetch=1, grid=(N_SHARD, r // rows),
            in_specs=[pl.BlockSpec((None, None, rows, w), lambda j, i, c_ref: (j, c_ref[0], i, 0)), spec],
            out_specs=spec),
        compiler_params=_cparams("parallel", "parallel"),
    )(c, part, got)


def _chip_sum(name, part, got, others, l, prev):
    _, _, r, w = part.shape
    rows = _sum_rows(r)
    cj = jnp.stack([lax.axis_index("c"), 2 * lax.axis_index("x") + lax.axis_index("y")]).astype(jnp.int32)

    def body(cj_ref, p_ref, g_ref, o_ref, *rest):
        acc = p_ref[...] + g_ref[...]
        for k in range(3):
            acc += o_ref[k].astype(F32)
        rest[-1][...] = acc

    ins, specs, alias = [cj, part, got, others], [], {}
    if prev is not None:
        ins.append(prev)
        specs.append(ANY)
        alias = {4: 0}
    return pl.pallas_call(
        body, name=name, out_shape=jax.ShapeDtypeStruct((2, 2, r, w), F32), input_output_aliases=alias,
        grid_spec=pltpu.PrefetchScalarGridSpec(
            num_scalar_prefetch=1, grid=(r // rows,),
            in_specs=[pl.BlockSpec((None, None, rows, w), lambda i, cj: (cj[1], cj[0], i, 0)),
                      pl.BlockSpec((None, rows, w), lambda i, cj: (cj[1], i, 0)),
                      pl.BlockSpec((3, rows, w), lambda i, cj: (0, i, 0))] + specs,
            out_specs=pl.BlockSpec((None, None, rows, w), lambda i, cj: (l, cj[0], i, 0))),
        compiler_params=_cparams("parallel"),
    )(*ins)


SMALL_ROWS = 40


def _sum_small(part):
    def body(p_ref, o_ref, land, send, recv):
        x, y, c, _ = _place()
        me = 4 * x + 2 * y + c
        cps = []
        for r in range(1, 8):
            cp = pltpu.make_async_remote_copy(
                src_ref=p_ref, dst_ref=land.at[r], send_sem=send.at[r], recv_sem=recv.at[r],
                device_id=(x ^ (r >> 2), y ^ ((r >> 1) & 1), c ^ (r & 1)), device_id_type=MESH)
            cp.start()
            cps.append(cp)
        land[0] = p_ref[...]
        for cp in cps:
            cp.wait()
        acc = land[me]
        for e in range(1, 8):
            acc += land[me ^ e]
        o_ref[...] = acc

    return pl.pallas_call(
        body, name="sum_small", in_specs=[pl.BlockSpec(memory_space=pltpu.VMEM)],
        out_specs=pl.BlockSpec(memory_space=pltpu.VMEM), out_shape=jax.ShapeDtypeStruct(part.shape, F32),
        scratch_shapes=[pltpu.VMEM((8,) + part.shape, F32), pltpu.SemaphoreType.DMA((8,)), pltpu.SemaphoreType.DMA((8,))],
    )(part)


BIG = ("ffn1_w_gu", "ffn1_w_down", "w_in", "w_out", "ffn2_w_gu", "ffn2_w_down")
SMALL = ("ln1_g", "ln1_b", "ln2_g", "ln2_b", "ln3_g", "ln3_b", "attn_sink", "cc_conv_b", "cc_ln_g", "cc_ln_b",
         "sc_conv_w", "cc_conv_w")
NAMES = ("ffn1_w_gu", "ffn1_w_down", "ln1_g", "ln1_b", "w_in", "sc_conv_w", "attn_sink", "cc_conv_w", "cc_conv_b",
         "cc_ln_g", "cc_ln_b", "w_out", "ln2_g", "ln2_b", "ffn2_w_gu", "ffn2_w_down", "ln3_g", "ln3_b")


def _rope_tables(t):
    half = HEAD_DIM // 2
    inv_freq = ROPE_THETA ** (-jnp.arange(half, dtype=F32) / half)
    ang = jnp.arange(t).astype(F32)[:, None] * inv_freq[None, :]
    cos, sin = jnp.cos(ang), jnp.sin(ang)
    return jnp.tile(jnp.concatenate([cos, cos], axis=1), (1, 2)), jnp.tile(jnp.concatenate([-sin, sin], axis=1), (1, 2))


def _pack_small(vals):
    flat = jnp.concatenate([vals[n].reshape(-1) for n in SMALL])
    return jnp.pad(flat, (0, SMALL_ROWS * D_MODEL - flat.shape[0])).reshape(SMALL_ROWS, D_MODEL)


def _unpack_small(packed, shapes):
    flat, out, at = packed.reshape(-1), {}, 0
    for n in SMALL:
        size = int(np.prod(shapes[n]))
        out[n] = flat[at:at + size].reshape(shapes[n])
        at += size
    return out


def kernel(x, ffn1_w_gu, ffn1_w_down, ln1_g, ln1_b, w_in, sc_conv_w, attn_sink, cc_conv_w, cc_conv_b, cc_ln_g, cc_ln_b, w_out, ln2_g, ln2_b, ffn2_w_gu, ffn2_w_down, ln3_g, ln3_b, loss_target, m_ffn1_w_gu, m_ffn1_w_down, m_ln1_g, m_ln1_b, m_w_in, m_sc_conv_w, m_attn_sink, m_cc_conv_w, m_cc_conv_b, m_cc_ln_g, m_cc_ln_b, m_w_out, m_ln2_g, m_ln2_b, m_ffn2_w_gu, m_ffn2_w_down, m_ln3_g, m_ln3_b, v_ffn1_w_gu, v_ffn1_w_down, v_ln1_g, v_ln1_b, v_w_in, v_sc_conv_w, v_attn_sink, v_cc_conv_w, v_cc_conv_b, v_cc_ln_g, v_cc_ln_b, v_w_out, v_ln2_g, v_ln2_b, v_ffn2_w_gu, v_ffn2_w_down, v_ln3_g, v_ln3_b):
    given = dict(locals())
    w = {n: given[n] for n in NAMES}
    mom = {n: given["m_" + n] for n in NAMES}
    var = {n: given["v_" + n] for n in NAMES}
    x0 = x[0]
    target = loss_target[0]
    t = x0.shape[0]
    chip = 2 * lax.axis_index("x") + lax.axis_index("y")

    conv_shard = jnp.pad(jnp.concatenate([sc_conv_w, cc_conv_w], axis=1), ((0, 0), (0, 14), (0, 64)))
    local = {n: _mx(w[n]) for n in BIG}
    local["conv"] = conv_shard
    full = [{}, {}]

    def gather(l, names):
        return _Gather([local[n][l].reshape(2, local[n].shape[1] // 2, local[n].shape[2]) for n in names])

    def land(l, names, arrays):
        for n, a in zip(names, arrays):
            full[l][n] = a.reshape(1, N_SHARD, 2 * a.shape[2], a.shape[3])

    def weights(l):
        f = full[l]
        conv = jnp.transpose(f["conv"][0, :, :SC_W + CC_W, :64], (1, 0, 2)).reshape(SC_W + CC_W, D_CONV)
        return dict(wgu1=f["ffn1_w_gu"], wd1=f["ffn1_w_down"].reshape(1, D_FF, D_MODEL), win=f["w_in"],
                    wout=f["w_out"].reshape(1, D_MODEL, D_MODEL), wgu2=f["ffn2_w_gu"],
                    wd2=f["ffn2_w_down"].reshape(1, D_FF, D_MODEL), sc=conv[:SC_W], cc=conv[SC_W:])

    first = ("ffn1_w_gu", "ffn1_w_down")
    mixer = ("w_in", "w_out", "conv")
    second = ("ffn2_w_gu", "ffn2_w_down")
    land(0, first, _run("gather_first", gather(0, first)))
    cos, sin = _rope_tables(t)

    def vec(a, l):
        return a[l][None, :]

    acts = []
    h = x0
    for l in range(2):
        ahead = (0, mixer + second) if l == 0 else (1, second)
        (y1, r1, gu1), got = _ffn_fwd("ffn_fwd_a%d" % l, h, full[l]["ffn1_w_gu"], full[l]["ffn1_w_down"].reshape(1, D_FF, D_MODEL),
                                      vec(ln1_g, l), vec(ln1_b, l), 0, gather(*ahead))
        land(*ahead, got)
        wl = weights(l)
        z = _in_proj(y1, wl["win"], 0)
        ysc, ycc, u2 = _conv_fwd(z, wl["sc"], wl["cc"], vec(cc_conv_b, l), vec(cc_ln_g, l), vec(cc_ln_b, l))
        qs, kf, vf = _attn_prep(z, cos, sin)
        o_nat, o, lse = _attn_fwd(qs, kf, vf, attn_sink[l])
        ycat = jnp.concatenate([ysc, o_nat, ycc], axis=1)
        y2, r2 = _out_proj(ycat, y1, wl["wout"], vec(ln2_g, l), vec(ln2_b, l), 0)
        ahead = (1, first + mixer) if l == 0 else None
        (y3, r3, gu2), got = _ffn_fwd("ffn_fwd_b%d" % l, y2, wl["wgu2"], wl["wd2"], vec(ln3_g, l), vec(ln3_b, l), 0,
                                      gather(*ahead) if ahead else None)
        if ahead:
            land(*ahead, got)
        acts.append(dict(x=h, y1=y1, r1=r1, gu1=gu1, z=z, u2=u2, qs=qs, kf=kf, vf=vf, o=o, lse=lse, ycat=ycat,
                         y2=y2, r2=r2, gu2=gu2, r3=r3, w=wl))
        h = y3
    loss_rows, res, do, dgb3_next = _loss_head(h, target, acts[1]["r3"], vec(ln3_g, 1))
    loss = lax.psum(loss_rows[0, 0], ("x", "y", "c"))

    upper = ("ffn2_w_gu", "ffn2_w_down", "w_out")
    lower = ("w_in", "ffn1_w_gu", "ffn1_w_down")
    part = [{}, {}]
    small = [None, None]
    stage = {}
    reduced = {n: None for n in BIG}
    row = pl.BlockSpec((TM, D_MODEL), lambda n, k: (k, 0))
    deep = pl.BlockSpec((TK, D_MODEL), lambda n, k: (k, 0))

    def halves(a, r):
        return a.reshape(N_SHARD, 2, r // 2, a.shape[-1])

    def pair_rider(l, names):
        return _PairExchange([part[l][n] for n in names])

    def after_pair(l, names, got):
        stage[l, names] = (got, [_pair_sum("pair_sum_%s_%d" % (n, l), part[l][n], g) for n, g in zip(names, got)])

    def chip_rider(l, names):
        return _ChipExchange(stage[l, names][1])

    def after_chip(l, names, others):
        for n, g, o in zip(names, stage[l, names][0], others):
            reduced[n] = _chip_sum("chip_sum_%s_%d" % (n, l), part[l][n], g, o, l, reduced[n])

    def ffn_weight_grads(which, l, xin, dh, a, do, rider_gu=None, make_rider_d=None):
        out, got_gu = _mm_tn(
            "%s_dwgu_%d" % (which, l), xin, dh, deep,
            pl.BlockSpec((None, TK, FF_CHUNK), lambda n, k: (n // N_CHUNK, k, n % N_CHUNK)),
            pl.BlockSpec((None, D_MODEL, FF_CHUNK), lambda n, k: (n, 0, 0)),
            (N_SHARD, D_MODEL, GU_SHARD), (D_MODEL, FF_CHUNK), (2 * N_CHUNK, t // TK), rider_gu)
        part[l][which + "_w_gu"] = halves(out, D_MODEL)
        rider_d = make_rider_d() if make_rider_d else None
        out, got_d = _mm_tn(
            "%s_dwd_%d" % (which, l), a, do, pl.BlockSpec((TK, FF_CHUNK), lambda n, k: (k, n)), deep,
            pl.BlockSpec((FF_CHUNK, D_MODEL), lambda n, k: (n, 0)),
            (D_FF, D_MODEL), (FF_CHUNK, D_MODEL), (N_CHUNK, t // TK), rider_d)
        part[l][which + "_w_down"] = halves(out, D_FF // N_SHARD)
        return got_gu, got_d

    w_in_only, w_gu_only, w_down_only = ("w_in",), ("ffn1_w_gu",), ("ffn1_w_down",)
    for l in (1, 0):
        s = acts[l]
        wl = s["w"]
        dgb3 = dgb3_next
        if l == 0:
            dy, dh, a, got = _ffn_bwd(res, do, s["gu2"], wl["wgu2"], wl["wd2"], 0, rider=pair_rider(1, lower))
            after_pair(1, lower, got)
            got, _ = ffn_weight_grads("ffn2", l, s["y2"], dh, a, do, chip_rider(1, lower))
            after_chip(1, lower, got)
        else:
            dy, dh, a, _ = _ffn_bwd(res, do, s["gu2"], wl["wgu2"], wl["wd2"], 0)
            ffn_weight_grads("ffn2", l, s["y2"], dh, a, do)
        res, dm, dycat, dgb2 = _out_proj_bwd(dy, s["r2"], wl["wout"], vec(ln2_g, l), 0)
        out, _ = _mm_tn("dwout_%d" % l, s["ycat"], dm, row, row, pl.BlockSpec((D_MODEL, D_MODEL), lambda n, k: (0, 0)),
                        (D_MODEL, D_MODEL), (D_MODEL, D_MODEL), (1, t // TM))
        part[l]["w_out"] = halves(out, OUT_SHARD)
        dz_sc, dz_cc, dconv = _conv_bwd(s["z"], dycat, s["u2"], wl["sc"], wl["cc"], vec(cc_ln_g, l), vec(cc_ln_b, l))
        dost, ld = _attn_dprep(dycat, s["o"], s["lse"])
        dqs, dkf, dvf, dsink = _attn_bwd(s["qs"], s["kf"], s["vf"], dost, ld, attn_sink[l])
        dz_att = _attn_prep_bwd(dqs, dkf, dvf, cos, sin)
        dz = jnp.concatenate([dz_sc, dz_att, dz_cc], axis=1)
        out, got = _mm_tn(
            "dwin_%d" % l, s["y1"], dz, row, pl.BlockSpec((TM, D_IN), lambda n, k: (k, 0)),
            pl.BlockSpec((N_SHARD, D_MODEL, IN_SHARD), lambda n, k: (0, 0, 0)),
            (N_SHARD, D_MODEL, IN_SHARD), (D_MODEL, D_IN), (1, t // TM), pair_rider(l, upper), split=N_SHARD)
        part[l]["w_in"] = halves(out, D_MODEL)
        after_pair(l, upper, got)
        res, do, dgb1 = _in_proj_bwd(dz, res, wl["win"], 0, s["r1"], vec(ln1_g, l))
        if l == 1:
            (res0, do0, dgb3_next), dh, a, _ = _ffn_bwd(res, do, s["gu1"], wl["wgu1"], wl["wd1"], 0,
                                                        tail=(acts[0]["r3"], vec(ln3_g, 0)))
            got, _ = ffn_weight_grads("ffn1", l, s["x"], dh, a, do, chip_rider(l, upper))
            after_chip(l, upper, got)
            res, do = res0, do0
        else:
            dy, dh, a, got = _ffn_bwd(res, do, s["gu1"], wl["wgu1"], wl["wd1"], 0, rider=pair_rider(0, w_in_only))
            after_pair(0, w_in_only, got)
            got, got_d = ffn_weight_grads("ffn1", l, s["x"], dh, a, do, _Both(chip_rider(0, upper), chip_rider(0, w_in_only)),
                                          lambda: pair_rider(0, w_gu_only))
            n_upper = len(upper)
            after_chip(0, upper, got[:n_upper])
            after_chip(0, w_in_only, got[n_upper:])
            after_pair(0, w_gu_only, got_d)
        small[l] = dict(ln1_g=dgb1[0], ln1_b=dgb1[1], ln2_g=dgb2[0], ln2_b=dgb2[1], ln3_g=dgb3[0], ln3_b=dgb3[1],
                        attn_sink=dsink[:, 0], cc_conv_b=dconv[ROW_CCB], cc_ln_g=dconv[ROW_CCG],
                        cc_ln_b=dconv[ROW_CCBETA], sc_conv_w=dconv[ROW_SCW:ROW_SCW + SC_W],
                        cc_conv_w=dconv[ROW_CCW:ROW_CCW + CC_W])
    grad_x = dy[None]

    after_pair(0, w_down_only, _run("pair_exchange_last", pair_rider(0, w_down_only)))
    got = _run("chip_exchange_last", _Both(chip_rider(0, w_gu_only), chip_rider(0, w_down_only)))
    after_chip(0, w_gu_only, got[:1])
    after_chip(0, w_down_only, got[1:])
    grads = dict(zip(BIG, _pair_share([reduced[n] for n in BIG])))
    for n in BIG:
        grads[n] = grads[n].reshape(w[n].shape)

    small_full = {n: jnp.stack([small[0][n], small[1][n]]) for n in SMALL}
    small_sum = _unpack_small(_sum_small(_pack_small(small_full)), {n: small_full[n].shape for n in SMALL})
    for n in SMALL:
        g = small_sum[n]
        if n in ("sc_conv_w", "cc_conv_w"):
            g = lax.dynamic_slice_in_dim(g, chip * 64, 64, axis=2)
        grads[n] = g

    delta, new_m, new_v = {}, {}, {}
    for n in BIG:
        shape = w[n].shape
        two_d = (shape[0] * shape[1], shape[2])
        outs = _adamw("adamw_" + n, w[n].reshape(two_d), grads[n].reshape(two_d), mom[n].reshape(two_d),
                      var[n].reshape(two_d), 128)
        delta[n], new_m[n], new_v[n] = [a.reshape(shape) for a in outs]
    shapes = {n: w[n].shape for n in SMALL}
    outs = _adamw("adamw_small", _pack_small({n: w[n] for n in SMALL}), _pack_small({n: grads[n] for n in SMALL}),
                  _pack_small({n: mom[n] for n in SMALL}), _pack_small({n: var[n] for n in SMALL}), 8)
    for d, packed in zip((delta, new_m, new_v), outs):
        d.update(_unpack_small(packed, shapes))

    return (loss, grad_x, *[grads[n] for n in NAMES], *[delta[n] for n in NAMES], *[new_m[n] for n in NAMES],
            *[new_v[n] for n in NAMES])
```

```python
import functools

import numpy as np
import jax
import jax.numpy as jnp
from jax import lax
from jax.experimental import pallas as pl
from jax.experimental.pallas import tpu as pltpu

F32 = jnp.float32
MXU_DTYPE = jnp.bfloat16

D_MODEL = 1024
D_FF = 2816
N_SHARD = 4
D_IN = 2048
GU_SHARD = 2 * D_FF // N_SHARD
FF_CHUNK = GU_SHARD
N_CHUNK = D_FF // FF_CHUNK
IN_SHARD = D_IN // N_SHARD
OUT_SHARD = D_MODEL // N_SHARD
HEAD_DIM = 64
N_Q_HEADS = 8
BLOCK = 128
SC_W = 3
CC_W = 31
D_CONV = 256
HALO = 16
LN_EPS = 1e-5
ALPHA = (2.0 * 2) ** 0.25
NEG = -1e30
ROPE_THETA = 10000.0
ADAM_LR, ADAM_B1, ADAM_B2, ADAM_EPS, ADAM_WD, ADAM_STEP = 0.001, 0.9, 0.999, 1e-08, 0.01, 10

TM = 512
TK = 1024
TMC = 256
VMEM_LIMIT = 56 * 1024 * 1024
MESH = pl.DeviceIdType.MESH
ANY = pl.BlockSpec(memory_space=pl.ANY)


def _cparams(*sem):
    return pltpu.CompilerParams(dimension_semantics=sem, vmem_limit_bytes=VMEM_LIMIT)


def _dot(a, b):
    return jnp.dot(a, b, preferred_element_type=F32)


def _dot_nt(a, b):
    return lax.dot_general(a, b, (((1,), (1,)), ((), ())), preferred_element_type=F32)


def _dot_tn(a, b):
    return lax.dot_general(a, b, (((0,), (0,)), ((), ())), preferred_element_type=F32)


def _mx(a):
    return a.astype(MXU_DTYPE)


def _mean(a):
    return jnp.mean(a, axis=-1, keepdims=True)


def _ln_stats(r):
    xc = r - _mean(r)
    rstd = lax.rsqrt(_mean(xc * xc) + LN_EPS)
    return xc * rstd, rstd


def _ln_bwd(dy, xh, rstd, gamma):
    dxh = dy * gamma
    return rstd * (dxh - _mean(dxh) - xh * _mean(dxh * xh))


def _colsum(a):
    return jnp.sum(a, axis=0, keepdims=True)


def _sigmoid(a):
    return 1.0 / (1.0 + jnp.exp(-a))


def _call(body, args, *, name, grid, in_specs, out_specs, out_shape, scratch, sem, rider=None):
    if rider is None:
        outs = pl.pallas_call(
            body, name=name, grid=grid, in_specs=in_specs, out_specs=out_specs, out_shape=out_shape,
            scratch_shapes=scratch, compiler_params=_cparams(*sem))(*args)
        return list(outs), []
    n_in, n_out, n_sc = len(in_specs), len(out_specs), len(scratch)
    r_in, r_out = len(rider.ins), len(rider.outs)

    def carrying(*refs):
        cuts = np.cumsum([0, n_in, r_in, n_out, r_out, n_sc])
        ins, rins, outs, routs, scr = [refs[a:b] for a, b in zip(cuts[:-1], cuts[1:])]
        rsems = refs[cuts[-1]:]
        first = functools.reduce(jnp.logical_and, [pl.program_id(d) == 0 for d in range(len(grid))])
        last = functools.reduce(jnp.logical_and, [pl.program_id(d) == grid[d] - 1 for d in range(len(grid))])

        @pl.when(first)
        def _():
            rider.start(rins, routs, rsems)

        body(*ins, *outs, *scr)

        @pl.when(last)
        def _():
            rider.finish(rins, routs, rsems)

    outs = pl.pallas_call(
        carrying, name=name, grid=grid, in_specs=list(in_specs) + [ANY] * r_in,
        out_specs=list(out_specs) + [ANY] * r_out, out_shape=list(out_shape) + list(rider.outs),
        scratch_shapes=list(scratch) + list(rider.sems), compiler_params=_cparams(*(("arbitrary",) * len(grid))),
    )(*args, *rider.ins)
    return list(outs[:n_out]), list(outs[n_out:])


def _ffn_fwd(name, x, wgu, wd, gamma, beta, l, rider=None):
    t = x.shape[0]
    nc = N_CHUNK

    def body(x_ref, wg_ref, wu_ref, wd_ref, g_ref, b_ref, y_ref, r_ref, gu_ref, xb_s, acc_s):
        c = pl.program_id(1)

        @pl.when(c == 0)
        def _():
            xb_s[...] = _mx(x_ref[...])
            acc_s[...] = jnp.zeros_like(acc_s)

        xb = xb_s[...]
        hg = _dot(xb, wg_ref[...])
        hu = _dot(xb, wu_ref[...])
        gu_ref[0] = _mx(hg)
        gu_ref[1] = _mx(hu)
        a = (hg * _sigmoid(hg)) * hu
        acc_s[...] += _dot(_mx(a), wd_ref[...])

        @pl.when(c == nc - 1)
        def _():
            r = ALPHA * x_ref[...] + 0.5 * acc_s[...]
            xh, _ = _ln_stats(r)
            r_ref[...] = r
            y_ref[...] = xh * g_ref[...] + b_ref[...]

    row = pl.BlockSpec((TM, D_MODEL), lambda i, c: (i, 0))
    vec = pl.BlockSpec((1, D_MODEL), lambda i, c: (0, 0))
    return _call(
        body, (x, wgu, wgu, wd, gamma, beta), name=name, grid=(t // TM, nc),
        in_specs=[row,
                  pl.BlockSpec((None, None, D_MODEL, FF_CHUNK), lambda i, c: (l, c, 0, 0)),
                  pl.BlockSpec((None, None, D_MODEL, FF_CHUNK), lambda i, c: (l, N_CHUNK + c, 0, 0)),
                  pl.BlockSpec((None, FF_CHUNK, D_MODEL), lambda i, c: (l, c, 0)),
                  vec, vec],
        out_specs=[row, row, pl.BlockSpec((2, TM, FF_CHUNK), lambda i, c: (0, i, c))],
        out_shape=[jax.ShapeDtypeStruct((t, D_MODEL), F32), jax.ShapeDtypeStruct((t, D_MODEL), F32),
                   jax.ShapeDtypeStruct((2, t, D_FF), MXU_DTYPE)],
        scratch=[pltpu.VMEM((TM, D_MODEL), MXU_DTYPE), pltpu.VMEM((TM, D_MODEL), F32)],
        sem=("parallel", "arbitrary"), rider=rider)


def _norm_bwd_tail(dy, r_ref, g_ref, res_ref, do_ref, dgb_ref):
    @pl.when(pl.program_id(0) == 0)
    def _():
        dgb_ref[...] = jnp.zeros_like(dgb_ref)

    xh, rstd = _ln_stats(r_ref[...])
    dr = _ln_bwd(dy, xh, rstd, g_ref[...])
    do_ref[...] = _mx(0.5 * dr)
    res_ref[...] = ALPHA * dr
    dgb_ref[0:1, :] += _colsum(dy * xh)
    dgb_ref[1:2, :] += _colsum(dy)


def _norm_tail_specs(t):
    row = pl.BlockSpec((TM, D_MODEL), lambda i: (i, 0))
    return ([row, pl.BlockSpec((1, D_MODEL), lambda i: (0, 0))],
            [row, row, pl.BlockSpec((8, D_MODEL), lambda i: (0, 0))],
            [jax.ShapeDtypeStruct((t, D_MODEL), F32), jax.ShapeDtypeStruct((t, D_MODEL), MXU_DTYPE),
             jax.ShapeDtypeStruct((8, D_MODEL), F32)])


def _ffn_bwd(res, do, gu, wgu, wd, l, tail=None, rider=None):
    t = res.shape[0]
    nc = N_CHUNK
    row1 = pl.BlockSpec((TM, D_MODEL), lambda i: (i, 0))

    def hidden_body(do_ref, gu_ref, wd_ref, dh_ref, a_ref):
        da = _dot_nt(do_ref[...], wd_ref[...])
        g = gu_ref[0].astype(F32)
        u = gu_ref[1].astype(F32)
        s = _sigmoid(g)
        sil = g * s
        a_ref[...] = _mx(sil * u)
        dh_ref[0] = _mx(da * u * (s * (1.0 + g * (1.0 - s))))
        dh_ref[1] = _mx(da * sil)

    hid = pl.BlockSpec((2, TM, FF_CHUNK), lambda c, i: (0, i, c))
    (dh, a), got = _call(
        hidden_body, (do, gu, wd), name="ffn_bwd_hidden" if rider is None else "ffn_bwd_hidden_carry", grid=(nc, t // TM),
        in_specs=[pl.BlockSpec((TM, D_MODEL), lambda c, i: (i, 0)), hid,
                  pl.BlockSpec((None, FF_CHUNK, D_MODEL), lambda c, i: (l, c, 0))],
        out_specs=[hid, pl.BlockSpec((TM, FF_CHUNK), lambda c, i: (i, c))],
        out_shape=[jax.ShapeDtypeStruct((2, t, D_FF), MXU_DTYPE), jax.ShapeDtypeStruct((t, D_FF), MXU_DTYPE)],
        scratch=[], sem=("parallel", "parallel"), rider=rider)

    def input_body(res_ref, dh_ref, w_ref, *rest):
        acc = res_ref[...]
        for j in range(N_SHARD):
            part = dh_ref[j // N_CHUNK][:, (j % N_CHUNK) * FF_CHUNK:(j % N_CHUNK + 1) * FF_CHUNK]
            acc += _dot_nt(part, w_ref[j])
        if tail is None:
            rest[0][...] = acc
        else:
            _norm_bwd_tail(acc, *rest)

    in_specs = [row1, pl.BlockSpec((2, TM, D_FF), lambda i: (0, i, 0)),
                pl.BlockSpec((None, N_SHARD, D_MODEL, GU_SHARD), lambda i: (l, 0, 0, 0))]
    if tail is None:
        dx = pl.pallas_call(
            input_body, name="ffn_bwd_input", grid=(t // TM,), in_specs=in_specs, out_specs=row1,
            out_shape=jax.ShapeDtypeStruct((t, D_MODEL), F32), compiler_params=_cparams("parallel"),
        )(res, dh, wgu)
    else:
        tail_in, tail_out, tail_shape = _norm_tail_specs(t)
        dx = pl.pallas_call(
            input_body, name="ffn_bwd_input_norm", grid=(t // TM,), in_specs=in_specs + tail_in, out_specs=tail_out,
            out_shape=tail_shape, compiler_params=_cparams("arbitrary"),
        )(res, dh, wgu, *tail)
    return dx, dh, a, got


def _mm_tn(name, a, b, a_spec, b_spec, out_spec, out_shape, acc_shape, grid, rider=None, split=1):
    nk = grid[-1]
    width = acc_shape[1] // split

    def body(*refs):
        a_ref, b_ref = refs[0], refs[1]
        o_ref, acc = refs[-2], refs[-1]
        k = pl.program_id(len(grid) - 1)

        @pl.when(k == 0)
        def _():
            acc[...] = jnp.zeros_like(acc)

        acc[...] += _dot_tn(_mx(a_ref[...]), _mx(b_ref[...]))

        @pl.when(k == nk - 1)
        def _():
            if split == 1:
                o_ref[...] = acc[...]
            else:
                for j in range(split):
                    o_ref[j] = acc[:, j * width:(j + 1) * width]

    sem = ("parallel",) * (len(grid) - 1) + ("arbitrary",)
    (out,), got = _call(
        body, (a, b), name=name, grid=grid, in_specs=[a_spec, b_spec], out_specs=[out_spec],
        out_shape=[jax.ShapeDtypeStruct(out_shape, F32)], scratch=[pltpu.VMEM(acc_shape, F32)], sem=sem, rider=rider)
    return out, got


def _in_proj(x, w_in, l):
    t = x.shape[0]

    def body(x_ref, w_ref, z_ref):
        xb = _mx(x_ref[...])
        for j in range(N_SHARD):
            z_ref[:, j * IN_SHARD:(j + 1) * IN_SHARD] = _dot(xb, w_ref[j])

    return pl.pallas_call(
        body, name="in_proj", grid=(t // TM,),
        in_specs=[pl.BlockSpec((TM, D_MODEL), lambda i: (i, 0)),
                  pl.BlockSpec((None, N_SHARD, D_MODEL, IN_SHARD), lambda i: (l, 0, 0, 0))],
        out_specs=pl.BlockSpec((TM, D_IN), lambda i: (i, 0)),
        out_shape=jax.ShapeDtypeStruct((t, D_IN), F32),
        compiler_params=_cparams("parallel"),
    )(x, w_in)


def _in_proj_bwd(dz, dx_res, w_in, l, r, gamma):
    t = dz.shape[0]

    def body(dz_ref, res_ref, w_ref, *tail):
        acc = res_ref[...]
        for j in range(N_SHARD):
            acc += _dot_nt(dz_ref[:, j * IN_SHARD:(j + 1) * IN_SHARD], w_ref[j])
        _norm_bwd_tail(acc, *tail)

    row = pl.BlockSpec((TM, D_MODEL), lambda i: (i, 0))
    tail_in, tail_out, tail_shape = _norm_tail_specs(t)
    return pl.pallas_call(
        body, name="in_proj_bwd", grid=(t // TM,),
        in_specs=[pl.BlockSpec((TM, D_IN), lambda i: (i, 0)), row,
                  pl.BlockSpec((None, N_SHARD, D_MODEL, IN_SHARD), lambda i: (l, 0, 0, 0))] + tail_in,
        out_specs=tail_out, out_shape=tail_shape, compiler_params=_cparams("arbitrary"),
    )(dz, dx_res, w_in, r, gamma)


def _out_proj(ycat, x, w_out, gamma, beta, l):
    t = x.shape[0]

    def body(yc_ref, x_ref, w_ref, g_ref, b_ref, y_ref, r_ref):
        r = ALPHA * x_ref[...] + _dot(yc_ref[...], w_ref[...])
        xh, _ = _ln_stats(r)
        r_ref[...] = r
        y_ref[...] = xh * g_ref[...] + b_ref[...]

    row = pl.BlockSpec((TM, D_MODEL), lambda i: (i, 0))
    vec = pl.BlockSpec((1, D_MODEL), lambda i: (0, 0))
    return pl.pallas_call(
        body, name="out_proj", grid=(t // TM,),
        in_specs=[row, row, pl.BlockSpec((None, D_MODEL, D_MODEL), lambda i: (l, 0, 0)), vec, vec],
        out_specs=[row, row],
        out_shape=[jax.ShapeDtypeStruct((t, D_MODEL), F32)] * 2,
        compiler_params=_cparams("parallel"),
    )(ycat, x, w_out, gamma, beta)


def _out_proj_bwd(dy, r, w_out, gamma, l):
    t = dy.shape[0]

    def body(dy_ref, r_ref, w_ref, g_ref, res_ref, dm_ref, dyc_ref, dgb_ref):
        @pl.when(pl.program_id(0) == 0)
        def _():
            dgb_ref[...] = jnp.zeros_like(dgb_ref)

        xh, rstd = _ln_stats(r_ref[...])
        dy = dy_ref[...]
        dr = _ln_bwd(dy, xh, rstd, g_ref[...])
        res_ref[...] = ALPHA * dr
        dm = _mx(dr)
        dm_ref[...] = dm
        dyc_ref[...] = _dot_nt(dm, w_ref[...])
        dgb_ref[0:1, :] += _colsum(dy * xh)
        dgb_ref[1:2, :] += _colsum(dy)

    row = pl.BlockSpec((TM, D_MODEL), lambda i: (i, 0))
    return pl.pallas_call(
        body, name="out_proj_bwd", grid=(t // TM,),
        in_specs=[row, row, pl.BlockSpec((None, D_MODEL, D_MODEL), lambda i: (l, 0, 0)),
                  pl.BlockSpec((1, D_MODEL), lambda i: (0, 0))],
        out_specs=[row, row, row, pl.BlockSpec((8, D_MODEL), lambda i: (0, 0))],
        out_shape=[jax.ShapeDtypeStruct((t, D_MODEL), F32), jax.ShapeDtypeStruct((t, D_MODEL), MXU_DTYPE),
                   jax.ShapeDtypeStruct((t, D_MODEL), F32), jax.ShapeDtypeStruct((8, D_MODEL), F32)],
        compiler_params=_cparams("arbitrary"),
    )(dy, r, w_out, gamma)


def _halo_specs(t, width, col):
    per = TMC // HALO
    last = t // HALO - 1
    return [pl.BlockSpec((HALO, width), lambda i: (jnp.maximum(i * per - 1, 0), col)),
            pl.BlockSpec((TMC, width), lambda i: (i, col)),
            pl.BlockSpec((HALO, width), lambda i: (jnp.minimum((i + 1) * per, last), col))]


def _extend(refs, i, nt):
    p_ref, c_ref, n_ref = refs
    p = jnp.where(i > 0, p_ref[...].astype(F32), 0.0)
    n = jnp.where(i < nt - 1, n_ref[...].astype(F32), 0.0)
    return jnp.concatenate([p, c_ref[...].astype(F32), n], axis=0)


def _shifted_copies(src_s, dst8_s):
    n = src_s.shape[0] - 8
    for b in range(8):
        dst8_s[b, 0:n, :] = src_s[pl.ds(b, n), :]


def _window(dst8_s, start):
    return dst8_s[start % 8, pl.ds(start - start % 8, TMC), :]


def _conv_fwd(z, sc_w, cc_w, cc_cb, cc_g, cc_b):
    t = z.shape[0]
    nt = t // TMC

    def body(*refs):
        b_ref = refs[0]
        c3, h3, a3, g3 = refs[1:4], refs[4:7], refs[7:10], refs[10:13]
        scw_ref, ccw_ref, cb_ref, lg_ref, lb_ref = refs[13:18]
        ysc_ref, ycc_ref, u2_ref, e_s, e8_s = refs[18:23]
        i = pl.program_id(0)
        e_s[...] = _extend(c3, i, nt) * _extend(h3, i, nt)
        cv = jnp.zeros((TMC, D_CONV), F32)
        for k in range(SC_W):
            cv += scw_ref[k:k + 1, :] * e_s[pl.ds(HALO + k - 1, TMC), :]
        ysc_ref[...] = _mx(b_ref[...] * cv)
        e_s[...] = _extend(a3, i, nt) * _sigmoid(_extend(g3, i, nt))
        _shifted_copies(e_s, e8_s)
        u2 = jnp.zeros((TMC, D_CONV), F32) + cb_ref[...]
        for k in range(CC_W):
            u2 += ccw_ref[k:k + 1, :] * _window(e8_s, HALO + k - 15)
        u2_ref[...] = u2
        xh, _ = _ln_stats(u2)
        n = xh * lg_ref[...] + lb_ref[...]
        ycc_ref[...] = _mx(n * _sigmoid(n))

    tile = pl.BlockSpec((TMC, D_CONV), lambda i: (i, 0))
    vec = pl.BlockSpec((1, D_CONV), lambda i: (0, 0))
    in_specs = ([pl.BlockSpec((TMC, D_CONV), lambda i: (i, 0))] + _halo_specs(t, D_CONV, 1) + _halo_specs(t, D_CONV, 2)
                + _halo_specs(t, D_CONV, 6) + _halo_specs(t, D_CONV, 7)
                + [pl.BlockSpec((SC_W, D_CONV), lambda i: (0, 0)), pl.BlockSpec((CC_W, D_CONV), lambda i: (0, 0)),
                   vec, vec, vec])
    return pl.pallas_call(
        body, name="conv_fwd", grid=(nt,), in_specs=in_specs, out_specs=[tile, tile, tile],
        out_shape=[jax.ShapeDtypeStruct((t, D_CONV), MXU_DTYPE), jax.ShapeDtypeStruct((t, D_CONV), MXU_DTYPE),
                   jax.ShapeDtypeStruct((t, D_CONV), F32)],
        scratch_shapes=[pltpu.VMEM((TMC + 2 * HALO, D_CONV), F32), pltpu.VMEM((8, TMC + 2 * HALO, D_CONV), F32)],
        compiler_params=_cparams("parallel"),
    )(*([z] * 13), sc_w, cc_w, cc_cb, cc_g, cc_b)


ROW_CCW, ROW_CCB, ROW_CCG, ROW_CCBETA, ROW_SCW, CONV_ROWS = 0, 31, 32, 33, 34, 40


def _conv_bwd(z, dycat, u2, sc_w, cc_w, cc_g, cc_b):
    t = z.shape[0]
    nt = t // TMC

    def body(*refs):
        b3, c3, h3, a3, g3 = refs[0:3], refs[3:6], refs[6:9], refs[9:12], refs[12:15]
        dys3, dyc3, u3 = refs[15:18], refs[18:21], refs[21:24]
        scw_ref, ccw_ref, lg_ref, lb_ref = refs[24:28]
        dsc_ref, dcc_ref, sm_ref, e_s, f_s, e8_s, f8_s = refs[28:35]
        i = pl.program_id(0)

        @pl.when(i == 0)
        def _():
            sm_ref[...] = jnp.zeros_like(sm_ref)

        cur = pl.ds(HALO, TMC)
        e_s[...] = _extend(c3, i, nt) * _extend(h3, i, nt)
        f_s[...] = _extend(dys3, i, nt) * _extend(b3, i, nt)
        cv = jnp.zeros((TMC, D_CONV), F32)
        dp = jnp.zeros((TMC, D_CONV), F32)
        dcv = f_s[cur, :]
        for k in range(SC_W):
            win = e_s[pl.ds(HALO + k - 1, TMC), :]
            cv += scw_ref[k:k + 1, :] * win
            dp += scw_ref[k:k + 1, :] * f_s[pl.ds(HALO - k + 1, TMC), :]
            sm_ref[ROW_SCW + k:ROW_SCW + k + 1, :] += _colsum(dcv * win)
        dsc_ref[:, 0:D_CONV] = _mx(dys3[1][...] * cv)
        dsc_ref[:, D_CONV:2 * D_CONV] = _mx(dp * h3[1][...])
        dsc_ref[:, 2 * D_CONV:3 * D_CONV] = _mx(dp * c3[1][...])
        xh, rstd = _ln_stats(_extend(u3, i, nt))
        n = xh * lg_ref[...] + lb_ref[...]
        sg = _sigmoid(n)
        dn = _extend(dyc3, i, nt) * (sg * (1.0 + n * (1.0 - sg)))
        f_s[...] = _ln_bwd(dn, xh, rstd, lg_ref[...])
        sm_ref[ROW_CCG:ROW_CCG + 1, :] += _colsum((dn * xh)[HALO:HALO + TMC])
        sm_ref[ROW_CCBETA:ROW_CCBETA + 1, :] += _colsum(dn[HALO:HALO + TMC])
        sig_g = _sigmoid(_extend(g3, i, nt))
        e_s[...] = _extend(a3, i, nt) * sig_g
        _shifted_copies(e_s, e8_s)
        _shifted_copies(f_s, f8_s)
        du2 = f_s[cur, :]
        sm_ref[ROW_CCB:ROW_CCB + 1, :] += _colsum(du2)
        duu = jnp.zeros((TMC, D_CONV), F32)
        for k in range(CC_W):
            duu += ccw_ref[k:k + 1, :] * _window(f8_s, HALO + 15 - k)
            sm_ref[ROW_CCW + k:ROW_CCW + k + 1, :] += _colsum(du2 * _window(e8_s, HALO + k - 15))
        sgc = sig_g[HALO:HALO + TMC]
        dcc_ref[:, 0:D_CONV] = _mx(duu * sgc)
        dcc_ref[:, D_CONV:2 * D_CONV] = _mx(duu * a3[1][...] * sgc * (1.0 - sgc))

    vec = pl.BlockSpec((1, D_CONV), lambda i: (0, 0))
    in_specs = []
    for col in (0, 1, 2, 6, 7):
        in_specs += _halo_specs(t, D_CONV, col)
    in_specs += _halo_specs(t, D_CONV, 0) + _halo_specs(t, D_CONV, 3) + _halo_specs(t, D_CONV, 0)
    in_specs += [pl.BlockSpec((SC_W, D_CONV), lambda i: (0, 0)), pl.BlockSpec((CC_W, D_CONV), lambda i: (0, 0)), vec, vec]
    return pl.pallas_call(
        body, name="conv_bwd", grid=(nt,), in_specs=in_specs,
        out_specs=[pl.BlockSpec((TMC, 3 * D_CONV), lambda i: (i, 0)), pl.BlockSpec((TMC, 2 * D_CONV), lambda i: (i, 0)),
                   pl.BlockSpec((CONV_ROWS, D_CONV), lambda i: (0, 0))],
        out_shape=[jax.ShapeDtypeStruct((t, 3 * D_CONV), MXU_DTYPE), jax.ShapeDtypeStruct((t, 2 * D_CONV), MXU_DTYPE),
                   jax.ShapeDtypeStruct((CONV_ROWS, D_CONV), F32)],
        scratch_shapes=[pltpu.VMEM((TMC + 2 * HALO, D_CONV), F32)] * 2
        + [pltpu.VMEM((8, TMC + 2 * HALO, D_CONV), F32)] * 2,
        compiler_params=_cparams("arbitrary"),
    )(*([z] * 15), *([dycat] * 6), *([u2] * 3), sc_w, cc_w, cc_g, cc_b)


def _lane(shape):
    return lax.broadcasted_iota(jnp.int32, shape, 1)


def _swap_halves(x):
    w = x.shape[1]
    lo = (_lane(x.shape) % HEAD_DIM) < HEAD_DIM // 2
    return jnp.where(lo, pltpu.roll(x, w - HEAD_DIM // 2, 1), pltpu.roll(x, HEAD_DIM // 2, 1))


def _half(shape, g):
    lane = _lane(shape)
    return lane < HEAD_DIM if g == 0 else lane >= HEAD_DIM


GROUP_ROWS = 4 * BLOCK


def _stack_heads(tiles, out_ref, nblk):
    for tt in range(4):
        g = tt // 2
        for slot in range(2):
            s = 2 * (tt % 2) + slot
            piece = tiles[tt] if slot == g else pltpu.roll(tiles[tt], HEAD_DIM, 1)
            piece = jnp.where(_half(piece.shape, g), piece, 0.0).astype(out_ref.dtype)
            for b in range(nblk):
                at = GROUP_ROWS * b + BLOCK * s
                out_ref[g, at:at + BLOCK, :] = piece[BLOCK * b:BLOCK * (b + 1)]


def _unstack_heads(ref, nblk):
    tiles = []
    for tt in range(4):
        g = tt // 2
        tile = None
        for slot in range(2):
            s = 2 * (tt % 2) + slot
            rows = [ref[g, GROUP_ROWS * b + BLOCK * s:GROUP_ROWS * b + BLOCK * (s + 1), :] for b in range(nblk)]
            piece = rows[0] if nblk == 1 else jnp.concatenate(rows, axis=0)
            if slot != g:
                piece = pltpu.roll(piece, HEAD_DIM, 1)
            tile = piece if tile is None else tile + piece
        tiles.append(tile)
    return tiles


def _attn_prep(z, cos, sin):
    t = z.shape[0]
    nblk = TM // BLOCK

    def body(qa_ref, qb_ref, k_ref, v_ref, cos_ref, sin_ref, qst_ref, kr_ref, vb_ref):
        cs, sn = cos_ref[...], sin_ref[...]

        def rope(x):
            return x * cs + _swap_halves(x) * sn

        tiles = []
        for tt in range(4):
            src = qa_ref if tt < 2 else qb_ref
            tiles.append(rope(src[:, (tt % 2) * BLOCK:(tt % 2 + 1) * BLOCK]) * (HEAD_DIM ** -0.5))
        _stack_heads(tiles, qst_ref, nblk)
        kr_ref[...] = _mx(rope(k_ref[...]))
        vb_ref[...] = _mx(v_ref[...])

    def col(width, j):
        return pl.BlockSpec((TM, width), lambda i: (i, j))

    return pl.pallas_call(
        body, name="attn_prep", grid=(t // TM,),
        in_specs=[col(256, 3), col(256, 4), col(128, 10), col(128, 11), col(128, 0), col(128, 0)],
        out_specs=[pl.BlockSpec((2, 4 * TM, BLOCK), lambda i: (0, i, 0)), col(128, 0), col(128, 0)],
        out_shape=[jax.ShapeDtypeStruct((2, 4 * t, BLOCK), MXU_DTYPE), jax.ShapeDtypeStruct((t, BLOCK), MXU_DTYPE),
                   jax.ShapeDtypeStruct((t, BLOCK), MXU_DTYPE)],
        compiler_params=_cparams("parallel"),
    )(z, z, z, z, cos, sin)


def _attn_dprep(dycat, ost, lst):
    t = dycat.shape[0]
    nblk = TM // BLOCK

    def body(da_ref, db_ref, o_ref, l_ref, dost_ref, ld_ref, st_s):
        tiles = []
        for tt in range(4):
            src = da_ref if tt < 2 else db_ref
            tiles.append(src[:, (tt % 2) * BLOCK:(tt % 2 + 1) * BLOCK])
        _stack_heads(tiles, st_s, nblk)
        for g in range(2):
            do = st_s[g]
            dost_ref[g] = _mx(do)
            dsum = jnp.sum(do * o_ref[g], axis=-1, keepdims=True)
            ld_ref[g] = jnp.where(_lane(do.shape) < HEAD_DIM, l_ref[g], dsum)

    stacked = pl.BlockSpec((2, 4 * TM, BLOCK), lambda i: (0, i, 0))
    return pl.pallas_call(
        body, name="attn_dprep", grid=(t // TM,),
        in_specs=[pl.BlockSpec((TM, 256), lambda i: (i, 1)), pl.BlockSpec((TM, 256), lambda i: (i, 2)), stacked, stacked],
        out_specs=[stacked, stacked],
        out_shape=[jax.ShapeDtypeStruct((2, 4 * t, BLOCK), MXU_DTYPE), jax.ShapeDtypeStruct((2, 4 * t, BLOCK), F32)],
        scratch_shapes=[pltpu.VMEM((2, 4 * TM, BLOCK), F32)],
        compiler_params=_cparams("parallel"),
    )(dycat, dycat, ost, lst)


def _attn_prep_bwd(dqst, dk, dv, cos, sin):
    t = dk.shape[0]
    nblk = TM // BLOCK

    def body(dq_ref, dk_ref, dv_ref, cos_ref, sin_ref, dz_ref):
        cs, sn = cos_ref[...], sin_ref[...]

        def rope_bwd(d):
            return d * cs + _swap_halves(d * sn)

        for tt, tile in enumerate(_unstack_heads(dq_ref, nblk)):
            dz_ref[:, tt * BLOCK:(tt + 1) * BLOCK] = _mx(rope_bwd(tile * (HEAD_DIM ** -0.5)))
        dz_ref[:, 4 * BLOCK:5 * BLOCK] = _mx(rope_bwd(dk_ref[...]))
        dz_ref[:, 5 * BLOCK:6 * BLOCK] = _mx(dv_ref[...])

    def col(width):
        return pl.BlockSpec((TM, width), lambda i: (i, 0))

    return pl.pallas_call(
        body, name="attn_prep_bwd", grid=(t // TM,),
        in_specs=[pl.BlockSpec((2, 4 * TM, BLOCK), lambda i: (0, i, 0)), col(128), col(128), col(128), col(128)],
        out_specs=col(768), out_shape=jax.ShapeDtypeStruct((t, 768), MXU_DTYPE),
        compiler_params=_cparams("parallel"),
    )(dqst, dk, dv, cos, sin)


def _nbr_specs(nb, width, col):
    return [pl.BlockSpec((BLOCK, width), lambda n: (jnp.maximum(n - 1, 0), col)),
            pl.BlockSpec((BLOCK, width), lambda n: (n, col)),
            pl.BlockSpec((BLOCK, width), lambda n: (jnp.minimum(n + 1, nb - 1), col))]


def _query_index():
    row = lax.broadcasted_iota(jnp.int32, (GROUP_ROWS, BLOCK), 0)
    return row & (BLOCK - 1), lax.broadcasted_iota(jnp.int32, (GROUP_ROWS, BLOCK), 1)


def _sink_column(sink_ref, g):
    band = lax.broadcasted_iota(jnp.int32, (GROUP_ROWS, 1), 0) // BLOCK
    col = jnp.zeros((GROUP_ROWS, 1), F32) + sink_ref[4 * g]
    for s in range(1, 4):
        col = jnp.where(band == s, sink_ref[4 * g + s], col)
    return col


def _attn_fwd(qst, kr, vb, sink):
    t = kr.shape[0]
    nb = t // BLOCK

    def body(q_ref, kp_ref, kc_ref, kn_ref, vp_ref, vc_ref, vn_ref, sink_ref, o_ref, ost_ref, lst_ref):
        n = pl.program_id(0)
        qi, kj = _query_index()
        m_prev, m_next = (kj >= qi) & (n > 0), (kj <= qi) & (n < nb - 1)
        nat = [None] * 4
        for g in range(2):
            q = q_ref[g]
            sp = jnp.where(m_prev, _dot_nt(q, kp_ref[...]), NEG)
            sc = _dot_nt(q, kc_ref[...])
            sn = jnp.where(m_next, _dot_nt(q, kn_ref[...]), NEG)
            sk = _sink_column(sink_ref, g)
            m = jnp.maximum(jnp.max(jnp.maximum(jnp.maximum(sp, sc), sn), axis=-1, keepdims=True), sk)
            pp, pc, pn = jnp.exp(sp - m), jnp.exp(sc - m), jnp.exp(sn - m)
            den = jnp.sum(pp + pc + pn, axis=-1, keepdims=True) + jnp.exp(sk - m)
            o = (_dot(_mx(pp), vp_ref[...]) + _dot(_mx(pc), vc_ref[...]) + _dot(_mx(pn), vn_ref[...])) / den
            o = jnp.where(_half(o.shape, g), o, 0.0)
            ost_ref[g] = o
            lst_ref[g] = jnp.broadcast_to(m + jnp.log(den), (GROUP_ROWS, BLOCK))
            for s in range(4):
                tt, slot = 2 * g + s // 2, s % 2
                piece = o[BLOCK * s:BLOCK * (s + 1)]
                if slot != g:
                    piece = pltpu.roll(piece, HEAD_DIM, 1)
                nat[tt] = piece if nat[tt] is None else nat[tt] + piece
        for tt in range(4):
            o_ref[:, tt * BLOCK:(tt + 1) * BLOCK] = _mx(nat[tt])

    stacked = pl.BlockSpec((2, GROUP_ROWS, BLOCK), lambda n: (0, n, 0))
    return pl.pallas_call(
        body, name="attn_fwd", grid=(nb,),
        in_specs=[stacked] + _nbr_specs(nb, BLOCK, 0) + _nbr_specs(nb, BLOCK, 0) + [pl.BlockSpec(memory_space=pltpu.SMEM)],
        out_specs=[pl.BlockSpec((BLOCK, 512), lambda n: (n, 0)), stacked, stacked],
        out_shape=[jax.ShapeDtypeStruct((t, 512), MXU_DTYPE), jax.ShapeDtypeStruct((2, 4 * t, BLOCK), F32),
                   jax.ShapeDtypeStruct((2, 4 * t, BLOCK), F32)],
        compiler_params=_cparams("parallel"),
    )(qst, kr, kr, kr, vb, vb, vb, sink)


def _lse_and_dsum(ld):
    return ld[:, 0:1], pltpu.roll(ld, HEAD_DIM, 1)[:, 0:1]


def _attn_bwd(qst, kr, vb, dost, ld, sink):
    t = kr.shape[0]
    nb = t // BLOCK

    def body(q_ref, kp_ref, kc_ref, kn_ref, vp_ref, vc_ref, vn_ref, do_ref, ld_ref, sink_ref,
             dq_ref, dk_ref, dv_ref, ds_ref):
        n = pl.program_id(0)

        @pl.when(n == 0)
        def _():
            ds_ref[...] = jnp.zeros_like(ds_ref)
            dk_ref[...] = jnp.zeros_like(dk_ref)
            dv_ref[...] = jnp.zeros_like(dv_ref)

        qi, kj = _query_index()
        m_prev, m_next = (kj >= qi) & (n > 0), (kj <= qi) & (n < nb - 1)
        key_rows = [pl.ds(pl.multiple_of(jnp.clip(n - 1 + b, 0, nb - 1) * BLOCK, BLOCK), BLOCK) for b in range(3)]
        for g in range(2):
            q, do = q_ref[g], do_ref[g]
            lse, dsum = _lse_and_dsum(ld_ref[g])
            acc = jnp.zeros((GROUP_ROWS, BLOCK), F32)
            for b, (k_ref, v_ref, valid) in enumerate(((kp_ref, vp_ref, m_prev), (kc_ref, vc_ref, None),
                                                       (kn_ref, vn_ref, m_next))):
                sc = _dot_nt(q, k_ref[...])
                if valid is not None:
                    sc = jnp.where(valid, sc, NEG)
                p = jnp.exp(sc - lse)
                dsc = _mx(p * (_dot_nt(do, v_ref[...]) - dsum))
                acc += _dot(dsc, k_ref[...])
                dv_ref[key_rows[b], :] += _dot_tn(_mx(p), do)
                dk_ref[key_rows[b], :] += _dot_tn(dsc, q)
            dq_ref[g] = jnp.where(_half(acc.shape, g), acc, 0.0)
            dsk = jnp.exp(_sink_column(sink_ref, g) - lse) * dsum
            for s in range(4):
                h = 4 * g + s
                ds_ref[h:h + 1, :] -= jnp.sum(dsk[BLOCK * s:BLOCK * (s + 1)], axis=0, keepdims=True)

    stacked = pl.BlockSpec((2, GROUP_ROWS, BLOCK), lambda n: (0, n, 0))
    whole = pl.BlockSpec((t, BLOCK), lambda n: (0, 0))
    return pl.pallas_call(
        body, name="attn_bwd", grid=(nb,),
        in_specs=[stacked] + _nbr_specs(nb, BLOCK, 0) + _nbr_specs(nb, BLOCK, 0)
        + [stacked, stacked, pl.BlockSpec(memory_space=pltpu.SMEM)],
        out_specs=[stacked, whole, whole, pl.BlockSpec((8, BLOCK), lambda n: (0, 0))],
        out_shape=[jax.ShapeDtypeStruct((2, 4 * t, BLOCK), F32), jax.ShapeDtypeStruct((t, BLOCK), F32),
                   jax.ShapeDtypeStruct((t, BLOCK), F32), jax.ShapeDtypeStruct((8, BLOCK), F32)],
        compiler_params=_cparams("arbitrary"),
    )(qst, kr, kr, kr, vb, vb, vb, dost, ld, sink)


def _loss_head(y, target, r, gamma):
    t = y.shape[0]

    def body(y_ref, t_ref, r_ref, g_ref, l_ref, res_ref, do_ref, dgb_ref):
        @pl.when(pl.program_id(0) == 0)
        def _():
            l_ref[...] = jnp.zeros_like(l_ref)

        e = y_ref[...] - t_ref[...]
        l_ref[...] += 0.5 * jnp.sum(_mean(e * e))
        _norm_bwd_tail(e / D_MODEL, r_ref, g_ref, res_ref, do_ref, dgb_ref)

    row = pl.BlockSpec((TM, D_MODEL), lambda i: (i, 0))
    tail_in, tail_out, tail_shape = _norm_tail_specs(t)
    return pl.pallas_call(
        body, name="loss_head", grid=(t // TM,), in_specs=[row, row] + tail_in,
        out_specs=[pl.BlockSpec((8, 128), lambda i: (0, 0))] + tail_out,
        out_shape=[jax.ShapeDtypeStruct((8, 128), F32)] + tail_shape,
        compiler_params=_cparams("arbitrary"),
    )(y, target, r, gamma)


def _adamw(name, w, g, m, v, rows):
    n, width = w.shape

    def body(w_ref, g_ref, m_ref, v_ref, d_ref, nm_ref, nv_ref):
        g = g_ref[...]
        m = ADAM_B1 * m_ref[...] + (1.0 - ADAM_B1) * g
        v = ADAM_B2 * v_ref[...] + (1.0 - ADAM_B2) * jnp.square(g)
        m_hat = m / (1.0 - ADAM_B1 ** ADAM_STEP)
        v_hat = v / (1.0 - ADAM_B2 ** ADAM_STEP)
        d_ref[...] = -ADAM_LR * (m_hat / (jnp.sqrt(v_hat) + ADAM_EPS) + ADAM_WD * w_ref[...])
        nm_ref[...] = m
        nv_ref[...] = v

    spec = pl.BlockSpec((rows, width), lambda i: (i, 0))
    return pl.pallas_call(
        body, name=name, grid=(n // rows,), in_specs=[spec] * 4, out_specs=[spec] * 3,
        out_shape=[jax.ShapeDtypeStruct((n, width), F32)] * 3, compiler_params=_cparams("parallel"),
    )(w, g, m, v)


def _place():
    x, y, c = lax.axis_index("x"), lax.axis_index("y"), lax.axis_index("c")
    chips = [(1 - x, y), (x, 1 - y), (1 - x, 1 - y)]
    return x, y, c, chips


class _Gather:
    def __init__(self, shards):
        na = len(shards)
        self.ins = list(shards)
        self.outs = [jax.ShapeDtypeStruct((N_SHARD,) + s.shape, s.dtype) for s in shards]
        sem = pltpu.SemaphoreType.DMA
        self.sems = [sem((2 * na,))] * 4 + [sem((3 * na,))] * 2 + [sem((na,))]

    def _copies(self, src, dst, sems):
        send, recv, rsend, rrecv, fsend, frecv, lsem = sems
        x, y, c, chips = _place()
        index = [2 * cx + cy for cx, cy in chips]
        mine = 2 * x + y

        def local(a):
            return pltpu.make_async_copy(src[a], dst[a].at[mine], lsem.at[a])

        def direct(a, k, shard):
            cx, cy = chips[k]
            return pltpu.make_async_remote_copy(
                src_ref=src[a].at[c], dst_ref=dst[a].at[shard, c], send_sem=send.at[2 * a + k], recv_sem=recv.at[2 * a + k],
                device_id=(cx, cy, c), device_id_type=MESH)

        def relay(a, k, shard):
            rows = src[a].shape[1] // 2
            cx, cy = chips[1 - k]
            block = dst[a].at[shard, c, pl.ds(k * rows, rows)]
            return pltpu.make_async_remote_copy(
                src_ref=block, dst_ref=block, send_sem=rsend.at[2 * a + k], recv_sem=rrecv.at[2 * a + k],
                device_id=(cx, cy, c), device_id_type=MESH)

        def d2d(a, k, half):
            block = dst[a].at[index[k], half]
            return pltpu.make_async_remote_copy(
                src_ref=block, dst_ref=block, send_sem=fsend.at[3 * a + k], recv_sem=frecv.at[3 * a + k],
                device_id=(x, y, 1 - c), device_id_type=MESH)

        return local, direct, relay, d2d, mine, index, c

    def start(self, src, dst, sems):
        local, direct, _, _, mine, _, _ = self._copies(src, dst, sems)
        for a in range(len(src)):
            local(a).start()
            for k in range(2):
                direct(a, k, mine).start()

    def finish(self, src, dst, sems):
        local, direct, relay, d2d, mine, index, c = self._copies(src, dst, sems)
        n = len(src)
        for a in range(n):
            for k in range(2):
                direct(a, k, index[k]).wait_recv()
                relay(a, k, index[k]).start()
                d2d(a, k, c).start()
        for a in range(n):
            for k in range(2):
                relay(a, k, index[2]).wait_recv()
            d2d(a, 2, c).start()
        for a in range(n):
            for k in range(3):
                d2d(a, k, 1 - c).wait_recv()
        for a in range(n):
            for k in range(2):
                direct(a, k, mine).wait_send()
                relay(a, k, index[k]).wait_send()
            for k in range(3):
                d2d(a, k, c).wait_send()
            local(a).wait()


class _PairExchange:
    def __init__(self, parts):
        self.ins = list(parts)
        self.outs = [jax.ShapeDtypeStruct((N_SHARD,) + p.shape[2:], p.dtype) for p in parts]
        self.sems = [pltpu.SemaphoreType.DMA((len(parts),))] * 2

    def _copy(self, a, src, dst, sems):
        x, y, c, _ = _place()
        return pltpu.make_async_remote_copy(
            src_ref=src[a].at[:, 1 - c], dst_ref=dst[a], send_sem=sems[0].at[a], recv_sem=sems[1].at[a],
            device_id=(x, y, 1 - c), device_id_type=MESH)

    def start(self, src, dst, sems):
        for a in range(len(src)):
            self._copy(a, src, dst, sems).start()

    def finish(self, src, dst, sems):
        for a in range(len(src)):
            self._copy(a, src, dst, sems).wait()


class _ChipExchange:
    def __init__(self, sums):
        self.ins = list(sums)
        self.outs = [jax.ShapeDtypeStruct((3,) + s.shape[1:], s.dtype) for s in sums]
        self.sems = [pltpu.SemaphoreType.DMA((3 * len(sums),))] * 2

    def _copy(self, a, k, src, dst, sems):
        _, _, c, chips = _place()
        cx, cy = chips[k]
        return pltpu.make_async_remote_copy(
            src_ref=src[a].at[2 * cx + cy], dst_ref=dst[a].at[k], send_sem=sems[0].at[3 * a + k],
            recv_sem=sems[1].at[3 * a + k], device_id=(cx, cy, c), device_id_type=MESH)

    def start(self, src, dst, sems):
        for a in range(len(src)):
            for k in range(3):
                self._copy(a, k, src, dst, sems).start()

    def finish(self, src, dst, sems):
        for a in range(len(src)):
            for k in range(3):
                self._copy(a, k, src, dst, sems).wait()


class _Both:
    def __init__(self, a, b):
        self.a, self.b = a, b
        self.ins, self.outs, self.sems = a.ins + b.ins, a.outs + b.outs, a.sems + b.sems

    def _each(self, method, ins, outs, sems):
        a = self.a
        getattr(a, method)(ins[:len(a.ins)], outs[:len(a.outs)], sems[:len(a.sems)])
        getattr(self.b, method)(ins[len(a.ins):], outs[len(a.outs):], sems[len(a.sems):])

    def start(self, ins, outs, sems):
        self._each("start", ins, outs, sems)

    def finish(self, ins, outs, sems):
        self._each("finish", ins, outs, sems)


def _run(name, rider):
    n_in, n_out = len(rider.ins), len(rider.outs)

    def body(*refs):
        ins, outs, sems = refs[:n_in], refs[n_in:n_in + n_out], refs[n_in + n_out:]
        rider.start(ins, outs, sems)
        rider.finish(ins, outs, sems)

    return list(pl.pallas_call(
        body, name=name, in_specs=[ANY] * n_in, out_specs=[ANY] * n_out, out_shape=rider.outs,
        scratch_shapes=rider.sems)(*rider.ins))


def _pair_share(halves):
    na = len(halves)

    def body(*refs):
        dst = refs[na:2 * na]
        send, recv = refs[2 * na:]
        x, y, c, _ = _place()
        cps = []
        for a in range(na):
            cp = pltpu.make_async_remote_copy(
                src_ref=dst[a].at[:, c], dst_ref=dst[a].at[:, c], send_sem=send.at[a], recv_sem=recv.at[a],
                device_id=(x, y, 1 - c), device_id_type=MESH)
            cp.start()
            cps.append(cp)
        for a in range(na):
            cps[a].wait_send()
            pltpu.make_async_remote_copy(
                src_ref=dst[a].at[:, 1 - c], dst_ref=dst[a].at[:, 1 - c], send_sem=send.at[a], recv_sem=recv.at[a],
                device_id=(x, y, 1 - c), device_id_type=MESH).wait_recv()

    return pl.pallas_call(
        body, name="pair_share", in_specs=[ANY] * na, out_specs=[ANY] * na,
        out_shape=[jax.ShapeDtypeStruct(h.shape, h.dtype) for h in halves],
        input_output_aliases={a: a for a in range(na)},
        scratch_shapes=[pltpu.SemaphoreType.DMA((na,))] * 2,
    )(*halves)


def _sum_rows(r):
    return r if r <= 352 else 256


def _pair_sum(name, part, got):
    _, _, r, w = part.shape
    rows = _sum_rows(r)
    c = lax.axis_index("c").astype(jnp.int32).reshape(1)

    def body(c_ref, p_ref, g_ref, o_ref):
        o_ref[...] = _mx(p_ref[...] + g_ref[...])

    spec = pl.BlockSpec((None, rows, w), lambda j, i, c_ref: (j, i, 0))
    return pl.pallas_call(
        body, name=name, out_shape=jax.ShapeDtypeStruct((N_SHARD, r, w), MXU_DTYPE),
        grid_spec=pltpu.PrefetchScalarGridSpec(
            num_scalar_prefetch=1, grid=(N_SHARD, r // rows),
            in_specs=[pl.BlockSpec((None, None, rows, w), lambda j, i, c_ref: (j, c_ref[0], i, 0)), spec],
            out_specs=spec),
        compiler_params=_cparams("parallel", "parallel"),
    )(c, part, got)


def _chip_sum(name, part, got, others, l, prev):
    _, _, r, w = part.shape
    rows = _sum_rows(r)
    cj = jnp.stack([lax.axis_index("c"), 2 * lax.axis_index("x") + lax.axis_index("y")]).astype(jnp.int32)

    def body(cj_ref, p_ref, g_ref, o_ref, *rest):
        acc = p_ref[...] + g_ref[...]
        for k in range(3):
            acc += o_ref[k].astype(F32)
        rest[-1][...] = acc

    ins, specs, alias = [cj, part, got, others], [], {}
    if prev is not None:
        ins.append(prev)
        specs.append(ANY)
        alias = {4: 0}
    return pl.pallas_call(
        body, name=name, out_shape=jax.ShapeDtypeStruct((2, 2, r, w), F32), input_output_aliases=alias,
        grid_spec=pltpu.PrefetchScalarGridSpec(
            num_scalar_prefetch=1, grid=(r // rows,),
            in_specs=[pl.BlockSpec((None, None, rows, w), lambda i, cj: (cj[1], cj[0], i, 0)),
                      pl.BlockSpec((None, rows, w), lambda i, cj: (cj[1], i, 0)),
                      pl.BlockSpec((3, rows, w), lambda i, cj: (0, i, 0))] + specs,
            out_specs=pl.BlockSpec((None, None, rows, w), lambda i, cj: (l, cj[0], i, 0))),
        compiler_params=_cparams("parallel"),
    )(*ins)


SMALL_ROWS = 40


def _sum_small(part):
    def body(p_ref, o_ref, land, send, recv):
        x, y, c, _ = _place()
        me = 4 * x + 2 * y + c
        cps = []
        for r in range(1, 8):
            cp = pltpu.make_async_remote_copy(
                src_ref=p_ref, dst_ref=land.at[r], send_sem=send.at[r], recv_sem=recv.at[r],
                device_id=(x ^ (r >> 2), y ^ ((r >> 1) & 1), c ^ (r & 1)), device_id_type=MESH)
            cp.start()
            cps.append(cp)
        land[0] = p_ref[...]
        for cp in cps:
            cp.wait()
        acc = land[me]
        for e in range(1, 8):
            acc += land[me ^ e]
        o_ref[...] = acc

    return pl.pallas_call(
        body, name="sum_small", in_specs=[pl.BlockSpec(memory_space=pltpu.VMEM)],
        out_specs=pl.BlockSpec(memory_space=pltpu.VMEM), out_shape=jax.ShapeDtypeStruct(part.shape, F32),
        scratch_shapes=[pltpu.VMEM((8,) + part.shape, F32), pltpu.SemaphoreType.DMA((8,)), pltpu.SemaphoreType.DMA((8,))],
    )(part)


BIG = ("ffn1_w_gu", "ffn1_w_down", "w_in", "w_out", "ffn2_w_gu", "ffn2_w_down")
SMALL = ("ln1_g", "ln1_b", "ln2_g", "ln2_b", "ln3_g", "ln3_b", "attn_sink", "cc_conv_b", "cc_ln_g", "cc_ln_b",
         "sc_conv_w", "cc_conv_w")
NAMES = ("ffn1_w_gu", "ffn1_w_down", "ln1_g", "ln1_b", "w_in", "sc_conv_w", "attn_sink", "cc_conv_w", "cc_conv_b",
         "cc_ln_g", "cc_ln_b", "w_out", "ln2_g", "ln2_b", "ffn2_w_gu", "ffn2_w_down", "ln3_g", "ln3_b")


def _rope_tables(t):
    half = HEAD_DIM // 2
    inv_freq = ROPE_THETA ** (-jnp.arange(half, dtype=F32) / half)
    ang = jnp.arange(t).astype(F32)[:, None] * inv_freq[None, :]
    cos, sin = jnp.cos(ang), jnp.sin(ang)
    return jnp.tile(jnp.concatenate([cos, cos], axis=1), (1, 2)), jnp.tile(jnp.concatenate([-sin, sin], axis=1), (1, 2))


def _pack_small(vals):
    flat = jnp.concatenate([vals[n].reshape(-1) for n in SMALL])
    return jnp.pad(flat, (0, SMALL_ROWS * D_MODEL - flat.shape[0])).reshape(SMALL_ROWS, D_MODEL)


def _unpack_small(packed, shapes):
    flat, out, at = packed.reshape(-1), {}, 0
    for n in SMALL:
        size = int(np.prod(shapes[n]))
        out[n] = flat[at:at + size].reshape(shapes[n])
        at += size
    return out


def kernel(x, ffn1_w_gu, ffn1_w_down, ln1_g, ln1_b, w_in, sc_conv_w, attn_sink, cc_conv_w, cc_conv_b, cc_ln_g, cc_ln_b, w_out, ln2_g, ln2_b, ffn2_w_gu, ffn2_w_down, ln3_g, ln3_b, loss_target, m_ffn1_w_gu, m_ffn1_w_down, m_ln1_g, m_ln1_b, m_w_in, m_sc_conv_w, m_attn_sink, m_cc_conv_w, m_cc_conv_b, m_cc_ln_g, m_cc_ln_b, m_w_out, m_ln2_g, m_ln2_b, m_ffn2_w_gu, m_ffn2_w_down, m_ln3_g, m_ln3_b, v_ffn1_w_gu, v_ffn1_w_down, v_ln1_g, v_ln1_b, v_w_in, v_sc_conv_w, v_attn_sink, v_cc_conv_w, v_cc_conv_b, v_cc_ln_g, v_cc_ln_b, v_w_out, v_ln2_g, v_ln2_b, v_ffn2_w_gu, v_ffn2_w_down, v_ln3_g, v_ln3_b):
    given = dict(locals())
    w = {n: given[n] for n in NAMES}
    mom = {n: given["m_" + n] for n in NAMES}
    var = {n: given["v_" + n] for n in NAMES}
    x0 = x[0]
    target = loss_target[0]
    t = x0.shape[0]
    chip = 2 * lax.axis_index("x") + lax.axis_index("y")

    conv_shard = jnp.pad(jnp.concatenate([sc_conv_w, cc_conv_w], axis=1), ((0, 0), (0, 30), (0, 64)))
    local = {n: _mx(w[n]) for n in BIG}
    local["conv"] = conv_shard
    full = [{}, {}]

    def gather(l, names):
        return _Gather([local[n][l].reshape(2, local[n].shape[1] // 2, local[n].shape[2]) for n in names])

    def land(l, names, arrays):
        for n, a in zip(names, arrays):
            full[l][n] = a.reshape(1, N_SHARD, 2 * a.shape[2], a.shape[3])

    def weights(l):
        f = full[l]
        conv = jnp.transpose(f["conv"][0, :, :SC_W + CC_W, :64], (1, 0, 2)).reshape(SC_W + CC_W, D_CONV)
        return dict(wgu1=f["ffn1_w_gu"], wd1=f["ffn1_w_down"].reshape(1, D_FF, D_MODEL), win=f["w_in"],
                    wout=f["w_out"].reshape(1, D_MODEL, D_MODEL), wgu2=f["ffn2_w_gu"],
                    wd2=f["ffn2_w_down"].reshape(1, D_FF, D_MODEL), sc=conv[:SC_W], cc=conv[SC_W:])

    first = ("ffn1_w_gu", "ffn1_w_down")
    mixer = ("w_in", "w_out", "conv")
    second = ("ffn2_w_gu", "ffn2_w_down")
    land(0, first, _run("gather_first", gather(0, first)))
    cos, sin = _rope_tables(t)

    def vec(a, l):
        return a[l][None, :]

    acts = []
    h = x0
    for l in range(2):
        ahead = (0, mixer + second) if l == 0 else (1, second)
        (y1, r1, gu1), got = _ffn_fwd("ffn_fwd_a%d" % l, h, full[l]["ffn1_w_gu"], full[l]["ffn1_w_down"].reshape(1, D_FF, D_MODEL),
                                      vec(ln1_g, l), vec(ln1_b, l), 0, gather(*ahead))
        land(*ahead, got)
        wl = weights(l)
        z = _in_proj(y1, wl["win"], 0)
        ysc, ycc, u2 = _conv_fwd(z, wl["sc"], wl["cc"], vec(cc_conv_b, l), vec(cc_ln_g, l), vec(cc_ln_b, l))
        qs, kf, vf = _attn_prep(z, cos, sin)
        o_nat, o, lse = _attn_fwd(qs, kf, vf, attn_sink[l])
        ycat = jnp.concatenate([ysc, o_nat, ycc], axis=1)
        y2, r2 = _out_proj(ycat, y1, wl["wout"], vec(ln2_g, l), vec(ln2_b, l), 0)
        ahead = (1, first + mixer) if l == 0 else None
        (y3, r3, gu2), got = _ffn_fwd("ffn_fwd_b%d" % l, y2, wl["wgu2"], wl["wd2"], vec(ln3_g, l), vec(ln3_b, l), 0,
                                      gather(*ahead) if ahead else None)
        if ahead:
            land(*ahead, got)
        acts.append(dict(x=h, y1=y1, r1=r1, gu1=gu1, z=z, u2=u2, qs=qs, kf=kf, vf=vf, o=o, lse=lse, ycat=ycat,
                         y2=y2, r2=r2, gu2=gu2, r3=r3, w=wl))
        h = y3
    loss_rows, res, do, dgb3_next = _loss_head(h, target, acts[1]["r3"], vec(ln3_g, 1))
    loss = lax.psum(loss_rows[0, 0], ("x", "y", "c"))

    upper = ("ffn2_w_gu", "ffn2_w_down", "w_out")
    lower = ("w_in", "ffn1_w_gu", "ffn1_w_down")
    part = [{}, {}]
    small = [None, None]
    stage = {}
    reduced = {n: None for n in BIG}
    row = pl.BlockSpec((TM, D_MODEL), lambda n, k: (k, 0))
    deep = pl.BlockSpec((TK, D_MODEL), lambda n, k: (k, 0))

    def halves(a, r):
        return a.reshape(N_SHARD, 2, r // 2, a.shape[-1])

    def pair_rider(l, names):
        return _PairExchange([part[l][n] for n in names])

    def after_pair(l, names, got):
        stage[l, names] = (got, [_pair_sum("pair_sum_%s_%d" % (n, l), part[l][n], g) for n, g in zip(names, got)])

    def chip_rider(l, names):
        return _ChipExchange(stage[l, names][1])

    def after_chip(l, names, others):
        for n, g, o in zip(names, stage[l, names][0], others):
            reduced[n] = _chip_sum("chip_sum_%s_%d" % (n, l), part[l][n], g, o, l, reduced[n])

    def ffn_weight_grads(which, l, xin, dh, a, do, rider_gu=None, make_rider_d=None):
        out, got_gu = _mm_tn(
            "%s_dwgu_%d" % (which, l), xin, dh, deep,
            pl.BlockSpec((None, TK, FF_CHUNK), lambda n, k: (n // N_CHUNK, k, n % N_CHUNK)),
            pl.BlockSpec((None, D_MODEL, FF_CHUNK), lambda n, k: (n, 0, 0)),
            (N_SHARD, D_MODEL, GU_SHARD), (D_MODEL, FF_CHUNK), (2 * N_CHUNK, t // TK), rider_gu)
        part[l][which + "_w_gu"] = halves(out, D_MODEL)
        rider_d = make_rider_d() if make_rider_d else None
        out, got_d = _mm_tn(
            "%s_dwd_%d" % (which, l), a, do, pl.BlockSpec((TK, FF_CHUNK), lambda n, k: (k, n)), deep,
            pl.BlockSpec((FF_CHUNK, D_MODEL), lambda n, k: (n, 0)),
            (D_FF, D_MODEL), (FF_CHUNK, D_MODEL), (N_CHUNK, t // TK), rider_d)
        part[l][which + "_w_down"] = halves(out, D_FF // N_SHARD)
        return got_gu, got_d

    w_in_only, w_gu_only, w_down_only = ("w_in",), ("ffn1_w_gu",), ("ffn1_w_down",)
    for l in (1, 0):
        s = acts[l]
        wl = s["w"]
        dgb3 = dgb3_next
        if l == 0:
            dy, dh, a, got = _ffn_bwd(res, do, s["gu2"], wl["wgu2"], wl["wd2"], 0, rider=pair_rider(1, lower))
            after_pair(1, lower, got)
            got, _ = ffn_weight_grads("ffn2", l, s["y2"], dh, a, do, chip_rider(1, lower))
            after_chip(1, lower, got)
        else:
            dy, dh, a, _ = _ffn_bwd(res, do, s["gu2"], wl["wgu2"], wl["wd2"], 0)
            ffn_weight_grads("ffn2", l, s["y2"], dh, a, do)
        res, dm, dycat, dgb2 = _out_proj_bwd(dy, s["r2"], wl["wout"], vec(ln2_g, l), 0)
        out, _ = _mm_tn("dwout_%d" % l, s["ycat"], dm, row, row, pl.BlockSpec((D_MODEL, D_MODEL), lambda n, k: (0, 0)),
                        (D_MODEL, D_MODEL), (D_MODEL, D_MODEL), (1, t // TM))
        part[l]["w_out"] = halves(out, OUT_SHARD)
        dz_sc, dz_cc, dconv = _conv_bwd(s["z"], dycat, s["u2"], wl["sc"], wl["cc"], vec(cc_ln_g, l), vec(cc_ln_b, l))
        dost, ld = _attn_dprep(dycat, s["o"], s["lse"])
        dqs, dkf, dvf, dsink = _attn_bwd(s["qs"], s["kf"], s["vf"], dost, ld, attn_sink[l])
        dz_att = _attn_prep_bwd(dqs, dkf, dvf, cos, sin)
        dz = jnp.concatenate([dz_sc, dz_att, dz_cc], axis=1)
        out, got = _mm_tn(
            "dwin_%d" % l, s["y1"], dz, row, pl.BlockSpec((TM, D_IN), lambda n, k: (k, 0)),
            pl.BlockSpec((N_SHARD, D_MODEL, IN_SHARD), lambda n, k: (0, 0, 0)),
            (N_SHARD, D_MODEL, IN_SHARD), (D_MODEL, D_IN), (1, t // TM), pair_rider(l, upper), split=N_SHARD)
        part[l]["w_in"] = halves(out, D_MODEL)
        after_pair(l, upper, got)
        res, do, dgb1 = _in_proj_bwd(dz, res, wl["win"], 0, s["r1"], vec(ln1_g, l))
        if l == 1:
            (res0, do0, dgb3_next), dh, a, _ = _ffn_bwd(res, do, s["gu1"], wl["wgu1"], wl["wd1"], 0,
                                                        tail=(acts[0]["r3"], vec(ln3_g, 0)))
            got, _ = ffn_weight_grads("ffn1", l, s["x"], dh, a, do, chip_rider(l, upper))
            after_chip(l, upper, got)
            res, do = res0, do0
        else:
            dy, dh, a, got = _ffn_bwd(res, do, s["gu1"], wl["wgu1"], wl["wd1"], 0, rider=pair_rider(0, w_in_only))
            after_pair(0, w_in_only, got)
            got, got_d = ffn_weight_grads("ffn1", l, s["x"], dh, a, do, _Both(chip_rider(0, upper), chip_rider(0, w_in_only)),
                                          lambda: pair_rider(0, w_gu_only))
            n_upper = len(upper)
            after_chip(0, upper, got[:n_upper])
            after_chip(0, w_in_only, got[n_upper:])
            after_pair(0, w_gu_only, got_d)
        small[l] = dict(ln1_g=dgb1[0], ln1_b=dgb1[1], ln2_g=dgb2[0], ln2_b=dgb2[1], ln3_g=dgb3[0], ln3_b=dgb3[1],
                        attn_sink=dsink[:, 0], cc_conv_b=dconv[ROW_CCB], cc_ln_g=dconv[ROW_CCG],
                        cc_ln_b=dconv[ROW_CCBETA], sc_conv_w=dconv[ROW_SCW:ROW_SCW + SC_W],
                        cc_conv_w=dconv[ROW_CCW:ROW_CCW + CC_W])
    grad_x = dy[None]

    after_pair(0, w_down_only, _run("pair_exchange_last", pair_rider(0, w_down_only)))
    got = _run("chip_exchange_last", _Both(chip_rider(0, w_gu_only), chip_rider(0, w_down_only)))
    after_chip(0, w_gu_only, got[:1])
    after_chip(0, w_down_only, got[1:])
    grads = dict(zip(BIG, _pair_share([reduced[n] for n in BIG])))
    for n in BIG:
        grads[n] = grads[n].reshape(w[n].shape)

    small_full = {n: jnp.stack([small[0][n], small[1][n]]) for n in SMALL}
    small_sum = _unpack_small(_sum_small(_pack_small(small_full)), {n: small_full[n].shape for n in SMALL})
    for n in SMALL:
        g = small_sum[n]
        if n in ("sc_conv_w", "cc_conv_w"):
            g = lax.dynamic_slice_in_dim(g, chip * 64, 64, axis=2)
        grads[n] = g

    delta, new_m, new_v = {}, {}, {}
    for n in BIG:
        shape = w[n].shape
        two_d = (shape[0] * shape[1], shape[2])
        outs = _adamw("adamw_" + n, w[n].reshape(two_d), grads[n].reshape(two_d), mom[n].reshape(two_d),
                      var[n].reshape(two_d), 128)
        delta[n], new_m[n], new_v[n] = [a.reshape(shape) for a in outs]
    shapes = {n: w[n].shape for n in SMALL}
    outs = _adamw("adamw_small", _pack_small({n: w[n] for n in SMALL}), _pack_small({n: grads[n] for n in SMALL}),
                  _pack_small({n: mom[n] for n in SMALL}), _pack_small({n: var[n] for n in SMALL}), 8)
    for d, packed in zip((delta, new_m, new_v), outs):
        d.update(_unpack_small(packed, shapes))

    return (loss, grad_x, *[grads[n] for n in NAMES], *[delta[n] for n in NAMES], *[new_m[n] for n in NAMES],
            *[new_v[n] for n in NAMES])
```

```python
import functools

import numpy as np
import jax
import jax.numpy as jnp
from jax import lax
from jax.experimental import pallas as pl
from jax.experimental.pallas import tpu as pltpu

F32 = jnp.float32
MXU_DTYPE = jnp.bfloat16

D_MODEL = 1024
D_FF = 2816
N_SHARD = 4
D_IN = 2048
GU_SHARD = 2 * D_FF // N_SHARD
FF_CHUNK = GU_SHARD
N_CHUNK = D_FF // FF_CHUNK
IN_SHARD = D_IN // N_SHARD
OUT_SHARD = D_MODEL // N_SHARD
HEAD_DIM = 64
N_Q_HEADS = 8
BLOCK = 128
SC_W = 3
CC_W = 31
D_CONV = 256
HALO = 16
LN_EPS = 1e-5
ALPHA = (2.0 * 2) ** 0.25
NEG = -1e30
ROPE_THETA = 10000.0
ADAM_LR, ADAM_B1, ADAM_B2, ADAM_EPS, ADAM_WD, ADAM_STEP = 0.001, 0.9, 0.999, 1e-08, 0.01, 10

TM = 512
TK = 1024
TMC = 256
VMEM_LIMIT = 56 * 1024 * 1024
MESH = pl.DeviceIdType.MESH
ANY = pl.BlockSpec(memory_space=pl.ANY)


def _cparams(*sem):
    return pltpu.CompilerParams(dimension_semantics=sem, vmem_limit_bytes=VMEM_LIMIT)


def _dot(a, b):
    return jnp.dot(a, b, preferred_element_type=F32)


def _dot_nt(a, b):
    return lax.dot_general(a, b, (((1,), (1,)), ((), ())), preferred_element_type=F32)


def _dot_tn(a, b):
    return lax.dot_general(a, b, (((0,), (0,)), ((), ())), preferred_element_type=F32)


def _mx(a):
    return a.astype(MXU_DTYPE)


def _mean(a):
    return jnp.mean(a, axis=-1, keepdims=True)


def _ln_stats(r):
    xc = r - _mean(r)
    rstd = lax.rsqrt(_mean(xc * xc) + LN_EPS)
    return xc * rstd, rstd


def _ln_bwd(dy, xh, rstd, gamma):
    dxh = dy * gamma
    return rstd * (dxh - _mean(dxh) - xh * _mean(dxh * xh))


def _colsum(a):
    return jnp.sum(a, axis=0, keepdims=True)


def _sigmoid(a):
    return 1.0 / (1.0 + jnp.exp(-a))


def _call(body, args, *, name, grid, in_specs, out_specs, out_shape, scratch, sem, rider=None):
    if rider is None:
        outs = pl.pallas_call(
            body, name=name, grid=grid, in_specs=in_specs, out_specs=out_specs, out_shape=out_shape,
            scratch_shapes=scratch, compiler_params=_cparams(*sem))(*args)
        return list(outs), []
    n_in, n_out, n_sc = len(in_specs), len(out_specs), len(scratch)
    r_in, r_out = len(rider.ins), len(rider.outs)

    def carrying(*refs):
        cuts = np.cumsum([0, n_in, r_in, n_out, r_out, n_sc])
        ins, rins, outs, routs, scr = [refs[a:b] for a, b in zip(cuts[:-1], cuts[1:])]
        rsems = refs[cuts[-1]:]
        first = functools.reduce(jnp.logical_and, [pl.program_id(d) == 0 for d in range(len(grid))])
        last = functools.reduce(jnp.logical_and, [pl.program_id(d) == grid[d] - 1 for d in range(len(grid))])

        @pl.when(first)
        def _():
            rider.start(rins, routs, rsems)

        body(*ins, *outs, *scr)

        @pl.when(last)
        def _():
            rider.finish(rins, routs, rsems)

    outs = pl.pallas_call(
        carrying, name=name, grid=grid, in_specs=list(in_specs) + [ANY] * r_in,
        out_specs=list(out_specs) + [ANY] * r_out, out_shape=list(out_shape) + list(rider.outs),
        scratch_shapes=list(scratch) + list(rider.sems), compiler_params=_cparams(*(("arbitrary",) * len(grid))),
    )(*args, *rider.ins)
    return list(outs[:n_out]), list(outs[n_out:])


def _ffn_fwd(name, x, wgu, wd, gamma, beta, l, rider=None):
    t = x.shape[0]
    nc = N_CHUNK

    def body(x_ref, wg_ref, wu_ref, wd_ref, g_ref, b_ref, y_ref, r_ref, gu_ref, xb_s, acc_s):
        c = pl.program_id(1)

        @pl.when(c == 0)
        def _():
            xb_s[...] = _mx(x_ref[...])
            acc_s[...] = jnp.zeros_like(acc_s)

        xb = xb_s[...]
        hg = _dot(xb, wg_ref[...])
        hu = _dot(xb, wu_ref[...])
        gu_ref[0] = _mx(hg)
        gu_ref[1] = _mx(hu)
        a = (hg * _sigmoid(hg)) * hu
        acc_s[...] += _dot(_mx(a), wd_ref[...])

        @pl.when(c == nc - 1)
        def _():
            r = ALPHA * x_ref[...] + 0.5 * acc_s[...]
            xh, _ = _ln_stats(r)
            r_ref[...] = r
            y_ref[...] = xh * g_ref[...] + b_ref[...]

    row = pl.BlockSpec((TM, D_MODEL), lambda i, c: (i, 0))
    vec = pl.BlockSpec((1, D_MODEL), lambda i, c: (0, 0))
    return _call(
        body, (x, wgu, wgu, wd, gamma, beta), name=name, grid=(t // TM, nc),
        in_specs=[row,
                  pl.BlockSpec((None, None, D_MODEL, FF_CHUNK), lambda i, c: (l, c, 0, 0)),
                  pl.BlockSpec((None, None, D_MODEL, FF_CHUNK), lambda i, c: (l, N_CHUNK + c, 0, 0)),
                  pl.BlockSpec((None, FF_CHUNK, D_MODEL), lambda i, c: (l, c, 0)),
                  vec, vec],
        out_specs=[row, row, pl.BlockSpec((2, TM, FF_CHUNK), lambda i, c: (0, i, c))],
        out_shape=[jax.ShapeDtypeStruct((t, D_MODEL), F32), jax.ShapeDtypeStruct((t, D_MODEL), F32),
                   jax.ShapeDtypeStruct((2, t, D_FF), MXU_DTYPE)],
        scratch=[pltpu.VMEM((TM, D_MODEL), MXU_DTYPE), pltpu.VMEM((TM, D_MODEL), F32)],
        sem=("parallel", "arbitrary"), rider=rider)


def _norm_bwd_tail(dy, r_ref, g_ref, res_ref, do_ref, dgb_ref):
    @pl.when(pl.program_id(0) == 0)
    def _():
        dgb_ref[...] = jnp.zeros_like(dgb_ref)

    xh, rstd = _ln_stats(r_ref[...])
    dr = _ln_bwd(dy, xh, rstd, g_ref[...])
    do_ref[...] = _mx(0.5 * dr)
    res_ref[...] = ALPHA * dr
    dgb_ref[0:1, :] += _colsum(dy * xh)
    dgb_ref[1:2, :] += _colsum(dy)


def _norm_tail_specs(t):
    row = pl.BlockSpec((TM, D_MODEL), lambda i: (i, 0))
    return ([row, pl.BlockSpec((1, D_MODEL), lambda i: (0, 0))],
            [row, row, pl.BlockSpec((8, D_MODEL), lambda i: (0, 0))],
            [jax.ShapeDtypeStruct((t, D_MODEL), F32), jax.ShapeDtypeStruct((t, D_MODEL), MXU_DTYPE),
             jax.ShapeDtypeStruct((8, D_MODEL), F32)])


def _ffn_bwd(res, do, gu, wgu, wd, l, tail=None, rider=None):
    t = res.shape[0]
    nc = N_CHUNK
    row1 = pl.BlockSpec((TM, D_MODEL), lambda i: (i, 0))

    def hidden_body(do_ref, gu_ref, wd_ref, dh_ref, a_ref):
        da = _dot_nt(do_ref[...], wd_ref[...])
        g = gu_ref[0].astype(F32)
        u = gu_ref[1].astype(F32)
        s = _sigmoid(g)
        sil = g * s
        a_ref[...] = _mx(sil * u)
        dh_ref[0] = _mx(da * u * (s * (1.0 + g * (1.0 - s))))
        dh_ref[1] = _mx(da * sil)

    hid = pl.BlockSpec((2, TM, FF_CHUNK), lambda c, i: (0, i, c))
    (dh, a), got = _call(
        hidden_body, (do, gu, wd), name="ffn_bwd_hidden" if rider is None else "ffn_bwd_hidden_carry", grid=(nc, t // TM),
        in_specs=[pl.BlockSpec((TM, D_MODEL), lambda c, i: (i, 0)), hid,
                  pl.BlockSpec((None, FF_CHUNK, D_MODEL), lambda c, i: (l, c, 0))],
        out_specs=[hid, pl.BlockSpec((TM, FF_CHUNK), lambda c, i: (i, c))],
        out_shape=[jax.ShapeDtypeStruct((2, t, D_FF), MXU_DTYPE), jax.ShapeDtypeStruct((t, D_FF), MXU_DTYPE)],
        scratch=[], sem=("parallel", "parallel"), rider=rider)

    def input_body(res_ref, dh_ref, w_ref, *rest):
        acc = res_ref[...]
        for j in range(N_SHARD):
            part = dh_ref[j // N_CHUNK][:, (j % N_CHUNK) * FF_CHUNK:(j % N_CHUNK + 1) * FF_CHUNK]
            acc += _dot_nt(part, w_ref[j])
        if tail is None:
            rest[0][...] = acc
        else:
            _norm_bwd_tail(acc, *rest)

    in_specs = [row1, pl.BlockSpec((2, TM, D_FF), lambda i: (0, i, 0)),
                pl.BlockSpec((None, N_SHARD, D_MODEL, GU_SHARD), lambda i: (l, 0, 0, 0))]
    if tail is None:
        dx = pl.pallas_call(
            input_body, name="ffn_bwd_input", grid=(t // TM,), in_specs=in_specs, out_specs=row1,
            out_shape=jax.ShapeDtypeStruct((t, D_MODEL), F32), compiler_params=_cparams("parallel"),
        )(res, dh, wgu)
    else:
        tail_in, tail_out, tail_shape = _norm_tail_specs(t)
        dx = pl.pallas_call(
            input_body, name="ffn_bwd_input_norm", grid=(t // TM,), in_specs=in_specs + tail_in, out_specs=tail_out,
            out_shape=tail_shape, compiler_params=_cparams("arbitrary"),
        )(res, dh, wgu, *tail)
    return dx, dh, a, got


def _mm_tn(name, a, b, a_spec, b_spec, out_spec, out_shape, acc_shape, grid, rider=None, split=1):
    nk = grid[-1]
    width = acc_shape[1] // split

    def body(*refs):
        a_ref, b_ref = refs[0], refs[1]
        o_ref, acc = refs[-2], refs[-1]
        k = pl.program_id(len(grid) - 1)

        @pl.when(k == 0)
        def _():
            acc[...] = jnp.zeros_like(acc)

        acc[...] += _dot_tn(_mx(a_ref[...]), _mx(b_ref[...]))

        @pl.when(k == nk - 1)
        def _():
            if split == 1:
                o_ref[...] = acc[...]
            else:
                for j in range(split):
                    o_ref[j] = acc[:, j * width:(j + 1) * width]

    sem = ("parallel",) * (len(grid) - 1) + ("arbitrary",)
    (out,), got = _call(
        body, (a, b), name=name, grid=grid, in_specs=[a_spec, b_spec], out_specs=[out_spec],
        out_shape=[jax.ShapeDtypeStruct(out_shape, F32)], scratch=[pltpu.VMEM(acc_shape, F32)], sem=sem, rider=rider)
    return out, got


def _in_proj(x, w_in, l):
    t = x.shape[0]

    def body(x_ref, w_ref, z_ref):
        xb = _mx(x_ref[...])
        for j in range(N_SHARD):
            z_ref[:, j * IN_SHARD:(j + 1) * IN_SHARD] = _dot(xb, w_ref[j])

    return pl.pallas_call(
        body, name="in_proj", grid=(t // TM,),
        in_specs=[pl.BlockSpec((TM, D_MODEL), lambda i: (i, 0)),
                  pl.BlockSpec((None, N_SHARD, D_MODEL, IN_SHARD), lambda i: (l, 0, 0, 0))],
        out_specs=pl.BlockSpec((TM, D_IN), lambda i: (i, 0)),
        out_shape=jax.ShapeDtypeStruct((t, D_IN), F32),
        compiler_params=_cparams("parallel"),
    )(x, w_in)


def _in_proj_bwd(dz, dx_res, w_in, l, r, gamma):
    t = dz.shape[0]

    def body(dz_ref, res_ref, w_ref, *tail):
        acc = res_ref[...]
        for j in range(N_SHARD):
            acc += _dot_nt(dz_ref[:, j * IN_SHARD:(j + 1) * IN_SHARD], w_ref[j])
        _norm_bwd_tail(acc, *tail)

    row = pl.BlockSpec((TM, D_MODEL), lambda i: (i, 0))
    tail_in, tail_out, tail_shape = _norm_tail_specs(t)
    return pl.pallas_call(
        body, name="in_proj_bwd", grid=(t // TM,),
        in_specs=[pl.BlockSpec((TM, D_IN), lambda i: (i, 0)), row,
                  pl.BlockSpec((None, N_SHARD, D_MODEL, IN_SHARD), lambda i: (l, 0, 0, 0))] + tail_in,
        out_specs=tail_out, out_shape=tail_shape, compiler_params=_cparams("arbitrary"),
    )(dz, dx_res, w_in, r, gamma)


def _out_proj(ycat, x, w_out, gamma, beta, l):
    t = x.shape[0]

    def body(yc_ref, x_ref, w_ref, g_ref, b_ref, y_ref, r_ref):
        r = ALPHA * x_ref[...] + _dot(yc_ref[...], w_ref[...])
        xh, _ = _ln_stats(r)
        r_ref[...] = r
        y_ref[...] = xh * g_ref[...] + b_ref[...]

    row = pl.BlockSpec((TM, D_MODEL), lambda i: (i, 0))
    vec = pl.BlockSpec((1, D_MODEL), lambda i: (0, 0))
    return pl.pallas_call(
        body, name="out_proj", grid=(t // TM,),
        in_specs=[row, row, pl.BlockSpec((None, D_MODEL, D_MODEL), lambda i: (l, 0, 0)), vec, vec],
        out_specs=[row, row],
        out_shape=[jax.ShapeDtypeStruct((t, D_MODEL), F32)] * 2,
        compiler_params=_cparams("parallel"),
    )(ycat, x, w_out, gamma, beta)


def _out_proj_bwd(dy, r, w_out, gamma, l):
    t = dy.shape[0]

    def body(dy_ref, r_ref, w_ref, g_ref, res_ref, dm_ref, dyc_ref, dgb_ref):
        @pl.when(pl.program_id(0) == 0)
        def _():
            dgb_ref[...] = jnp.zeros_like(dgb_ref)

        xh, rstd = _ln_stats(r_ref[...])
        dy = dy_ref[...]
        dr = _ln_bwd(dy, xh, rstd, g_ref[...])
        res_ref[...] = ALPHA * dr
        dm = _mx(dr)
        dm_ref[...] = dm
        dyc_ref[...] = _dot_nt(dm, w_ref[...])
        dgb_ref[0:1, :] += _colsum(dy * xh)
        dgb_ref[1:2, :] += _colsum(dy)

    row = pl.BlockSpec((TM, D_MODEL), lambda i: (i, 0))
    return pl.pallas_call(
        body, name="out_proj_bwd", grid=(t // TM,),
        in_specs=[row, row, pl.BlockSpec((None, D_MODEL, D_MODEL), lambda i: (l, 0, 0)),
                  pl.BlockSpec((1, D_MODEL), lambda i: (0, 0))],
        out_specs=[row, row, row, pl.BlockSpec((8, D_MODEL), lambda i: (0, 0))],
        out_shape=[jax.ShapeDtypeStruct((t, D_MODEL), F32), jax.ShapeDtypeStruct((t, D_MODEL), MXU_DTYPE),
                   jax.ShapeDtypeStruct((t, D_MODEL), F32), jax.ShapeDtypeStruct((8, D_MODEL), F32)],
        compiler_params=_cparams("arbitrary"),
    )(dy, r, w_out, gamma)


def _halo_specs(t, width, col):
    per = TMC // HALO
    last = t // HALO - 1
    return [pl.BlockSpec((HALO, width), lambda i: (jnp.maximum(i * per - 1, 0), col)),
            pl.BlockSpec((TMC, width), lambda i: (i, col)),
            pl.BlockSpec((HALO, width), lambda i: (jnp.minimum((i + 1) * per, last), col))]


def _extend(refs, i, nt):
    p_ref, c_ref, n_ref = refs
    p = jnp.where(i > 0, p_ref[...].astype(F32), 0.0)
    n = jnp.where(i < nt - 1, n_ref[...].astype(F32), 0.0)
    return jnp.concatenate([p, c_ref[...].astype(F32), n], axis=0)


def _shifted_copies(src_s, dst8_s):
    n = src_s.shape[0] - 8
    for b in range(8):
        dst8_s[b, 0:n, :] = src_s[pl.ds(b, n), :]


def _window(dst8_s, start):
    return dst8_s[start % 8, pl.ds(start - start % 8, TMC), :]


def _conv_fwd(z, sc_w, cc_w, cc_cb, cc_g, cc_b):
    t = z.shape[0]
    nt = t // TMC

    def body(*refs):
        b_ref = refs[0]
        c3, h3, a3, g3 = refs[1:4], refs[4:7], refs[7:10], refs[10:13]
        scw_ref, ccw_ref, cb_ref, lg_ref, lb_ref = refs[13:18]
        ysc_ref, ycc_ref, u2_ref, e_s, e8_s = refs[18:23]
        i = pl.program_id(0)
        e_s[...] = _extend(c3, i, nt) * _extend(h3, i, nt)
        cv = jnp.zeros((TMC, D_CONV), F32)
        for k in range(SC_W):
            cv += scw_ref[k:k + 1, :] * e_s[pl.ds(HALO + k - 1, TMC), :]
        ysc_ref[...] = _mx(b_ref[...] * cv)
        e_s[...] = _extend(a3, i, nt) * _sigmoid(_extend(g3, i, nt))
        _shifted_copies(e_s, e8_s)
        u2 = jnp.zeros((TMC, D_CONV), F32) + cb_ref[...]
        for k in range(CC_W):
            u2 += ccw_ref[k:k + 1, :] * _window(e8_s, HALO + k - 15)
        u2_ref[...] = u2
        xh, _ = _ln_stats(u2)
        n = xh * lg_ref[...] + lb_ref[...]
        ycc_ref[...] = _mx(n * _sigmoid(n))

    tile = pl.BlockSpec((TMC, D_CONV), lambda i: (i, 0))
    vec = pl.BlockSpec((1, D_CONV), lambda i: (0, 0))
    in_specs = ([pl.BlockSpec((TMC, D_CONV), lambda i: (i, 0))] + _halo_specs(t, D_CONV, 1) + _halo_specs(t, D_CONV, 2)
                + _halo_specs(t, D_CONV, 6) + _halo_specs(t, D_CONV, 7)
                + [pl.BlockSpec((SC_W, D_CONV), lambda i: (0, 0)), pl.BlockSpec((CC_W, D_CONV), lambda i: (0, 0)),
                   vec, vec, vec])
    return pl.pallas_call(
        body, name="conv_fwd", grid=(nt,), in_specs=in_specs, out_specs=[tile, tile, tile],
        out_shape=[jax.ShapeDtypeStruct((t, D_CONV), MXU_DTYPE), jax.ShapeDtypeStruct((t, D_CONV), MXU_DTYPE),
                   jax.ShapeDtypeStruct((t, D_CONV), F32)],
        scratch_shapes=[pltpu.VMEM((TMC + 2 * HALO, D_CONV), F32), pltpu.VMEM((8, TMC + 2 * HALO, D_CONV), F32)],
        compiler_params=_cparams("parallel"),
    )(*([z] * 13), sc_w, cc_w, cc_cb, cc_g, cc_b)


ROW_CCW, ROW_CCB, ROW_CCG, ROW_CCBETA, ROW_SCW, CONV_ROWS = 0, 31, 32, 33, 34, 40


def _conv_bwd(z, dycat, u2, sc_w, cc_w, cc_g, cc_b):
    t = z.shape[0]
    nt = t // TMC

    def body(*refs):
        b3, c3, h3, a3, g3 = refs[0:3], refs[3:6], refs[6:9], refs[9:12], refs[12:15]
        dys3, dyc3, u3 = refs[15:18], refs[18:21], refs[21:24]
        scw_ref, ccw_ref, lg_ref, lb_ref = refs[24:28]
        dsc_ref, dcc_ref, sm_ref, e_s, f_s, e8_s, f8_s = refs[28:35]
        i = pl.program_id(0)

        @pl.when(i == 0)
        def _():
            sm_ref[...] = jnp.zeros_like(sm_ref)

        cur = pl.ds(HALO, TMC)
        e_s[...] = _extend(c3, i, nt) * _extend(h3, i, nt)
        f_s[...] = _extend(dys3, i, nt) * _extend(b3, i, nt)
        cv = jnp.zeros((TMC, D_CONV), F32)
        dp = jnp.zeros((TMC, D_CONV), F32)
        dcv = f_s[cur, :]
        for k in range(SC_W):
            win = e_s[pl.ds(HALO + k - 1, TMC), :]
            cv += scw_ref[k:k + 1, :] * win
            dp += scw_ref[k:k + 1, :] * f_s[pl.ds(HALO - k + 1, TMC), :]
            sm_ref[ROW_SCW + k:ROW_SCW + k + 1, :] += _colsum(dcv * win)
        dsc_ref[:, 0:D_CONV] = _mx(dys3[1][...] * cv)
        dsc_ref[:, D_CONV:2 * D_CONV] = _mx(dp * h3[1][...])
        dsc_ref[:, 2 * D_CONV:3 * D_CONV] = _mx(dp * c3[1][...])
        xh, rstd = _ln_stats(_extend(u3, i, nt))
        n = xh * lg_ref[...] + lb_ref[...]
        sg = _sigmoid(n)
        dn = _extend(dyc3, i, nt) * (sg * (1.0 + n * (1.0 - sg)))
        f_s[...] = _ln_bwd(dn, xh, rstd, lg_ref[...])
        sm_ref[ROW_CCG:ROW_CCG + 1, :] += _colsum((dn * xh)[HALO:HALO + TMC])
        sm_ref[ROW_CCBETA:ROW_CCBETA + 1, :] += _colsum(dn[HALO:HALO + TMC])
        sig_g = _sigmoid(_extend(g3, i, nt))
        e_s[...] = _extend(a3, i, nt) * sig_g
        _shifted_copies(e_s, e8_s)
        _shifted_copies(f_s, f8_s)
        du2 = f_s[cur, :]
        sm_ref[ROW_CCB:ROW_CCB + 1, :] += _colsum(du2)
        duu = jnp.zeros((TMC, D_CONV), F32)
        for k in range(CC_W):
            duu += ccw_ref[k:k + 1, :] * _window(f8_s, HALO + 15 - k)
            sm_ref[ROW_CCW + k:ROW_CCW + k + 1, :] += _colsum(du2 * _window(e8_s, HALO + k - 15))
        sgc = sig_g[HALO:HALO + TMC]
        dcc_ref[:, 0:D_CONV] = _mx(duu * sgc)
        dcc_ref[:, D_CONV:2 * D_CONV] = _mx(duu * a3[1][...] * sgc * (1.0 - sgc))

    vec = pl.BlockSpec((1, D_CONV), lambda i: (0, 0))
    in_specs = []
    for col in (0, 1, 2, 6, 7):
        in_specs += _halo_specs(t, D_CONV, col)
    in_specs += _halo_specs(t, D_CONV, 0) + _halo_specs(t, D_CONV, 3) + _halo_specs(t, D_CONV, 0)
    in_specs += [pl.BlockSpec((SC_W, D_CONV), lambda i: (0, 0)), pl.BlockSpec((CC_W, D_CONV), lambda i: (0, 0)), vec, vec]
    return pl.pallas_call(
        body, name="conv_bwd", grid=(nt,), in_specs=in_specs,
        out_specs=[pl.BlockSpec((TMC, 3 * D_CONV), lambda i: (i, 0)), pl.BlockSpec((TMC, 2 * D_CONV), lambda i: (i, 0)),
                   pl.BlockSpec((CONV_ROWS, D_CONV), lambda i: (0, 0))],
        out_shape=[jax.ShapeDtypeStruct((t, 3 * D_CONV), MXU_DTYPE), jax.ShapeDtypeStruct((t, 2 * D_CONV), MXU_DTYPE),
                   jax.ShapeDtypeStruct((CONV_ROWS, D_CONV), F32)],
        scratch_shapes=[pltpu.VMEM((TMC + 2 * HALO, D_CONV), F32)] * 2
        + [pltpu.VMEM((8, TMC + 2 * HALO, D_CONV), F32)] * 2,
        compiler_params=_cparams("arbitrary"),
    )(*([z] * 15), *([dycat] * 6), *([u2] * 3), sc_w, cc_w, cc_g, cc_b)


def _lane(shape):
    return lax.broadcasted_iota(jnp.int32, shape, 1)


def _swap_halves(x):
    w = x.shape[1]
    lo = (_lane(x.shape) % HEAD_DIM) < HEAD_DIM // 2
    return jnp.where(lo, pltpu.roll(x, w - HEAD_DIM // 2, 1), pltpu.roll(x, HEAD_DIM // 2, 1))


def _half(shape, g):
    lane = _lane(shape)
    return lane < HEAD_DIM if g == 0 else lane >= HEAD_DIM


GROUP_ROWS = 4 * BLOCK


def _stack_heads(tiles, out_ref, nblk):
    for tt in range(4):
        g = tt // 2
        for slot in range(2):
            s = 2 * (tt % 2) + slot
            piece = tiles[tt] if slot == g else pltpu.roll(tiles[tt], HEAD_DIM, 1)
            piece = jnp.where(_half(piece.shape, g), piece, 0.0).astype(out_ref.dtype)
            for b in range(nblk):
                at = GROUP_ROWS * b + BLOCK * s
                out_ref[g, at:at + BLOCK, :] = piece[BLOCK * b:BLOCK * (b + 1)]


def _unstack_heads(ref, nblk):
    tiles = []
    for tt in range(4):
        g = tt // 2
        tile = None
        for slot in range(2):
            s = 2 * (tt % 2) + slot
            rows = [ref[g, GROUP_ROWS * b + BLOCK * s:GROUP_ROWS * b + BLOCK * (s + 1), :] for b in range(nblk)]
            piece = rows[0] if nblk == 1 else jnp.concatenate(rows, axis=0)
            if slot != g:
                piece = pltpu.roll(piece, HEAD_DIM, 1)
            tile = piece if tile is None else tile + piece
        tiles.append(tile)
    return tiles


def _attn_prep(z, cos, sin):
    t = z.shape[0]
    nblk = TM // BLOCK

    def body(qa_ref, qb_ref, k_ref, v_ref, cos_ref, sin_ref, qst_ref, kr_ref, vb_ref):
        cs, sn = cos_ref[...], sin_ref[...]

        def rope(x):
            return x * cs + _swap_halves(x) * sn

        tiles = []
        for tt in range(4):
            src = qa_ref if tt < 2 else qb_ref
            tiles.append(rope(src[:, (tt % 2) * BLOCK:(tt % 2 + 1) * BLOCK]) * (HEAD_DIM ** -0.5))
        _stack_heads(tiles, qst_ref, nblk)
        kr_ref[...] = _mx(rope(k_ref[...]))
        vb_ref[...] = _mx(v_ref[...])

    def col(width, j):
        return pl.BlockSpec((TM, width), lambda i: (i, j))

    return pl.pallas_call(
        body, name="attn_prep", grid=(t // TM,),
        in_specs=[col(256, 3), col(256, 4), col(128, 10), col(128, 11), col(128, 0), col(128, 0)],
        out_specs=[pl.BlockSpec((2, 4 * TM, BLOCK), lambda i: (0, i, 0)), col(128, 0), col(128, 0)],
        out_shape=[jax.ShapeDtypeStruct((2, 4 * t, BLOCK), MXU_DTYPE), jax.ShapeDtypeStruct((t, BLOCK), MXU_DTYPE),
                   jax.ShapeDtypeStruct((t, BLOCK), MXU_DTYPE)],
        compiler_params=_cparams("parallel"),
    )(z, z, z, z, cos, sin)


def _attn_dprep(dycat, ost, lst):
    t = dycat.shape[0]
    nblk = TM // BLOCK

    def body(da_ref, db_ref, o_ref, l_ref, dost_ref, ld_ref, st_s):
        tiles = []
        for tt in range(4):
            src = da_ref if tt < 2 else db_ref
            tiles.append(src[:, (tt % 2) * BLOCK:(tt % 2 + 1) * BLOCK])
        _stack_heads(tiles, st_s, nblk)
        for g in range(2):
            do = st_s[g]
            dost_ref[g] = _mx(do)
            dsum = jnp.sum(do * o_ref[g], axis=-1, keepdims=True)
            ld_ref[g] = jnp.where(_lane(do.shape) < HEAD_DIM, l_ref[g], dsum)

    stacked = pl.BlockSpec((2, 4 * TM, BLOCK), lambda i: (0, i, 0))
    return pl.pallas_call(
        body, name="attn_dprep", grid=(t // TM,),
        in_specs=[pl.BlockSpec((TM, 256), lambda i: (i, 1)), pl.BlockSpec((TM, 256), lambda i: (i, 2)), stacked, stacked],
        out_specs=[stacked, stacked],
        out_shape=[jax.ShapeDtypeStruct((2, 4 * t, BLOCK), MXU_DTYPE), jax.ShapeDtypeStruct((2, 4 * t, BLOCK), F32)],
        scratch_shapes=[pltpu.VMEM((2, 4 * TM, BLOCK), F32)],
        compiler_params=_cparams("parallel"),
    )(dycat, dycat, ost, lst)


def _attn_prep_bwd(dqst, dk, dv, cos, sin):
    t = dk.shape[0]
    nblk = TM // BLOCK

    def body(dq_ref, dk_ref, dv_ref, cos_ref, sin_ref, dz_ref):
        cs, sn = cos_ref[...], sin_ref[...]

        def rope_bwd(d):
            return d * cs + _swap_halves(d * sn)

        for tt, tile in enumerate(_unstack_heads(dq_ref, nblk)):
            dz_ref[:, tt * BLOCK:(tt + 1) * BLOCK] = _mx(rope_bwd(tile * (HEAD_DIM ** -0.5)))
        dz_ref[:, 4 * BLOCK:5 * BLOCK] = _mx(rope_bwd(dk_ref[...]))
        dz_ref[:, 5 * BLOCK:6 * BLOCK] = _mx(dv_ref[...])

    def col(width):
        return pl.BlockSpec((TM, width), lambda i: (i, 0))

    return pl.pallas_call(
        body, name="attn_prep_bwd", grid=(t // TM,),
        in_specs=[pl.BlockSpec((2, 4 * TM, BLOCK), lambda i: (0, i, 0)), col(128), col(128), col(128), col(128)],
        out_specs=col(768), out_shape=jax.ShapeDtypeStruct((t, 768), MXU_DTYPE),
        compiler_params=_cparams("parallel"),
    )(dqst, dk, dv, cos, sin)


def _nbr_specs(nb, width, col):
    return [pl.BlockSpec((BLOCK, width), lambda n: (jnp.maximum(n - 1, 0), col)),
            pl.BlockSpec((BLOCK, width), lambda n: (n, col)),
            pl.BlockSpec((BLOCK, width), lambda n: (jnp.minimum(n + 1, nb - 1), col))]


def _query_index():
    row = lax.broadcasted_iota(jnp.int32, (GROUP_ROWS, BLOCK), 0)
    return row & (BLOCK - 1), lax.broadcasted_iota(jnp.int32, (GROUP_ROWS, BLOCK), 1)


def _sink_column(sink_ref, g):
    band = lax.broadcasted_iota(jnp.int32, (GROUP_ROWS, 1), 0) // BLOCK
    col = jnp.zeros((GROUP_ROWS, 1), F32) + sink_ref[4 * g]
    for s in range(1, 4):
        col = jnp.where(band == s, sink_ref[4 * g + s], col)
    return col


def _attn_fwd(qst, kr, vb, sink):
    t = kr.shape[0]
    nb = t // BLOCK

    def body(q_ref, kp_ref, kc_ref, kn_ref, vp_ref, vc_ref, vn_ref, sink_ref, o_ref, ost_ref, lst_ref):
        n = pl.program_id(0)
        qi, kj = _query_index()
        m_prev, m_next = (kj >= qi) & (n > 0), (kj <= qi) & (n < nb - 1)
        nat = [None] * 4
        for g in range(2):
            q = q_ref[g]
            sp = jnp.where(m_prev, _dot_nt(q, kp_ref[...]), NEG)
            sc = _dot_nt(q, kc_ref[...])
            sn = jnp.where(m_next, _dot_nt(q, kn_ref[...]), NEG)
            sk = _sink_column(sink_ref, g)
            m = jnp.maximum(jnp.max(jnp.maximum(jnp.maximum(sp, sc), sn), axis=-1, keepdims=True), sk)
            pp, pc, pn = jnp.exp(sp - m), jnp.exp(sc - m), jnp.exp(sn - m)
            den = jnp.sum(pp + pc + pn, axis=-1, keepdims=True) + jnp.exp(sk - m)
            o = (_dot(_mx(pp), vp_ref[...]) + _dot(_mx(pc), vc_ref[...]) + _dot(_mx(pn), vn_ref[...])) / den
            o = jnp.where(_half(o.shape, g), o, 0.0)
            ost_ref[g] = o
            lst_ref[g] = jnp.broadcast_to(m + jnp.log(den), (GROUP_ROWS, BLOCK))
            for s in range(4):
                tt, slot = 2 * g + s // 2, s % 2
                piece = o[BLOCK * s:BLOCK * (s + 1)]
                if slot != g:
                    piece = pltpu.roll(piece, HEAD_DIM, 1)
                nat[tt] = piece if nat[tt] is None else nat[tt] + piece
        for tt in range(4):
            o_ref[:, tt * BLOCK:(tt + 1) * BLOCK] = _mx(nat[tt])

    stacked = pl.BlockSpec((2, GROUP_ROWS, BLOCK), lambda n: (0, n, 0))
    return pl.pallas_call(
        body, name="attn_fwd", grid=(nb,),
        in_specs=[stacked] + _nbr_specs(nb, BLOCK, 0) + _nbr_specs(nb, BLOCK, 0) + [pl.BlockSpec(memory_space=pltpu.SMEM)],
        out_specs=[pl.BlockSpec((BLOCK, 512), lambda n: (n, 0)), stacked, stacked],
        out_shape=[jax.ShapeDtypeStruct((t, 512), MXU_DTYPE), jax.ShapeDtypeStruct((2, 4 * t, BLOCK), F32),
                   jax.ShapeDtypeStruct((2, 4 * t, BLOCK), F32)],
        compiler_params=_cparams("parallel"),
    )(qst, kr, kr, kr, vb, vb, vb, sink)


def _lse_and_dsum(ld):
    return ld[:, 0:1], pltpu.roll(ld, HEAD_DIM, 1)[:, 0:1]


def _attn_bwd(qst, kr, vb, dost, ld, sink):
    t = kr.shape[0]
    nb = t // BLOCK

    def body(q_ref, kp_ref, kc_ref, kn_ref, vp_ref, vc_ref, vn_ref, do_ref, ld_ref, sink_ref,
             dq_ref, dk_ref, dv_ref, ds_ref):
        n = pl.program_id(0)

        @pl.when(n == 0)
        def _():
            ds_ref[...] = jnp.zeros_like(ds_ref)
            dk_ref[...] = jnp.zeros_like(dk_ref)
            dv_ref[...] = jnp.zeros_like(dv_ref)

        qi, kj = _query_index()
        m_prev, m_next = (kj >= qi) & (n > 0), (kj <= qi) & (n < nb - 1)
        key_rows = [pl.ds(pl.multiple_of(jnp.clip(n - 1 + b, 0, nb - 1) * BLOCK, BLOCK), BLOCK) for b in range(3)]
        for g in range(2):
            q, do = q_ref[g], do_ref[g]
            lse, dsum = _lse_and_dsum(ld_ref[g])
            acc = jnp.zeros((GROUP_ROWS, BLOCK), F32)
            for b, (k_ref, v_ref, valid) in enumerate(((kp_ref, vp_ref, m_prev), (kc_ref, vc_ref, None),
                                                       (kn_ref, vn_ref, m_next))):
                sc = _dot_nt(q, k_ref[...])
                if valid is not None:
                    sc = jnp.where(valid, sc, NEG)
                p = jnp.exp(sc - lse)
                dsc = _mx(p * (_dot_nt(do, v_ref[...]) - dsum))
                acc += _dot(dsc, k_ref[...])
                dv_ref[key_rows[b], :] += _dot_tn(_mx(p), do)
                dk_ref[key_rows[b], :] += _dot_tn(dsc, q)
            dq_ref[g] = jnp.where(_half(acc.shape, g), acc, 0.0)
            dsk = jnp.exp(_sink_column(sink_ref, g) - lse) * dsum
            for s in range(4):
                h = 4 * g + s
                ds_ref[h:h + 1, :] -= jnp.sum(dsk[BLOCK * s:BLOCK * (s + 1)], axis=0, keepdims=True)

    stacked = pl.BlockSpec((2, GROUP_ROWS, BLOCK), lambda n: (0, n, 0))
    whole = pl.BlockSpec((t, BLOCK), lambda n: (0, 0))
    return pl.pallas_call(
        body, name="attn_bwd", grid=(nb,),
        in_specs=[stacked] + _nbr_specs(nb, BLOCK, 0) + _nbr_specs(nb, BLOCK, 0)
        + [stacked, stacked, pl.BlockSpec(memory_space=pltpu.SMEM)],
        out_specs=[stacked, whole, whole, pl.BlockSpec((8, BLOCK), lambda n: (0, 0))],
        out_shape=[jax.ShapeDtypeStruct((2, 4 * t, BLOCK), F32), jax.ShapeDtypeStruct((t, BLOCK), F32),
                   jax.ShapeDtypeStruct((t, BLOCK), F32), jax.ShapeDtypeStruct((8, BLOCK), F32)],
        compiler_params=_cparams("arbitrary"),
    )(qst, kr, kr, kr, vb, vb, vb, dost, ld, sink)


def _loss_head(y, target, r, gamma):
    t = y.shape[0]

    def body(y_ref, t_ref, r_ref, g_ref, l_ref, res_ref, do_ref, dgb_ref):
        @pl.when(pl.program_id(0) == 0)
        def _():
            l_ref[...] = jnp.zeros_like(l_ref)

        e = y_ref[...] - t_ref[...]
        l_ref[...] += 0.5 * jnp.sum(_mean(e * e))
        _norm_bwd_tail(e / D_MODEL, r_ref, g_ref, res_ref, do_ref, dgb_ref)

    row = pl.BlockSpec((TM, D_MODEL), lambda i: (i, 0))
    tail_in, tail_out, tail_shape = _norm_tail_specs(t)
    return pl.pallas_call(
        body, name="loss_head", grid=(t // TM,), in_specs=[row, row] + tail_in,
        out_specs=[pl.BlockSpec((8, 128), lambda i: (0, 0))] + tail_out,
        out_shape=[jax.ShapeDtypeStruct((8, 128), F32)] + tail_shape,
        compiler_params=_cparams("arbitrary"),
    )(y, target, r, gamma)


def _adamw(name, w, g, m, v, rows):
    n, width = w.shape

    def body(w_ref, g_ref, m_ref, v_ref, d_ref, nm_ref, nv_ref):
        g = g_ref[...]
        m = ADAM_B1 * m_ref[...] + (1.0 - ADAM_B1) * g
        v = ADAM_B2 * v_ref[...] + (1.0 - ADAM_B2) * jnp.square(g)
        m_hat = m / (1.0 - ADAM_B1 ** ADAM_STEP)
        v_hat = v / (1.0 - ADAM_B2 ** ADAM_STEP)
        d_ref[...] = -ADAM_LR * (m_hat / (jnp.sqrt(v_hat) + ADAM_EPS) + ADAM_WD * w_ref[...])
        nm_ref[...] = m
        nv_ref[...] = v

    spec = pl.BlockSpec((rows, width), lambda i: (i, 0))
    return pl.pallas_call(
        body, name=name, grid=(n // rows,), in_specs=[spec] * 4, out_specs=[spec] * 3,
        out_shape=[jax.ShapeDtypeStruct((n, width), F32)] * 3, compiler_params=_cparams("parallel"),
    )(w, g, m, v)


def _place():
    x, y, c = lax.axis_index("x"), lax.axis_index("y"), lax.axis_index("c")
    chips = [(1 - x, y), (x, 1 - y), (1 - x, 1 - y)]
    return x, y, c, chips


class _Gather:
    def __init__(self, shards):
        na = len(shards)
        self.ins = list(shards)
        self.outs = [jax.ShapeDtypeStruct((N_SHARD,) + s.shape, s.dtype) for s in shards]
        self.sems = [pltpu.SemaphoreType.DMA((3 * na,))] * 4 + [pltpu.SemaphoreType.DMA((na,))]

    def _copies(self, src, dst, sems):
        send, recv, fsend, frecv, lsem = sems
        x, y, c, chips = _place()
        mine = 2 * x + y

        def local(a):
            return pltpu.make_async_copy(src[a], dst[a].at[mine], lsem.at[a])

        def ici(a, k, shard):
            cx, cy = chips[k]
            return pltpu.make_async_remote_copy(
                src_ref=src[a].at[c], dst_ref=dst[a].at[shard, c], send_sem=send.at[3 * a + k], recv_sem=recv.at[3 * a + k],
                device_id=(cx, cy, c), device_id_type=MESH)

        def d2d(a, k, half):
            cx, cy = chips[k]
            block = dst[a].at[2 * cx + cy, half]
            return pltpu.make_async_remote_copy(
                src_ref=block, dst_ref=block, send_sem=fsend.at[3 * a + k], recv_sem=frecv.at[3 * a + k],
                device_id=(x, y, 1 - c), device_id_type=MESH)

        return local, ici, d2d, mine, c, chips

    def start(self, src, dst, sems):
        local, ici, _, mine, _, _ = self._copies(src, dst, sems)
        for a in range(len(src)):
            local(a).start()
            for k in range(3):
                ici(a, k, mine).start()

    def finish(self, src, dst, sems):
        local, ici, d2d, mine, c, chips = self._copies(src, dst, sems)
        for a in range(len(src)):
            for k, (cx, cy) in enumerate(chips):
                ici(a, k, 2 * cx + cy).wait_recv()
                d2d(a, k, c).start()
        for a in range(len(src)):
            for k in range(3):
                d2d(a, k, 1 - c).wait_recv()
        for a in range(len(src)):
            for k in range(3):
                ici(a, k, mine).wait_send()
                d2d(a, k, c).wait_send()
            local(a).wait()


class _PairExchange:
    def __init__(self, parts):
        self.ins = list(parts)
        self.outs = [jax.ShapeDtypeStruct((N_SHARD,) + p.shape[2:], p.dtype) for p in parts]
        self.sems = [pltpu.SemaphoreType.DMA((len(parts),))] * 2

    def _copy(self, a, src, dst, sems):
        x, y, c, _ = _place()
        return pltpu.make_async_remote_copy(
            src_ref=src[a].at[:, 1 - c], dst_ref=dst[a], send_sem=sems[0].at[a], recv_sem=sems[1].at[a],
            device_id=(x, y, 1 - c), device_id_type=MESH)

    def start(self, src, dst, sems):
        for a in range(len(src)):
            self._copy(a, src, dst, sems).start()

    def finish(self, src, dst, sems):
        for a in range(len(src)):
            self._copy(a, src, dst, sems).wait()


class _ChipExchange:
    def __init__(self, sums):
        self.ins = list(sums)
        self.outs = [jax.ShapeDtypeStruct((3,) + s.shape[1:], s.dtype) for s in sums]
        self.sems = [pltpu.SemaphoreType.DMA((3 * len(sums),))] * 2

    def _copy(self, a, k, src, dst, sems):
        _, _, c, chips = _place()
        cx, cy = chips[k]
        return pltpu.make_async_remote_copy(
            src_ref=src[a].at[2 * cx + cy], dst_ref=dst[a].at[k], send_sem=sems[0].at[3 * a + k],
            recv_sem=sems[1].at[3 * a + k], device_id=(cx, cy, c), device_id_type=MESH)

    def start(self, src, dst, sems):
        for a in range(len(src)):
            for k in range(3):
                self._copy(a, k, src, dst, sems).start()

    def finish(self, src, dst, sems):
        for a in range(len(src)):
            for k in range(3):
                self._copy(a, k, src, dst, sems).wait()


class _Both:
    def __init__(self, a, b):
        self.a, self.b = a, b
        self.ins, self.outs, self.sems = a.ins + b.ins, a.outs + b.outs, a.sems + b.sems

    def _each(self, method, ins, outs, sems):
        a = self.a
        getattr(a, method)(ins[:len(a.ins)], outs[:len(a.outs)], sems[:len(a.sems)])
        getattr(self.b, method)(ins[len(a.ins):], outs[len(a.outs):], sems[len(a.sems):])

    def start(self, ins, outs, sems):
        self._each("start", ins, outs, sems)

    def finish(self, ins, outs, sems):
        self._each("finish", ins, outs, sems)


def _run(name, rider):
    n_in, n_out = len(rider.ins), len(rider.outs)

    def body(*refs):
        ins, outs, sems = refs[:n_in], refs[n_in:n_in + n_out], refs[n_in + n_out:]
        rider.start(ins, outs, sems)
        rider.finish(ins, outs, sems)

    return list(pl.pallas_call(
        body, name=name, in_specs=[ANY] * n_in, out_specs=[ANY] * n_out, out_shape=rider.outs,
        scratch_shapes=rider.sems)(*rider.ins))


def _pair_share(halves):
    na = len(halves)

    def body(*refs):
        dst = refs[na:2 * na]
        send, recv = refs[2 * na:]
        x, y, c, _ = _place()
        cps = []
        for a in range(na):
            cp = pltpu.make_async_remote_copy(
                src_ref=dst[a].at[:, c], dst_ref=dst[a].at[:, c], send_sem=send.at[a], recv_sem=recv.at[a],
                device_id=(x, y, 1 - c), device_id_type=MESH)
            cp.start()
            cps.append(cp)
        for a in range(na):
            cps[a].wait_send()
            pltpu.make_async_remote_copy(
                src_ref=dst[a].at[:, 1 - c], dst_ref=dst[a].at[:, 1 - c], send_sem=send.at[a], recv_sem=recv.at[a],
                device_id=(x, y, 1 - c), device_id_type=MESH).wait_recv()

    return pl.pallas_call(
        body, name="pair_share", in_specs=[ANY] * na, out_specs=[ANY] * na,
        out_shape=[jax.ShapeDtypeStruct(h.shape, h.dtype) for h in halves],
        input_output_aliases={a: a for a in range(na)},
        scratch_shapes=[pltpu.SemaphoreType.DMA((na,))] * 2,
    )(*halves)


def _sum_rows(r):
    return r if r <= 352 else 256


def _pair_sum(name, part, got):
    _, _, r, w = part.shape
    rows = _sum_rows(r)
    c = lax.axis_index("c").astype(jnp.int32).reshape(1)

    def body(c_ref, p_ref, g_ref, o_ref):
        o_ref[...] = _mx(p_ref[...] + g_ref[...])

    spec = pl.BlockSpec((None, rows, w), lambda j, i, c_ref: (j, i, 0))
    return pl.pallas_call(
        body, name=name, out_shape=jax.ShapeDtypeStruct((N_SHARD, r, w), MXU_DTYPE),
        grid_spec=pltpu.PrefetchScalarGridSpec(
            num_scalar_prefetch=1, grid=(N_SHARD, r // rows),
            in_specs=[pl.BlockSpec((None, None, rows, w), lambda j, i, c_ref: (j, c_ref[0], i, 0)), spec],
            out_specs=spec),
        compiler_params=_cparams("parallel", "parallel"),
    )(c, part, got)


def _chip_sum(name, part, got, others, l, prev):
    _, _, r, w = part.shape
    rows = _sum_rows(r)
    cj = jnp.stack([lax.axis_index("c"), 2 * lax.axis_index("x") + lax.axis_index("y")]).astype(jnp.int32)

    def body(cj_ref, p_ref, g_ref, o_ref, *rest):
        acc = p_ref[...] + g_ref[...]
        for k in range(3):
            acc += o_ref[k].astype(F32)
        rest[-1][...] = acc

    ins, specs, alias = [cj, part, got, others], [], {}
    if prev is not None:
        ins.append(prev)
        specs.append(ANY)
        alias = {4: 0}
    return pl.pallas_call(
        body, name=name, out_shape=jax.ShapeDtypeStruct((2, 2, r, w), F32), input_output_aliases=alias,
        grid_spec=pltpu.PrefetchScalarGridSpec(
            num_scalar_prefetch=1, grid=(r // rows,),
            in_specs=[pl.BlockSpec((None, None, rows, w), lambda i, cj: (cj[1], cj[0], i, 0)),
                      pl.BlockSpec((None, rows, w), lambda i, cj: (cj[1], i, 0)),
                      pl.BlockSpec((3, rows, w), lambda i, cj: (0, i, 0))] + specs,
            out_specs=pl.BlockSpec((None, None, rows, w), lambda i, cj: (l, cj[0], i, 0))),
        compiler_params=_cparams("parallel"),
    )(*ins)


SMALL_ROWS = 40


def _sum_small(part):
    def body(p_ref, o_ref, land, send, recv):
        x, y, c, _ = _place()
        me = 4 * x + 2 * y + c
        cps = []
        for r in range(1, 8):
            cp = pltpu.make_async_remote_copy(
                src_ref=p_ref, dst_ref=land.at[r], send_sem=send.at[r], recv_sem=recv.at[r],
                device_id=(x ^ (r >> 2), y ^ ((r >> 1) & 1), c ^ (r & 1)), device_id_type=MESH)
            cp.start()
            cps.append(cp)
        land[0] = p_ref[...]
        for cp in cps:
            cp.wait()
        acc = land[me]
        for e in range(1, 8):
            acc += land[me ^ e]
        o_ref[...] = acc

    return pl.pallas_call(
        body, name="sum_small", in_specs=[pl.BlockSpec(memory_space=pltpu.VMEM)],
        out_specs=pl.BlockSpec(memory_space=pltpu.VMEM), out_shape=jax.ShapeDtypeStruct(part.shape, F32),
        scratch_shapes=[pltpu.VMEM((8,) + part.shape, F32), pltpu.SemaphoreType.DMA((8,)), pltpu.SemaphoreType.DMA((8,))],
    )(part)


BIG = ("ffn1_w_gu", "ffn1_w_down", "w_in", "w_out", "ffn2_w_gu", "ffn2_w_down")
SMALL = ("ln1_g", "ln1_b", "ln2_g", "ln2_b", "ln3_g", "ln3_b", "attn_sink", "cc_conv_b", "cc_ln_g", "cc_ln_b",
         "sc_conv_w", "cc_conv_w")
NAMES = ("ffn1_w_gu", "ffn1_w_down", "ln1_g", "ln1_b", "w_in", "sc_conv_w", "attn_sink", "cc_conv_w", "cc_conv_b",
         "cc_ln_g", "cc_ln_b", "w_out", "ln2_g", "ln2_b", "ffn2_w_gu", "ffn2_w_down", "ln3_g", "ln3_b")


def _rope_tables(t):
    half = HEAD_DIM // 2
    inv_freq = ROPE_THETA ** (-jnp.arange(half, dtype=F32) / half)
    ang = jnp.arange(t).astype(F32)[:, None] * inv_freq[None, :]
    cos, sin = jnp.cos(ang), jnp.sin(ang)
    return jnp.tile(jnp.concatenate([cos, cos], axis=1), (1, 2)), jnp.tile(jnp.concatenate([-sin, sin], axis=1), (1, 2))


def _pack_small(vals):
    flat = jnp.concatenate([vals[n].reshape(-1) for n in SMALL])
    return jnp.pad(flat, (0, SMALL_ROWS * D_MODEL - flat.shape[0])).reshape(SMALL_ROWS, D_MODEL)


def _unpack_small(packed, shapes):
    flat, out, at = packed.reshape(-1), {}, 0
    for n in SMALL:
        size = int(np.prod(shapes[n]))
        out[n] = flat[at:at + size].reshape(shapes[n])
        at += size
    return out


def kernel(x, ffn1_w_gu, ffn1_w_down, ln1_g, ln1_b, w_in, sc_conv_w, attn_sink, cc_conv_w, cc_conv_b, cc_ln_g, cc_ln_b, w_out, ln2_g, ln2_b, ffn2_w_gu, ffn2_w_down, ln3_g, ln3_b, loss_target, m_ffn1_w_gu, m_ffn1_w_down, m_ln1_g, m_ln1_b, m_w_in, m_sc_conv_w, m_attn_sink, m_cc_conv_w, m_cc_conv_b, m_cc_ln_g, m_cc_ln_b, m_w_out, m_ln2_g, m_ln2_b, m_ffn2_w_gu, m_ffn2_w_down, m_ln3_g, m_ln3_b, v_ffn1_w_gu, v_ffn1_w_down, v_ln1_g, v_ln1_b, v_w_in, v_sc_conv_w, v_attn_sink, v_cc_conv_w, v_cc_conv_b, v_cc_ln_g, v_cc_ln_b, v_w_out, v_ln2_g, v_ln2_b, v_ffn2_w_gu, v_ffn2_w_down, v_ln3_g, v_ln3_b):
    given = dict(locals())
    w = {n: given[n] for n in NAMES}
    mom = {n: given["m_" + n] for n in NAMES}
    var = {n: given["v_" + n] for n in NAMES}
    x0 = x[0]
    target = loss_target[0]
    t = x0.shape[0]
    chip = 2 * lax.axis_index("x") + lax.axis_index("y")

    conv_shard = jnp.pad(jnp.concatenate([sc_conv_w, cc_conv_w], axis=1), ((0, 0), (0, 30), (0, 64)))
    local = {n: _mx(w[n]) for n in BIG}
    local["conv"] = conv_shard
    full = [{}, {}]

    def gather(l, names):
        return _Gather([local[n][l].reshape(2, local[n].shape[1] // 2, local[n].shape[2]) for n in names])

    def land(l, names, arrays):
        for n, a in zip(names, arrays):
            full[l][n] = a.reshape(1, N_SHARD, 2 * a.shape[2], a.shape[3])

    def weights(l):
        f = full[l]
        conv = jnp.transpose(f["conv"][0, :, :SC_W + CC_W, :64], (1, 0, 2)).reshape(SC_W + CC_W, D_CONV)
        return dict(wgu1=f["ffn1_w_gu"], wd1=f["ffn1_w_down"].reshape(1, D_FF, D_MODEL), win=f["w_in"],
                    wout=f["w_out"].reshape(1, D_MODEL, D_MODEL), wgu2=f["ffn2_w_gu"],
                    wd2=f["ffn2_w_down"].reshape(1, D_FF, D_MODEL), sc=conv[:SC_W], cc=conv[SC_W:])

    first = ("ffn1_w_gu", "ffn1_w_down")
    mixer = ("w_in", "w_out", "conv")
    second = ("ffn2_w_gu", "ffn2_w_down")
    land(0, first, _run("gather_first", gather(0, first)))
    cos, sin = _rope_tables(t)

    def vec(a, l):
        return a[l][None, :]

    acts = []
    h = x0
    for l in range(2):
        ahead = (0, mixer + second) if l == 0 else (1, second)
        (y1, r1, gu1), got = _ffn_fwd("ffn_fwd_a%d" % l, h, full[l]["ffn1_w_gu"], full[l]["ffn1_w_down"].reshape(1, D_FF, D_MODEL),
                                      vec(ln1_g, l), vec(ln1_b, l), 0, gather(*ahead))
        land(*ahead, got)
        wl = weights(l)
        z = _in_proj(y1, wl["win"], 0)
        ysc, ycc, u2 = _conv_fwd(z, wl["sc"], wl["cc"], vec(cc_conv_b, l), vec(cc_ln_g, l), vec(cc_ln_b, l))
        qs, kf, vf = _attn_prep(z, cos, sin)
        o_nat, o, lse = _attn_fwd(qs, kf, vf, attn_sink[l])
        ycat = jnp.concatenate([ysc, o_nat, ycc], axis=1)
        y2, r2 = _out_proj(ycat, y1, wl["wout"], vec(ln2_g, l), vec(ln2_b, l), 0)
        ahead = (1, first + mixer) if l == 0 else None
        (y3, r3, gu2), got = _ffn_fwd("ffn_fwd_b%d" % l, y2, wl["wgu2"], wl["wd2"], vec(ln3_g, l), vec(ln3_b, l), 0,
                                      gather(*ahead) if ahead else None)
        if ahead:
            land(*ahead, got)
        acts.append(dict(x=h, y1=y1, r1=r1, gu1=gu1, z=z, u2=u2, qs=qs, kf=kf, vf=vf, o=o, lse=lse, ycat=ycat,
                         y2=y2, r2=r2, gu2=gu2, r3=r3, w=wl))
        h = y3
    loss_rows, res, do, dgb3_next = _loss_head(h, target, acts[1]["r3"], vec(ln3_g, 1))
    loss = lax.psum(loss_rows[0, 0], ("x", "y", "c"))

    upper = ("ffn2_w_gu", "ffn2_w_down", "w_out")
    lower = ("w_in", "ffn1_w_gu", "ffn1_w_down")
    part = [{}, {}]
    small = [None, None]
    stage = {}
    reduced = {n: None for n in BIG}
    row = pl.BlockSpec((TM, D_MODEL), lambda n, k: (k, 0))
    deep = pl.BlockSpec((TK, D_MODEL), lambda n, k: (k, 0))

    def halves(a, r):
        return a.reshape(N_SHARD, 2, r // 2, a.shape[-1])

    def pair_rider(l, names):
        return _PairExchange([part[l][n] for n in names])

    def after_pair(l, names, got):
        stage[l, names] = (got, [_pair_sum("pair_sum_%s_%d" % (n, l), part[l][n], g) for n, g in zip(names, got)])

    def chip_rider(l, names):
        return _ChipExchange(stage[l, names][1])

    def after_chip(l, names, others):
        for n, g, o in zip(names, stage[l, names][0], others):
            reduced[n] = _chip_sum("chip_sum_%s_%d" % (n, l), part[l][n], g, o, l, reduced[n])

    def ffn_weight_grads(which, l, xin, dh, a, do, rider_gu=None, make_rider_d=None):
        out, got_gu = _mm_tn(
            "%s_dwgu_%d" % (which, l), xin, dh, deep,
            pl.BlockSpec((None, TK, FF_CHUNK), lambda n, k: (n // N_CHUNK, k, n % N_CHUNK)),
            pl.BlockSpec((None, D_MODEL, FF_CHUNK), lambda n, k: (n, 0, 0)),
            (N_SHARD, D_MODEL, GU_SHARD), (D_MODEL, FF_CHUNK), (2 * N_CHUNK, t // TK), rider_gu)
        part[l][which + "_w_gu"] = halves(out, D_MODEL)
        rider_d = make_rider_d() if make_rider_d else None
        out, got_d = _mm_tn(
            "%s_dwd_%d" % (which, l), a, do, pl.BlockSpec((TK, FF_CHUNK), lambda n, k: (k, n)), deep,
            pl.BlockSpec((FF_CHUNK, D_MODEL), lambda n, k: (n, 0)),
            (D_FF, D_MODEL), (FF_CHUNK, D_MODEL), (N_CHUNK, t // TK), rider_d)
        part[l][which + "_w_down"] = halves(out, D_FF // N_SHARD)
        return got_gu, got_d

    w_in_only, w_gu_only, w_down_only = ("w_in",), ("ffn1_w_gu",), ("ffn1_w_down",)
    for l in (1, 0):
        s = acts[l]
        wl = s["w"]
        dgb3 = dgb3_next
        if l == 0:
            dy, dh, a, got = _ffn_bwd(res, do, s["gu2"], wl["wgu2"], wl["wd2"], 0, rider=pair_rider(1, lower))
            after_pair(1, lower, got)
            got, _ = ffn_weight_grads("ffn2", l, s["y2"], dh, a, do, chip_rider(1, lower))
            after_chip(1, lower, got)
        else:
            dy, dh, a, _ = _ffn_bwd(res, do, s["gu2"], wl["wgu2"], wl["wd2"], 0)
            ffn_weight_grads("ffn2", l, s["y2"], dh, a, do)
        res, dm, dycat, dgb2 = _out_proj_bwd(dy, s["r2"], wl["wout"], vec(ln2_g, l), 0)
        out, _ = _mm_tn("dwout_%d" % l, s["ycat"], dm, row, row, pl.BlockSpec((D_MODEL, D_MODEL), lambda n, k: (0, 0)),
                        (D_MODEL, D_MODEL), (D_MODEL, D_MODEL), (1, t // TM))
        part[l]["w_out"] = halves(out, OUT_SHARD)
        dz_sc, dz_cc, dconv = _conv_bwd(s["z"], dycat, s["u2"], wl["sc"], wl["cc"], vec(cc_ln_g, l), vec(cc_ln_b, l))
        dost, ld = _attn_dprep(dycat, s["o"], s["lse"])
        dqs, dkf, dvf, dsink = _attn_bwd(s["qs"], s["kf"], s["vf"], dost, ld, attn_sink[l])
        dz_att = _attn_prep_bwd(dqs, dkf, dvf, cos, sin)
        dz = jnp.concatenate([dz_sc, dz_att, dz_cc], axis=1)
        out, got = _mm_tn(
            "dwin_%d" % l, s["y1"], dz, row, pl.BlockSpec((TM, D_IN), lambda n, k: (k, 0)),
            pl.BlockSpec((N_SHARD, D_MODEL, IN_SHARD), lambda n, k: (0, 0, 0)),
            (N_SHARD, D_MODEL, IN_SHARD), (D_MODEL, D_IN), (1, t // TM), pair_rider(l, upper), split=N_SHARD)
        part[l]["w_in"] = halves(out, D_MODEL)
        after_pair(l, upper, got)
        res, do, dgb1 = _in_proj_bwd(dz, res, wl["win"], 0, s["r1"], vec(ln1_g, l))
        if l == 1:
            (res0, do0, dgb3_next), dh, a, _ = _ffn_bwd(res, do, s["gu1"], wl["wgu1"], wl["wd1"], 0,
                                                        tail=(acts[0]["r3"], vec(ln3_g, 0)))
            got, _ = ffn_weight_grads("ffn1", l, s["x"], dh, a, do, chip_rider(l, upper))
            after_chip(l, upper, got)
            res, do = res0, do0
        else:
            dy, dh, a, got = _ffn_bwd(res, do, s["gu1"], wl["wgu1"], wl["wd1"], 0, rider=pair_rider(0, w_in_only))
            after_pair(0, w_in_only, got)
            got, got_d = ffn_weight_grads("ffn1", l, s["x"], dh, a, do, _Both(chip_rider(0, upper), chip_rider(0, w_in_only)),
                                          lambda: pair_rider(0, w_gu_only))
            n_upper = len(upper)
            after_chip(0, upper, got[:n_upper])
            after_chip(0, w_in_only, got[n_upper:])
            after_pair(0, w_gu_only, got_d)
        small[l] = dict(ln1_g=dgb1[0], ln1_b=dgb1[1], ln2_g=dgb2[0], ln2_b=dgb2[1], ln3_g=dgb3[0], ln3_b=dgb3[1],
                        attn_sink=dsink[:, 0], cc_conv_b=dconv[ROW_CCB], cc_ln_g=dconv[ROW_CCG],
                        cc_ln_b=dconv[ROW_CCBETA], sc_conv_w=dconv[ROW_SCW:ROW_SCW + SC_W],
                        cc_conv_w=dconv[ROW_CCW:ROW_CCW + CC_W])
    grad_x = dy[None]

    after_pair(0, w_down_only, _run("pair_exchange_last", pair_rider(0, w_down_only)))
    got = _run("chip_exchange_last", _Both(chip_rider(0, w_gu_only), chip_rider(0, w_down_only)))
    after_chip(0, w_gu_only, got[:1])
    after_chip(0, w_down_only, got[1:])
    grads = dict(zip(BIG, _pair_share([reduced[n] for n in BIG])))
    for n in BIG:
        grads[n] = grads[n].reshape(w[n].shape)

    small_full = {n: jnp.stack([small[0][n], small[1][n]]) for n in SMALL}
    small_sum = _unpack_small(_sum_small(_pack_small(small_full)), {n: small_full[n].shape for n in SMALL})
    for n in SMALL:
        g = small_sum[n]
        if n in ("sc_conv_w", "cc_conv_w"):
            g = lax.dynamic_slice_in_dim(g, chip * 64, 64, axis=2)
        grads[n] = g

    delta, new_m, new_v = {}, {}, {}
    for n in BIG:
        shape = w[n].shape
        two_d = (shape[0] * shape[1], shape[2])
        outs = _adamw("adamw_" + n, w[n].reshape(two_d), grads[n].reshape(two_d), mom[n].reshape(two_d),
                      var[n].reshape(two_d), 128)
        delta[n], new_m[n], new_v[n] = [a.reshape(shape) for a in outs]
    shapes = {n: w[n].shape for n in SMALL}
    outs = _adamw("adamw_small", _pack_small({n: w[n] for n in SMALL}), _pack_small({n: grads[n] for n in SMALL}),
                  _pack_small({n: mom[n] for n in SMALL}), _pack_small({n: var[n] for n in SMALL}), 8)
    for d, packed in zip((delta, new_m, new_v), outs):
        d.update(_unpack_small(packed, shapes))

    return (loss, grad_x, *[grads[n] for n in NAMES], *[delta[n] for n in NAMES], *[new_m[n] for n in NAMES],
            *[new_v[n] for n in NAMES])
```

```python
import functools

import numpy as np
import jax
import jax.numpy as jnp
from jax import lax
from jax.experimental import pallas as pl
from jax.experimental.pallas import tpu as pltpu

F32 = jnp.float32
MXU_DTYPE = jnp.bfloat16

D_MODEL = 1024
D_FF = 2816
N_SHARD = 4
D_IN = 2048
GU_SHARD = 2 * D_FF // N_SHARD
FF_CHUNK = GU_SHARD
N_CHUNK = D_FF // FF_CHUNK
IN_SHARD = D_IN // N_SHARD
OUT_SHARD = D_MODEL // N_SHARD
HEAD_DIM = 64
N_Q_HEADS = 8
BLOCK = 128
SC_W = 3
CC_W = 31
D_CONV = 256
HALO = 16
LN_EPS = 1e-5
ALPHA = (2.0 * 2) ** 0.25
NEG = -1e30
ROPE_THETA = 10000.0
ADAM_LR, ADAM_B1, ADAM_B2, ADAM_EPS, ADAM_WD, ADAM_STEP = 0.001, 0.9, 0.999, 1e-08, 0.01, 10

TM = 512
TK = 1024
TMC = 256
VMEM_LIMIT = 56 * 1024 * 1024
MESH = pl.DeviceIdType.MESH
ANY = pl.BlockSpec(memory_space=pl.ANY)


def _cparams(*sem):
    return pltpu.CompilerParams(dimension_semantics=sem, vmem_limit_bytes=VMEM_LIMIT)


def _dot(a, b):
    return jnp.dot(a, b, preferred_element_type=F32)


def _dot_nt(a, b):
    return lax.dot_general(a, b, (((1,), (1,)), ((), ())), preferred_element_type=F32)


def _dot_tn(a, b):
    return lax.dot_general(a, b, (((0,), (0,)), ((), ())), preferred_element_type=F32)


def _mx(a):
    return a.astype(MXU_DTYPE)


def _mean(a):
    return jnp.mean(a, axis=-1, keepdims=True)


def _ln_stats(r):
    xc = r - _mean(r)
    rstd = lax.rsqrt(_mean(xc * xc) + LN_EPS)
    return xc * rstd, rstd


def _ln_bwd(dy, xh, rstd, gamma):
    dxh = dy * gamma
    return rstd * (dxh - _mean(dxh) - xh * _mean(dxh * xh))


def _colsum(a):
    return jnp.sum(a, axis=0, keepdims=True)


def _sigmoid(a):
    return 1.0 / (1.0 + jnp.exp(-a))


def _call(body, args, *, name, grid, in_specs, out_specs, out_shape, scratch, sem, rider=None):
    if rider is None:
        outs = pl.pallas_call(
            body, name=name, grid=grid, in_specs=in_specs, out_specs=out_specs, out_shape=out_shape,
            scratch_shapes=scratch, compiler_params=_cparams(*sem))(*args)
        return list(outs), []
    n_in, n_out, n_sc = len(in_specs), len(out_specs), len(scratch)
    r_in, r_out = len(rider.ins), len(rider.outs)

    def carrying(*refs):
        cuts = np.cumsum([0, n_in, r_in, n_out, r_out, n_sc])
        ins, rins, outs, routs, scr = [refs[a:b] for a, b in zip(cuts[:-1], cuts[1:])]
        rsems = refs[cuts[-1]:]
        first = functools.reduce(jnp.logical_and, [pl.program_id(d) == 0 for d in range(len(grid))])
        last = functools.reduce(jnp.logical_and, [pl.program_id(d) == grid[d] - 1 for d in range(len(grid))])

        @pl.when(first)
        def _():
            rider.start(rins, routs, rsems)

        body(*ins, *outs, *scr)

        nsteps = int(np.prod(grid))
        step = functools.reduce(lambda lin, d: lin * grid[d] + pl.program_id(d), range(len(grid)), 0)
        for a, at in enumerate(rider.pass_on_steps(nsteps)):
            @pl.when(step == at)
            def _(a=a):
                rider.pass_on(a, rins, routs, rsems)

        @pl.when(last)
        def _():
            rider.finish(rins, routs, rsems)

    outs = pl.pallas_call(
        carrying, name=name, grid=grid, in_specs=list(in_specs) + [ANY] * r_in,
        out_specs=list(out_specs) + [ANY] * r_out, out_shape=list(out_shape) + list(rider.outs),
        scratch_shapes=list(scratch) + list(rider.sems), compiler_params=_cparams(*(("arbitrary",) * len(grid))),
    )(*args, *rider.ins)
    return list(outs[:n_out]), list(outs[n_out:])


def _ffn_fwd(name, x, wgu, wd, gamma, beta, l, rider=None):
    t = x.shape[0]
    nc = N_CHUNK

    def body(x_ref, wg_ref, wu_ref, wd_ref, g_ref, b_ref, y_ref, r_ref, gu_ref, xb_s, acc_s):
        c = pl.program_id(1)

        @pl.when(c == 0)
        def _():
            xb_s[...] = _mx(x_ref[...])
            acc_s[...] = jnp.zeros_like(acc_s)

        xb = xb_s[...]
        hg = _dot(xb, wg_ref[...])
        hu = _dot(xb, wu_ref[...])
        gu_ref[0] = _mx(hg)
        gu_ref[1] = _mx(hu)
        a = (hg * _sigmoid(hg)) * hu
        acc_s[...] += _dot(_mx(a), wd_ref[...])

        @pl.when(c == nc - 1)
        def _():
            r = ALPHA * x_ref[...] + 0.5 * acc_s[...]
            xh, _ = _ln_stats(r)
            r_ref[...] = r
            y_ref[...] = xh * g_ref[...] + b_ref[...]

    row = pl.BlockSpec((TM, D_MODEL), lambda i, c: (i, 0))
    vec = pl.BlockSpec((1, D_MODEL), lambda i, c: (0, 0))
    return _call(
        body, (x, wgu, wgu, wd, gamma, beta), name=name, grid=(t // TM, nc),
        in_specs=[row,
                  pl.BlockSpec((None, None, D_MODEL, FF_CHUNK), lambda i, c: (l, c, 0, 0)),
                  pl.BlockSpec((None, None, D_MODEL, FF_CHUNK), lambda i, c: (l, N_CHUNK + c, 0, 0)),
                  pl.BlockSpec((None, FF_CHUNK, D_MODEL), lambda i, c: (l, c, 0)),
                  vec, vec],
        out_specs=[row, row, pl.BlockSpec((2, TM, FF_CHUNK), lambda i, c: (0, i, c))],
        out_shape=[jax.ShapeDtypeStruct((t, D_MODEL), F32), jax.ShapeDtypeStruct((t, D_MODEL), F32),
                   jax.ShapeDtypeStruct((2, t, D_FF), MXU_DTYPE)],
        scratch=[pltpu.VMEM((TM, D_MODEL), MXU_DTYPE), pltpu.VMEM((TM, D_MODEL), F32)],
        sem=("parallel", "arbitrary"), rider=rider)


def _norm_bwd_tail(dy, r_ref, g_ref, res_ref, do_ref, dgb_ref):
    @pl.when(pl.program_id(0) == 0)
    def _():
        dgb_ref[...] = jnp.zeros_like(dgb_ref)

    xh, rstd = _ln_stats(r_ref[...])
    dr = _ln_bwd(dy, xh, rstd, g_ref[...])
    do_ref[...] = _mx(0.5 * dr)
    res_ref[...] = ALPHA * dr
    dgb_ref[0:1, :] += _colsum(dy * xh)
    dgb_ref[1:2, :] += _colsum(dy)


def _norm_tail_specs(t):
    row = pl.BlockSpec((TM, D_MODEL), lambda i: (i, 0))
    return ([row, pl.BlockSpec((1, D_MODEL), lambda i: (0, 0))],
            [row, row, pl.BlockSpec((8, D_MODEL), lambda i: (0, 0))],
            [jax.ShapeDtypeStruct((t, D_MODEL), F32), jax.ShapeDtypeStruct((t, D_MODEL), MXU_DTYPE),
             jax.ShapeDtypeStruct((8, D_MODEL), F32)])


def _ffn_bwd(res, do, gu, wgu, wd, l, tail=None, rider=None):
    t = res.shape[0]
    nc = N_CHUNK
    row1 = pl.BlockSpec((TM, D_MODEL), lambda i: (i, 0))

    def hidden_body(do_ref, gu_ref, wd_ref, dh_ref, a_ref):
        da = _dot_nt(do_ref[...], wd_ref[...])
        g = gu_ref[0].astype(F32)
        u = gu_ref[1].astype(F32)
        s = _sigmoid(g)
        sil = g * s
        a_ref[...] = _mx(sil * u)
        dh_ref[0] = _mx(da * u * (s * (1.0 + g * (1.0 - s))))
        dh_ref[1] = _mx(da * sil)

    hid = pl.BlockSpec((2, TM, FF_CHUNK), lambda c, i: (0, i, c))
    (dh, a), got = _call(
        hidden_body, (do, gu, wd), name="ffn_bwd_hidden" if rider is None else "ffn_bwd_hidden_carry", grid=(nc, t // TM),
        in_specs=[pl.BlockSpec((TM, D_MODEL), lambda c, i: (i, 0)), hid,
                  pl.BlockSpec((None, FF_CHUNK, D_MODEL), lambda c, i: (l, c, 0))],
        out_specs=[hid, pl.BlockSpec((TM, FF_CHUNK), lambda c, i: (i, c))],
        out_shape=[jax.ShapeDtypeStruct((2, t, D_FF), MXU_DTYPE), jax.ShapeDtypeStruct((t, D_FF), MXU_DTYPE)],
        scratch=[], sem=("parallel", "parallel"), rider=rider)

    def input_body(res_ref, dh_ref, w_ref, *rest):
        acc = res_ref[...]
        for j in range(N_SHARD):
            part = dh_ref[j // N_CHUNK][:, (j % N_CHUNK) * FF_CHUNK:(j % N_CHUNK + 1) * FF_CHUNK]
            acc += _dot_nt(part, w_ref[j])
        if tail is None:
            rest[0][...] = acc
        else:
            _norm_bwd_tail(acc, *rest)

    in_specs = [row1, pl.BlockSpec((2, TM, D_FF), lambda i: (0, i, 0)),
                pl.BlockSpec((None, N_SHARD, D_MODEL, GU_SHARD), lambda i: (l, 0, 0, 0))]
    if tail is None:
        dx = pl.pallas_call(
            input_body, name="ffn_bwd_input", grid=(t // TM,), in_specs=in_specs, out_specs=row1,
            out_shape=jax.ShapeDtypeStruct((t, D_MODEL), F32), compiler_params=_cparams("parallel"),
        )(res, dh, wgu)
    else:
        tail_in, tail_out, tail_shape = _norm_tail_specs(t)
        dx = pl.pallas_call(
            input_body, name="ffn_bwd_input_norm", grid=(t // TM,), in_specs=in_specs + tail_in, out_specs=tail_out,
            out_shape=tail_shape, compiler_params=_cparams("arbitrary"),
        )(res, dh, wgu, *tail)
    return dx, dh, a, got


def _mm_tn(name, a, b, a_spec, b_spec, out_spec, out_shape, acc_shape, grid, rider=None, split=1):
    nk = grid[-1]
    width = acc_shape[1] // split

    def body(*refs):
        a_ref, b_ref = refs[0], refs[1]
        o_ref, acc = refs[-2], refs[-1]
        k = pl.program_id(len(grid) - 1)

        @pl.when(k == 0)
        def _():
            acc[...] = jnp.zeros_like(acc)

        acc[...] += _dot_tn(_mx(a_ref[...]), _mx(b_ref[...]))

        @pl.when(k == nk - 1)
        def _():
            if split == 1:
                o_ref[...] = acc[...]
            else:
                for j in range(split):
                    o_ref[j] = acc[:, j * width:(j + 1) * width]

    sem = ("parallel",) * (len(grid) - 1) + ("arbitrary",)
    (out,), got = _call(
        body, (a, b), name=name, grid=grid, in_specs=[a_spec, b_spec], out_specs=[out_spec],
        out_shape=[jax.ShapeDtypeStruct(out_shape, F32)], scratch=[pltpu.VMEM(acc_shape, F32)], sem=sem, rider=rider)
    return out, got


def _in_proj(x, w_in, l):
    t = x.shape[0]

    def body(x_ref, w_ref, z_ref):
        xb = _mx(x_ref[...])
        for j in range(N_SHARD):
            z_ref[:, j * IN_SHARD:(j + 1) * IN_SHARD] = _dot(xb, w_ref[j])

    return pl.pallas_call(
        body, name="in_proj", grid=(t // TM,),
        in_specs=[pl.BlockSpec((TM, D_MODEL), lambda i: (i, 0)),
                  pl.BlockSpec((None, N_SHARD, D_MODEL, IN_SHARD), lambda i: (l, 0, 0, 0))],
        out_specs=pl.BlockSpec((TM, D_IN), lambda i: (i, 0)),
        out_shape=jax.ShapeDtypeStruct((t, D_IN), F32),
        compiler_params=_cparams("parallel"),
    )(x, w_in)


def _in_proj_bwd(dz, dx_res, w_in, l, r, gamma):
    t = dz.shape[0]

    def body(dz_ref, res_ref, w_ref, *tail):
        acc = res_ref[...]
        for j in range(N_SHARD):
            acc += _dot_nt(dz_ref[:, j * IN_SHARD:(j + 1) * IN_SHARD], w_ref[j])
        _norm_bwd_tail(acc, *tail)

    row = pl.BlockSpec((TM, D_MODEL), lambda i: (i, 0))
    tail_in, tail_out, tail_shape = _norm_tail_specs(t)
    return pl.pallas_call(
        body, name="in_proj_bwd", grid=(t // TM,),
        in_specs=[pl.BlockSpec((TM, D_IN), lambda i: (i, 0)), row,
                  pl.BlockSpec((None, N_SHARD, D_MODEL, IN_SHARD), lambda i: (l, 0, 0, 0))] + tail_in,
        out_specs=tail_out, out_shape=tail_shape, compiler_params=_cparams("arbitrary"),
    )(dz, dx_res, w_in, r, gamma)


def _out_proj(ycat, x, w_out, gamma, beta, l):
    t = x.shape[0]

    def body(yc_ref, x_ref, w_ref, g_ref, b_ref, y_ref, r_ref):
        r = ALPHA * x_ref[...] + _dot(yc_ref[...], w_ref[...])
        xh, _ = _ln_stats(r)
        r_ref[...] = r
        y_ref[...] = xh * g_ref[...] + b_ref[...]

    row = pl.BlockSpec((TM, D_MODEL), lambda i: (i, 0))
    vec = pl.BlockSpec((1, D_MODEL), lambda i: (0, 0))
    return pl.pallas_call(
        body, name="out_proj", grid=(t // TM,),
        in_specs=[row, row, pl.BlockSpec((None, D_MODEL, D_MODEL), lambda i: (l, 0, 0)), vec, vec],
        out_specs=[row, row],
        out_shape=[jax.ShapeDtypeStruct((t, D_MODEL), F32)] * 2,
        compiler_params=_cparams("parallel"),
    )(ycat, x, w_out, gamma, beta)


def _out_proj_bwd(dy, r, w_out, gamma, l):
    t = dy.shape[0]

    def body(dy_ref, r_ref, w_ref, g_ref, res_ref, dm_ref, dyc_ref, dgb_ref):
        @pl.when(pl.program_id(0) == 0)
        def _():
            dgb_ref[...] = jnp.zeros_like(dgb_ref)

        xh, rstd = _ln_stats(r_ref[...])
        dy = dy_ref[...]
        dr = _ln_bwd(dy, xh, rstd, g_ref[...])
        res_ref[...] = ALPHA * dr
        dm = _mx(dr)
        dm_ref[...] = dm
        dyc_ref[...] = _dot_nt(dm, w_ref[...])
        dgb_ref[0:1, :] += _colsum(dy * xh)
        dgb_ref[1:2, :] += _colsum(dy)

    row = pl.BlockSpec((TM, D_MODEL), lambda i: (i, 0))
    return pl.pallas_call(
        body, name="out_proj_bwd", grid=(t // TM,),
        in_specs=[row, row, pl.BlockSpec((None, D_MODEL, D_MODEL), lambda i: (l, 0, 0)),
                  pl.BlockSpec((1, D_MODEL), lambda i: (0, 0))],
        out_specs=[row, row, row, pl.BlockSpec((8, D_MODEL), lambda i: (0, 0))],
        out_shape=[jax.ShapeDtypeStruct((t, D_MODEL), F32), jax.ShapeDtypeStruct((t, D_MODEL), MXU_DTYPE),
                   jax.ShapeDtypeStruct((t, D_MODEL), F32), jax.ShapeDtypeStruct((8, D_MODEL), F32)],
        compiler_params=_cparams("arbitrary"),
    )(dy, r, w_out, gamma)


def _halo_specs(t, width, col):
    per = TMC // HALO
    last = t // HALO - 1
    return [pl.BlockSpec((HALO, width), lambda i: (jnp.maximum(i * per - 1, 0), col)),
            pl.BlockSpec((TMC, width), lambda i: (i, col)),
            pl.BlockSpec((HALO, width), lambda i: (jnp.minimum((i + 1) * per, last), col))]


def _extend(refs, i, nt):
    p_ref, c_ref, n_ref = refs
    p = jnp.where(i > 0, p_ref[...].astype(F32), 0.0)
    n = jnp.where(i < nt - 1, n_ref[...].astype(F32), 0.0)
    return jnp.concatenate([p, c_ref[...].astype(F32), n], axis=0)


def _shifted_copies(src_s, dst8_s):
    n = src_s.shape[0] - 8
    for b in range(8):
        dst8_s[b, 0:n, :] = src_s[pl.ds(b, n), :]


def _window(dst8_s, start):
    return dst8_s[start % 8, pl.ds(start - start % 8, TMC), :]


def _conv_fwd(z, sc_w, cc_w, cc_cb, cc_g, cc_b):
    t = z.shape[0]
    nt = t // TMC

    def body(*refs):
        b_ref = refs[0]
        c3, h3, a3, g3 = refs[1:4], refs[4:7], refs[7:10], refs[10:13]
        scw_ref, ccw_ref, cb_ref, lg_ref, lb_ref = refs[13:18]
        ysc_ref, ycc_ref, u2_ref, e_s, e8_s = refs[18:23]
        i = pl.program_id(0)
        e_s[...] = _extend(c3, i, nt) * _extend(h3, i, nt)
        cv = jnp.zeros((TMC, D_CONV), F32)
        for k in range(SC_W):
            cv += scw_ref[k:k + 1, :] * e_s[pl.ds(HALO + k - 1, TMC), :]
        ysc_ref[...] = _mx(b_ref[...] * cv)
        e_s[...] = _extend(a3, i, nt) * _sigmoid(_extend(g3, i, nt))
        _shifted_copies(e_s, e8_s)
        u2 = jnp.zeros((TMC, D_CONV), F32) + cb_ref[...]
        for k in range(CC_W):
            u2 += ccw_ref[k:k + 1, :] * _window(e8_s, HALO + k - 15)
        u2_ref[...] = u2
        xh, _ = _ln_stats(u2)
        n = xh * lg_ref[...] + lb_ref[...]
        ycc_ref[...] = _mx(n * _sigmoid(n))

    tile = pl.BlockSpec((TMC, D_CONV), lambda i: (i, 0))
    vec = pl.BlockSpec((1, D_CONV), lambda i: (0, 0))
    in_specs = ([pl.BlockSpec((TMC, D_CONV), lambda i: (i, 0))] + _halo_specs(t, D_CONV, 1) + _halo_specs(t, D_CONV, 2)
                + _halo_specs(t, D_CONV, 6) + _halo_specs(t, D_CONV, 7)
                + [pl.BlockSpec((SC_W, D_CONV), lambda i: (0, 0)), pl.BlockSpec((CC_W, D_CONV), lambda i: (0, 0)),
                   vec, vec, vec])
    return pl.pallas_call(
        body, name="conv_fwd", grid=(nt,), in_specs=in_specs, out_specs=[tile, tile, tile],
        out_shape=[jax.ShapeDtypeStruct((t, D_CONV), MXU_DTYPE), jax.ShapeDtypeStruct((t, D_CONV), MXU_DTYPE),
                   jax.ShapeDtypeStruct((t, D_CONV), F32)],
        scratch_shapes=[pltpu.VMEM((TMC + 2 * HALO, D_CONV), F32), pltpu.VMEM((8, TMC + 2 * HALO, D_CONV), F32)],
        compiler_params=_cparams("parallel"),
    )(*([z] * 13), sc_w, cc_w, cc_cb, cc_g, cc_b)


ROW_CCW, ROW_CCB, ROW_CCG, ROW_CCBETA, ROW_SCW, CONV_ROWS = 0, 31, 32, 33, 34, 40


def _conv_bwd(z, dycat, u2, sc_w, cc_w, cc_g, cc_b):
    t = z.shape[0]
    nt = t // TMC

    def body(*refs):
        b3, c3, h3, a3, g3 = refs[0:3], refs[3:6], refs[6:9], refs[9:12], refs[12:15]
        dys3, dyc3, u3 = refs[15:18], refs[18:21], refs[21:24]
        scw_ref, ccw_ref, lg_ref, lb_ref = refs[24:28]
        dsc_ref, dcc_ref, sm_ref, e_s, f_s, e8_s, f8_s = refs[28:35]
        i = pl.program_id(0)

        @pl.when(i == 0)
        def _():
            sm_ref[...] = jnp.zeros_like(sm_ref)

        cur = pl.ds(HALO, TMC)
        e_s[...] = _extend(c3, i, nt) * _extend(h3, i, nt)
        f_s[...] = _extend(dys3, i, nt) * _extend(b3, i, nt)
        cv = jnp.zeros((TMC, D_CONV), F32)
        dp = jnp.zeros((TMC, D_CONV), F32)
        dcv = f_s[cur, :]
        for k in range(SC_W):
            win = e_s[pl.ds(HALO + k - 1, TMC), :]
            cv += scw_ref[k:k + 1, :] * win
            dp += scw_ref[k:k + 1, :] * f_s[pl.ds(HALO - k + 1, TMC), :]
            sm_ref[ROW_SCW + k:ROW_SCW + k + 1, :] += _colsum(dcv * win)
        dsc_ref[:, 0:D_CONV] = _mx(dys3[1][...] * cv)
        dsc_ref[:, D_CONV:2 * D_CONV] = _mx(dp * h3[1][...])
        dsc_ref[:, 2 * D_CONV:3 * D_CONV] = _mx(dp * c3[1][...])
        xh, rstd = _ln_stats(_extend(u3, i, nt))
        n = xh * lg_ref[...] + lb_ref[...]
        sg = _sigmoid(n)
        dn = _extend(dyc3, i, nt) * (sg * (1.0 + n * (1.0 - sg)))
        f_s[...] = _ln_bwd(dn, xh, rstd, lg_ref[...])
        sm_ref[ROW_CCG:ROW_CCG + 1, :] += _colsum((dn * xh)[HALO:HALO + TMC])
        sm_ref[ROW_CCBETA:ROW_CCBETA + 1, :] += _colsum(dn[HALO:HALO + TMC])
        sig_g = _sigmoid(_extend(g3, i, nt))
        e_s[...] = _extend(a3, i, nt) * sig_g
        _shifted_copies(e_s, e8_s)
        _shifted_copies(f_s, f8_s)
        du2 = f_s[cur, :]
        sm_ref[ROW_CCB:ROW_CCB + 1, :] += _colsum(du2)
        duu = jnp.zeros((TMC, D_CONV), F32)
        for k in range(CC_W):
            duu += ccw_ref[k:k + 1, :] * _window(f8_s, HALO + 15 - k)
            sm_ref[ROW_CCW + k:ROW_CCW + k + 1, :] += _colsum(du2 * _window(e8_s, HALO + k - 15))
        sgc = sig_g[HALO:HALO + TMC]
        dcc_ref[:, 0:D_CONV] = _mx(duu * sgc)
        dcc_ref[:, D_CONV:2 * D_CONV] = _mx(duu * a3[1][...] * sgc * (1.0 - sgc))

    vec = pl.BlockSpec((1, D_CONV), lambda i: (0, 0))
    in_specs = []
    for col in (0, 1, 2, 6, 7):
        in_specs += _halo_specs(t, D_CONV, col)
    in_specs += _halo_specs(t, D_CONV, 0) + _halo_specs(t, D_CONV, 3) + _halo_specs(t, D_CONV, 0)
    in_specs += [pl.BlockSpec((SC_W, D_CONV), lambda i: (0, 0)), pl.BlockSpec((CC_W, D_CONV), lambda i: (0, 0)), vec, vec]
    return pl.pallas_call(
        body, name="conv_bwd", grid=(nt,), in_specs=in_specs,
        out_specs=[pl.BlockSpec((TMC, 3 * D_CONV), lambda i: (i, 0)), pl.BlockSpec((TMC, 2 * D_CONV), lambda i: (i, 0)),
                   pl.BlockSpec((CONV_ROWS, D_CONV), lambda i: (0, 0))],
        out_shape=[jax.ShapeDtypeStruct((t, 3 * D_CONV), MXU_DTYPE), jax.ShapeDtypeStruct((t, 2 * D_CONV), MXU_DTYPE),
                   jax.ShapeDtypeStruct((CONV_ROWS, D_CONV), F32)],
        scratch_shapes=[pltpu.VMEM((TMC + 2 * HALO, D_CONV), F32)] * 2
        + [pltpu.VMEM((8, TMC + 2 * HALO, D_CONV), F32)] * 2,
        compiler_params=_cparams("arbitrary"),
    )(*([z] * 15), *([dycat] * 6), *([u2] * 3), sc_w, cc_w, cc_g, cc_b)


def _lane(shape):
    return lax.broadcasted_iota(jnp.int32, shape, 1)


def _swap_halves(x):
    w = x.shape[1]
    lo = (_lane(x.shape) % HEAD_DIM) < HEAD_DIM // 2
    return jnp.where(lo, pltpu.roll(x, w - HEAD_DIM // 2, 1), pltpu.roll(x, HEAD_DIM // 2, 1))


def _half(shape, g):
    lane = _lane(shape)
    return lane < HEAD_DIM if g == 0 else lane >= HEAD_DIM


GROUP_ROWS = 4 * BLOCK


def _stack_heads(tiles, out_ref, nblk):
    for tt in range(4):
        g = tt // 2
        for slot in range(2):
            s = 2 * (tt % 2) + slot
            piece = tiles[tt] if slot == g else pltpu.roll(tiles[tt], HEAD_DIM, 1)
            piece = jnp.where(_half(piece.shape, g), piece, 0.0).astype(out_ref.dtype)
            for b in range(nblk):
                at = GROUP_ROWS * b + BLOCK * s
                out_ref[g, at:at + BLOCK, :] = piece[BLOCK * b:BLOCK * (b + 1)]


def _unstack_heads(ref, nblk):
    tiles = []
    for tt in range(4):
        g = tt // 2
        tile = None
        for slot in range(2):
            s = 2 * (tt % 2) + slot
            rows = [ref[g, GROUP_ROWS * b + BLOCK * s:GROUP_ROWS * b + BLOCK * (s + 1), :] for b in range(nblk)]
            piece = rows[0] if nblk == 1 else jnp.concatenate(rows, axis=0)
            if slot != g:
                piece = pltpu.roll(piece, HEAD_DIM, 1)
            tile = piece if tile is None else tile + piece
        tiles.append(tile)
    return tiles


def _attn_prep(z, cos, sin):
    t = z.shape[0]
    nblk = TM // BLOCK

    def body(qa_ref, qb_ref, k_ref, v_ref, cos_ref, sin_ref, qst_ref, kr_ref, vb_ref):
        cs, sn = cos_ref[...], sin_ref[...]

        def rope(x):
            return x * cs + _swap_halves(x) * sn

        tiles = []
        for tt in range(4):
            src = qa_ref if tt < 2 else qb_ref
            tiles.append(rope(src[:, (tt % 2) * BLOCK:(tt % 2 + 1) * BLOCK]) * (HEAD_DIM ** -0.5))
        _stack_heads(tiles, qst_ref, nblk)
        kr_ref[...] = _mx(rope(k_ref[...]))
        vb_ref[...] = _mx(v_ref[...])

    def col(width, j):
        return pl.BlockSpec((TM, width), lambda i: (i, j))

    return pl.pallas_call(
        body, name="attn_prep", grid=(t // TM,),
        in_specs=[col(256, 3), col(256, 4), col(128, 10), col(128, 11), col(128, 0), col(128, 0)],
        out_specs=[pl.BlockSpec((2, 4 * TM, BLOCK), lambda i: (0, i, 0)), col(128, 0), col(128, 0)],
        out_shape=[jax.ShapeDtypeStruct((2, 4 * t, BLOCK), MXU_DTYPE), jax.ShapeDtypeStruct((t, BLOCK), MXU_DTYPE),
                   jax.ShapeDtypeStruct((t, BLOCK), MXU_DTYPE)],
        compiler_params=_cparams("parallel"),
    )(z, z, z, z, cos, sin)


def _attn_dprep(dycat, ost, lst):
    t = dycat.shape[0]
    nblk = TM // BLOCK

    def body(da_ref, db_ref, o_ref, l_ref, dost_ref, ld_ref, st_s):
        tiles = []
        for tt in range(4):
            src = da_ref if tt < 2 else db_ref
            tiles.append(src[:, (tt % 2) * BLOCK:(tt % 2 + 1) * BLOCK])
        _stack_heads(tiles, st_s, nblk)
        for g in range(2):
            do = st_s[g]
            dost_ref[g] = _mx(do)
            dsum = jnp.sum(do * o_ref[g], axis=-1, keepdims=True)
            ld_ref[g] = jnp.where(_lane(do.shape) < HEAD_DIM, l_ref[g], dsum)

    stacked = pl.BlockSpec((2, 4 * TM, BLOCK), lambda i: (0, i, 0))
    return pl.pallas_call(
        body, name="attn_dprep", grid=(t // TM,),
        in_specs=[pl.BlockSpec((TM, 256), lambda i: (i, 1)), pl.BlockSpec((TM, 256), lambda i: (i, 2)), stacked, stacked],
        out_specs=[stacked, stacked],
        out_shape=[jax.ShapeDtypeStruct((2, 4 * t, BLOCK), MXU_DTYPE), jax.ShapeDtypeStruct((2, 4 * t, BLOCK), F32)],
        scratch_shapes=[pltpu.VMEM((2, 4 * TM, BLOCK), F32)],
        compiler_params=_cparams("parallel"),
    )(dycat, dycat, ost, lst)


def _attn_prep_bwd(dqst, dk, dv, cos, sin):
    t = dk.shape[0]
    nblk = TM // BLOCK

    def body(dq_ref, dk_ref, dv_ref, cos_ref, sin_ref, dz_ref):
        cs, sn = cos_ref[...], sin_ref[...]

        def rope_bwd(d):
            return d * cs + _swap_halves(d * sn)

        for tt, tile in enumerate(_unstack_heads(dq_ref, nblk)):
            dz_ref[:, tt * BLOCK:(tt + 1) * BLOCK] = _mx(rope_bwd(tile * (HEAD_DIM ** -0.5)))
        dz_ref[:, 4 * BLOCK:5 * BLOCK] = _mx(rope_bwd(dk_ref[...]))
        dz_ref[:, 5 * BLOCK:6 * BLOCK] = _mx(dv_ref[...])

    def col(width):
        return pl.BlockSpec((TM, width), lambda i: (i, 0))

    return pl.pallas_call(
        body, name="attn_prep_bwd", grid=(t // TM,),
        in_specs=[pl.BlockSpec((2, 4 * TM, BLOCK), lambda i: (0, i, 0)), col(128), col(128), col(128), col(128)],
        out_specs=col(768), out_shape=jax.ShapeDtypeStruct((t, 768), MXU_DTYPE),
        compiler_params=_cparams("parallel"),
    )(dqst, dk, dv, cos, sin)


def _nbr_specs(nb, width, col):
    return [pl.BlockSpec((BLOCK, width), lambda n: (jnp.maximum(n - 1, 0), col)),
            pl.BlockSpec((BLOCK, width), lambda n: (n, col)),
            pl.BlockSpec((BLOCK, width), lambda n: (jnp.minimum(n + 1, nb - 1), col))]


def _query_index():
    row = lax.broadcasted_iota(jnp.int32, (GROUP_ROWS, BLOCK), 0)
    return row & (BLOCK - 1), lax.broadcasted_iota(jnp.int32, (GROUP_ROWS, BLOCK), 1)


def _sink_column(sink_ref, g):
    band = lax.broadcasted_iota(jnp.int32, (GROUP_ROWS, 1), 0) // BLOCK
    col = jnp.zeros((GROUP_ROWS, 1), F32) + sink_ref[4 * g]
    for s in range(1, 4):
        col = jnp.where(band == s, sink_ref[4 * g + s], col)
    return col


def _attn_fwd(qst, kr, vb, sink):
    t = kr.shape[0]
    nb = t // BLOCK

    def body(q_ref, kp_ref, kc_ref, kn_ref, vp_ref, vc_ref, vn_ref, sink_ref, o_ref, ost_ref, lst_ref):
        n = pl.program_id(0)
        qi, kj = _query_index()
        m_prev, m_next = (kj >= qi) & (n > 0), (kj <= qi) & (n < nb - 1)
        nat = [None] * 4
        for g in range(2):
            q = q_ref[g]
            sp = jnp.where(m_prev, _dot_nt(q, kp_ref[...]), NEG)
            sc = _dot_nt(q, kc_ref[...])
            sn = jnp.where(m_next, _dot_nt(q, kn_ref[...]), NEG)
            sk = _sink_column(sink_ref, g)
            m = jnp.maximum(jnp.max(jnp.maximum(jnp.maximum(sp, sc), sn), axis=-1, keepdims=True), sk)
            pp, pc, pn = jnp.exp(sp - m), jnp.exp(sc - m), jnp.exp(sn - m)
            den = jnp.sum(pp + pc + pn, axis=-1, keepdims=True) + jnp.exp(sk - m)
            o = (_dot(_mx(pp), vp_ref[...]) + _dot(_mx(pc), vc_ref[...]) + _dot(_mx(pn), vn_ref[...])) / den
            o = jnp.where(_half(o.shape, g), o, 0.0)
            ost_ref[g] = o
            lst_ref[g] = jnp.broadcast_to(m + jnp.log(den), (GROUP_ROWS, BLOCK))
            for s in range(4):
                tt, slot = 2 * g + s // 2, s % 2
                piece = o[BLOCK * s:BLOCK * (s + 1)]
                if slot != g:
                    piece = pltpu.roll(piece, HEAD_DIM, 1)
                nat[tt] = piece if nat[tt] is None else nat[tt] + piece
        for tt in range(4):
            o_ref[:, tt * BLOCK:(tt + 1) * BLOCK] = _mx(nat[tt])

    stacked = pl.BlockSpec((2, GROUP_ROWS, BLOCK), lambda n: (0, n, 0))
    return pl.pallas_call(
        body, name="attn_fwd", grid=(nb,),
        in_specs=[stacked] + _nbr_specs(nb, BLOCK, 0) + _nbr_specs(nb, BLOCK, 0) + [pl.BlockSpec(memory_space=pltpu.SMEM)],
        out_specs=[pl.BlockSpec((BLOCK, 512), lambda n: (n, 0)), stacked, stacked],
        out_shape=[jax.ShapeDtypeStruct((t, 512), MXU_DTYPE), jax.ShapeDtypeStruct((2, 4 * t, BLOCK), F32),
                   jax.ShapeDtypeStruct((2, 4 * t, BLOCK), F32)],
        compiler_params=_cparams("parallel"),
    )(qst, kr, kr, kr, vb, vb, vb, sink)


def _lse_and_dsum(ld):
    return ld[:, 0:1], pltpu.roll(ld, HEAD_DIM, 1)[:, 0:1]


def _attn_bwd(qst, kr, vb, dost, ld, sink):
    t = kr.shape[0]
    nb = t // BLOCK

    def body(q_ref, kp_ref, kc_ref, kn_ref, vp_ref, vc_ref, vn_ref, do_ref, ld_ref, sink_ref,
             dq_ref, dk_ref, dv_ref, ds_ref):
        n = pl.program_id(0)

        @pl.when(n == 0)
        def _():
            ds_ref[...] = jnp.zeros_like(ds_ref)
            dk_ref[...] = jnp.zeros_like(dk_ref)
            dv_ref[...] = jnp.zeros_like(dv_ref)

        qi, kj = _query_index()
        m_prev, m_next = (kj >= qi) & (n > 0), (kj <= qi) & (n < nb - 1)
        key_rows = [pl.ds(pl.multiple_of(jnp.clip(n - 1 + b, 0, nb - 1) * BLOCK, BLOCK), BLOCK) for b in range(3)]
        for g in range(2):
            q, do = q_ref[g], do_ref[g]
            lse, dsum = _lse_and_dsum(ld_ref[g])
            acc = jnp.zeros((GROUP_ROWS, BLOCK), F32)
            for b, (k_ref, v_ref, valid) in enumerate(((kp_ref, vp_ref, m_prev), (kc_ref, vc_ref, None),
                                                       (kn_ref, vn_ref, m_next))):
                sc = _dot_nt(q, k_ref[...])
                if valid is not None:
                    sc = jnp.where(valid, sc, NEG)
                p = jnp.exp(sc - lse)
                dsc = _mx(p * (_dot_nt(do, v_ref[...]) - dsum))
                acc += _dot(dsc, k_ref[...])
                dv_ref[key_rows[b], :] += _dot_tn(_mx(p), do)
                dk_ref[key_rows[b], :] += _dot_tn(dsc, q)
            dq_ref[g] = jnp.where(_half(acc.shape, g), acc, 0.0)
            dsk = jnp.exp(_sink_column(sink_ref, g) - lse) * dsum
            for s in range(4):
                h = 4 * g + s
                ds_ref[h:h + 1, :] -= jnp.sum(dsk[BLOCK * s:BLOCK * (s + 1)], axis=0, keepdims=True)

    stacked = pl.BlockSpec((2, GROUP_ROWS, BLOCK), lambda n: (0, n, 0))
    whole = pl.BlockSpec((t, BLOCK), lambda n: (0, 0))
    return pl.pallas_call(
        body, name="attn_bwd", grid=(nb,),
        in_specs=[stacked] + _nbr_specs(nb, BLOCK, 0) + _nbr_specs(nb, BLOCK, 0)
        + [stacked, stacked, pl.BlockSpec(memory_space=pltpu.SMEM)],
        out_specs=[stacked, whole, whole, pl.BlockSpec((8, BLOCK), lambda n: (0, 0))],
        out_shape=[jax.ShapeDtypeStruct((2, 4 * t, BLOCK), F32), jax.ShapeDtypeStruct((t, BLOCK), F32),
                   jax.ShapeDtypeStruct((t, BLOCK), F32), jax.ShapeDtypeStruct((8, BLOCK), F32)],
        compiler_params=_cparams("arbitrary"),
    )(qst, kr, kr, kr, vb, vb, vb, dost, ld, sink)


def _loss_head(y, target, r, gamma):
    t = y.shape[0]

    def body(y_ref, t_ref, r_ref, g_ref, l_ref, res_ref, do_ref, dgb_ref):
        @pl.when(pl.program_id(0) == 0)
        def _():
            l_ref[...] = jnp.zeros_like(l_ref)

        e = y_ref[...] - t_ref[...]
        l_ref[...] += 0.5 * jnp.sum(_mean(e * e))
        _norm_bwd_tail(e / D_MODEL, r_ref, g_ref, res_ref, do_ref, dgb_ref)

    row = pl.BlockSpec((TM, D_MODEL), lambda i: (i, 0))
    tail_in, tail_out, tail_shape = _norm_tail_specs(t)
    return pl.pallas_call(
        body, name="loss_head", grid=(t // TM,), in_specs=[row, row] + tail_in,
        out_specs=[pl.BlockSpec((8, 128), lambda i: (0, 0))] + tail_out,
        out_shape=[jax.ShapeDtypeStruct((8, 128), F32)] + tail_shape,
        compiler_params=_cparams("arbitrary"),
    )(y, target, r, gamma)


def _adamw(name, w, g, m, v, rows):
    n, width = w.shape

    def body(w_ref, g_ref, m_ref, v_ref, d_ref, nm_ref, nv_ref):
        g = g_ref[...]
        m = ADAM_B1 * m_ref[...] + (1.0 - ADAM_B1) * g
        v = ADAM_B2 * v_ref[...] + (1.0 - ADAM_B2) * jnp.square(g)
        m_hat = m / (1.0 - ADAM_B1 ** ADAM_STEP)
        v_hat = v / (1.0 - ADAM_B2 ** ADAM_STEP)
        d_ref[...] = -ADAM_LR * (m_hat / (jnp.sqrt(v_hat) + ADAM_EPS) + ADAM_WD * w_ref[...])
        nm_ref[...] = m
        nv_ref[...] = v

    spec = pl.BlockSpec((rows, width), lambda i: (i, 0))
    return pl.pallas_call(
        body, name=name, grid=(n // rows,), in_specs=[spec] * 4, out_specs=[spec] * 3,
        out_shape=[jax.ShapeDtypeStruct((n, width), F32)] * 3, compiler_params=_cparams("parallel"),
    )(w, g, m, v)


def _place():
    x, y, c = lax.axis_index("x"), lax.axis_index("y"), lax.axis_index("c")
    chips = [(1 - x, y), (x, 1 - y), (1 - x, 1 - y)]
    return x, y, c, chips


class _Rider:
    def pass_on_steps(self, nsteps):
        return []

    def pass_on(self, a, ins, outs, sems):
        pass


class _Gather(_Rider):
    def __init__(self, shards):
        na = len(shards)
        self.ins = list(shards)
        self.outs = [jax.ShapeDtypeStruct((N_SHARD,) + s.shape, s.dtype) for s in shards]
        self.sems = [pltpu.SemaphoreType.DMA((3 * na,))] * 4 + [pltpu.SemaphoreType.DMA((na,))]

    def _copies(self, src, dst, sems):
        send, recv, fsend, frecv, lsem = sems
        x, y, c, chips = _place()
        mine = 2 * x + y

        def local(a):
            return pltpu.make_async_copy(src[a], dst[a].at[mine], lsem.at[a])

        def ici(a, k, shard):
            cx, cy = chips[k]
            return pltpu.make_async_remote_copy(
                src_ref=src[a].at[c], dst_ref=dst[a].at[shard, c], send_sem=send.at[3 * a + k], recv_sem=recv.at[3 * a + k],
                device_id=(cx, cy, c), device_id_type=MESH)

        def d2d(a, k, half):
            cx, cy = chips[k]
            block = dst[a].at[2 * cx + cy, half]
            return pltpu.make_async_remote_copy(
                src_ref=block, dst_ref=block, send_sem=fsend.at[3 * a + k], recv_sem=frecv.at[3 * a + k],
                device_id=(x, y, 1 - c), device_id_type=MESH)

        return local, ici, d2d, mine, c, chips

    def start(self, src, dst, sems):
        local, ici, _, mine, _, _ = self._copies(src, dst, sems)
        for a in range(len(src)):
            local(a).start()
            for k in range(3):
                ici(a, k, mine).start()

    def pass_on_steps(self, nsteps):
        sizes = np.cumsum([float(np.prod(s.shape)) * s.dtype.itemsize for s in self.ins])
        steps = [min(nsteps - 1, int(nsteps * done / sizes[-1]) + 1) for done in sizes]
        steps[-1] = nsteps - 1
        return steps

    def pass_on(self, a, src, dst, sems):
        _, ici, d2d, _, c, chips = self._copies(src, dst, sems)
        for k, (cx, cy) in enumerate(chips):
            ici(a, k, 2 * cx + cy).wait_recv()
            d2d(a, k, c).start()

    def finish(self, src, dst, sems):
        local, ici, d2d, mine, c, chips = self._copies(src, dst, sems)
        for a in range(len(src)):
            for k in range(3):
                d2d(a, k, 1 - c).wait_recv()
        for a in range(len(src)):
            for k in range(3):
                ici(a, k, mine).wait_send()
                d2d(a, k, c).wait_send()
            local(a).wait()


class _PairExchange(_Rider):
    def __init__(self, parts):
        self.ins = list(parts)
        self.outs = [jax.ShapeDtypeStruct((N_SHARD,) + p.shape[2:], p.dtype) for p in parts]
        self.sems = [pltpu.SemaphoreType.DMA((len(parts),))] * 2

    def _copy(self, a, src, dst, sems):
        x, y, c, _ = _place()
        return pltpu.make_async_remote_copy(
            src_ref=src[a].at[:, 1 - c], dst_ref=dst[a], send_sem=sems[0].at[a], recv_sem=sems[1].at[a],
            device_id=(x, y, 1 - c), device_id_type=MESH)

    def start(self, src, dst, sems):
        for a in range(len(src)):
            self._copy(a, src, dst, sems).start()

    def finish(self, src, dst, sems):
        for a in range(len(src)):
            self._copy(a, src, dst, sems).wait()


class _ChipExchange(_Rider):
    def __init__(self, sums):
        self.ins = list(sums)
        self.outs = [jax.ShapeDtypeStruct((3,) + s.shape[1:], s.dtype) for s in sums]
        self.sems = [pltpu.SemaphoreType.DMA((3 * len(sums),))] * 2

    def _copy(self, a, k, src, dst, sems):
        _, _, c, chips = _place()
        cx, cy = chips[k]
        return pltpu.make_async_remote_copy(
            src_ref=src[a].at[2 * cx + cy], dst_ref=dst[a].at[k], send_sem=sems[0].at[3 * a + k],
            recv_sem=sems[1].at[3 * a + k], device_id=(cx, cy, c), device_id_type=MESH)

    def start(self, src, dst, sems):
        for a in range(len(src)):
            for k in range(3):
                self._copy(a, k, src, dst, sems).start()

    def finish(self, src, dst, sems):
        for a in range(len(src)):
            for k in range(3):
                self._copy(a, k, src, dst, sems).wait()


class _Both(_Rider):
    def __init__(self, a, b):
        self.a, self.b = a, b
        self.ins, self.outs, self.sems = a.ins + b.ins, a.outs + b.outs, a.sems + b.sems

    def _each(self, method, ins, outs, sems):
        a = self.a
        getattr(a, method)(ins[:len(a.ins)], outs[:len(a.outs)], sems[:len(a.sems)])
        getattr(self.b, method)(ins[len(a.ins):], outs[len(a.outs):], sems[len(a.sems):])

    def start(self, ins, outs, sems):
        self._each("start", ins, outs, sems)

    def finish(self, ins, outs, sems):
        self._each("finish", ins, outs, sems)


def _run(name, rider):
    n_in, n_out = len(rider.ins), len(rider.outs)

    def body(*refs):
        ins, outs, sems = refs[:n_in], refs[n_in:n_in + n_out], refs[n_in + n_out:]
        rider.start(ins, outs, sems)
        for a in range(len(rider.pass_on_steps(1))):
            rider.pass_on(a, ins, outs, sems)
        rider.finish(ins, outs, sems)

    return list(pl.pallas_call(
        body, name=name, in_specs=[ANY] * n_in, out_specs=[ANY] * n_out, out_shape=rider.outs,
        scratch_shapes=rider.sems)(*rider.ins))


def _pair_share(halves):
    na = len(halves)

    def body(*refs):
        dst = refs[na:2 * na]
        send, recv = refs[2 * na:]
        x, y, c, _ = _place()
        cps = []
        for a in range(na):
            cp = pltpu.make_async_remote_copy(
                src_ref=dst[a].at[:, c], dst_ref=dst[a].at[:, c], send_sem=send.at[a], recv_sem=recv.at[a],
                device_id=(x, y, 1 - c), device_id_type=MESH)
            cp.start()
            cps.append(cp)
        for a in range(na):
            cps[a].wait_send()
            pltpu.make_async_remote_copy(
                src_ref=dst[a].at[:, 1 - c], dst_ref=dst[a].at[:, 1 - c], send_sem=send.at[a], recv_sem=recv.at[a],
                device_id=(x, y, 1 - c), device_id_type=MESH).wait_recv()

    return pl.pallas_call(
        body, name="pair_share", in_specs=[ANY] * na, out_specs=[ANY] * na,
        out_shape=[jax.ShapeDtypeStruct(h.shape, h.dtype) for h in halves],
        input_output_aliases={a: a for a in range(na)},
        scratch_shapes=[pltpu.SemaphoreType.DMA((na,))] * 2,
    )(*halves)


def _sum_rows(r):
    return r if r <= 352 else 256


def _pair_sum(name, part, got):
    _, _, r, w = part.shape
    rows = _sum_rows(r)
    c = lax.axis_index("c").astype(jnp.int32).reshape(1)

    def body(c_ref, p_ref, g_ref, o_ref):
        o_ref[...] = _mx(p_ref[...] + g_ref[...])

    spec = pl.BlockSpec((None, rows, w), lambda j, i, c_ref: (j, i, 0))
    return pl.pallas_call(
        body, name=name, out_shape=jax.ShapeDtypeStruct((N_SHARD, r, w), MXU_DTYPE),
        grid_spec=pltpu.PrefetchScalarGridSpec(
            num_scalar_prefetch=1, grid=(N_SHARD, r // rows),
            in_specs=[pl.BlockSpec((None, None, rows, w), lambda j, i, c_ref: (j, c_ref[0], i, 0)), spec],
            out_specs=spec),
        compiler_params=_cparams("parallel", "parallel"),
    )(c, part, got)


def _chip_sum(name, part, got, others, l, prev):
    _, _, r, w = part.shape
    rows = _sum_rows(r)
    cj = jnp.stack([lax.axis_index("c"), 2 * lax.axis_index("x") + lax.axis_index("y")]).astype(jnp.int32)

    def body(cj_ref, p_ref, g_ref, o_ref, *rest):
        acc = p_ref[...] + g_ref[...]
        for k in range(3):
            acc += o_ref[k].astype(F32)
        rest[-1][...] = acc

    ins, specs, alias = [cj, part, got, others], [], {}
    if prev is not None:
        ins.append(prev)
        specs.append(ANY)
        alias = {4: 0}
    return pl.pallas_call(
        body, name=name, out_shape=jax.ShapeDtypeStruct((2, 2, r, w), F32), input_output_aliases=alias,
        grid_spec=pltpu.PrefetchScalarGridSpec(
            num_scalar_prefetch=1, grid=(r // rows,),
            in_specs=[pl.BlockSpec((None, None, rows, w), lambda i, cj: (cj[1], cj[0], i, 0)),
                      pl.BlockSpec((None, rows, w), lambda i, cj: (cj[1], i, 0)),
                      pl.BlockSpec((3, rows, w), lambda i, cj: (0, i, 0))] + specs,
            out_specs=pl.BlockSpec((None, None, rows, w), lambda i, cj: (l, cj[0], i, 0))),
        compiler_params=_cparams("parallel"),
    )(*ins)


SMALL_ROWS = 40


def _sum_small(part):
    def body(p_ref, o_ref, land, send, recv):
        x, y, c, _ = _place()
        me = 4 * x + 2 * y + c
        cps = []
        for r in range(1, 8):
            cp = pltpu.make_async_remote_copy(
                src_ref=p_ref, dst_ref=land.at[r], send_sem=send.at[r], recv_sem=recv.at[r],
                device_id=(x ^ (r >> 2), y ^ ((r >> 1) & 1), c ^ (r & 1)), device_id_type=MESH)
            cp.start()
            cps.append(cp)
        land[0] = p_ref[...]
        for cp in cps:
            cp.wait()
        acc = land[me]
        for e in range(1, 8):
            acc += land[me ^ e]
        o_ref[...] = acc

    return pl.pallas_call(
        body, name="sum_small", in_specs=[pl.BlockSpec(memory_space=pltpu.VMEM)],
        out_specs=pl.BlockSpec(memory_space=pltpu.VMEM), out_shape=jax.ShapeDtypeStruct(part.shape, F32),
        scratch_shapes=[pltpu.VMEM((8,) + part.shape, F32), pltpu.SemaphoreType.DMA((8,)), pltpu.SemaphoreType.DMA((8,))],
    )(part)


BIG = ("ffn1_w_gu", "ffn1_w_down", "w_in", "w_out", "ffn2_w_gu", "ffn2_w_down")
SMALL = ("ln1_g", "ln1_b", "ln2_g", "ln2_b", "ln3_g", "ln3_b", "attn_sink", "cc_conv_b", "cc_ln_g", "cc_ln_b",
         "sc_conv_w", "cc_conv_w")
NAMES = ("ffn1_w_gu", "ffn1_w_down", "ln1_g", "ln1_b", "w_in", "sc_conv_w", "attn_sink", "cc_conv_w", "cc_conv_b",
         "cc_ln_g", "cc_ln_b", "w_out", "ln2_g", "ln2_b", "ffn2_w_gu", "ffn2_w_down", "ln3_g", "ln3_b")


def _rope_tables(t):
    half = HEAD_DIM // 2
    inv_freq = ROPE_THETA ** (-jnp.arange(half, dtype=F32) / half)
    ang = jnp.arange(t).astype(F32)[:, None] * inv_freq[None, :]
    cos, sin = jnp.cos(ang), jnp.sin(ang)
    return jnp.tile(jnp.concatenate([cos, cos], axis=1), (1, 2)), jnp.tile(jnp.concatenate([-sin, sin], axis=1), (1, 2))


def _pack_small(vals):
    flat = jnp.concatenate([vals[n].reshape(-1) for n in SMALL])
    return jnp.pad(flat, (0, SMALL_ROWS * D_MODEL - flat.shape[0])).reshape(SMALL_ROWS, D_MODEL)


def _unpack_small(packed, shapes):
    flat, out, at = packed.reshape(-1), {}, 0
    for n in SMALL:
        size = int(np.prod(shapes[n]))
        out[n] = flat[at:at + size].reshape(shapes[n])
        at += size
    return out


def kernel(x, ffn1_w_gu, ffn1_w_down, ln1_g, ln1_b, w_in, sc_conv_w, attn_sink, cc_conv_w, cc_conv_b, cc_ln_g, cc_ln_b, w_out, ln2_g, ln2_b, ffn2_w_gu, ffn2_w_down, ln3_g, ln3_b, loss_target, m_ffn1_w_gu, m_ffn1_w_down, m_ln1_g, m_ln1_b, m_w_in, m_sc_conv_w, m_attn_sink, m_cc_conv_w, m_cc_conv_b, m_cc_ln_g, m_cc_ln_b, m_w_out, m_ln2_g, m_ln2_b, m_ffn2_w_gu, m_ffn2_w_down, m_ln3_g, m_ln3_b, v_ffn1_w_gu, v_ffn1_w_down, v_ln1_g, v_ln1_b, v_w_in, v_sc_conv_w, v_attn_sink, v_cc_conv_w, v_cc_conv_b, v_cc_ln_g, v_cc_ln_b, v_w_out, v_ln2_g, v_ln2_b, v_ffn2_w_gu, v_ffn2_w_down, v_ln3_g, v_ln3_b):
    given = dict(locals())
    w = {n: given[n] for n in NAMES}
    mom = {n: given["m_" + n] for n in NAMES}
    var = {n: given["v_" + n] for n in NAMES}
    x0 = x[0]
    target = loss_target[0]
    t = x0.shape[0]
    chip = 2 * lax.axis_index("x") + lax.axis_index("y")

    conv_shard = jnp.pad(jnp.concatenate([sc_conv_w, cc_conv_w], axis=1), ((0, 0), (0, 30), (0, 64)))
    local = {n: _mx(w[n]) for n in BIG}
    local["conv"] = conv_shard
    full = [{}, {}]

    def gather(l, names):
        return _Gather([local[n][l].reshape(2, local[n].shape[1] // 2, local[n].shape[2]) for n in names])

    def land(l, names, arrays):
        for n, a in zip(names, arrays):
            full[l][n] = a.reshape(1, N_SHARD, 2 * a.shape[2], a.shape[3])

    def weights(l):
        f = full[l]
        conv = jnp.transpose(f["conv"][0, :, :SC_W + CC_W, :64], (1, 0, 2)).reshape(SC_W + CC_W, D_CONV)
        return dict(wgu1=f["ffn1_w_gu"], wd1=f["ffn1_w_down"].reshape(1, D_FF, D_MODEL), win=f["w_in"],
                    wout=f["w_out"].reshape(1, D_MODEL, D_MODEL), wgu2=f["ffn2_w_gu"],
                    wd2=f["ffn2_w_down"].reshape(1, D_FF, D_MODEL), sc=conv[:SC_W], cc=conv[SC_W:])

    first = ("ffn1_w_gu", "ffn1_w_down")
    mixer = ("w_in", "w_out", "conv")
    second = ("ffn2_w_gu", "ffn2_w_down")
    land(0, first, _run("gather_first", gather(0, first)))
    cos, sin = _rope_tables(t)

    def vec(a, l):
        return a[l][None, :]

    acts = []
    h = x0
    for l in range(2):
        ahead = (0, mixer + second) if l == 0 else (1, second)
        (y1, r1, gu1), got = _ffn_fwd("ffn_fwd_a%d" % l, h, full[l]["ffn1_w_gu"], full[l]["ffn1_w_down"].reshape(1, D_FF, D_MODEL),
                                      vec(ln1_g, l), vec(ln1_b, l), 0, gather(*ahead))
        land(*ahead, got)
        wl = weights(l)
        z = _in_proj(y1, wl["win"], 0)
        ysc, ycc, u2 = _conv_fwd(z, wl["sc"], wl["cc"], vec(cc_conv_b, l), vec(cc_ln_g, l), vec(cc_ln_b, l))
        qs, kf, vf = _attn_prep(z, cos, sin)
        o_nat, o, lse = _attn_fwd(qs, kf, vf, attn_sink[l])
        ycat = jnp.concatenate([ysc, o_nat, ycc], axis=1)
        y2, r2 = _out_proj(ycat, y1, wl["wout"], vec(ln2_g, l), vec(ln2_b, l), 0)
        ahead = (1, first + mixer) if l == 0 else None
        (y3, r3, gu2), got = _ffn_fwd("ffn_fwd_b%d" % l, y2, wl["wgu2"], wl["wd2"], vec(ln3_g, l), vec(ln3_b, l), 0,
                                      gather(*ahead) if ahead else None)
        if ahead:
            land(*ahead, got)
        acts.append(dict(x=h, y1=y1, r1=r1, gu1=gu1, z=z, u2=u2, qs=qs, kf=kf, vf=vf, o=o, lse=lse, ycat=ycat,
                         y2=y2, r2=r2, gu2=gu2, r3=r3, w=wl))
        h = y3
    loss_rows, res, do, dgb3_next = _loss_head(h, target, acts[1]["r3"], vec(ln3_g, 1))
    loss = lax.psum(loss_rows[0, 0], ("x", "y", "c"))

    upper = ("ffn2_w_gu", "ffn2_w_down", "w_out")
    lower = ("w_in", "ffn1_w_gu", "ffn1_w_down")
    part = [{}, {}]
    small = [None, None]
    stage = {}
    reduced = {n: None for n in BIG}
    row = pl.BlockSpec((TM, D_MODEL), lambda n, k: (k, 0))
    deep = pl.BlockSpec((TK, D_MODEL), lambda n, k: (k, 0))

    def halves(a, r):
        return a.reshape(N_SHARD, 2, r // 2, a.shape[-1])

    def pair_rider(l, names):
        return _PairExchange([part[l][n] for n in names])

    def after_pair(l, names, got):
        stage[l, names] = (got, [_pair_sum("pair_sum_%s_%d" % (n, l), part[l][n], g) for n, g in zip(names, got)])

    def chip_rider(l, names):
        return _ChipExchange(stage[l, names][1])

    def after_chip(l, names, others):
        for n, g, o in zip(names, stage[l, names][0], others):
            reduced[n] = _chip_sum("chip_sum_%s_%d" % (n, l), part[l][n], g, o, l, reduced[n])

    def ffn_weight_grads(which, l, xin, dh, a, do, rider_gu=None, make_rider_d=None):
        out, got_gu = _mm_tn(
            "%s_dwgu_%d" % (which, l), xin, dh, deep,
            pl.BlockSpec((None, TK, FF_CHUNK), lambda n, k: (n // N_CHUNK, k, n % N_CHUNK)),
            pl.BlockSpec((None, D_MODEL, FF_CHUNK), lambda n, k: (n, 0, 0)),
            (N_SHARD, D_MODEL, GU_SHARD), (D_MODEL, FF_CHUNK), (2 * N_CHUNK, t // TK), rider_gu)
        part[l][which + "_w_gu"] = halves(out, D_MODEL)
        rider_d = make_rider_d() if make_rider_d else None
        out, got_d = _mm_tn(
            "%s_dwd_%d" % (which, l), a, do, pl.BlockSpec((TK, FF_CHUNK), lambda n, k: (k, n)), deep,
            pl.BlockSpec((FF_CHUNK, D_MODEL), lambda n, k: (n, 0)),
            (D_FF, D_MODEL), (FF_CHUNK, D_MODEL), (N_CHUNK, t // TK), rider_d)
        part[l][which + "_w_down"] = halves(out, D_FF // N_SHARD)
        return got_gu, got_d

    w_in_only, w_gu_only, w_down_only = ("w_in",), ("ffn1_w_gu",), ("ffn1_w_down",)
    for l in (1, 0):
        s = acts[l]
        wl = s["w"]
        dgb3 = dgb3_next
        if l == 0:
            dy, dh, a, got = _ffn_bwd(res, do, s["gu2"], wl["wgu2"], wl["wd2"], 0, rider=pair_rider(1, lower))
            after_pair(1, lower, got)
            got, _ = ffn_weight_grads("ffn2", l, s["y2"], dh, a, do, chip_rider(1, lower))
            after_chip(1, lower, got)
        else:
            dy, dh, a, _ = _ffn_bwd(res, do, s["gu2"], wl["wgu2"], wl["wd2"], 0)
            ffn_weight_grads("ffn2", l, s["y2"], dh, a, do)
        res, dm, dycat, dgb2 = _out_proj_bwd(dy, s["r2"], wl["wout"], vec(ln2_g, l), 0)
        out, _ = _mm_tn("dwout_%d" % l, s["ycat"], dm, row, row, pl.BlockSpec((D_MODEL, D_MODEL), lambda n, k: (0, 0)),
                        (D_MODEL, D_MODEL), (D_MODEL, D_MODEL), (1, t // TM))
        part[l]["w_out"] = halves(out, OUT_SHARD)
        dz_sc, dz_cc, dconv = _conv_bwd(s["z"], dycat, s["u2"], wl["sc"], wl["cc"], vec(cc_ln_g, l), vec(cc_ln_b, l))
        dost, ld = _attn_dprep(dycat, s["o"], s["lse"])
        dqs, dkf, dvf, dsink = _attn_bwd(s["qs"], s["kf"], s["vf"], dost, ld, attn_sink[l])
        dz_att = _attn_prep_bwd(dqs, dkf, dvf, cos, sin)
        dz = jnp.concatenate([dz_sc, dz_att, dz_cc], axis=1)
        out, got = _mm_tn(
            "dwin_%d" % l, s["y1"], dz, row, pl.BlockSpec((TM, D_IN), lambda n, k: (k, 0)),
            pl.BlockSpec((N_SHARD, D_MODEL, IN_SHARD), lambda n, k: (0, 0, 0)),
            (N_SHARD, D_MODEL, IN_SHARD), (D_MODEL, D_IN), (1, t // TM), pair_rider(l, upper), split=N_SHARD)
        part[l]["w_in"] = halves(out, D_MODEL)
        after_pair(l, upper, got)
        res, do, dgb1 = _in_proj_bwd(dz, res, wl["win"], 0, s["r1"], vec(ln1_g, l))
        if l == 1:
            (res0, do0, dgb3_next), dh, a, _ = _ffn_bwd(res, do, s["gu1"], wl["wgu1"], wl["wd1"], 0,
                                                        tail=(acts[0]["r3"], vec(ln3_g, 0)))
            got, _ = ffn_weight_grads("ffn1", l, s["x"], dh, a, do, chip_rider(l, upper))
            after_chip(l, upper, got)
            res, do = res0, do0
        else:
            dy, dh, a, got = _ffn_bwd(res, do, s["gu1"], wl["wgu1"], wl["wd1"], 0, rider=pair_rider(0, w_in_only))
            after_pair(0, w_in_only, got)
            got, got_d = ffn_weight_grads("ffn1", l, s["x"], dh, a, do, _Both(chip_rider(0, upper), chip_rider(0, w_in_only)),
                                          lambda: pair_rider(0, w_gu_only))
            n_upper = len(upper)
            after_chip(0, upper, got[:n_upper])
            after_chip(0, w_in_only, got[n_upper:])
            after_pair(0, w_gu_only, got_d)
        small[l] = dict(ln1_g=dgb1[0], ln1_b=dgb1[1], ln2_g=dgb2[0], ln2_b=dgb2[1], ln3_g=dgb3[0], ln3_b=dgb3[1],
                        attn_sink=dsink[:, 0], cc_conv_b=dconv[ROW_CCB], cc_ln_g=dconv[ROW_CCG],
                        cc_ln_b=dconv[ROW_CCBETA], sc_conv_w=dconv[ROW_SCW:ROW_SCW + SC_W],
                        cc_conv_w=dconv[ROW_CCW:ROW_CCW + CC_W])
    grad_x = dy[None]

    after_pair(0, w_down_only, _run("pair_exchange_last", pair_rider(0, w_down_only)))
    got = _run("chip_exchange_last", _Both(chip_rider(0, w_gu_only), chip_rider(0, w_down_only)))
    after_chip(0, w_gu_only, got[:1])
    after_chip(0, w_down_only, got[1:])
    grads = dict(zip(BIG, _pair_share([reduced[n] for n in BIG])))
    for n in BIG:
        grads[n] = grads[n].reshape(w[n].shape)

    small_full = {n: jnp.stack([small[0][n], small[1][n]]) for n in SMALL}
    small_sum = _unpack_small(_sum_small(_pack_small(small_full)), {n: small_full[n].shape for n in SMALL})
    for n in SMALL:
        g = small_sum[n]
        if n in ("sc_conv_w", "cc_conv_w"):
            g = lax.dynamic_slice_in_dim(g, chip * 64, 64, axis=2)
        grads[n] = g

    delta, new_m, new_v = {}, {}, {}
    for n in BIG:
        shape = w[n].shape
        two_d = (shape[0] * shape[1], shape[2])
        outs = _adamw("adamw_" + n, w[n].reshape(two_d), grads[n].reshape(two_d), mom[n].reshape(two_d),
                      var[n].reshape(two_d), 128)
        delta[n], new_m[n], new_v[n] = [a.reshape(shape) for a in outs]
    shapes = {n: w[n].shape for n in SMALL}
    outs = _adamw("adamw_small", _pack_small({n: w[n] for n in SMALL}), _pack_small({n: grads[n] for n in SMALL}),
                  _pack_small({n: mom[n] for n in SMALL}), _pack_small({n: var[n] for n in SMALL}), 8)
    for d, packed in zip((delta, new_m, new_v), outs):
        d.update(_unpack_small(packed, shapes))

    return (loss, grad_x, *[grads[n] for n in NAMES], *[delta[n] for n in NAMES], *[new_m[n] for n in NAMES],
            *[new_v[n] for n in NAMES])
```

```python
import functools

import numpy as np
import jax
import jax.numpy as jnp
from jax import lax
from jax.experimental import pallas as pl
from jax.experimental.pallas import tpu as pltpu

F32 = jnp.float32
MXU_DTYPE = jnp.bfloat16

D_MODEL = 1024
D_FF = 2816
N_SHARD = 4
D_IN = 2048
GU_SHARD = 2 * D_FF // N_SHARD
FF_CHUNK = GU_SHARD
N_CHUNK = D_FF // FF_CHUNK
IN_SHARD = D_IN // N_SHARD
OUT_SHARD = D_MODEL // N_SHARD
HEAD_DIM = 64
BLOCK = 128
SC_W = 3
CC_W = 31
D_CONV = 256
HALO = 16
LN_EPS = 1e-5
ALPHA = (2.0 * 2) ** 0.25
NEG = -1e30
ROPE_THETA = 10000.0
ADAM_LR, ADAM_B1, ADAM_B2, ADAM_EPS, ADAM_WD, ADAM_STEP = 0.001, 0.9, 0.999, 1e-08, 0.01, 10

TM = 512
TK = 1024
TMC = 256
VMEM_LIMIT = 56 * 1024 * 1024
MESH = pl.DeviceIdType.MESH
ANY = pl.BlockSpec(memory_space=pl.ANY)


def _cparams(*sem):
    return pltpu.CompilerParams(dimension_semantics=sem, vmem_limit_bytes=VMEM_LIMIT)


def _dot(a, b):
    return jnp.dot(a, b, preferred_element_type=F32)


def _dot_nt(a, b):
    return lax.dot_general(a, b, (((1,), (1,)), ((), ())), preferred_element_type=F32)


def _dot_tn(a, b):
    return lax.dot_general(a, b, (((0,), (0,)), ((), ())), preferred_element_type=F32)


def _mx(a):
    return a.astype(MXU_DTYPE)


def _mean(a):
    return jnp.mean(a, axis=-1, keepdims=True)


def _ln_stats(r):
    xc = r - _mean(r)
    rstd = lax.rsqrt(_mean(xc * xc) + LN_EPS)
    return xc * rstd, rstd


def _ln_bwd(dy, xh, rstd, gamma):
    dxh = dy * gamma
    return rstd * (dxh - _mean(dxh) - xh * _mean(dxh * xh))


def _colsum(a):
    return jnp.sum(a, axis=0, keepdims=True)


def _sigmoid(a):
    return 1.0 / (1.0 + jnp.exp(-a))


def _call(body, args, *, name, grid, in_specs, out_specs, out_shape, scratch, sem, rider=None):
    if rider is None:
        outs = pl.pallas_call(
            body, name=name, grid=grid, in_specs=in_specs, out_specs=out_specs, out_shape=out_shape,
            scratch_shapes=scratch, compiler_params=_cparams(*sem))(*args)
        return list(outs), []
    n_in, n_out, n_sc = len(in_specs), len(out_specs), len(scratch)
    r_in, r_out = len(rider.ins), len(rider.outs)

    def carrying(*refs):
        cuts = np.cumsum([0, n_in, r_in, n_out, r_out, n_sc])
        ins, rins, outs, routs, scr = [refs[a:b] for a, b in zip(cuts[:-1], cuts[1:])]
        rsems = refs[cuts[-1]:]
        first = functools.reduce(jnp.logical_and, [pl.program_id(d) == 0 for d in range(len(grid))])
        last = functools.reduce(jnp.logical_and, [pl.program_id(d) == grid[d] - 1 for d in range(len(grid))])

        @pl.when(first)
        def _():
            rider.start(rins, routs, rsems)

        body(*ins, *outs, *scr)

        nsteps = int(np.prod(grid))
        step = functools.reduce(lambda lin, d: lin * grid[d] + pl.program_id(d), range(len(grid)), 0)
        for a, at in enumerate(rider.pass_on_steps(nsteps)):
            @pl.when(step == at)
            def _(a=a):
                rider.pass_on(a, rins, routs, rsems)

        @pl.when(last)
        def _():
            rider.finish(rins, routs, rsems)

    outs = pl.pallas_call(
        carrying, name=name, grid=grid, in_specs=list(in_specs) + [ANY] * r_in,
        out_specs=list(out_specs) + [ANY] * r_out, out_shape=list(out_shape) + list(rider.outs),
        scratch_shapes=list(scratch) + list(rider.sems), compiler_params=_cparams(*(("arbitrary",) * len(grid))),
    )(*args, *rider.ins)
    return list(outs[:n_out]), list(outs[n_out:])


def _ffn_fwd(name, x, wgu, wd, gamma, beta, l, rider=None):
    t = x.shape[0]
    nc = N_CHUNK

    def body(x_ref, wg_ref, wu_ref, wd_ref, g_ref, b_ref, y_ref, r_ref, gu_ref, xb_s, acc_s):
        c = pl.program_id(1)

        @pl.when(c == 0)
        def _():
            xb_s[...] = _mx(x_ref[...])
            acc_s[...] = jnp.zeros_like(acc_s)

        xb = xb_s[...]
        hg = _dot(xb, wg_ref[...])
        hu = _dot(xb, wu_ref[...])
        gu_ref[0] = _mx(hg)
        gu_ref[1] = _mx(hu)
        a = (hg * _sigmoid(hg)) * hu
        acc_s[...] += _dot(_mx(a), wd_ref[...])

        @pl.when(c == nc - 1)
        def _():
            r = ALPHA * x_ref[...] + 0.5 * acc_s[...]
            xh, _ = _ln_stats(r)
            r_ref[...] = r
            y_ref[...] = xh * g_ref[...] + b_ref[...]

    row = pl.BlockSpec((TM, D_MODEL), lambda i, c: (i, 0))
    vec = pl.BlockSpec((1, D_MODEL), lambda i, c: (0, 0))
    return _call(
        body, (x, wgu, wgu, wd, gamma, beta), name=name, grid=(t // TM, nc),
        in_specs=[row,
                  pl.BlockSpec((None, None, D_MODEL, FF_CHUNK), lambda i, c: (l, c, 0, 0)),
                  pl.BlockSpec((None, None, D_MODEL, FF_CHUNK), lambda i, c: (l, N_CHUNK + c, 0, 0)),
                  pl.BlockSpec((None, FF_CHUNK, D_MODEL), lambda i, c: (l, c, 0)),
                  vec, vec],
        out_specs=[row, row, pl.BlockSpec((2, TM, FF_CHUNK), lambda i, c: (0, i, c))],
        out_shape=[jax.ShapeDtypeStruct((t, D_MODEL), F32), jax.ShapeDtypeStruct((t, D_MODEL), F32),
                   jax.ShapeDtypeStruct((2, t, D_FF), MXU_DTYPE)],
        scratch=[pltpu.VMEM((TM, D_MODEL), MXU_DTYPE), pltpu.VMEM((TM, D_MODEL), F32)],
        sem=("parallel", "arbitrary"), rider=rider)


def _norm_bwd_tail(dy, r_ref, g_ref, res_ref, do_ref, dgb_ref):
    @pl.when(pl.program_id(0) == 0)
    def _():
        dgb_ref[...] = jnp.zeros_like(dgb_ref)

    xh, rstd = _ln_stats(r_ref[...])
    dr = _ln_bwd(dy, xh, rstd, g_ref[...])
    do_ref[...] = _mx(0.5 * dr)
    res_ref[...] = ALPHA * dr
    dgb_ref[0:1, :] += _colsum(dy * xh)
    dgb_ref[1:2, :] += _colsum(dy)


def _norm_tail_specs(t):
    row = pl.BlockSpec((TM, D_MODEL), lambda i: (i, 0))
    return ([row, pl.BlockSpec((1, D_MODEL), lambda i: (0, 0))],
            [row, row, pl.BlockSpec((8, D_MODEL), lambda i: (0, 0))],
            [jax.ShapeDtypeStruct((t, D_MODEL), F32), jax.ShapeDtypeStruct((t, D_MODEL), MXU_DTYPE),
             jax.ShapeDtypeStruct((8, D_MODEL), F32)])


def _ffn_bwd(res, do, gu, wgu, wd, l, tail=None, rider=None):
    t = res.shape[0]
    nc = N_CHUNK
    row1 = pl.BlockSpec((TM, D_MODEL), lambda i: (i, 0))

    def hidden_body(do_ref, gu_ref, wd_ref, dh_ref, a_ref):
        da = _dot_nt(do_ref[...], wd_ref[...])
        g = gu_ref[0].astype(F32)
        u = gu_ref[1].astype(F32)
        s = _sigmoid(g)
        sil = g * s
        a_ref[...] = _mx(sil * u)
        dh_ref[0] = _mx(da * u * (s * (1.0 + g * (1.0 - s))))
        dh_ref[1] = _mx(da * sil)

    hid = pl.BlockSpec((2, TM, FF_CHUNK), lambda c, i: (0, i, c))
    (dh, a), got = _call(
        hidden_body, (do, gu, wd), name="ffn_bwd_hidden" if rider is None else "ffn_bwd_hidden_carry", grid=(nc, t // TM),
        in_specs=[pl.BlockSpec((TM, D_MODEL), lambda c, i: (i, 0)), hid,
                  pl.BlockSpec((None, FF_CHUNK, D_MODEL), lambda c, i: (l, c, 0))],
        out_specs=[hid, pl.BlockSpec((TM, FF_CHUNK), lambda c, i: (i, c))],
        out_shape=[jax.ShapeDtypeStruct((2, t, D_FF), MXU_DTYPE), jax.ShapeDtypeStruct((t, D_FF), MXU_DTYPE)],
        scratch=[], sem=("parallel", "parallel"), rider=rider)

    def input_body(res_ref, dh_ref, w_ref, *rest):
        acc = res_ref[...]
        for j in range(N_SHARD):
            part = dh_ref[j // N_CHUNK][:, (j % N_CHUNK) * FF_CHUNK:(j % N_CHUNK + 1) * FF_CHUNK]
            acc += _dot_nt(part, w_ref[j])
        if tail is None:
            rest[0][...] = acc
        else:
            _norm_bwd_tail(acc, *rest)

    in_specs = [row1, pl.BlockSpec((2, TM, D_FF), lambda i: (0, i, 0)),
                pl.BlockSpec((None, N_SHARD, D_MODEL, GU_SHARD), lambda i: (l, 0, 0, 0))]
    if tail is None:
        dx = pl.pallas_call(
            input_body, name="ffn_bwd_input", grid=(t // TM,), in_specs=in_specs, out_specs=row1,
            out_shape=jax.ShapeDtypeStruct((t, D_MODEL), F32), compiler_params=_cparams("parallel"),
        )(res, dh, wgu)
    else:
        tail_in, tail_out, tail_shape = _norm_tail_specs(t)
        dx = pl.pallas_call(
            input_body, name="ffn_bwd_input_norm", grid=(t // TM,), in_specs=in_specs + tail_in, out_specs=tail_out,
            out_shape=tail_shape, compiler_params=_cparams("arbitrary"),
        )(res, dh, wgu, *tail)
    return dx, dh, a, got


def _mm_tn(name, a, b, a_spec, b_spec, out_spec, out_shape, acc_shape, grid, rider=None, split=1):
    nk = grid[-1]
    width = acc_shape[1] // split

    def body(*refs):
        a_ref, b_ref = refs[0], refs[1]
        o_ref, acc = refs[-2], refs[-1]
        k = pl.program_id(len(grid) - 1)

        @pl.when(k == 0)
        def _():
            acc[...] = jnp.zeros_like(acc)

        acc[...] += _dot_tn(_mx(a_ref[...]), _mx(b_ref[...]))

        @pl.when(k == nk - 1)
        def _():
            if split == 1:
                o_ref[...] = acc[...]
            else:
                for j in range(split):
                    o_ref[j] = acc[:, j * width:(j + 1) * width]

    sem = ("parallel",) * (len(grid) - 1) + ("arbitrary",)
    (out,), got = _call(
        body, (a, b), name=name, grid=grid, in_specs=[a_spec, b_spec], out_specs=[out_spec],
        out_shape=[jax.ShapeDtypeStruct(out_shape, F32)], scratch=[pltpu.VMEM(acc_shape, F32)], sem=sem, rider=rider)
    return out, got


def _in_proj(x, w_in, l):
    t = x.shape[0]

    def body(x_ref, w_ref, z_ref):
        xb = _mx(x_ref[...])
        for j in range(N_SHARD):
            z_ref[:, j * IN_SHARD:(j + 1) * IN_SHARD] = _dot(xb, w_ref[j])

    return pl.pallas_call(
        body, name="in_proj", grid=(t // TM,),
        in_specs=[pl.BlockSpec((TM, D_MODEL), lambda i: (i, 0)),
                  pl.BlockSpec((None, N_SHARD, D_MODEL, IN_SHARD), lambda i: (l, 0, 0, 0))],
        out_specs=pl.BlockSpec((TM, D_IN), lambda i: (i, 0)),
        out_shape=jax.ShapeDtypeStruct((t, D_IN), F32),
        compiler_params=_cparams("parallel"),
    )(x, w_in)


def _in_proj_bwd(dz, dx_res, w_in, l, r, gamma):
    t = dz.shape[0]

    def body(dz_ref, res_ref, w_ref, *tail):
        acc = res_ref[...]
        for j in range(N_SHARD):
            acc += _dot_nt(dz_ref[:, j * IN_SHARD:(j + 1) * IN_SHARD], w_ref[j])
        _norm_bwd_tail(acc, *tail)

    row = pl.BlockSpec((TM, D_MODEL), lambda i: (i, 0))
    tail_in, tail_out, tail_shape = _norm_tail_specs(t)
    return pl.pallas_call(
        body, name="in_proj_bwd", grid=(t // TM,),
        in_specs=[pl.BlockSpec((TM, D_IN), lambda i: (i, 0)), row,
                  pl.BlockSpec((None, N_SHARD, D_MODEL, IN_SHARD), lambda i: (l, 0, 0, 0))] + tail_in,
        out_specs=tail_out, out_shape=tail_shape, compiler_params=_cparams("arbitrary"),
    )(dz, dx_res, w_in, r, gamma)


def _out_proj(ycat, x, w_out, gamma, beta, l):
    t = x.shape[0]

    def body(yc_ref, x_ref, w_ref, g_ref, b_ref, y_ref, r_ref):
        r = ALPHA * x_ref[...] + _dot(yc_ref[...], w_ref[...])
        xh, _ = _ln_stats(r)
        r_ref[...] = r
        y_ref[...] = xh * g_ref[...] + b_ref[...]

    row = pl.BlockSpec((TM, D_MODEL), lambda i: (i, 0))
    vec = pl.BlockSpec((1, D_MODEL), lambda i: (0, 0))
    return pl.pallas_call(
        body, name="out_proj", grid=(t // TM,),
        in_specs=[row, row, pl.BlockSpec((None, D_MODEL, D_MODEL), lambda i: (l, 0, 0)), vec, vec],
        out_specs=[row, row],
        out_shape=[jax.ShapeDtypeStruct((t, D_MODEL), F32)] * 2,
        compiler_params=_cparams("parallel"),
    )(ycat, x, w_out, gamma, beta)


def _out_proj_bwd(dy, r, w_out, gamma, l):
    t = dy.shape[0]

    def body(dy_ref, r_ref, w_ref, g_ref, res_ref, dm_ref, dyc_ref, dgb_ref):
        @pl.when(pl.program_id(0) == 0)
        def _():
            dgb_ref[...] = jnp.zeros_like(dgb_ref)

        xh, rstd = _ln_stats(r_ref[...])
        dy = dy_ref[...]
        dr = _ln_bwd(dy, xh, rstd, g_ref[...])
        res_ref[...] = ALPHA * dr
        dm = _mx(dr)
        dm_ref[...] = dm
        dyc_ref[...] = _dot_nt(dm, w_ref[...])
        dgb_ref[0:1, :] += _colsum(dy * xh)
        dgb_ref[1:2, :] += _colsum(dy)

    row = pl.BlockSpec((TM, D_MODEL), lambda i: (i, 0))
    return pl.pallas_call(
        body, name="out_proj_bwd", grid=(t // TM,),
        in_specs=[row, row, pl.BlockSpec((None, D_MODEL, D_MODEL), lambda i: (l, 0, 0)),
                  pl.BlockSpec((1, D_MODEL), lambda i: (0, 0))],
        out_specs=[row, row, row, pl.BlockSpec((8, D_MODEL), lambda i: (0, 0))],
        out_shape=[jax.ShapeDtypeStruct((t, D_MODEL), F32), jax.ShapeDtypeStruct((t, D_MODEL), MXU_DTYPE),
                   jax.ShapeDtypeStruct((t, D_MODEL), F32), jax.ShapeDtypeStruct((8, D_MODEL), F32)],
        compiler_params=_cparams("arbitrary"),
    )(dy, r, w_out, gamma)


def _halo_specs(t, width, col):
    per = TMC // HALO
    last = t // HALO - 1
    return [pl.BlockSpec((HALO, width), lambda i: (jnp.maximum(i * per - 1, 0), col)),
            pl.BlockSpec((TMC, width), lambda i: (i, col)),
            pl.BlockSpec((HALO, width), lambda i: (jnp.minimum((i + 1) * per, last), col))]


def _extend(refs, i, nt):
    p_ref, c_ref, n_ref = refs
    p = jnp.where(i > 0, p_ref[...].astype(F32), 0.0)
    n = jnp.where(i < nt - 1, n_ref[...].astype(F32), 0.0)
    return jnp.concatenate([p, c_ref[...].astype(F32), n], axis=0)


def _shifted_copies(src_s, dst8_s):
    n = src_s.shape[0] - 8
    for b in range(8):
        dst8_s[b, 0:n, :] = src_s[pl.ds(b, n), :]


def _window(dst8_s, start):
    return dst8_s[start % 8, pl.ds(start - start % 8, TMC), :]


def _conv_fwd(z, sc_w, cc_w, cc_cb, cc_g, cc_b):
    t = z.shape[0]
    nt = t // TMC

    def body(*refs):
        b_ref = refs[0]
        c3, h3, a3, g3 = refs[1:4], refs[4:7], refs[7:10], refs[10:13]
        scw_ref, ccw_ref, cb_ref, lg_ref, lb_ref = refs[13:18]
        ysc_ref, ycc_ref, u2_ref, e_s, e8_s = refs[18:23]
        i = pl.program_id(0)
        e_s[...] = _extend(c3, i, nt) * _extend(h3, i, nt)
        cv = jnp.zeros((TMC, D_CONV), F32)
        for k in range(SC_W):
            cv += scw_ref[k:k + 1, :] * e_s[pl.ds(HALO + k - 1, TMC), :]
        ysc_ref[...] = _mx(b_ref[...] * cv)
        e_s[...] = _extend(a3, i, nt) * _sigmoid(_extend(g3, i, nt))
        _shifted_copies(e_s, e8_s)
        u2 = jnp.zeros((TMC, D_CONV), F32) + cb_ref[...]
        for k in range(CC_W):
            u2 += ccw_ref[k:k + 1, :] * _window(e8_s, HALO + k - 15)
        u2_ref[...] = u2
        xh, _ = _ln_stats(u2)
        n = xh * lg_ref[...] + lb_ref[...]
        ycc_ref[...] = _mx(n * _sigmoid(n))

    tile = pl.BlockSpec((TMC, D_CONV), lambda i: (i, 0))
    vec = pl.BlockSpec((1, D_CONV), lambda i: (0, 0))
    in_specs = ([pl.BlockSpec((TMC, D_CONV), lambda i: (i, 0))] + _halo_specs(t, D_CONV, 1) + _halo_specs(t, D_CONV, 2)
                + _halo_specs(t, D_CONV, 6) + _halo_specs(t, D_CONV, 7)
                + [pl.BlockSpec((SC_W, D_CONV), lambda i: (0, 0)), pl.BlockSpec((CC_W, D_CONV), lambda i: (0, 0)),
                   vec, vec, vec])
    return pl.pallas_call(
        body, name="conv_fwd", grid=(nt,), in_specs=in_specs, out_specs=[tile, tile, tile],
        out_shape=[jax.ShapeDtypeStruct((t, D_CONV), MXU_DTYPE), jax.ShapeDtypeStruct((t, D_CONV), MXU_DTYPE),
                   jax.ShapeDtypeStruct((t, D_CONV), F32)],
        scratch_shapes=[pltpu.VMEM((TMC + 2 * HALO, D_CONV), F32), pltpu.VMEM((8, TMC + 2 * HALO, D_CONV), F32)],
        compiler_params=_cparams("parallel"),
    )(*([z] * 13), sc_w, cc_w, cc_cb, cc_g, cc_b)


ROW_CCW, ROW_CCB, ROW_CCG, ROW_CCBETA, ROW_SCW, CONV_ROWS = 0, 31, 32, 33, 34, 40


def _conv_bwd(z, dycat, u2, sc_w, cc_w, cc_g, cc_b):
    t = z.shape[0]
    nt = t // TMC

    def body(*refs):
        b3, c3, h3, a3, g3 = refs[0:3], refs[3:6], refs[6:9], refs[9:12], refs[12:15]
        dys3, dyc3, u3 = refs[15:18], refs[18:21], refs[21:24]
        scw_ref, ccw_ref, lg_ref, lb_ref = refs[24:28]
        dsc_ref, dcc_ref, sm_ref, e_s, f_s, e8_s, f8_s = refs[28:35]
        i = pl.program_id(0)

        @pl.when(i == 0)
        def _():
            sm_ref[...] = jnp.zeros_like(sm_ref)

        cur = pl.ds(HALO, TMC)
        e_s[...] = _extend(c3, i, nt) * _extend(h3, i, nt)
        f_s[...] = _extend(dys3, i, nt) * _extend(b3, i, nt)
        cv = jnp.zeros((TMC, D_CONV), F32)
        dp = jnp.zeros((TMC, D_CONV), F32)
        dcv = f_s[cur, :]
        for k in range(SC_W):
            win = e_s[pl.ds(HALO + k - 1, TMC), :]
            cv += scw_ref[k:k + 1, :] * win
            dp += scw_ref[k:k + 1, :] * f_s[pl.ds(HALO - k + 1, TMC), :]
            sm_ref[ROW_SCW + k:ROW_SCW + k + 1, :] += _colsum(dcv * win)
        dsc_ref[:, 0:D_CONV] = _mx(dys3[1][...] * cv)
        dsc_ref[:, D_CONV:2 * D_CONV] = _mx(dp * h3[1][...])
        dsc_ref[:, 2 * D_CONV:3 * D_CONV] = _mx(dp * c3[1][...])
        xh, rstd = _ln_stats(_extend(u3, i, nt))
        n = xh * lg_ref[...] + lb_ref[...]
        sg = _sigmoid(n)
        dn = _extend(dyc3, i, nt) * (sg * (1.0 + n * (1.0 - sg)))
        f_s[...] = _ln_bwd(dn, xh, rstd, lg_ref[...])
        sm_ref[ROW_CCG:ROW_CCG + 1, :] += _colsum((dn * xh)[HALO:HALO + TMC])
        sm_ref[ROW_CCBETA:ROW_CCBETA + 1, :] += _colsum(dn[HALO:HALO + TMC])
        sig_g = _sigmoid(_extend(g3, i, nt))
        e_s[...] = _extend(a3, i, nt) * sig_g
        _shifted_copies(e_s, e8_s)
        _shifted_copies(f_s, f8_s)
        du2 = f_s[cur, :]
        sm_ref[ROW_CCB:ROW_CCB + 1, :] += _colsum(du2)
        duu = jnp.zeros((TMC, D_CONV), F32)
        for k in range(CC_W):
            duu += ccw_ref[k:k + 1, :] * _window(f8_s, HALO + 15 - k)
            sm_ref[ROW_CCW + k:ROW_CCW + k + 1, :] += _colsum(du2 * _window(e8_s, HALO + k - 15))
        sgc = sig_g[HALO:HALO + TMC]
        dcc_ref[:, 0:D_CONV] = _mx(duu * sgc)
        dcc_ref[:, D_CONV:2 * D_CONV] = _mx(duu * a3[1][...] * sgc * (1.0 - sgc))

    vec = pl.BlockSpec((1, D_CONV), lambda i: (0, 0))
    in_specs = []
    for col in (0, 1, 2, 6, 7):
        in_specs += _halo_specs(t, D_CONV, col)
    in_specs += _halo_specs(t, D_CONV, 0) + _halo_specs(t, D_CONV, 3) + _halo_specs(t, D_CONV, 0)
    in_specs += [pl.BlockSpec((SC_W, D_CONV), lambda i: (0, 0)), pl.BlockSpec((CC_W, D_CONV), lambda i: (0, 0)), vec, vec]
    return pl.pallas_call(
        body, name="conv_bwd", grid=(nt,), in_specs=in_specs,
        out_specs=[pl.BlockSpec((TMC, 3 * D_CONV), lambda i: (i, 0)), pl.BlockSpec((TMC, 2 * D_CONV), lambda i: (i, 0)),
                   pl.BlockSpec((CONV_ROWS, D_CONV), lambda i: (0, 0))],
        out_shape=[jax.ShapeDtypeStruct((t, 3 * D_CONV), MXU_DTYPE), jax.ShapeDtypeStruct((t, 2 * D_CONV), MXU_DTYPE),
                   jax.ShapeDtypeStruct((CONV_ROWS, D_CONV), F32)],
        scratch_shapes=[pltpu.VMEM((TMC + 2 * HALO, D_CONV), F32)] * 2
        + [pltpu.VMEM((8, TMC + 2 * HALO, D_CONV), F32)] * 2,
        compiler_params=_cparams("arbitrary"),
    )(*([z] * 15), *([dycat] * 6), *([u2] * 3), sc_w, cc_w, cc_g, cc_b)


def _lane(shape):
    return lax.broadcasted_iota(jnp.int32, shape, 1)


def _swap_halves(x):
    w = x.shape[1]
    lo = (_lane(x.shape) % HEAD_DIM) < HEAD_DIM // 2
    return jnp.where(lo, pltpu.roll(x, w - HEAD_DIM // 2, 1), pltpu.roll(x, HEAD_DIM // 2, 1))


def _half(shape, g):
    lane = _lane(shape)
    return lane < HEAD_DIM if g == 0 else lane >= HEAD_DIM


GROUP_ROWS = 4 * BLOCK


def _stack_heads(tiles, out_ref, nblk):
    for tt in range(4):
        g = tt // 2
        for slot in range(2):
            s = 2 * (tt % 2) + slot
            piece = tiles[tt] if slot == g else pltpu.roll(tiles[tt], HEAD_DIM, 1)
            piece = jnp.where(_half(piece.shape, g), piece, 0.0).astype(out_ref.dtype)
            for b in range(nblk):
                at = GROUP_ROWS * b + BLOCK * s
                out_ref[g, at:at + BLOCK, :] = piece[BLOCK * b:BLOCK * (b + 1)]


def _unstack_heads(ref, nblk):
    tiles = []
    for tt in range(4):
        g = tt // 2
        tile = None
        for slot in range(2):
            s = 2 * (tt % 2) + slot
            rows = [ref[g, GROUP_ROWS * b + BLOCK * s:GROUP_ROWS * b + BLOCK * (s + 1), :] for b in range(nblk)]
            piece = rows[0] if nblk == 1 else jnp.concatenate(rows, axis=0)
            if slot != g:
                piece = pltpu.roll(piece, HEAD_DIM, 1)
            tile = piece if tile is None else tile + piece
        tiles.append(tile)
    return tiles


def _attn_prep(z, cos, sin):
    t = z.shape[0]
    nblk = TM // BLOCK

    def body(qa_ref, qb_ref, k_ref, v_ref, cos_ref, sin_ref, qst_ref, kr_ref, vb_ref):
        cs, sn = cos_ref[...], sin_ref[...]

        def rope(x):
            return x * cs + _swap_halves(x) * sn

        tiles = []
        for tt in range(4):
            src = qa_ref if tt < 2 else qb_ref
            tiles.append(rope(src[:, (tt % 2) * BLOCK:(tt % 2 + 1) * BLOCK]) * (HEAD_DIM ** -0.5))
        _stack_heads(tiles, qst_ref, nblk)
        kr_ref[...] = _mx(rope(k_ref[...]))
        vb_ref[...] = _mx(v_ref[...])

    def col(width, j):
        return pl.BlockSpec((TM, width), lambda i: (i, j))

    return pl.pallas_call(
        body, name="attn_prep", grid=(t // TM,),
        in_specs=[col(256, 3), col(256, 4), col(128, 10), col(128, 11), col(128, 0), col(128, 0)],
        out_specs=[pl.BlockSpec((2, 4 * TM, BLOCK), lambda i: (0, i, 0)), col(128, 0), col(128, 0)],
        out_shape=[jax.ShapeDtypeStruct((2, 4 * t, BLOCK), MXU_DTYPE), jax.ShapeDtypeStruct((t, BLOCK), MXU_DTYPE),
                   jax.ShapeDtypeStruct((t, BLOCK), MXU_DTYPE)],
        compiler_params=_cparams("parallel"),
    )(z, z, z, z, cos, sin)


def _attn_dprep(dycat, ost, lst):
    t = dycat.shape[0]
    nblk = TM // BLOCK

    def body(da_ref, db_ref, o_ref, l_ref, dost_ref, ld_ref, st_s):
        tiles = []
        for tt in range(4):
            src = da_ref if tt < 2 else db_ref
            tiles.append(src[:, (tt % 2) * BLOCK:(tt % 2 + 1) * BLOCK])
        _stack_heads(tiles, st_s, nblk)
        for g in range(2):
            do = st_s[g]
            dost_ref[g] = _mx(do)
            dsum = jnp.sum(do * o_ref[g], axis=-1, keepdims=True)
            ld_ref[g] = jnp.where(_lane(do.shape) < HEAD_DIM, l_ref[g], dsum)

    stacked = pl.BlockSpec((2, 4 * TM, BLOCK), lambda i: (0, i, 0))
    return pl.pallas_call(
        body, name="attn_dprep", grid=(t // TM,),
        in_specs=[pl.BlockSpec((TM, 256), lambda i: (i, 1)), pl.BlockSpec((TM, 256), lambda i: (i, 2)), stacked, stacked],
        out_specs=[stacked, stacked],
        out_shape=[jax.ShapeDtypeStruct((2, 4 * t, BLOCK), MXU_DTYPE), jax.ShapeDtypeStruct((2, 4 * t, BLOCK), F32)],
        scratch_shapes=[pltpu.VMEM((2, 4 * TM, BLOCK), F32)],
        compiler_params=_cparams("parallel"),
    )(dycat, dycat, ost, lst)


def _attn_prep_bwd(dqst, dk, dv, cos, sin):
    t = dk.shape[0]
    nblk = TM // BLOCK

    def body(dq_ref, dk_ref, dv_ref, cos_ref, sin_ref, dz_ref):
        cs, sn = cos_ref[...], sin_ref[...]

        def rope_bwd(d):
            return d * cs + _swap_halves(d * sn)

        for tt, tile in enumerate(_unstack_heads(dq_ref, nblk)):
            dz_ref[:, tt * BLOCK:(tt + 1) * BLOCK] = _mx(rope_bwd(tile * (HEAD_DIM ** -0.5)))
        dz_ref[:, 4 * BLOCK:5 * BLOCK] = _mx(rope_bwd(dk_ref[...]))
        dz_ref[:, 5 * BLOCK:6 * BLOCK] = _mx(dv_ref[...])

    def col(width):
        return pl.BlockSpec((TM, width), lambda i: (i, 0))

    return pl.pallas_call(
        body, name="attn_prep_bwd", grid=(t // TM,),
        in_specs=[pl.BlockSpec((2, 4 * TM, BLOCK), lambda i: (0, i, 0)), col(128), col(128), col(128), col(128)],
        out_specs=col(768), out_shape=jax.ShapeDtypeStruct((t, 768), MXU_DTYPE),
        compiler_params=_cparams("parallel"),
    )(dqst, dk, dv, cos, sin)


def _nbr_specs(nb, width, col):
    return [pl.BlockSpec((BLOCK, width), lambda n: (jnp.maximum(n - 1, 0), col)),
            pl.BlockSpec((BLOCK, width), lambda n: (n, col)),
            pl.BlockSpec((BLOCK, width), lambda n: (jnp.minimum(n + 1, nb - 1), col))]


def _query_index():
    row = lax.broadcasted_iota(jnp.int32, (GROUP_ROWS, BLOCK), 0)
    return row & (BLOCK - 1), lax.broadcasted_iota(jnp.int32, (GROUP_ROWS, BLOCK), 1)


def _sink_column(sink_ref, g):
    band = lax.broadcasted_iota(jnp.int32, (GROUP_ROWS, 1), 0) // BLOCK
    col = jnp.zeros((GROUP_ROWS, 1), F32) + sink_ref[4 * g]
    for s in range(1, 4):
        col = jnp.where(band == s, sink_ref[4 * g + s], col)
    return col


def _attn_fwd(qst, kr, vb, sink):
    t = kr.shape[0]
    nb = t // BLOCK

    def body(q_ref, kp_ref, kc_ref, kn_ref, vp_ref, vc_ref, vn_ref, sink_ref, o_ref, ost_ref, lst_ref):
        n = pl.program_id(0)
        qi, kj = _query_index()
        m_prev, m_next = (kj >= qi) & (n > 0), (kj <= qi) & (n < nb - 1)
        nat = [None] * 4
        for g in range(2):
            q = q_ref[g]
            sp = jnp.where(m_prev, _dot_nt(q, kp_ref[...]), NEG)
            sc = _dot_nt(q, kc_ref[...])
            sn = jnp.where(m_next, _dot_nt(q, kn_ref[...]), NEG)
            sk = _sink_column(sink_ref, g)
            m = jnp.maximum(jnp.max(jnp.maximum(jnp.maximum(sp, sc), sn), axis=-1, keepdims=True), sk)
            pp, pc, pn = jnp.exp(sp - m), jnp.exp(sc - m), jnp.exp(sn - m)
            den = jnp.sum(pp + pc + pn, axis=-1, keepdims=True) + jnp.exp(sk - m)
            o = (_dot(_mx(pp), vp_ref[...]) + _dot(_mx(pc), vc_ref[...]) + _dot(_mx(pn), vn_ref[...])) / den
            o = jnp.where(_half(o.shape, g), o, 0.0)
            ost_ref[g] = o
            lst_ref[g] = jnp.broadcast_to(m + jnp.log(den), (GROUP_ROWS, BLOCK))
            for s in range(4):
                tt, slot = 2 * g + s // 2, s % 2
                piece = o[BLOCK * s:BLOCK * (s + 1)]
                if slot != g:
                    piece = pltpu.roll(piece, HEAD_DIM, 1)
                nat[tt] = piece if nat[tt] is None else nat[tt] + piece
        for tt in range(4):
            o_ref[:, tt * BLOCK:(tt + 1) * BLOCK] = _mx(nat[tt])

    stacked = pl.BlockSpec((2, GROUP_ROWS, BLOCK), lambda n: (0, n, 0))
    return pl.pallas_call(
        body, name="attn_fwd", grid=(nb,),
        in_specs=[stacked] + _nbr_specs(nb, BLOCK, 0) + _nbr_specs(nb, BLOCK, 0) + [pl.BlockSpec(memory_space=pltpu.SMEM)],
        out_specs=[pl.BlockSpec((BLOCK, 512), lambda n: (n, 0)), stacked, stacked],
        out_shape=[jax.ShapeDtypeStruct((t, 512), MXU_DTYPE), jax.ShapeDtypeStruct((2, 4 * t, BLOCK), F32),
                   jax.ShapeDtypeStruct((2, 4 * t, BLOCK), F32)],
        compiler_params=_cparams("parallel"),
    )(qst, kr, kr, kr, vb, vb, vb, sink)


def _lse_and_dsum(ld):
    return ld[:, 0:1], pltpu.roll(ld, HEAD_DIM, 1)[:, 0:1]


def _attn_bwd(qst, kr, vb, dost, ld, sink):
    t = kr.shape[0]
    nb = t // BLOCK

    def body(q_ref, kp_ref, kc_ref, kn_ref, vp_ref, vc_ref, vn_ref, do_ref, ld_ref, sink_ref,
             dq_ref, dk_ref, dv_ref, ds_ref):
        n = pl.program_id(0)

        @pl.when(n == 0)
        def _():
            ds_ref[...] = jnp.zeros_like(ds_ref)
            dk_ref[...] = jnp.zeros_like(dk_ref)
            dv_ref[...] = jnp.zeros_like(dv_ref)

        qi, kj = _query_index()
        m_prev, m_next = (kj >= qi) & (n > 0), (kj <= qi) & (n < nb - 1)
        key_rows = [pl.ds(pl.multiple_of(jnp.clip(n - 1 + b, 0, nb - 1) * BLOCK, BLOCK), BLOCK) for b in range(3)]
        for g in range(2):
            q, do = q_ref[g], do_ref[g]
            lse, dsum = _lse_and_dsum(ld_ref[g])
            acc = jnp.zeros((GROUP_ROWS, BLOCK), F32)
            for b, (k_ref, v_ref, valid) in enumerate(((kp_ref, vp_ref, m_prev), (kc_ref, vc_ref, None),
                                                       (kn_ref, vn_ref, m_next))):
                sc = _dot_nt(q, k_ref[...])
                if valid is not None:
                    sc = jnp.where(valid, sc, NEG)
                p = jnp.exp(sc - lse)
                dsc = _mx(p * (_dot_nt(do, v_ref[...]) - dsum))
                acc += _dot(dsc, k_ref[...])
                dv_ref[key_rows[b], :] += _dot_tn(_mx(p), do)
                dk_ref[key_rows[b], :] += _dot_tn(dsc, q)
            dq_ref[g] = jnp.where(_half(acc.shape, g), acc, 0.0)
            dsk = jnp.exp(_sink_column(sink_ref, g) - lse) * dsum
            for s in range(4):
                h = 4 * g + s
                ds_ref[h:h + 1, :] -= jnp.sum(dsk[BLOCK * s:BLOCK * (s + 1)], axis=0, keepdims=True)

    stacked = pl.BlockSpec((2, GROUP_ROWS, BLOCK), lambda n: (0, n, 0))
    whole = pl.BlockSpec((t, BLOCK), lambda n: (0, 0))
    return pl.pallas_call(
        body, name="attn_bwd", grid=(nb,),
        in_specs=[stacked] + _nbr_specs(nb, BLOCK, 0) + _nbr_specs(nb, BLOCK, 0)
        + [stacked, stacked, pl.BlockSpec(memory_space=pltpu.SMEM)],
        out_specs=[stacked, whole, whole, pl.BlockSpec((8, BLOCK), lambda n: (0, 0))],
        out_shape=[jax.ShapeDtypeStruct((2, 4 * t, BLOCK), F32), jax.ShapeDtypeStruct((t, BLOCK), F32),
                   jax.ShapeDtypeStruct((t, BLOCK), F32), jax.ShapeDtypeStruct((8, BLOCK), F32)],
        compiler_params=_cparams("arbitrary"),
    )(qst, kr, kr, kr, vb, vb, vb, dost, ld, sink)


def _loss_head(y, target, r, gamma):
    t = y.shape[0]

    def body(y_ref, t_ref, r_ref, g_ref, l_ref, res_ref, do_ref, dgb_ref):
        @pl.when(pl.program_id(0) == 0)
        def _():
            l_ref[...] = jnp.zeros_like(l_ref)

        e = y_ref[...] - t_ref[...]
        l_ref[...] += 0.5 * jnp.sum(_mean(e * e))
        _norm_bwd_tail(e / D_MODEL, r_ref, g_ref, res_ref, do_ref, dgb_ref)

    row = pl.BlockSpec((TM, D_MODEL), lambda i: (i, 0))
    tail_in, tail_out, tail_shape = _norm_tail_specs(t)
    return pl.pallas_call(
        body, name="loss_head", grid=(t // TM,), in_specs=[row, row] + tail_in,
        out_specs=[pl.BlockSpec((8, 128), lambda i: (0, 0))] + tail_out,
        out_shape=[jax.ShapeDtypeStruct((8, 128), F32)] + tail_shape,
        compiler_params=_cparams("arbitrary"),
    )(y, target, r, gamma)


def _adamw(name, w, g, m, v, rows):
    n, width = w.shape

    def body(w_ref, g_ref, m_ref, v_ref, d_ref, nm_ref, nv_ref):
        g = g_ref[...]
        m = ADAM_B1 * m_ref[...] + (1.0 - ADAM_B1) * g
        v = ADAM_B2 * v_ref[...] + (1.0 - ADAM_B2) * jnp.square(g)
        m_hat = m / (1.0 - ADAM_B1 ** ADAM_STEP)
        v_hat = v / (1.0 - ADAM_B2 ** ADAM_STEP)
        d_ref[...] = -ADAM_LR * (m_hat / (jnp.sqrt(v_hat) + ADAM_EPS) + ADAM_WD * w_ref[...])
        nm_ref[...] = m
        nv_ref[...] = v

    spec = pl.BlockSpec((rows, width), lambda i: (i, 0))
    return pl.pallas_call(
        body, name=name, grid=(n // rows,), in_specs=[spec] * 4, out_specs=[spec] * 3,
        out_shape=[jax.ShapeDtypeStruct((n, width), F32)] * 3, compiler_params=_cparams("parallel"),
    )(w, g, m, v)


def _place():
    x, y, c = lax.axis_index("x"), lax.axis_index("y"), lax.axis_index("c")
    chips = [(1 - x, y), (x, 1 - y), (1 - x, 1 - y)]
    return x, y, c, chips


class _Rider:
    def pass_on_steps(self, nsteps):
        return []

    def pass_on(self, a, ins, outs, sems):
        pass


class _Gather(_Rider):
    def __init__(self, shards):
        na = len(shards)
        self.ins = list(shards)
        self.outs = [jax.ShapeDtypeStruct((N_SHARD,) + s.shape, s.dtype) for s in shards]
        self.sems = [pltpu.SemaphoreType.DMA((3 * na,))] * 4 + [pltpu.SemaphoreType.DMA((na,))]

    def _copies(self, src, dst, sems):
        send, recv, fsend, frecv, lsem = sems
        x, y, c, chips = _place()
        mine = 2 * x + y

        def local(a):
            return pltpu.make_async_copy(src[a], dst[a].at[mine], lsem.at[a])

        def ici(a, k, shard):
            cx, cy = chips[k]
            return pltpu.make_async_remote_copy(
                src_ref=src[a].at[c], dst_ref=dst[a].at[shard, c], send_sem=send.at[3 * a + k], recv_sem=recv.at[3 * a + k],
                device_id=(cx, cy, c), device_id_type=MESH)

        def d2d(a, k, half):
            cx, cy = chips[k]
            block = dst[a].at[2 * cx + cy, half]
            return pltpu.make_async_remote_copy(
                src_ref=block, dst_ref=block, send_sem=fsend.at[3 * a + k], recv_sem=frecv.at[3 * a + k],
                device_id=(x, y, 1 - c), device_id_type=MESH)

        return local, ici, d2d, mine, c, chips

    def start(self, src, dst, sems):
        local, ici, _, mine, _, _ = self._copies(src, dst, sems)
        for a in range(len(src)):
            local(a).start()
            for k in range(3):
                ici(a, k, mine).start()

    def pass_on_steps(self, nsteps):
        sizes = np.cumsum([float(np.prod(s.shape)) * s.dtype.itemsize for s in self.ins])
        steps = [min(nsteps - 1, int(nsteps * done / sizes[-1]) + 1) for done in sizes]
        steps[-1] = nsteps - 1
        return steps

    def pass_on(self, a, src, dst, sems):
        _, ici, d2d, _, c, chips = self._copies(src, dst, sems)
        for k, (cx, cy) in enumerate(chips):
            ici(a, k, 2 * cx + cy).wait_recv()
            d2d(a, k, c).start()

    def finish(self, src, dst, sems):
        local, ici, d2d, mine, c, chips = self._copies(src, dst, sems)
        for a in range(len(src)):
            for k in range(3):
                d2d(a, k, 1 - c).wait_recv()
        for a in range(len(src)):
            for k in range(3):
                ici(a, k, mine).wait_send()
                d2d(a, k, c).wait_send()
            local(a).wait()


class _PairExchange(_Rider):
    def __init__(self, parts):
        self.ins = list(parts)
        self.outs = [jax.ShapeDtypeStruct((N_SHARD,) + p.shape[2:], p.dtype) for p in parts]
        self.sems = [pltpu.SemaphoreType.DMA((len(parts),))] * 2

    def _copy(self, a, src, dst, sems):
        x, y, c, _ = _place()
        return pltpu.make_async_remote_copy(
            src_ref=src[a].at[:, 1 - c], dst_ref=dst[a], send_sem=sems[0].at[a], recv_sem=sems[1].at[a],
            device_id=(x, y, 1 - c), device_id_type=MESH)

    def start(self, src, dst, sems):
        for a in range(len(src)):
            self._copy(a, src, dst, sems).start()

    def finish(self, src, dst, sems):
        for a in range(len(src)):
            self._copy(a, src, dst, sems).wait()


class _ChipExchange(_Rider):
    def __init__(self, sums):
        self.ins = list(sums)
        self.outs = [jax.ShapeDtypeStruct((3,) + s.shape[1:], s.dtype) for s in sums]
        self.sems = [pltpu.SemaphoreType.DMA((3 * len(sums),))] * 2

    def _copy(self, a, k, src, dst, sems):
        _, _, c, chips = _place()
        cx, cy = chips[k]
        return pltpu.make_async_remote_copy(
            src_ref=src[a].at[2 * cx + cy], dst_ref=dst[a].at[k], send_sem=sems[0].at[3 * a + k],
            recv_sem=sems[1].at[3 * a + k], device_id=(cx, cy, c), device_id_type=MESH)

    def start(self, src, dst, sems):
        for a in range(len(src)):
            for k in range(3):
                self._copy(a, k, src, dst, sems).start()

    def finish(self, src, dst, sems):
        for a in range(len(src)):
            for k in range(3):
                self._copy(a, k, src, dst, sems).wait()


class _Both(_Rider):
    def __init__(self, a, b):
        self.a, self.b = a, b
        self.ins, self.outs, self.sems = a.ins + b.ins, a.outs + b.outs, a.sems + b.sems

    def _each(self, method, ins, outs, sems):
        a = self.a
        getattr(a, method)(ins[:len(a.ins)], outs[:len(a.outs)], sems[:len(a.sems)])
        getattr(self.b, method)(ins[len(a.ins):], outs[len(a.outs):], sems[len(a.sems):])

    def start(self, ins, outs, sems):
        self._each("start", ins, outs, sems)

    def finish(self, ins, outs, sems):
        self._each("finish", ins, outs, sems)


def _run(name, rider):
    n_in, n_out = len(rider.ins), len(rider.outs)

    def body(*refs):
        ins, outs, sems = refs[:n_in], refs[n_in:n_in + n_out], refs[n_in + n_out:]
        rider.start(ins, outs, sems)
        for a in range(len(rider.pass_on_steps(1))):
            rider.pass_on(a, ins, outs, sems)
        rider.finish(ins, outs, sems)

    return list(pl.pallas_call(
        body, name=name, in_specs=[ANY] * n_in, out_specs=[ANY] * n_out, out_shape=rider.outs,
        scratch_shapes=rider.sems)(*rider.ins))


def _pair_share(halves):
    na = len(halves)

    def body(*refs):
        dst = refs[na:2 * na]
        send, recv = refs[2 * na:]
        x, y, c, _ = _place()
        cps = []
        for a in range(na):
            cp = pltpu.make_async_remote_copy(
                src_ref=dst[a].at[:, c], dst_ref=dst[a].at[:, c], send_sem=send.at[a], recv_sem=recv.at[a],
                device_id=(x, y, 1 - c), device_id_type=MESH)
            cp.start()
            cps.append(cp)
        for a in range(na):
            cps[a].wait_send()
            pltpu.make_async_remote_copy(
                src_ref=dst[a].at[:, 1 - c], dst_ref=dst[a].at[:, 1 - c], send_sem=send.at[a], recv_sem=recv.at[a],
                device_id=(x, y, 1 - c), device_id_type=MESH).wait_recv()

    return pl.pallas_call(
        body, name="pair_share", in_specs=[ANY] * na, out_specs=[ANY] * na,
        out_shape=[jax.ShapeDtypeStruct(h.shape, h.dtype) for h in halves],
        input_output_aliases={a: a for a in range(na)},
        scratch_shapes=[pltpu.SemaphoreType.DMA((na,))] * 2,
    )(*halves)


def _sum_rows(r):
    return r if r <= 352 else 256


def _pair_sum(name, part, got):
    _, _, r, w = part.shape
    rows = _sum_rows(r)
    c = lax.axis_index("c").astype(jnp.int32).reshape(1)

    def body(c_ref, p_ref, g_ref, o_ref):
        o_ref[...] = _mx(p_ref[...] + g_ref[...])

    spec = pl.BlockSpec((None, rows, w), lambda j, i, c_ref: (j, i, 0))
    return pl.pallas_call(
        body, name=name, out_shape=jax.ShapeDtypeStruct((N_SHARD, r, w), MXU_DTYPE),
        grid_spec=pltpu.PrefetchScalarGridSpec(
            num_scalar_prefetch=1, grid=(N_SHARD, r // rows),
            in_specs=[pl.BlockSpec((None, None, rows, w), lambda j, i, c_ref: (j, c_ref[0], i, 0)), spec],
            out_specs=spec),
        compiler_params=_cparams("parallel", "parallel"),
    )(c, part, got)


def _chip_sum(name, part, got, others, l, prev):
    _, _, r, w = part.shape
    rows = _sum_rows(r)
    cj = jnp.stack([lax.axis_index("c"), 2 * lax.axis_index("x") + lax.axis_index("y")]).astype(jnp.int32)

    def body(cj_ref, p_ref, g_ref, o_ref, *rest):
        acc = p_ref[...] + g_ref[...]
        for k in range(3):
            acc += o_ref[k].astype(F32)
        rest[-1][...] = acc

    ins, specs, alias = [cj, part, got, others], [], {}
    if prev is not None:
        ins.append(prev)
        specs.append(ANY)
        alias = {4: 0}
    return pl.pallas_call(
        body, name=name, out_shape=jax.ShapeDtypeStruct((2, 2, r, w), F32), input_output_aliases=alias,
        grid_spec=pltpu.PrefetchScalarGridSpec(
            num_scalar_prefetch=1, grid=(r // rows,),
            in_specs=[pl.BlockSpec((None, None, rows, w), lambda i, cj: (cj[1], cj[0], i, 0)),
                      pl.BlockSpec((None, rows, w), lambda i, cj: (cj[1], i, 0)),
                      pl.BlockSpec((3, rows, w), lambda i, cj: (0, i, 0))] + specs,
            out_specs=pl.BlockSpec((None, None, rows, w), lambda i, cj: (l, cj[0], i, 0))),
        compiler_params=_cparams("parallel"),
    )(*ins)


SMALL_ROWS = 40


def _sum_small(part):
    def body(p_ref, o_ref, land, send, recv):
        x, y, c, _ = _place()
        me = 4 * x + 2 * y + c
        cps = []
        for r in range(1, 8):
            cp = pltpu.make_async_remote_copy(
                src_ref=p_ref, dst_ref=land.at[r], send_sem=send.at[r], recv_sem=recv.at[r],
                device_id=(x ^ (r >> 2), y ^ ((r >> 1) & 1), c ^ (r & 1)), device_id_type=MESH)
            cp.start()
            cps.append(cp)
        land[0] = p_ref[...]
        for cp in cps:
            cp.wait()
        acc = land[me]
        for e in range(1, 8):
            acc += land[me ^ e]
        o_ref[...] = acc

    return pl.pallas_call(
        body, name="sum_small", in_specs=[pl.BlockSpec(memory_space=pltpu.VMEM)],
        out_specs=pl.BlockSpec(memory_space=pltpu.VMEM), out_shape=jax.ShapeDtypeStruct(part.shape, F32),
        scratch_shapes=[pltpu.VMEM((8,) + part.shape, F32), pltpu.SemaphoreType.DMA((8,)), pltpu.SemaphoreType.DMA((8,))],
    )(part)


BIG = ("ffn1_w_gu", "ffn1_w_down", "w_in", "w_out", "ffn2_w_gu", "ffn2_w_down")
SMALL = ("ln1_g", "ln1_b", "ln2_g", "ln2_b", "ln3_g", "ln3_b", "attn_sink", "cc_conv_b", "cc_ln_g", "cc_ln_b",
         "sc_conv_w", "cc_conv_w")
NAMES = ("ffn1_w_gu", "ffn1_w_down", "ln1_g", "ln1_b", "w_in", "sc_conv_w", "attn_sink", "cc_conv_w", "cc_conv_b",
         "cc_ln_g", "cc_ln_b", "w_out", "ln2_g", "ln2_b", "ffn2_w_gu", "ffn2_w_down", "ln3_g", "ln3_b")


def _rope_tables(t):
    half = HEAD_DIM // 2
    inv_freq = ROPE_THETA ** (-jnp.arange(half, dtype=F32) / half)
    ang = jnp.arange(t).astype(F32)[:, None] * inv_freq[None, :]
    cos, sin = jnp.cos(ang), jnp.sin(ang)
    return jnp.tile(jnp.concatenate([cos, cos], axis=1), (1, 2)), jnp.tile(jnp.concatenate([-sin, sin], axis=1), (1, 2))


def _pack_small(vals):
    flat = jnp.concatenate([vals[n].reshape(-1) for n in SMALL])
    return jnp.pad(flat, (0, SMALL_ROWS * D_MODEL - flat.shape[0])).reshape(SMALL_ROWS, D_MODEL)


def _unpack_small(packed, shapes):
    flat, out, at = packed.reshape(-1), {}, 0
    for n in SMALL:
        size = int(np.prod(shapes[n]))
        out[n] = flat[at:at + size].reshape(shapes[n])
        at += size
    return out


def kernel(x, ffn1_w_gu, ffn1_w_down, ln1_g, ln1_b, w_in, sc_conv_w, attn_sink, cc_conv_w, cc_conv_b, cc_ln_g, cc_ln_b, w_out, ln2_g, ln2_b, ffn2_w_gu, ffn2_w_down, ln3_g, ln3_b, loss_target, m_ffn1_w_gu, m_ffn1_w_down, m_ln1_g, m_ln1_b, m_w_in, m_sc_conv_w, m_attn_sink, m_cc_conv_w, m_cc_conv_b, m_cc_ln_g, m_cc_ln_b, m_w_out, m_ln2_g, m_ln2_b, m_ffn2_w_gu, m_ffn2_w_down, m_ln3_g, m_ln3_b, v_ffn1_w_gu, v_ffn1_w_down, v_ln1_g, v_ln1_b, v_w_in, v_sc_conv_w, v_attn_sink, v_cc_conv_w, v_cc_conv_b, v_cc_ln_g, v_cc_ln_b, v_w_out, v_ln2_g, v_ln2_b, v_ffn2_w_gu, v_ffn2_w_down, v_ln3_g, v_ln3_b):
    given = dict(locals())
    w = {n: given[n] for n in NAMES}
    mom = {n: given["m_" + n] for n in NAMES}
    var = {n: given["v_" + n] for n in NAMES}
    x0 = x[0]
    target = loss_target[0]
    t = x0.shape[0]
    chip = 2 * lax.axis_index("x") + lax.axis_index("y")

    conv_shard = jnp.pad(jnp.concatenate([sc_conv_w, cc_conv_w], axis=1), ((0, 0), (0, 30), (0, 64)))
    local = {n: _mx(w[n]) for n in BIG}
    local["conv"] = conv_shard
    full = [{}, {}]

    def gather(l, names):
        return _Gather([local[n][l].reshape(2, local[n].shape[1] // 2, local[n].shape[2]) for n in names])

    def land(l, names, arrays):
        for n, a in zip(names, arrays):
            full[l][n] = a.reshape(1, N_SHARD, 2 * a.shape[2], a.shape[3])

    def weights(l):
        f = full[l]
        conv = jnp.transpose(f["conv"][0, :, :SC_W + CC_W, :64], (1, 0, 2)).reshape(SC_W + CC_W, D_CONV)
        return dict(wgu1=f["ffn1_w_gu"], wd1=f["ffn1_w_down"].reshape(1, D_FF, D_MODEL), win=f["w_in"],
                    wout=f["w_out"].reshape(1, D_MODEL, D_MODEL), wgu2=f["ffn2_w_gu"],
                    wd2=f["ffn2_w_down"].reshape(1, D_FF, D_MODEL), sc=conv[:SC_W], cc=conv[SC_W:])

    first = ("ffn1_w_gu", "ffn1_w_down")
    mixer = ("w_in", "w_out", "conv")
    second = ("ffn2_w_gu", "ffn2_w_down")
    land(0, first, _run("gather_first", gather(0, first)))
    cos, sin = _rope_tables(t)

    def vec(a, l):
        return a[l][None, :]

    acts = []
    h = x0
    for l in range(2):
        ahead = (0, second + mixer) if l == 0 else (1, second)
        (y1, r1, gu1), got = _ffn_fwd("ffn_fwd_a%d" % l, h, full[l]["ffn1_w_gu"], full[l]["ffn1_w_down"].reshape(1, D_FF, D_MODEL),
                                      vec(ln1_g, l), vec(ln1_b, l), 0, gather(*ahead))
        land(*ahead, got)
        wl = weights(l)
        z = _in_proj(y1, wl["win"], 0)
        ysc, ycc, u2 = _conv_fwd(z, wl["sc"], wl["cc"], vec(cc_conv_b, l), vec(cc_ln_g, l), vec(cc_ln_b, l))
        qs, kf, vf = _attn_prep(z, cos, sin)
        o_nat, o, lse = _attn_fwd(qs, kf, vf, attn_sink[l])
        ycat = jnp.concatenate([ysc, o_nat, ycc], axis=1)
        y2, r2 = _out_proj(ycat, y1, wl["wout"], vec(ln2_g, l), vec(ln2_b, l), 0)
        ahead = (1, first + mixer) if l == 0 else None
        (y3, r3, gu2), got = _ffn_fwd("ffn_fwd_b%d" % l, y2, wl["wgu2"], wl["wd2"], vec(ln3_g, l), vec(ln3_b, l), 0,
                                      gather(*ahead) if ahead else None)
        if ahead:
            land(*ahead, got)
        acts.append(dict(x=h, y1=y1, r1=r1, gu1=gu1, z=z, u2=u2, qs=qs, kf=kf, vf=vf, o=o, lse=lse, ycat=ycat,
                         y2=y2, r2=r2, gu2=gu2, r3=r3, w=wl))
        h = y3
    loss_rows, res, do, dgb3_next = _loss_head(h, target, acts[1]["r3"], vec(ln3_g, 1))
    loss = lax.psum(loss_rows[0, 0], ("x", "y", "c"))

    upper = ("ffn2_w_gu", "ffn2_w_down", "w_out")
    lower = ("w_in", "ffn1_w_gu", "ffn1_w_down")
    part = [{}, {}]
    small = [None, None]
    stage = {}
    reduced = {n: None for n in BIG}
    row = pl.BlockSpec((TM, D_MODEL), lambda n, k: (k, 0))
    deep = pl.BlockSpec((TK, D_MODEL), lambda n, k: (k, 0))

    def halves(a, r):
        return a.reshape(N_SHARD, 2, r // 2, a.shape[-1])

    def pair_rider(l, names):
        return _PairExchange([part[l][n] for n in names])

    def after_pair(l, names, got):
        stage[l, names] = (got, [_pair_sum("pair_sum_%s_%d" % (n, l), part[l][n], g) for n, g in zip(names, got)])

    def chip_rider(l, names):
        return _ChipExchange(stage[l, names][1])

    def after_chip(l, names, others):
        for n, g, o in zip(names, stage[l, names][0], others):
            reduced[n] = _chip_sum("chip_sum_%s_%d" % (n, l), part[l][n], g, o, l, reduced[n])

    def ffn_weight_grads(which, l, xin, dh, a, do, rider_gu=None, make_rider_d=None):
        out, got_gu = _mm_tn(
            "%s_dwgu_%d" % (which, l), xin, dh, deep,
            pl.BlockSpec((None, TK, FF_CHUNK), lambda n, k: (n // N_CHUNK, k, n % N_CHUNK)),
            pl.BlockSpec((None, D_MODEL, FF_CHUNK), lambda n, k: (n, 0, 0)),
            (N_SHARD, D_MODEL, GU_SHARD), (D_MODEL, FF_CHUNK), (2 * N_CHUNK, t // TK), rider_gu)
        part[l][which + "_w_gu"] = halves(out, D_MODEL)
        rider_d = make_rider_d() if make_rider_d else None
        out, got_d = _mm_tn(
            "%s_dwd_%d" % (which, l), a, do, pl.BlockSpec((TK, FF_CHUNK), lambda n, k: (k, n)), deep,
            pl.BlockSpec((FF_CHUNK, D_MODEL), lambda n, k: (n, 0)),
            (D_FF, D_MODEL), (FF_CHUNK, D_MODEL), (N_CHUNK, t // TK), rider_d)
        part[l][which + "_w_down"] = halves(out, D_FF // N_SHARD)
        return got_gu, got_d

    w_in_only, w_gu_only, w_down_only = ("w_in",), ("ffn1_w_gu",), ("ffn1_w_down",)
    for l in (1, 0):
        s = acts[l]
        wl = s["w"]
        dgb3 = dgb3_next
        if l == 0:
            dy, dh, a, got = _ffn_bwd(res, do, s["gu2"], wl["wgu2"], wl["wd2"], 0, rider=pair_rider(1, lower))
            after_pair(1, lower, got)
            got, _ = ffn_weight_grads("ffn2", l, s["y2"], dh, a, do, chip_rider(1, lower))
            after_chip(1, lower, got)
        else:
            dy, dh, a, _ = _ffn_bwd(res, do, s["gu2"], wl["wgu2"], wl["wd2"], 0)
            ffn_weight_grads("ffn2", l, s["y2"], dh, a, do)
        res, dm, dycat, dgb2 = _out_proj_bwd(dy, s["r2"], wl["wout"], vec(ln2_g, l), 0)
        out, _ = _mm_tn("dwout_%d" % l, s["ycat"], dm, row, row, pl.BlockSpec((D_MODEL, D_MODEL), lambda n, k: (0, 0)),
                        (D_MODEL, D_MODEL), (D_MODEL, D_MODEL), (1, t // TM))
        part[l]["w_out"] = halves(out, OUT_SHARD)
        dz_sc, dz_cc, dconv = _conv_bwd(s["z"], dycat, s["u2"], wl["sc"], wl["cc"], vec(cc_ln_g, l), vec(cc_ln_b, l))
        dost, ld = _attn_dprep(dycat, s["o"], s["lse"])
        dqs, dkf, dvf, dsink = _attn_bwd(s["qs"], s["kf"], s["vf"], dost, ld, attn_sink[l])
        dz_att = _attn_prep_bwd(dqs, dkf, dvf, cos, sin)
        dz = jnp.concatenate([dz_sc, dz_att, dz_cc], axis=1)
        out, got = _mm_tn(
            "dwin_%d" % l, s["y1"], dz, row, pl.BlockSpec((TM, D_IN), lambda n, k: (k, 0)),
            pl.BlockSpec((N_SHARD, D_MODEL, IN_SHARD), lambda n, k: (0, 0, 0)),
            (N_SHARD, D_MODEL, IN_SHARD), (D_MODEL, D_IN), (1, t // TM), pair_rider(l, upper), split=N_SHARD)
        part[l]["w_in"] = halves(out, D_MODEL)
        after_pair(l, upper, got)
        res, do, dgb1 = _in_proj_bwd(dz, res, wl["win"], 0, s["r1"], vec(ln1_g, l))
        if l == 1:
            (res0, do0, dgb3_next), dh, a, _ = _ffn_bwd(res, do, s["gu1"], wl["wgu1"], wl["wd1"], 0,
                                                        tail=(acts[0]["r3"], vec(ln3_g, 0)))
            got, _ = ffn_weight_grads("ffn1", l, s["x"], dh, a, do, chip_rider(l, upper))
            after_chip(l, upper, got)
            res, do = res0, do0
        else:
            dy, dh, a, got = _ffn_bwd(res, do, s["gu1"], wl["wgu1"], wl["wd1"], 0, rider=pair_rider(0, w_in_only))
            after_pair(0, w_in_only, got)
            got, got_d = ffn_weight_grads("ffn1", l, s["x"], dh, a, do, _Both(chip_rider(0, upper), chip_rider(0, w_in_only)),
                                          lambda: pair_rider(0, w_gu_only))
            n_upper = len(upper)
            after_chip(0, upper, got[:n_upper])
            after_chip(0, w_in_only, got[n_upper:])
            after_pair(0, w_gu_only, got_d)
        small[l] = dict(ln1_g=dgb1[0], ln1_b=dgb1[1], ln2_g=dgb2[0], ln2_b=dgb2[1], ln3_g=dgb3[0], ln3_b=dgb3[1],
                        attn_sink=dsink[:, 0], cc_conv_b=dconv[ROW_CCB], cc_ln_g=dconv[ROW_CCG],
                        cc_ln_b=dconv[ROW_CCBETA], sc_conv_w=dconv[ROW_SCW:ROW_SCW + SC_W],
                        cc_conv_w=dconv[ROW_CCW:ROW_CCW + CC_W])
    grad_x = dy[None]

    after_pair(0, w_down_only, _run("pair_exchange_last", pair_rider(0, w_down_only)))
    got = _run("chip_exchange_last", _Both(chip_rider(0, w_gu_only), chip_rider(0, w_down_only)))
    after_chip(0, w_gu_only, got[:1])
    after_chip(0, w_down_only, got[1:])
    grads = dict(zip(BIG, _pair_share([reduced[n] for n in BIG])))
    for n in BIG:
        grads[n] = grads[n].reshape(w[n].shape)

    small_full = {n: jnp.stack([small[0][n], small[1][n]]) for n in SMALL}
    small_sum = _unpack_small(_sum_small(_pack_small(small_full)), {n: small_full[n].shape for n in SMALL})
    for n in SMALL:
        g = small_sum[n]
        if n in ("sc_conv_w", "cc_conv_w"):
            g = lax.dynamic_slice_in_dim(g, chip * 64, 64, axis=2)
        grads[n] = g

    delta, new_m, new_v = {}, {}, {}
    for n in BIG:
        shape = w[n].shape
        two_d = (shape[0] * shape[1], shape[2])
        outs = _adamw("adamw_" + n, w[n].reshape(two_d), grads[n].reshape(two_d), mom[n].reshape(two_d),
                      var[n].reshape(two_d), 128)
        delta[n], new_m[n], new_v[n] = [a.reshape(shape) for a in outs]
    shapes = {n: w[n].shape for n in SMALL}
    outs = _adamw("adamw_small", _pack_small({n: w[n] for n in SMALL}), _pack_small({n: grads[n] for n in SMALL}),
                  _pack_small({n: mom[n] for n in SMALL}), _pack_small({n: var[n] for n in SMALL}), 8)
    for d, packed in zip((delta, new_m, new_v), outs):
        d.update(_unpack_small(packed, shapes))

    return (loss, grad_x, *[grads[n] for n in NAMES], *[delta[n] for n in NAMES], *[new_m[n] for n in NAMES],
            *[new_v[n] for n in NAMES])
```

```python
import functools

import numpy as np
import jax
import jax.numpy as jnp
from jax import lax
from jax.experimental import pallas as pl
from jax.experimental.pallas import tpu as pltpu

F32 = jnp.float32
MXU_DTYPE = jnp.bfloat16

D_MODEL = 1024
D_FF = 2816
N_SHARD = 4
D_IN = 2048
GU_SHARD = 2 * D_FF // N_SHARD
FF_CHUNK = GU_SHARD
N_CHUNK = D_FF // FF_CHUNK
IN_SHARD = D_IN // N_SHARD
OUT_SHARD = D_MODEL // N_SHARD
HEAD_DIM = 64
BLOCK = 128
SC_W = 3
CC_W = 31
D_CONV = 256
HALO = 16
SLAB = 64
LN_EPS = 1e-5
ALPHA = (2.0 * 2) ** 0.25
NEG = -1e30
ROPE_THETA = 10000.0
ADAM_LR, ADAM_B1, ADAM_B2, ADAM_EPS, ADAM_WD, ADAM_STEP = 0.001, 0.9, 0.999, 1e-08, 0.01, 10

TM = 512
TK = 1024
TMC = 256
VMEM_LIMIT = 56 * 1024 * 1024
MESH = pl.DeviceIdType.MESH
ANY = pl.BlockSpec(memory_space=pl.ANY)


def _cparams(*sem):
    return pltpu.CompilerParams(dimension_semantics=sem, vmem_limit_bytes=VMEM_LIMIT)


def _dot(a, b):
    return jnp.dot(a, b, preferred_element_type=F32)


def _dot_nt(a, b):
    return lax.dot_general(a, b, (((1,), (1,)), ((), ())), preferred_element_type=F32)


def _dot_tn(a, b):
    return lax.dot_general(a, b, (((0,), (0,)), ((), ())), preferred_element_type=F32)


def _mx(a):
    return a.astype(MXU_DTYPE)


def _mean(a):
    return jnp.mean(a, axis=-1, keepdims=True)


def _ln_stats(r):
    xc = r - _mean(r)
    rstd = lax.rsqrt(_mean(xc * xc) + LN_EPS)
    return xc * rstd, rstd


def _ln_bwd(dy, xh, rstd, gamma):
    dxh = dy * gamma
    return rstd * (dxh - _mean(dxh) - xh * _mean(dxh * xh))


def _colsum(a):
    return jnp.sum(a, axis=0, keepdims=True)


def _sigmoid(a):
    return 1.0 / (1.0 + jnp.exp(-a))


def _call(body, args, *, name, grid, in_specs, out_specs, out_shape, scratch, sem, rider=None):
    if rider is None:
        outs = pl.pallas_call(
            body, name=name, grid=grid, in_specs=in_specs, out_specs=out_specs, out_shape=out_shape,
            scratch_shapes=scratch, compiler_params=_cparams(*sem))(*args)
        return list(outs), []
    n_in, n_out, n_sc = len(in_specs), len(out_specs), len(scratch)
    r_in, r_out = len(rider.ins), len(rider.outs)

    def carrying(*refs):
        cuts = np.cumsum([0, n_in, r_in, n_out, r_out, n_sc])
        ins, rins, outs, routs, scr = [refs[a:b] for a, b in zip(cuts[:-1], cuts[1:])]
        rsems = refs[cuts[-1]:]
        first = functools.reduce(jnp.logical_and, [pl.program_id(d) == 0 for d in range(len(grid))])
        last = functools.reduce(jnp.logical_and, [pl.program_id(d) == grid[d] - 1 for d in range(len(grid))])

        @pl.when(first)
        def _():
            rider.start(rins, routs, rsems)

        body(*ins, *outs, *scr)

        nsteps = int(np.prod(grid))
        step = functools.reduce(lambda lin, d: lin * grid[d] + pl.program_id(d), range(len(grid)), 0)
        for a, at in enumerate(rider.pass_on_steps(nsteps)):
            @pl.when(step == at)
            def _(a=a):
                rider.pass_on(a, rins, routs, rsems)

        @pl.when(last)
        def _():
            rider.finish(rins, routs, rsems)

    outs = pl.pallas_call(
        carrying, name=name, grid=grid, in_specs=list(in_specs) + [ANY] * r_in,
        out_specs=list(out_specs) + [ANY] * r_out, out_shape=list(out_shape) + list(rider.outs),
        scratch_shapes=list(scratch) + list(rider.sems), compiler_params=_cparams(*(("arbitrary",) * len(grid))),
    )(*args, *rider.ins)
    return list(outs[:n_out]), list(outs[n_out:])


def _ffn_fwd(name, x, wgu, wd, gamma, beta, l, rider=None):
    t = x.shape[0]
    nc = N_CHUNK

    def body(x_ref, wg_ref, wu_ref, wd_ref, g_ref, b_ref, y_ref, r_ref, gu_ref, xb_s, acc_s):
        c = pl.program_id(1)

        @pl.when(c == 0)
        def _():
            xb_s[...] = _mx(x_ref[...])
            acc_s[...] = jnp.zeros_like(acc_s)

        xb = xb_s[...]
        hg = _dot(xb, wg_ref[...])
        hu = _dot(xb, wu_ref[...])
        gu_ref[0] = _mx(hg)
        gu_ref[1] = _mx(hu)
        a = (hg * _sigmoid(hg)) * hu
        acc_s[...] += _dot(_mx(a), wd_ref[...])

        @pl.when(c == nc - 1)
        def _():
            r = ALPHA * x_ref[...] + 0.5 * acc_s[...]
            xh, _ = _ln_stats(r)
            r_ref[...] = r
            y_ref[...] = xh * g_ref[...] + b_ref[...]

    row = pl.BlockSpec((TM, D_MODEL), lambda i, c: (i, 0))
    vec = pl.BlockSpec((1, D_MODEL), lambda i, c: (0, 0))
    return _call(
        body, (x, wgu, wgu, wd, gamma, beta), name=name, grid=(t // TM, nc),
        in_specs=[row,
                  pl.BlockSpec((None, None, D_MODEL, FF_CHUNK), lambda i, c: (l, c, 0, 0)),
                  pl.BlockSpec((None, None, D_MODEL, FF_CHUNK), lambda i, c: (l, N_CHUNK + c, 0, 0)),
                  pl.BlockSpec((None, FF_CHUNK, D_MODEL), lambda i, c: (l, c, 0)),
                  vec, vec],
        out_specs=[row, row, pl.BlockSpec((2, TM, FF_CHUNK), lambda i, c: (0, i, c))],
        out_shape=[jax.ShapeDtypeStruct((t, D_MODEL), F32), jax.ShapeDtypeStruct((t, D_MODEL), F32),
                   jax.ShapeDtypeStruct((2, t, D_FF), MXU_DTYPE)],
        scratch=[pltpu.VMEM((TM, D_MODEL), MXU_DTYPE), pltpu.VMEM((TM, D_MODEL), F32)],
        sem=("parallel", "arbitrary"), rider=rider)


def _norm_bwd_tail(dy, r_ref, g_ref, res_ref, do_ref, dgb_ref):
    @pl.when(pl.program_id(0) == 0)
    def _():
        dgb_ref[...] = jnp.zeros_like(dgb_ref)

    xh, rstd = _ln_stats(r_ref[...])
    dr = _ln_bwd(dy, xh, rstd, g_ref[...])
    do_ref[...] = _mx(0.5 * dr)
    res_ref[...] = ALPHA * dr
    dgb_ref[0:1, :] += _colsum(dy * xh)
    dgb_ref[1:2, :] += _colsum(dy)


def _norm_tail_specs(t):
    row = pl.BlockSpec((TM, D_MODEL), lambda i: (i, 0))
    return ([row, pl.BlockSpec((1, D_MODEL), lambda i: (0, 0))],
            [row, row, pl.BlockSpec((8, D_MODEL), lambda i: (0, 0))],
            [jax.ShapeDtypeStruct((t, D_MODEL), F32), jax.ShapeDtypeStruct((t, D_MODEL), MXU_DTYPE),
             jax.ShapeDtypeStruct((8, D_MODEL), F32)])


def _ffn_bwd(res, do, gu, wgu, wd, l, tail=None, rider=None):
    t = res.shape[0]
    nc = N_CHUNK
    row1 = pl.BlockSpec((TM, D_MODEL), lambda i: (i, 0))

    def hidden_body(do_ref, gu_ref, wd_ref, dh_ref, a_ref):
        da = _dot_nt(do_ref[...], wd_ref[...])
        g = gu_ref[0].astype(F32)
        u = gu_ref[1].astype(F32)
        s = _sigmoid(g)
        sil = g * s
        a_ref[...] = _mx(sil * u)
        dh_ref[0] = _mx(da * u * (s * (1.0 + g * (1.0 - s))))
        dh_ref[1] = _mx(da * sil)

    hid = pl.BlockSpec((2, TM, FF_CHUNK), lambda c, i: (0, i, c))
    (dh, a), got = _call(
        hidden_body, (do, gu, wd), name="ffn_bwd_hidden" if rider is None else "ffn_bwd_hidden_carry", grid=(nc, t // TM),
        in_specs=[pl.BlockSpec((TM, D_MODEL), lambda c, i: (i, 0)), hid,
                  pl.BlockSpec((None, FF_CHUNK, D_MODEL), lambda c, i: (l, c, 0))],
        out_specs=[hid, pl.BlockSpec((TM, FF_CHUNK), lambda c, i: (i, c))],
        out_shape=[jax.ShapeDtypeStruct((2, t, D_FF), MXU_DTYPE), jax.ShapeDtypeStruct((t, D_FF), MXU_DTYPE)],
        scratch=[], sem=("parallel", "parallel"), rider=rider)

    def input_body(res_ref, dh_ref, w_ref, *rest):
        acc = res_ref[...]
        for j in range(N_SHARD):
            part = dh_ref[j // N_CHUNK][:, (j % N_CHUNK) * FF_CHUNK:(j % N_CHUNK + 1) * FF_CHUNK]
            acc += _dot_nt(part, w_ref[j])
        if tail is None:
            rest[0][...] = acc
        else:
            _norm_bwd_tail(acc, *rest)

    in_specs = [row1, pl.BlockSpec((2, TM, D_FF), lambda i: (0, i, 0)),
                pl.BlockSpec((None, N_SHARD, D_MODEL, GU_SHARD), lambda i: (l, 0, 0, 0))]
    if tail is None:
        dx = pl.pallas_call(
            input_body, name="ffn_bwd_input", grid=(t // TM,), in_specs=in_specs, out_specs=row1,
            out_shape=jax.ShapeDtypeStruct((t, D_MODEL), F32), compiler_params=_cparams("parallel"),
        )(res, dh, wgu)
    else:
        tail_in, tail_out, tail_shape = _norm_tail_specs(t)
        dx = pl.pallas_call(
            input_body, name="ffn_bwd_input_norm", grid=(t // TM,), in_specs=in_specs + tail_in, out_specs=tail_out,
            out_shape=tail_shape, compiler_params=_cparams("arbitrary"),
        )(res, dh, wgu, *tail)
    return dx, dh, a, got


def _mm_tn(name, a, b, a_spec, b_spec, out_spec, out_shape, acc_shape, grid, rider=None, split=1):
    nk = grid[-1]
    width = acc_shape[1] // split

    def body(*refs):
        a_ref, b_ref = refs[0], refs[1]
        o_ref, acc = refs[-2], refs[-1]
        k = pl.program_id(len(grid) - 1)

        @pl.when(k == 0)
        def _():
            acc[...] = jnp.zeros_like(acc)

        acc[...] += _dot_tn(_mx(a_ref[...]), _mx(b_ref[...]))

        @pl.when(k == nk - 1)
        def _():
            if split == 1:
                o_ref[...] = acc[...]
            else:
                for j in range(split):
                    o_ref[j] = acc[:, j * width:(j + 1) * width]

    sem = ("parallel",) * (len(grid) - 1) + ("arbitrary",)
    (out,), got = _call(
        body, (a, b), name=name, grid=grid, in_specs=[a_spec, b_spec], out_specs=[out_spec],
        out_shape=[jax.ShapeDtypeStruct(out_shape, F32)], scratch=[pltpu.VMEM(acc_shape, F32)], sem=sem, rider=rider)
    return out, got


def _in_proj(x, w_in, l):
    t = x.shape[0]

    def body(x_ref, w_ref, z_ref):
        xb = _mx(x_ref[...])
        for j in range(N_SHARD):
            z_ref[:, j * IN_SHARD:(j + 1) * IN_SHARD] = _dot(xb, w_ref[j])

    return pl.pallas_call(
        body, name="in_proj", grid=(t // TM,),
        in_specs=[pl.BlockSpec((TM, D_MODEL), lambda i: (i, 0)),
                  pl.BlockSpec((None, N_SHARD, D_MODEL, IN_SHARD), lambda i: (l, 0, 0, 0))],
        out_specs=pl.BlockSpec((TM, D_IN), lambda i: (i, 0)),
        out_shape=jax.ShapeDtypeStruct((t, D_IN), F32),
        compiler_params=_cparams("parallel"),
    )(x, w_in)


def _in_proj_bwd(dz, dx_res, w_in, l, r, gamma):
    t = dz.shape[0]

    def body(dz_ref, res_ref, w_ref, *tail):
        acc = res_ref[...]
        for j in range(N_SHARD):
            acc += _dot_nt(dz_ref[:, j * IN_SHARD:(j + 1) * IN_SHARD], w_ref[j])
        _norm_bwd_tail(acc, *tail)

    row = pl.BlockSpec((TM, D_MODEL), lambda i: (i, 0))
    tail_in, tail_out, tail_shape = _norm_tail_specs(t)
    return pl.pallas_call(
        body, name="in_proj_bwd", grid=(t // TM,),
        in_specs=[pl.BlockSpec((TM, D_IN), lambda i: (i, 0)), row,
                  pl.BlockSpec((None, N_SHARD, D_MODEL, IN_SHARD), lambda i: (l, 0, 0, 0))] + tail_in,
        out_specs=tail_out, out_shape=tail_shape, compiler_params=_cparams("arbitrary"),
    )(dz, dx_res, w_in, r, gamma)


def _out_proj(ycat, x, w_out, gamma, beta, l):
    t = x.shape[0]

    def body(yc_ref, x_ref, w_ref, g_ref, b_ref, y_ref, r_ref):
        r = ALPHA * x_ref[...] + _dot(yc_ref[...], w_ref[...])
        xh, _ = _ln_stats(r)
        r_ref[...] = r
        y_ref[...] = xh * g_ref[...] + b_ref[...]

    row = pl.BlockSpec((TM, D_MODEL), lambda i: (i, 0))
    vec = pl.BlockSpec((1, D_MODEL), lambda i: (0, 0))
    return pl.pallas_call(
        body, name="out_proj", grid=(t // TM,),
        in_specs=[row, row, pl.BlockSpec((None, D_MODEL, D_MODEL), lambda i: (l, 0, 0)), vec, vec],
        out_specs=[row, row],
        out_shape=[jax.ShapeDtypeStruct((t, D_MODEL), F32)] * 2,
        compiler_params=_cparams("parallel"),
    )(ycat, x, w_out, gamma, beta)


def _out_proj_bwd(dy, r, w_out, gamma, l):
    t = dy.shape[0]

    def body(dy_ref, r_ref, w_ref, g_ref, res_ref, dm_ref, dyc_ref, dgb_ref):
        @pl.when(pl.program_id(0) == 0)
        def _():
            dgb_ref[...] = jnp.zeros_like(dgb_ref)

        xh, rstd = _ln_stats(r_ref[...])
        dy = dy_ref[...]
        dr = _ln_bwd(dy, xh, rstd, g_ref[...])
        res_ref[...] = ALPHA * dr
        dm = _mx(dr)
        dm_ref[...] = dm
        dyc_ref[...] = _dot_nt(dm, w_ref[...])
        dgb_ref[0:1, :] += _colsum(dy * xh)
        dgb_ref[1:2, :] += _colsum(dy)

    row = pl.BlockSpec((TM, D_MODEL), lambda i: (i, 0))
    return pl.pallas_call(
        body, name="out_proj_bwd", grid=(t // TM,),
        in_specs=[row, row, pl.BlockSpec((None, D_MODEL, D_MODEL), lambda i: (l, 0, 0)),
                  pl.BlockSpec((1, D_MODEL), lambda i: (0, 0))],
        out_specs=[row, row, row, pl.BlockSpec((8, D_MODEL), lambda i: (0, 0))],
        out_shape=[jax.ShapeDtypeStruct((t, D_MODEL), F32), jax.ShapeDtypeStruct((t, D_MODEL), MXU_DTYPE),
                   jax.ShapeDtypeStruct((t, D_MODEL), F32), jax.ShapeDtypeStruct((8, D_MODEL), F32)],
        compiler_params=_cparams("arbitrary"),
    )(dy, r, w_out, gamma)


def _halo_specs(t, width, col):
    per = TMC // HALO
    last = t // HALO - 1
    return [pl.BlockSpec((HALO, width), lambda i: (jnp.maximum(i * per - 1, 0), col)),
            pl.BlockSpec((TMC, width), lambda i: (i, col)),
            pl.BlockSpec((HALO, width), lambda i: (jnp.minimum((i + 1) * per, last), col))]


def _extend(refs, i, nt):
    p_ref, c_ref, n_ref = refs
    p = jnp.where(i > 0, p_ref[...].astype(F32), 0.0)
    n = jnp.where(i < nt - 1, n_ref[...].astype(F32), 0.0)
    return jnp.concatenate([p, c_ref[...].astype(F32), n], axis=0)


def _shifted_copies(src_s, dst8_s):
    n = src_s.shape[0] - 8
    for b in range(8):
        dst8_s[b, 0:n, :] = src_s[pl.ds(b, n), :]


def _window(dst8_s, start):
    return dst8_s[start % 8, pl.ds(start - start % 8, TMC), :]


def _conv_fwd(z, sc_w, cc_w, cc_cb, cc_g, cc_b):
    t = z.shape[0]
    nt = t // TMC

    def body(*refs):
        b_ref = refs[0]
        c3, h3, a3, g3 = refs[1:4], refs[4:7], refs[7:10], refs[10:13]
        scw_ref, ccw_ref, cb_ref, lg_ref, lb_ref = refs[13:18]
        ysc_ref, ycc_ref, u2_ref, e_s, e8_s = refs[18:23]
        i = pl.program_id(0)
        e_s[...] = _extend(c3, i, nt) * _extend(h3, i, nt)
        cv = jnp.zeros((TMC, D_CONV), F32)
        for k in range(SC_W):
            cv += scw_ref[k:k + 1, :] * e_s[pl.ds(HALO + k - 1, TMC), :]
        ysc_ref[...] = _mx(b_ref[...] * cv)
        e_s[...] = _extend(a3, i, nt) * _sigmoid(_extend(g3, i, nt))
        _shifted_copies(e_s, e8_s)
        u2 = jnp.zeros((TMC, D_CONV), F32) + cb_ref[...]
        for k in range(CC_W):
            u2 += ccw_ref[k:k + 1, :] * _window(e8_s, HALO + k - 15)
        u2_ref[...] = u2
        xh, _ = _ln_stats(u2)
        n = xh * lg_ref[...] + lb_ref[...]
        ycc_ref[...] = _mx(n * _sigmoid(n))

    tile = pl.BlockSpec((TMC, D_CONV), lambda i: (i, 0))
    vec = pl.BlockSpec((1, D_CONV), lambda i: (0, 0))
    in_specs = ([pl.BlockSpec((TMC, D_CONV), lambda i: (i, 0))] + _halo_specs(t, D_CONV, 1) + _halo_specs(t, D_CONV, 2)
                + _halo_specs(t, D_CONV, 6) + _halo_specs(t, D_CONV, 7)
                + [pl.BlockSpec((SC_W, D_CONV), lambda i: (0, 0)), pl.BlockSpec((CC_W, D_CONV), lambda i: (0, 0)),
                   vec, vec, vec])
    return pl.pallas_call(
        body, name="conv_fwd", grid=(nt,), in_specs=in_specs, out_specs=[tile, tile, tile],
        out_shape=[jax.ShapeDtypeStruct((t, D_CONV), MXU_DTYPE), jax.ShapeDtypeStruct((t, D_CONV), MXU_DTYPE),
                   jax.ShapeDtypeStruct((t, D_CONV), F32)],
        scratch_shapes=[pltpu.VMEM((TMC + 2 * HALO, D_CONV), F32), pltpu.VMEM((8, TMC + 2 * HALO, D_CONV), F32)],
        compiler_params=_cparams("parallel"),
    )(*([z] * 13), sc_w, cc_w, cc_cb, cc_g, cc_b)


ROW_CCW, ROW_CCB, ROW_CCG, ROW_CCBETA, ROW_SCW, CONV_ROWS = 0, 31, 32, 33, 34, 40


def _conv_bwd(z, dycat, u2, sc_w, cc_w, cc_g, cc_b):
    t = z.shape[0]
    nt = t // TMC

    def body(*refs):
        b3, c3, h3, a3, g3 = refs[0:3], refs[3:6], refs[6:9], refs[9:12], refs[12:15]
        dys3, dyc3, u3 = refs[15:18], refs[18:21], refs[21:24]
        scw_ref, ccw_ref, lg_ref, lb_ref = refs[24:28]
        dsc_ref, dcc_ref, sm_ref, e_s, f_s, e8_s, f8_s, acc8_s = refs[28:36]
        i = pl.program_id(0)

        @pl.when(i == 0)
        def _():
            sm_ref[...] = jnp.zeros_like(sm_ref)

        cur = pl.ds(HALO, TMC)
        e_s[...] = _extend(c3, i, nt) * _extend(h3, i, nt)
        f_s[...] = _extend(dys3, i, nt) * _extend(b3, i, nt)
        cv = jnp.zeros((TMC, D_CONV), F32)
        dp = jnp.zeros((TMC, D_CONV), F32)
        dcv = f_s[cur, :]
        for k in range(SC_W):
            win = e_s[pl.ds(HALO + k - 1, TMC), :]
            cv += scw_ref[k:k + 1, :] * win
            dp += scw_ref[k:k + 1, :] * f_s[pl.ds(HALO - k + 1, TMC), :]
            sm_ref[ROW_SCW + k:ROW_SCW + k + 1, :] += _colsum(dcv * win)
        dsc_ref[:, 0:D_CONV] = _mx(dys3[1][...] * cv)
        dsc_ref[:, D_CONV:2 * D_CONV] = _mx(dp * h3[1][...])
        dsc_ref[:, 2 * D_CONV:3 * D_CONV] = _mx(dp * c3[1][...])
        xh, rstd = _ln_stats(_extend(u3, i, nt))
        n = xh * lg_ref[...] + lb_ref[...]
        sg = _sigmoid(n)
        dn = _extend(dyc3, i, nt) * (sg * (1.0 + n * (1.0 - sg)))
        f_s[...] = _ln_bwd(dn, xh, rstd, lg_ref[...])
        sm_ref[ROW_CCG:ROW_CCG + 1, :] += _colsum((dn * xh)[HALO:HALO + TMC])
        sm_ref[ROW_CCBETA:ROW_CCBETA + 1, :] += _colsum(dn[HALO:HALO + TMC])
        sig_g = _sigmoid(_extend(g3, i, nt))
        e_s[...] = _extend(a3, i, nt) * sig_g
        _shifted_copies(e_s, e8_s)
        _shifted_copies(f_s, f8_s)
        sm_ref[ROW_CCB:ROW_CCB + 1, :] += _colsum(f_s[cur, :])

        @pl.when(i == 0)
        def _():
            acc8_s[...] = jnp.zeros_like(acc8_s)

        def slab(r, carry):
            r0 = pl.multiple_of(r * SLAB, SLAB)
            for lanes in (slice(0, 128), slice(128, 256)):
                du2 = f_s[pl.ds(HALO + r0, SLAB), lanes]
                duu = jnp.zeros((SLAB, 128), F32)
                for k in range(CC_W):
                    sf, se = HALO + 15 - k, HALO + k - 15
                    duu += ccw_ref[k:k + 1, lanes] * f8_s[sf % 8, pl.ds(r0 + sf - sf % 8, SLAB), lanes]
                    prod = du2 * e8_s[se % 8, pl.ds(r0 + se - se % 8, SLAB), lanes]
                    acc8_s[k, :, lanes] += functools.reduce(jnp.add, [prod[8 * j:8 * j + 8] for j in range(SLAB // 8)])
                e_s[pl.ds(r0, SLAB), lanes] = duu
            return carry

        lax.fori_loop(0, TMC // SLAB, slab, 0)

        @pl.when(i == nt - 1)
        def _():
            for k in range(CC_W):
                sm_ref[ROW_CCW + k:ROW_CCW + k + 1, :] = _colsum(acc8_s[k])

        duu = e_s[0:TMC, :]
        sgc = sig_g[HALO:HALO + TMC]
        dcc_ref[:, 0:D_CONV] = _mx(duu * sgc)
        dcc_ref[:, D_CONV:2 * D_CONV] = _mx(duu * a3[1][...] * sgc * (1.0 - sgc))

    vec = pl.BlockSpec((1, D_CONV), lambda i: (0, 0))
    in_specs = []
    for col in (0, 1, 2, 6, 7):
        in_specs += _halo_specs(t, D_CONV, col)
    in_specs += _halo_specs(t, D_CONV, 0) + _halo_specs(t, D_CONV, 3) + _halo_specs(t, D_CONV, 0)
    in_specs += [pl.BlockSpec((SC_W, D_CONV), lambda i: (0, 0)), pl.BlockSpec((CC_W, D_CONV), lambda i: (0, 0)), vec, vec]
    return pl.pallas_call(
        body, name="conv_bwd", grid=(nt,), in_specs=in_specs,
        out_specs=[pl.BlockSpec((TMC, 3 * D_CONV), lambda i: (i, 0)), pl.BlockSpec((TMC, 2 * D_CONV), lambda i: (i, 0)),
                   pl.BlockSpec((CONV_ROWS, D_CONV), lambda i: (0, 0))],
        out_shape=[jax.ShapeDtypeStruct((t, 3 * D_CONV), MXU_DTYPE), jax.ShapeDtypeStruct((t, 2 * D_CONV), MXU_DTYPE),
                   jax.ShapeDtypeStruct((CONV_ROWS, D_CONV), F32)],
        scratch_shapes=[pltpu.VMEM((TMC + 2 * HALO, D_CONV), F32)] * 2
        + [pltpu.VMEM((8, TMC + 2 * HALO, D_CONV), F32)] * 2 + [pltpu.VMEM((CC_W, 8, D_CONV), F32)],
        compiler_params=_cparams("arbitrary"),
    )(*([z] * 15), *([dycat] * 6), *([u2] * 3), sc_w, cc_w, cc_g, cc_b)


def _lane(shape):
    return lax.broadcasted_iota(jnp.int32, shape, 1)


def _swap_halves(x):
    w = x.shape[1]
    lo = (_lane(x.shape) % HEAD_DIM) < HEAD_DIM // 2
    return jnp.where(lo, pltpu.roll(x, w - HEAD_DIM // 2, 1), pltpu.roll(x, HEAD_DIM // 2, 1))


def _half(shape, g):
    lane = _lane(shape)
    return lane < HEAD_DIM if g == 0 else lane >= HEAD_DIM


GROUP_ROWS = 4 * BLOCK


def _stack_heads(tiles, out_ref, nblk):
    for tt in range(4):
        g = tt // 2
        for slot in range(2):
            s = 2 * (tt % 2) + slot
            piece = tiles[tt] if slot == g else pltpu.roll(tiles[tt], HEAD_DIM, 1)
            piece = jnp.where(_half(piece.shape, g), piece, 0.0).astype(out_ref.dtype)
            for b in range(nblk):
                at = GROUP_ROWS * b + BLOCK * s
                out_ref[g, at:at + BLOCK, :] = piece[BLOCK * b:BLOCK * (b + 1)]


def _unstack_heads(ref, nblk):
    tiles = []
    for tt in range(4):
        g = tt // 2
        tile = None
        for slot in range(2):
            s = 2 * (tt % 2) + slot
            rows = [ref[g, GROUP_ROWS * b + BLOCK * s:GROUP_ROWS * b + BLOCK * (s + 1), :] for b in range(nblk)]
            piece = rows[0] if nblk == 1 else jnp.concatenate(rows, axis=0)
            if slot != g:
                piece = pltpu.roll(piece, HEAD_DIM, 1)
            tile = piece if tile is None else tile + piece
        tiles.append(tile)
    return tiles


def _attn_prep(z, cos, sin):
    t = z.shape[0]
    nblk = TM // BLOCK

    def body(qa_ref, qb_ref, k_ref, v_ref, cos_ref, sin_ref, qst_ref, kr_ref, vb_ref):
        cs, sn = cos_ref[...], sin_ref[...]

        def rope(x):
            return x * cs + _swap_halves(x) * sn

        tiles = []
        for tt in range(4):
            src = qa_ref if tt < 2 else qb_ref
            tiles.append(rope(src[:, (tt % 2) * BLOCK:(tt % 2 + 1) * BLOCK]) * (HEAD_DIM ** -0.5))
        _stack_heads(tiles, qst_ref, nblk)
        kr_ref[...] = _mx(rope(k_ref[...]))
        vb_ref[...] = _mx(v_ref[...])

    def col(width, j):
        return pl.BlockSpec((TM, width), lambda i: (i, j))

    return pl.pallas_call(
        body, name="attn_prep", grid=(t // TM,),
        in_specs=[col(256, 3), col(256, 4), col(128, 10), col(128, 11), col(128, 0), col(128, 0)],
        out_specs=[pl.BlockSpec((2, 4 * TM, BLOCK), lambda i: (0, i, 0)), col(128, 0), col(128, 0)],
        out_shape=[jax.ShapeDtypeStruct((2, 4 * t, BLOCK), MXU_DTYPE), jax.ShapeDtypeStruct((t, BLOCK), MXU_DTYPE),
                   jax.ShapeDtypeStruct((t, BLOCK), MXU_DTYPE)],
        compiler_params=_cparams("parallel"),
    )(z, z, z, z, cos, sin)


def _attn_dprep(dycat, ost, lst):
    t = dycat.shape[0]
    nblk = TM // BLOCK

    def body(da_ref, db_ref, o_ref, l_ref, dost_ref, ld_ref, st_s):
        tiles = []
        for tt in range(4):
            src = da_ref if tt < 2 else db_ref
            tiles.append(src[:, (tt % 2) * BLOCK:(tt % 2 + 1) * BLOCK])
        _stack_heads(tiles, st_s, nblk)
        for g in range(2):
            do = st_s[g]
            dost_ref[g] = _mx(do)
            dsum = jnp.sum(do * o_ref[g], axis=-1, keepdims=True)
            ld_ref[g] = jnp.where(_lane(do.shape) < HEAD_DIM, l_ref[g], dsum)

    stacked = pl.BlockSpec((2, 4 * TM, BLOCK), lambda i: (0, i, 0))
    return pl.pallas_call(
        body, name="attn_dprep", grid=(t // TM,),
        in_specs=[pl.BlockSpec((TM, 256), lambda i: (i, 1)), pl.BlockSpec((TM, 256), lambda i: (i, 2)), stacked, stacked],
        out_specs=[stacked, stacked],
        out_shape=[jax.ShapeDtypeStruct((2, 4 * t, BLOCK), MXU_DTYPE), jax.ShapeDtypeStruct((2, 4 * t, BLOCK), F32)],
        scratch_shapes=[pltpu.VMEM((2, 4 * TM, BLOCK), F32)],
        compiler_params=_cparams("parallel"),
    )(dycat, dycat, ost, lst)


def _attn_prep_bwd(dqst, dk, dv, cos, sin):
    t = dk.shape[0]
    nblk = TM // BLOCK

    def body(dq_ref, dk_ref, dv_ref, cos_ref, sin_ref, dz_ref):
        cs, sn = cos_ref[...], sin_ref[...]

        def rope_bwd(d):
            return d * cs + _swap_halves(d * sn)

        for tt, tile in enumerate(_unstack_heads(dq_ref, nblk)):
            dz_ref[:, tt * BLOCK:(tt + 1) * BLOCK] = _mx(rope_bwd(tile * (HEAD_DIM ** -0.5)))
        dz_ref[:, 4 * BLOCK:5 * BLOCK] = _mx(rope_bwd(dk_ref[...]))
        dz_ref[:, 5 * BLOCK:6 * BLOCK] = _mx(dv_ref[...])

    def col(width):
        return pl.BlockSpec((TM, width), lambda i: (i, 0))

    return pl.pallas_call(
        body, name="attn_prep_bwd", grid=(t // TM,),
        in_specs=[pl.BlockSpec((2, 4 * TM, BLOCK), lambda i: (0, i, 0)), col(128), col(128), col(128), col(128)],
        out_specs=col(768), out_shape=jax.ShapeDtypeStruct((t, 768), MXU_DTYPE),
        compiler_params=_cparams("parallel"),
    )(dqst, dk, dv, cos, sin)


def _nbr_specs(nb, width, col):
    return [pl.BlockSpec((BLOCK, width), lambda n: (jnp.maximum(n - 1, 0), col)),
            pl.BlockSpec((BLOCK, width), lambda n: (n, col)),
            pl.BlockSpec((BLOCK, width), lambda n: (jnp.minimum(n + 1, nb - 1), col))]


def _query_index():
    row = lax.broadcasted_iota(jnp.int32, (GROUP_ROWS, BLOCK), 0)
    return row & (BLOCK - 1), lax.broadcasted_iota(jnp.int32, (GROUP_ROWS, BLOCK), 1)


def _sink_column(sink_ref, g):
    band = lax.broadcasted_iota(jnp.int32, (GROUP_ROWS, 1), 0) // BLOCK
    col = jnp.zeros((GROUP_ROWS, 1), F32) + sink_ref[4 * g]
    for s in range(1, 4):
        col = jnp.where(band == s, sink_ref[4 * g + s], col)
    return col


def _attn_fwd(qst, kr, vb, sink):
    t = kr.shape[0]
    nb = t // BLOCK

    def body(q_ref, kp_ref, kc_ref, kn_ref, vp_ref, vc_ref, vn_ref, sink_ref, o_ref, ost_ref, lst_ref):
        n = pl.program_id(0)
        qi, kj = _query_index()
        m_prev, m_next = (kj >= qi) & (n > 0), (kj <= qi) & (n < nb - 1)
        nat = [None] * 4
        for g in range(2):
            q = q_ref[g]
            sp = jnp.where(m_prev, _dot_nt(q, kp_ref[...]), NEG)
            sc = _dot_nt(q, kc_ref[...])
            sn = jnp.where(m_next, _dot_nt(q, kn_ref[...]), NEG)
            sk = _sink_column(sink_ref, g)
            m = jnp.maximum(jnp.max(jnp.maximum(jnp.maximum(sp, sc), sn), axis=-1, keepdims=True), sk)
            pp, pc, pn = jnp.exp(sp - m), jnp.exp(sc - m), jnp.exp(sn - m)
            den = jnp.sum(pp + pc + pn, axis=-1, keepdims=True) + jnp.exp(sk - m)
            o = (_dot(_mx(pp), vp_ref[...]) + _dot(_mx(pc), vc_ref[...]) + _dot(_mx(pn), vn_ref[...])) / den
            o = jnp.where(_half(o.shape, g), o, 0.0)
            ost_ref[g] = o
            lst_ref[g] = jnp.broadcast_to(m + jnp.log(den), (GROUP_ROWS, BLOCK))
            for s in range(4):
                tt, slot = 2 * g + s // 2, s % 2
                piece = o[BLOCK * s:BLOCK * (s + 1)]
                if slot != g:
                    piece = pltpu.roll(piece, HEAD_DIM, 1)
                nat[tt] = piece if nat[tt] is None else nat[tt] + piece
        for tt in range(4):
            o_ref[:, tt * BLOCK:(tt + 1) * BLOCK] = _mx(nat[tt])

    stacked = pl.BlockSpec((2, GROUP_ROWS, BLOCK), lambda n: (0, n, 0))
    return pl.pallas_call(
        body, name="attn_fwd", grid=(nb,),
        in_specs=[stacked] + _nbr_specs(nb, BLOCK, 0) + _nbr_specs(nb, BLOCK, 0) + [pl.BlockSpec(memory_space=pltpu.SMEM)],
        out_specs=[pl.BlockSpec((BLOCK, 512), lambda n: (n, 0)), stacked, stacked],
        out_shape=[jax.ShapeDtypeStruct((t, 512), MXU_DTYPE), jax.ShapeDtypeStruct((2, 4 * t, BLOCK), F32),
                   jax.ShapeDtypeStruct((2, 4 * t, BLOCK), F32)],
        compiler_params=_cparams("parallel"),
    )(qst, kr, kr, kr, vb, vb, vb, sink)


def _lse_and_dsum(ld):
    return ld[:, 0:1], pltpu.roll(ld, HEAD_DIM, 1)[:, 0:1]


def _attn_bwd(qst, kr, vb, dost, ld, sink):
    t = kr.shape[0]
    nb = t // BLOCK

    def body(q_ref, kp_ref, kc_ref, kn_ref, vp_ref, vc_ref, vn_ref, do_ref, ld_ref, sink_ref,
             dq_ref, dk_ref, dv_ref, ds_ref):
        n = pl.program_id(0)

        @pl.when(n == 0)
        def _():
            ds_ref[...] = jnp.zeros_like(ds_ref)
            dk_ref[...] = jnp.zeros_like(dk_ref)
            dv_ref[...] = jnp.zeros_like(dv_ref)

        qi, kj = _query_index()
        m_prev, m_next = (kj >= qi) & (n > 0), (kj <= qi) & (n < nb - 1)
        key_rows = [pl.ds(pl.multiple_of(jnp.clip(n - 1 + b, 0, nb - 1) * BLOCK, BLOCK), BLOCK) for b in range(3)]
        for g in range(2):
            q, do = q_ref[g], do_ref[g]
            lse, dsum = _lse_and_dsum(ld_ref[g])
            acc = jnp.zeros((GROUP_ROWS, BLOCK), F32)
            for b, (k_ref, v_ref, valid) in enumerate(((kp_ref, vp_ref, m_prev), (kc_ref, vc_ref, None),
                                                       (kn_ref, vn_ref, m_next))):
                sc = _dot_nt(q, k_ref[...])
                if valid is not None:
                    sc = jnp.where(valid, sc, NEG)
                p = jnp.exp(sc - lse)
                dsc = _mx(p * (_dot_nt(do, v_ref[...]) - dsum))
                acc += _dot(dsc, k_ref[...])
                dv_ref[key_rows[b], :] += _dot_tn(_mx(p), do)
                dk_ref[key_rows[b], :] += _dot_tn(dsc, q)
            dq_ref[g] = jnp.where(_half(acc.shape, g), acc, 0.0)
            dsk = jnp.exp(_sink_column(sink_ref, g) - lse) * dsum
            for s in range(4):
                h = 4 * g + s
                ds_ref[h:h + 1, :] -= jnp.sum(dsk[BLOCK * s:BLOCK * (s + 1)], axis=0, keepdims=True)

    stacked = pl.BlockSpec((2, GROUP_ROWS, BLOCK), lambda n: (0, n, 0))
    whole = pl.BlockSpec((t, BLOCK), lambda n: (0, 0))
    return pl.pallas_call(
        body, name="attn_bwd", grid=(nb,),
        in_specs=[stacked] + _nbr_specs(nb, BLOCK, 0) + _nbr_specs(nb, BLOCK, 0)
        + [stacked, stacked, pl.BlockSpec(memory_space=pltpu.SMEM)],
        out_specs=[stacked, whole, whole, pl.BlockSpec((8, BLOCK), lambda n: (0, 0))],
        out_shape=[jax.ShapeDtypeStruct((2, 4 * t, BLOCK), F32), jax.ShapeDtypeStruct((t, BLOCK), F32),
                   jax.ShapeDtypeStruct((t, BLOCK), F32), jax.ShapeDtypeStruct((8, BLOCK), F32)],
        compiler_params=_cparams("arbitrary"),
    )(qst, kr, kr, kr, vb, vb, vb, dost, ld, sink)


def _loss_head(y, target, r, gamma):
    t = y.shape[0]

    def body(y_ref, t_ref, r_ref, g_ref, l_ref, res_ref, do_ref, dgb_ref):
        @pl.when(pl.program_id(0) == 0)
        def _():
            l_ref[...] = jnp.zeros_like(l_ref)

        e = y_ref[...] - t_ref[...]
        l_ref[...] += 0.5 * jnp.sum(_mean(e * e))
        _norm_bwd_tail(e / D_MODEL, r_ref, g_ref, res_ref, do_ref, dgb_ref)

    row = pl.BlockSpec((TM, D_MODEL), lambda i: (i, 0))
    tail_in, tail_out, tail_shape = _norm_tail_specs(t)
    return pl.pallas_call(
        body, name="loss_head", grid=(t // TM,), in_specs=[row, row] + tail_in,
        out_specs=[pl.BlockSpec((8, 128), lambda i: (0, 0))] + tail_out,
        out_shape=[jax.ShapeDtypeStruct((8, 128), F32)] + tail_shape,
        compiler_params=_cparams("arbitrary"),
    )(y, target, r, gamma)


def _adamw(name, w, g, m, v, rows):
    n, width = w.shape

    def body(w_ref, g_ref, m_ref, v_ref, d_ref, nm_ref, nv_ref):
        g = g_ref[...]
        m = ADAM_B1 * m_ref[...] + (1.0 - ADAM_B1) * g
        v = ADAM_B2 * v_ref[...] + (1.0 - ADAM_B2) * jnp.square(g)
        m_hat = m / (1.0 - ADAM_B1 ** ADAM_STEP)
        v_hat = v / (1.0 - ADAM_B2 ** ADAM_STEP)
        d_ref[...] = -ADAM_LR * (m_hat / (jnp.sqrt(v_hat) + ADAM_EPS) + ADAM_WD * w_ref[...])
        nm_ref[...] = m
        nv_ref[...] = v

    spec = pl.BlockSpec((rows, width), lambda i: (i, 0))
    return pl.pallas_call(
        body, name=name, grid=(n // rows,), in_specs=[spec] * 4, out_specs=[spec] * 3,
        out_shape=[jax.ShapeDtypeStruct((n, width), F32)] * 3, compiler_params=_cparams("parallel"),
    )(w, g, m, v)


def _place():
    x, y, c = lax.axis_index("x"), lax.axis_index("y"), lax.axis_index("c")
    chips = [(1 - x, y), (x, 1 - y), (1 - x, 1 - y)]
    return x, y, c, chips


class _Rider:
    def pass_on_steps(self, nsteps):
        return []

    def pass_on(self, a, ins, outs, sems):
        pass


class _Gather(_Rider):
    def __init__(self, shards):
        na = len(shards)
        self.ins = list(shards)
        self.outs = [jax.ShapeDtypeStruct((N_SHARD,) + s.shape, s.dtype) for s in shards]
        self.sems = [pltpu.SemaphoreType.DMA((3 * na,))] * 4 + [pltpu.SemaphoreType.DMA((na,))]

    def _copies(self, src, dst, sems):
        send, recv, fsend, frecv, lsem = sems
        x, y, c, chips = _place()
        mine = 2 * x + y

        def local(a):
            return pltpu.make_async_copy(src[a], dst[a].at[mine], lsem.at[a])

        def ici(a, k, shard):
            cx, cy = chips[k]
            return pltpu.make_async_remote_copy(
                src_ref=src[a].at[c], dst_ref=dst[a].at[shard, c], send_sem=send.at[3 * a + k], recv_sem=recv.at[3 * a + k],
                device_id=(cx, cy, c), device_id_type=MESH)

        def d2d(a, k, half):
            cx, cy = chips[k]
            block = dst[a].at[2 * cx + cy, half]
            return pltpu.make_async_remote_copy(
                src_ref=block, dst_ref=block, send_sem=fsend.at[3 * a + k], recv_sem=frecv.at[3 * a + k],
                device_id=(x, y, 1 - c), device_id_type=MESH)

        return local, ici, d2d, mine, c, chips

    def start(self, src, dst, sems):
        local, ici, _, mine, _, _ = self._copies(src, dst, sems)
        for a in range(len(src)):
            local(a).start()
            for k in range(3):
                ici(a, k, mine).start()

    def pass_on_steps(self, nsteps):
        sizes = np.cumsum([float(np.prod(s.shape)) * s.dtype.itemsize for s in self.ins])
        steps = [min(nsteps - 1, int(nsteps * done / sizes[-1]) + 1) for done in sizes]
        steps[-1] = nsteps - 1
        return steps

    def pass_on(self, a, src, dst, sems):
        _, ici, d2d, _, c, chips = self._copies(src, dst, sems)
        for k, (cx, cy) in enumerate(chips):
            ici(a, k, 2 * cx + cy).wait_recv()
            d2d(a, k, c).start()

    def finish(self, src, dst, sems):
        local, ici, d2d, mine, c, chips = self._copies(src, dst, sems)
        for a in range(len(src)):
            for k in range(3):
                d2d(a, k, 1 - c).wait_recv()
        for a in range(len(src)):
            for k in range(3):
                ici(a, k, mine).wait_send()
                d2d(a, k, c).wait_send()
            local(a).wait()


class _PairExchange(_Rider):
    def __init__(self, parts):
        self.ins = list(parts)
        self.outs = [jax.ShapeDtypeStruct((N_SHARD,) + p.shape[2:], p.dtype) for p in parts]
        self.sems = [pltpu.SemaphoreType.DMA((len(parts),))] * 2

    def _copy(self, a, src, dst, sems):
        x, y, c, _ = _place()
        return pltpu.make_async_remote_copy(
            src_ref=src[a].at[:, 1 - c], dst_ref=dst[a], send_sem=sems[0].at[a], recv_sem=sems[1].at[a],
            device_id=(x, y, 1 - c), device_id_type=MESH)

    def start(self, src, dst, sems):
        for a in range(len(src)):
            self._copy(a, src, dst, sems).start()

    def finish(self, src, dst, sems):
        for a in range(len(src)):
            self._copy(a, src, dst, sems).wait()


class _ChipExchange(_Rider):
    def __init__(self, sums):
        self.ins = list(sums)
        self.outs = [jax.ShapeDtypeStruct((3,) + s.shape[1:], s.dtype) for s in sums]
        self.sems = [pltpu.SemaphoreType.DMA((3 * len(sums),))] * 2

    def _copy(self, a, k, src, dst, sems):
        _, _, c, chips = _place()
        cx, cy = chips[k]
        return pltpu.make_async_remote_copy(
            src_ref=src[a].at[2 * cx + cy], dst_ref=dst[a].at[k], send_sem=sems[0].at[3 * a + k],
            recv_sem=sems[1].at[3 * a + k], device_id=(cx, cy, c), device_id_type=MESH)

    def start(self, src, dst, sems):
        for a in range(len(src)):
            for k in range(3):
                self._copy(a, k, src, dst, sems).start()

    def finish(self, src, dst, sems):
        for a in range(len(src)):
            for k in range(3):
                self._copy(a, k, src, dst, sems).wait()


class _Both(_Rider):
    def __init__(self, a, b):
        self.a, self.b = a, b
        self.ins, self.outs, self.sems = a.ins + b.ins, a.outs + b.outs, a.sems + b.sems

    def _each(self, method, ins, outs, sems):
        a = self.a
        getattr(a, method)(ins[:len(a.ins)], outs[:len(a.outs)], sems[:len(a.sems)])
        getattr(self.b, method)(ins[len(a.ins):], outs[len(a.outs):], sems[len(a.sems):])

    def start(self, ins, outs, sems):
        self._each("start", ins, outs, sems)

    def finish(self, ins, outs, sems):
        self._each("finish", ins, outs, sems)


def _run(name, rider):
    n_in, n_out = len(rider.ins), len(rider.outs)

    def body(*refs):
        ins, outs, sems = refs[:n_in], refs[n_in:n_in + n_out], refs[n_in + n_out:]
        rider.start(ins, outs, sems)
        for a in range(len(rider.pass_on_steps(1))):
            rider.pass_on(a, ins, outs, sems)
        rider.finish(ins, outs, sems)

    return list(pl.pallas_call(
        body, name=name, in_specs=[ANY] * n_in, out_specs=[ANY] * n_out, out_shape=rider.outs,
        scratch_shapes=rider.sems)(*rider.ins))


def _pair_share(halves):
    na = len(halves)

    def body(*refs):
        dst = refs[na:2 * na]
        send, recv = refs[2 * na:]
        x, y, c, _ = _place()
        cps = []
        for a in range(na):
            cp = pltpu.make_async_remote_copy(
                src_ref=dst[a].at[:, c], dst_ref=dst[a].at[:, c], send_sem=send.at[a], recv_sem=recv.at[a],
                device_id=(x, y, 1 - c), device_id_type=MESH)
            cp.start()
            cps.append(cp)
        for a in range(na):
            cps[a].wait_send()
            pltpu.make_async_remote_copy(
                src_ref=dst[a].at[:, 1 - c], dst_ref=dst[a].at[:, 1 - c], send_sem=send.at[a], recv_sem=recv.at[a],
                device_id=(x, y, 1 - c), device_id_type=MESH).wait_recv()

    return pl.pallas_call(
        body, name="pair_share", in_specs=[ANY] * na, out_specs=[ANY] * na,
        out_shape=[jax.ShapeDtypeStruct(h.shape, h.dtype) for h in halves],
        input_output_aliases={a: a for a in range(na)},
        scratch_shapes=[pltpu.SemaphoreType.DMA((na,))] * 2,
    )(*halves)


def _sum_rows(r):
    return r if r <= 352 else 256


def _pair_sum(name, part, got):
    _, _, r, w = part.shape
    rows = _sum_rows(r)
    c = lax.axis_index("c").astype(jnp.int32).reshape(1)

    def body(c_ref, p_ref, g_ref, o_ref):
        o_ref[...] = _mx(p_ref[...] + g_ref[...])

    spec = pl.BlockSpec((None, rows, w), lambda j, i, c_ref: (j, i, 0))
    return pl.pallas_call(
        body, name=name, out_shape=jax.ShapeDtypeStruct((N_SHARD, r, w), MXU_DTYPE),
        grid_spec=pltpu.PrefetchScalarGridSpec(
            num_scalar_prefetch=1, grid=(N_SHARD, r // rows),
            in_specs=[pl.BlockSpec((None, None, rows, w), lambda j, i, c_ref: (j, c_ref[0], i, 0)), spec],
            out_specs=spec),
        compiler_params=_cparams("parallel", "parallel"),
    )(c, part, got)


def _chip_sum(name, part, got, others, l, prev):
    _, _, r, w = part.shape
    rows = _sum_rows(r)
    cj = jnp.stack([lax.axis_index("c"), 2 * lax.axis_index("x") + lax.axis_index("y")]).astype(jnp.int32)

    def body(cj_ref, p_ref, g_ref, o_ref, *rest):
        acc = p_ref[...] + g_ref[...]
        for k in range(3):
            acc += o_ref[k].astype(F32)
        rest[-1][...] = acc

    ins, specs, alias = [cj, part, got, others], [], {}
    if prev is not None:
        ins.append(prev)
        specs.append(ANY)
        alias = {4: 0}
    return pl.pallas_call(
        body, name=name, out_shape=jax.ShapeDtypeStruct((2, 2, r, w), F32), input_output_aliases=alias,
        grid_spec=pltpu.PrefetchScalarGridSpec(
            num_scalar_prefetch=1, grid=(r // rows,),
            in_specs=[pl.BlockSpec((None, None, rows, w), lambda i, cj: (cj[1], cj[0], i, 0)),
                      pl.BlockSpec((None, rows, w), lambda i, cj: (cj[1], i, 0)),
                      pl.BlockSpec((3, rows, w), lambda i, cj: (0, i, 0))] + specs,
            out_specs=pl.BlockSpec((None, None, rows, w), lambda i, cj: (l, cj[0], i, 0))),
        compiler_params=_cparams("parallel"),
    )(*ins)


SMALL_ROWS = 40


def _sum_small(part):
    def body(p_ref, o_ref, land, send, recv):
        x, y, c, _ = _place()
        me = 4 * x + 2 * y + c
        cps = []
        for r in range(1, 8):
            cp = pltpu.make_async_remote_copy(
                src_ref=p_ref, dst_ref=land.at[r], send_sem=send.at[r], recv_sem=recv.at[r],
                device_id=(x ^ (r >> 2), y ^ ((r >> 1) & 1), c ^ (r & 1)), device_id_type=MESH)
            cp.start()
            cps.append(cp)
        land[0] = p_ref[...]
        for cp in cps:
            cp.wait()
        acc = land[me]
        for e in range(1, 8):
            acc += land[me ^ e]
        o_ref[...] = acc

    return pl.pallas_call(
        body, name="sum_small", in_specs=[pl.BlockSpec(memory_space=pltpu.VMEM)],
        out_specs=pl.BlockSpec(memory_space=pltpu.VMEM), out_shape=jax.ShapeDtypeStruct(part.shape, F32),
        scratch_shapes=[pltpu.VMEM((8,) + part.shape, F32), pltpu.SemaphoreType.DMA((8,)), pltpu.SemaphoreType.DMA((8,))],
    )(part)


BIG = ("ffn1_w_gu", "ffn1_w_down", "w_in", "w_out", "ffn2_w_gu", "ffn2_w_down")
SMALL = ("ln1_g", "ln1_b", "ln2_g", "ln2_b", "ln3_g", "ln3_b", "attn_sink", "cc_conv_b", "cc_ln_g", "cc_ln_b",
         "sc_conv_w", "cc_conv_w")
NAMES = ("ffn1_w_gu", "ffn1_w_down", "ln1_g", "ln1_b", "w_in", "sc_conv_w", "attn_sink", "cc_conv_w", "cc_conv_b",
         "cc_ln_g", "cc_ln_b", "w_out", "ln2_g", "ln2_b", "ffn2_w_gu", "ffn2_w_down", "ln3_g", "ln3_b")


def _rope_tables(t):
    half = HEAD_DIM // 2
    inv_freq = ROPE_THETA ** (-jnp.arange(half, dtype=F32) / half)
    ang = jnp.arange(t).astype(F32)[:, None] * inv_freq[None, :]
    cos, sin = jnp.cos(ang), jnp.sin(ang)
    return jnp.tile(jnp.concatenate([cos, cos], axis=1), (1, 2)), jnp.tile(jnp.concatenate([-sin, sin], axis=1), (1, 2))


def _pack_small(vals):
    flat = jnp.concatenate([vals[n].reshape(-1) for n in SMALL])
    return jnp.pad(flat, (0, SMALL_ROWS * D_MODEL - flat.shape[0])).reshape(SMALL_ROWS, D_MODEL)


def _unpack_small(packed, shapes):
    flat, out, at = packed.reshape(-1), {}, 0
    for n in SMALL:
        size = int(np.prod(shapes[n]))
        out[n] = flat[at:at + size].reshape(shapes[n])
        at += size
    return out


def kernel(x, ffn1_w_gu, ffn1_w_down, ln1_g, ln1_b, w_in, sc_conv_w, attn_sink, cc_conv_w, cc_conv_b, cc_ln_g, cc_ln_b, w_out, ln2_g, ln2_b, ffn2_w_gu, ffn2_w_down, ln3_g, ln3_b, loss_target, m_ffn1_w_gu, m_ffn1_w_down, m_ln1_g, m_ln1_b, m_w_in, m_sc_conv_w, m_attn_sink, m_cc_conv_w, m_cc_conv_b, m_cc_ln_g, m_cc_ln_b, m_w_out, m_ln2_g, m_ln2_b, m_ffn2_w_gu, m_ffn2_w_down, m_ln3_g, m_ln3_b, v_ffn1_w_gu, v_ffn1_w_down, v_ln1_g, v_ln1_b, v_w_in, v_sc_conv_w, v_attn_sink, v_cc_conv_w, v_cc_conv_b, v_cc_ln_g, v_cc_ln_b, v_w_out, v_ln2_g, v_ln2_b, v_ffn2_w_gu, v_ffn2_w_down, v_ln3_g, v_ln3_b):
    given = dict(locals())
    w = {n: given[n] for n in NAMES}
    mom = {n: given["m_" + n] for n in NAMES}
    var = {n: given["v_" + n] for n in NAMES}
    x0 = x[0]
    target = loss_target[0]
    t = x0.shape[0]
    chip = 2 * lax.axis_index("x") + lax.axis_index("y")

    conv_shard = jnp.pad(jnp.concatenate([sc_conv_w, cc_conv_w], axis=1), ((0, 0), (0, 30), (0, 64)))
    local = {n: _mx(w[n]) for n in BIG}
    local["conv"] = conv_shard
    full = [{}, {}]

    def gather(l, names):
        return _Gather([local[n][l].reshape(2, local[n].shape[1] // 2, local[n].shape[2]) for n in names])

    def land(l, names, arrays):
        for n, a in zip(names, arrays):
            full[l][n] = a.reshape(1, N_SHARD, 2 * a.shape[2], a.shape[3])

    def weights(l):
        f = full[l]
        conv = jnp.transpose(f["conv"][0, :, :SC_W + CC_W, :64], (1, 0, 2)).reshape(SC_W + CC_W, D_CONV)
        return dict(wgu1=f["ffn1_w_gu"], wd1=f["ffn1_w_down"].reshape(1, D_FF, D_MODEL), win=f["w_in"],
                    wout=f["w_out"].reshape(1, D_MODEL, D_MODEL), wgu2=f["ffn2_w_gu"],
                    wd2=f["ffn2_w_down"].reshape(1, D_FF, D_MODEL), sc=conv[:SC_W], cc=conv[SC_W:])

    first = ("ffn1_w_gu", "ffn1_w_down")
    mixer = ("w_in", "w_out", "conv")
    second = ("ffn2_w_gu", "ffn2_w_down")
    land(0, first, _run("gather_first", gather(0, first)))
    cos, sin = _rope_tables(t)

    def vec(a, l):
        return a[l][None, :]

    acts = []
    h = x0
    for l in range(2):
        ahead = (0, second + mixer) if l == 0 else (1, second)
        (y1, r1, gu1), got = _ffn_fwd("ffn_fwd_a%d" % l, h, full[l]["ffn1_w_gu"], full[l]["ffn1_w_down"].reshape(1, D_FF, D_MODEL),
                                      vec(ln1_g, l), vec(ln1_b, l), 0, gather(*ahead))
        land(*ahead, got)
        wl = weights(l)
        z = _in_proj(y1, wl["win"], 0)
        ysc, ycc, u2 = _conv_fwd(z, wl["sc"], wl["cc"], vec(cc_conv_b, l), vec(cc_ln_g, l), vec(cc_ln_b, l))
        qs, kf, vf = _attn_prep(z, cos, sin)
        o_nat, o, lse = _attn_fwd(qs, kf, vf, attn_sink[l])
        ycat = jnp.concatenate([ysc, o_nat, ycc], axis=1)
        y2, r2 = _out_proj(ycat, y1, wl["wout"], vec(ln2_g, l), vec(ln2_b, l), 0)
        ahead = (1, first + mixer) if l == 0 else None
        (y3, r3, gu2), got = _ffn_fwd("ffn_fwd_b%d" % l, y2, wl["wgu2"], wl["wd2"], vec(ln3_g, l), vec(ln3_b, l), 0,
                                      gather(*ahead) if ahead else None)
        if ahead:
            land(*ahead, got)
        acts.append(dict(x=h, y1=y1, r1=r1, gu1=gu1, z=z, u2=u2, qs=qs, kf=kf, vf=vf, o=o, lse=lse, ycat=ycat,
                         y2=y2, r2=r2, gu2=gu2, r3=r3, w=wl))
        h = y3
    loss_rows, res, do, dgb3_next = _loss_head(h, target, acts[1]["r3"], vec(ln3_g, 1))
    loss = lax.psum(loss_rows[0, 0], ("x", "y", "c"))

    upper = ("ffn2_w_gu", "ffn2_w_down", "w_out")
    lower = ("w_in", "ffn1_w_gu", "ffn1_w_down")
    part = [{}, {}]
    small = [None, None]
    stage = {}
    reduced = {n: None for n in BIG}
    row = pl.BlockSpec((TM, D_MODEL), lambda n, k: (k, 0))
    deep = pl.BlockSpec((TK, D_MODEL), lambda n, k: (k, 0))

    def halves(a, r):
        return a.reshape(N_SHARD, 2, r // 2, a.shape[-1])

    def pair_rider(l, names):
        return _PairExchange([part[l][n] for n in names])

    def after_pair(l, names, got):
        stage[l, names] = (got, [_pair_sum("pair_sum_%s_%d" % (n, l), part[l][n], g) for n, g in zip(names, got)])

    def chip_rider(l, names):
        return _ChipExchange(stage[l, names][1])

    def after_chip(l, names, others):
        for n, g, o in zip(names, stage[l, names][0], others):
            reduced[n] = _chip_sum("chip_sum_%s_%d" % (n, l), part[l][n], g, o, l, reduced[n])

    def ffn_weight_grads(which, l, xin, dh, a, do, rider_gu=None, make_rider_d=None):
        out, got_gu = _mm_tn(
            "%s_dwgu_%d" % (which, l), xin, dh, deep,
            pl.BlockSpec((None, TK, FF_CHUNK), lambda n, k: (n // N_CHUNK, k, n % N_CHUNK)),
            pl.BlockSpec((None, D_MODEL, FF_CHUNK), lambda n, k: (n, 0, 0)),
            (N_SHARD, D_MODEL, GU_SHARD), (D_MODEL, FF_CHUNK), (2 * N_CHUNK, t // TK), rider_gu)
        part[l][which + "_w_gu"] = halves(out, D_MODEL)
        rider_d = make_rider_d() if make_rider_d else None
        out, got_d = _mm_tn(
            "%s_dwd_%d" % (which, l), a, do, pl.BlockSpec((TK, FF_CHUNK), lambda n, k: (k, n)), deep,
            pl.BlockSpec((FF_CHUNK, D_MODEL), lambda n, k: (n, 0)),
            (D_FF, D_MODEL), (FF_CHUNK, D_MODEL), (N_CHUNK, t // TK), rider_d)
        part[l][which + "_w_down"] = halves(out, D_FF // N_SHARD)
        return got_gu, got_d

    w_in_only, w_gu_only, w_down_only = ("w_in",), ("ffn1_w_gu",), ("ffn1_w_down",)
    for l in (1, 0):
        s = acts[l]
        wl = s["w"]
        dgb3 = dgb3_next
        if l == 0:
            dy, dh, a, got = _ffn_bwd(res, do, s["gu2"], wl["wgu2"], wl["wd2"], 0, rider=pair_rider(1, lower))
            after_pair(1, lower, got)
            got, _ = ffn_weight_grads("ffn2", l, s["y2"], dh, a, do, chip_rider(1, lower))
            after_chip(1, lower, got)
        else:
            dy, dh, a, _ = _ffn_bwd(res, do, s["gu2"], wl["wgu2"], wl["wd2"], 0)
            ffn_weight_grads("ffn2", l, s["y2"], dh, a, do)
        res, dm, dycat, dgb2 = _out_proj_bwd(dy, s["r2"], wl["wout"], vec(ln2_g, l), 0)
        out, _ = _mm_tn("dwout_%d" % l, s["ycat"], dm, row, row, pl.BlockSpec((D_MODEL, D_MODEL), lambda n, k: (0, 0)),
                        (D_MODEL, D_MODEL), (D_MODEL, D_MODEL), (1, t // TM))
        part[l]["w_out"] = halves(out, OUT_SHARD)
        dz_sc, dz_cc, dconv = _conv_bwd(s["z"], dycat, s["u2"], wl["sc"], wl["cc"], vec(cc_ln_g, l), vec(cc_ln_b, l))
        dost, ld = _attn_dprep(dycat, s["o"], s["lse"])
        dqs, dkf, dvf, dsink = _attn_bwd(s["qs"], s["kf"], s["vf"], dost, ld, attn_sink[l])
        dz_att = _attn_prep_bwd(dqs, dkf, dvf, cos, sin)
        dz = jnp.concatenate([dz_sc, dz_att, dz_cc], axis=1)
        out, got = _mm_tn(
            "dwin_%d" % l, s["y1"], dz, row, pl.BlockSpec((TM, D_IN), lambda n, k: (k, 0)),
            pl.BlockSpec((N_SHARD, D_MODEL, IN_SHARD), lambda n, k: (0, 0, 0)),
            (N_SHARD, D_MODEL, IN_SHARD), (D_MODEL, D_IN), (1, t // TM), pair_rider(l, upper), split=N_SHARD)
        part[l]["w_in"] = halves(out, D_MODEL)
        after_pair(l, upper, got)
        res, do, dgb1 = _in_proj_bwd(dz, res, wl["win"], 0, s["r1"], vec(ln1_g, l))
        if l == 1:
            (res0, do0, dgb3_next), dh, a, _ = _ffn_bwd(res, do, s["gu1"], wl["wgu1"], wl["wd1"], 0,
                                                        tail=(acts[0]["r3"], vec(ln3_g, 0)))
            got, _ = ffn_weight_grads("ffn1", l, s["x"], dh, a, do, chip_rider(l, upper))
            after_chip(l, upper, got)
            res, do = res0, do0
        else:
            dy, dh, a, got = _ffn_bwd(res, do, s["gu1"], wl["wgu1"], wl["wd1"], 0, rider=pair_rider(0, w_in_only))
            after_pair(0, w_in_only, got)
            got, got_d = ffn_weight_grads("ffn1", l, s["x"], dh, a, do, _Both(chip_rider(0, upper), chip_rider(0, w_in_only)),
                                          lambda: pair_rider(0, w_gu_only))
            n_upper = len(upper)
            after_chip(0, upper, got[:n_upper])
            after_chip(0, w_in_only, got[n_upper:])
            after_pair(0, w_gu_only, got_d)
        small[l] = dict(ln1_g=dgb1[0], ln1_b=dgb1[1], ln2_g=dgb2[0], ln2_b=dgb2[1], ln3_g=dgb3[0], ln3_b=dgb3[1],
                        attn_sink=dsink[:, 0], cc_conv_b=dconv[ROW_CCB], cc_ln_g=dconv[ROW_CCG],
                        cc_ln_b=dconv[ROW_CCBETA], sc_conv_w=dconv[ROW_SCW:ROW_SCW + SC_W],
                        cc_conv_w=dconv[ROW_CCW:ROW_CCW + CC_W])
    grad_x = dy[None]

    after_pair(0, w_down_only, _run("pair_exchange_last", pair_rider(0, w_down_only)))
    got = _run("chip_exchange_last", _Both(chip_rider(0, w_gu_only), chip_rider(0, w_down_only)))
    after_chip(0, w_gu_only, got[:1])
    after_chip(0, w_down_only, got[1:])
    grads = dict(zip(BIG, _pair_share([reduced[n] for n in BIG])))
    for n in BIG:
        grads[n] = grads[n].reshape(w[n].shape)

    small_full = {n: jnp.stack([small[0][n], small[1][n]]) for n in SMALL}
    small_sum = _unpack_small(_sum_small(_pack_small(small_full)), {n: small_full[n].shape for n in SMALL})
    for n in SMALL:
        g = small_sum[n]
        if n in ("sc_conv_w", "cc_conv_w"):
            g = lax.dynamic_slice_in_dim(g, chip * 64, 64, axis=2)
        grads[n] = g

    delta, new_m, new_v = {}, {}, {}
    for n in BIG:
        shape = w[n].shape
        two_d = (shape[0] * shape[1], shape[2])
        outs = _adamw("adamw_" + n, w[n].reshape(two_d), grads[n].reshape(two_d), mom[n].reshape(two_d),
                      var[n].reshape(two_d), 128)
        delta[n], new_m[n], new_v[n] = [a.reshape(shape) for a in outs]
    shapes = {n: w[n].shape for n in SMALL}
    outs = _adamw("adamw_small", _pack_small({n: w[n] for n in SMALL}), _pack_small({n: grads[n] for n in SMALL}),
                  _pack_small({n: mom[n] for n in SMALL}), _pack_small({n: var[n] for n in SMALL}), 8)
    for d, packed in zip((delta, new_m, new_v), outs):
        d.update(_unpack_small(packed, shapes))

    return (loss, grad_x, *[grads[n] for n in NAMES], *[delta[n] for n in NAMES], *[new_m[n] for n in NAMES],
            *[new_v[n] for n in NAMES])
```
